```python
import jax, jax.numpy as jnp
from jax import lax
import numpy as np

D_MODEL = 2048
BATCH = 4
SEQ = 4096
DEPTH = 4

GLA_HEADS = 4
GLA_DK = D_MODEL // 2
GLA_DV = D_MODEL
GLA_HEAD_K = GLA_DK // GLA_HEADS
GLA_HEAD_V = GLA_DV // GLA_HEADS
GLA_RANK = 16
GLA_TAU = 16.0
GLA_CHUNK = 64

DIL_GROUPS = ((128, 1), (512, 4), (2048, 16))
N_DIL = 3
ATT_HEADS_PER_GROUP = 4
HEAD_DIM = 128
ATT_WIDTH = N_DIL * ATT_HEADS_PER_GROUP * HEAD_DIM
ATT_OUT = ATT_HEADS_PER_GROUP * HEAD_DIM
ATT_BLOCK = 128
ROPE_THETA = 10000.0

N_GROUPS = 4
EXPERTS_PER_GROUP = 8
N_EXPERTS = N_GROUPS * EXPERTS_PER_GROUP
TOP_K_INNER = 2
D_EXPERT = 512

N_MOD = 6
EPS = 1e-6

IN_SPLITS = (GLA_DK, GLA_DK, GLA_DV, GLA_DV, GLA_RANK,
             ATT_WIDTH, ATT_WIDTH, ATT_WIDTH,
             D_MODEL, D_MODEL)
D_IN = GLA_DK * 2 + GLA_DV * 2 + GLA_RANK + ATT_WIDTH * 3 + D_MODEL * 2

kernel_name = "hybrid_gla_dilated_attn_hmoe_adaln"


def rms_norm(x, g):
    xf = x.astype(jnp.float32)
    y = xf * lax.rsqrt(jnp.mean(xf * xf, axis=-1, keepdims=True) + EPS)
    return (y * g.astype(jnp.float32)).astype(x.dtype)


def rope(x, cos, sin):
    x1, x2 = jnp.split(x, 2, axis=-1)
    return jnp.concatenate([x1 * cos - x2 * sin, x2 * cos + x1 * sin], axis=-1)


def gla_chunked(q, k, v, log_a):
    B, S, H, dk = q.shape
    dv = v.shape[-1]
    C = GLA_CHUNK
    N = S // C

    def chunks(t):
        return t.reshape(B, N, C, H, t.shape[-1]).transpose(1, 0, 3, 2, 4)

    q, k, v, log_a = chunks(q), chunks(k), chunks(v), chunks(log_a)
    b = jnp.cumsum(log_a, axis=3)
    b_last = b[:, :, :, -1:, :]
    q_dec = q * jnp.exp(b)
    k_inv = k * jnp.exp(-b)
    k_upd = k * jnp.exp(b_last - b)
    causal = jnp.tril(jnp.ones((C, C), dtype=bool))
    att = jnp.where(causal, jnp.einsum('nbhid,nbhjd->nbhij', q_dec, k_inv), 0.0)
    o_intra = jnp.einsum('nbhij,nbhje->nbhie', att, v)
    decay = jnp.exp(b_last[:, :, :, 0, :])

    def step(state, inp):
        qd, ku, vc, dc = inp
        o = jnp.einsum('bhcd,bhde->bhce', qd, state)
        state = dc[..., None] * state + jnp.einsum('bhcd,bhce->bhde', ku, vc)
        return state, o

    s0 = jnp.zeros((B, H, dk, dv), q.dtype)
    _, o_inter = lax.scan(step, s0, (q_dec, k_upd, v, decay))
    o = o_inter + o_intra
    return o.transpose(1, 0, 3, 2, 4).reshape(B, S, H, dv)


def banded_window_attention(q, k, v, n_steps):
    lead = q.shape[:-2]
    L, D = q.shape[-2:]
    nb = -(-L // ATT_BLOCK)
    Lp = nb * ATT_BLOCK
    nl = len(lead)
    pad = [(0, 0)] * nl + [(0, Lp - L), (0, 0)]
    qb, kb, vb = (jnp.pad(t, pad).reshape(*lead, nb, ATT_BLOCK, D) for t in (q, k, v))

    def with_prev(t):
        prev = jnp.pad(t, [(0, 0)] * nl + [(1, 0), (0, 0), (0, 0)])[..., :-1, :, :]
        return jnp.concatenate([prev, t], axis=-2)

    kb, vb = with_prev(kb), with_prev(vb)
    s = jnp.einsum('...nqd,...nkd->...nqk', qb, kb) * (D ** -0.5)
    qi = jnp.arange(ATT_BLOCK)[:, None]
    kj = jnp.arange(2 * ATT_BLOCK)[None, :] - ATT_BLOCK
    dist = qi - kj
    blk = jnp.arange(nb)[:, None, None]
    valid = (dist >= 0) & (dist <= n_steps) & (blk * ATT_BLOCK + kj[None] >= 0)
    s = jnp.where(valid, s, -jnp.inf)
    lse = jax.nn.logsumexp(s, axis=-1)
    p = jnp.exp(s - lse[..., None])
    o = jnp.einsum('...nqk,...nkd->...nqd', p, vb)
    return o.reshape(*lead, Lp, D)[..., :L, :], lse.reshape(*lead, Lp)[..., :L]


def dilated_attention(q, k, v):
    B, S, G, H, D = q.shape
    outs, lses = [], []
    for g, (window, dil) in enumerate(DIL_GROUPS):
        def to_phase(t):
            return t[:, :, g].reshape(B, S // dil, dil, H, D).transpose(0, 2, 3, 1, 4)
        o, lse = banded_window_attention(to_phase(q), to_phase(k), to_phase(v), window // dil)
        outs.append(o.transpose(0, 3, 1, 2, 4).reshape(B, S, H, D))
        lses.append(lse.transpose(0, 3, 1, 2).reshape(B, S, H))
    w = jax.nn.softmax(jnp.stack(lses, axis=2), axis=2)
    return jnp.einsum('bsgh,bsghd->bshd', w, jnp.stack(outs, axis=2))


def token_mixer(h, cos, sin, w_in, w_fg, b_fg, g_gla_out, g_q, g_k, w_gla_proj, w_attn_proj, w_out):
    B, S, _ = h.shape
    f32 = jnp.float32
    proj = h @ w_in
    split_points = np.cumsum(IN_SPLITS)[:-1].tolist()
    q_l, k_l, v_l, r_l, fg_l, q_a, k_a, v_a, gate_gla, gate_att = jnp.split(proj, split_points, axis=-1)

    log_a = jax.nn.log_sigmoid((fg_l @ w_fg + b_fg).astype(f32)) / GLA_TAU

    def heads(t, d):
        return t.reshape(B, S, GLA_HEADS, d).astype(f32)

    o_gla = gla_chunked(heads(q_l, GLA_HEAD_K) * (GLA_HEAD_K ** -0.5), heads(k_l, GLA_HEAD_K),
                        heads(v_l, GLA_HEAD_V), log_a.reshape(B, S, GLA_HEADS, GLA_HEAD_K))
    o_gla = rms_norm(o_gla, g_gla_out.reshape(GLA_HEADS, GLA_HEAD_V)).reshape(B, S, GLA_DV)
    o_gla = o_gla.astype(h.dtype) * jax.nn.silu(r_l)

    att_shape = (B, S, N_DIL, ATT_HEADS_PER_GROUP, HEAD_DIM)
    qa = rope(rms_norm(q_a.reshape(att_shape), g_q).astype(f32), cos, sin)
    ka = rope(rms_norm(k_a.reshape(att_shape), g_k).astype(f32), cos, sin)
    va = v_a.reshape(att_shape).astype(f32)
    o_att = dilated_attention(qa, ka, va).reshape(B, S, ATT_OUT).astype(h.dtype)

    merged = jax.nn.sigmoid(gate_gla) * (o_gla @ w_gla_proj) + jax.nn.sigmoid(gate_att) * (o_att @ w_attn_proj)
    return merged @ w_out


def hierarchical_moe(h, w_rg, b_rg, w_re, b_re, w_gate, w_up, w_down):
    T = h.shape[0]
    f32 = jnp.float32
    g_logits = (h @ w_rg + b_rg).astype(f32)
    g_prob = jax.nn.softmax(g_logits, axis=-1)
    g_idx = jnp.argmax(g_logits, axis=-1)
    g_weight = jnp.take_along_axis(g_prob, g_idx[:, None], axis=-1)
    e_logits = (h @ w_re + b_re).astype(f32).reshape(T, N_GROUPS, EXPERTS_PER_GROUP)
    e_logits = jnp.take_along_axis(e_logits, g_idx[:, None, None], axis=1)[:, 0]
    top_val, top_idx = lax.top_k(e_logits, TOP_K_INNER)
    top_w = jax.nn.softmax(top_val, axis=-1) * g_weight
    inner = jnp.sum(jax.nn.one_hot(top_idx, EXPERTS_PER_GROUP, dtype=f32) * top_w[..., None], axis=1)
    gates = (jax.nn.one_hot(g_idx, N_GROUPS, dtype=f32)[:, :, None] * inner[:, None, :]).astype(h.dtype)
    y = jnp.zeros_like(h)
    for grp in range(N_GROUPS):
        sl = slice(grp * EXPERTS_PER_GROUP, (grp + 1) * EXPERTS_PER_GROUP)
        a = jnp.einsum('td,edf->tef', h, w_gate[sl])
        u = jnp.einsum('td,edf->tef', h, w_up[sl])
        hid = jax.nn.silu(a) * u * gates[:, grp, :, None]
        y = y + jnp.einsum('tef,efd->td', hid, w_down[sl])
    return y


def setup_inputs(seed: int = 0) -> dict:
    key = jax.random.key(seed)
    ks = jax.random.split(key, 24)
    D = D_MODEL
    L = DEPTH

    def nrm(k, shape, scale):
        return jax.random.normal(k, shape, jnp.float32) * scale

    x = nrm(ks[0], (BATCH, SEQ, D), 1.0)
    c = nrm(ks[1], (BATCH, D), 1.0)
    offsets = jax.random.randint(ks[2], (BATCH, 1), 0, 1024, dtype=jnp.int32)
    positions = offsets + jnp.arange(SEQ, dtype=jnp.int32)[None, :]
    return {
        "x": x,
        "c": c,
        "positions": positions,
        "w_ada": nrm(ks[3], (L, D, N_MOD * D), 0.5 * D ** -0.5),
        "b_ada": nrm(ks[4], (L, N_MOD * D), 0.02),
        "g_norm_mix": 1.0 + nrm(ks[5], (L, D), 0.1),
        "g_norm_ffn": 1.0 + nrm(ks[6], (L, D), 0.1),
        "w_in": nrm(ks[7], (L, D, D_IN), D ** -0.5),
        "w_fg": nrm(ks[8], (L, GLA_RANK, GLA_DK), GLA_RANK ** -0.5),
        "b_fg": nrm(ks[9], (L, GLA_DK), 0.1),
        "g_gla_out": 1.0 + nrm(ks[10], (L, GLA_DV), 0.1),
        "g_q": 1.0 + nrm(ks[11], (L, HEAD_DIM), 0.1),
        "g_k": 1.0 + nrm(ks[12], (L, HEAD_DIM), 0.1),
        "w_gla_proj": nrm(ks[13], (L, GLA_DV, D), GLA_DV ** -0.5),
        "w_attn_proj": nrm(ks[14], (L, ATT_OUT, D), ATT_OUT ** -0.5),
        "w_out": nrm(ks[15], (L, D, D), D ** -0.5),
        "w_route_group": nrm(ks[16], (L, D, N_GROUPS), D ** -0.5),
        "b_route_group": nrm(ks[17], (L, N_GROUPS), 0.01),
        "w_route_expert": nrm(ks[18], (L, D, N_EXPERTS), D ** -0.5),
        "b_route_expert": nrm(ks[19], (L, N_EXPERTS), 0.01),
        "w_exp_gate": nrm(ks[20], (L, N_EXPERTS, D, D_EXPERT), D ** -0.5),
        "w_exp_up": nrm(ks[21], (L, N_EXPERTS, D, D_EXPERT), D ** -0.5),
        "w_exp_down": nrm(ks[22], (L, N_EXPERTS, D_EXPERT, D), D_EXPERT ** -0.5),
    }


def reference(x, c, positions, w_ada, b_ada, g_norm_mix, g_norm_ffn, w_in, w_fg, b_fg, g_gla_out,
              g_q, g_k, w_gla_proj, w_attn_proj, w_out, w_route_group, b_route_group,
              w_route_expert, b_route_expert, w_exp_gate, w_exp_up, w_exp_down):
    B, S, D = x.shape
    f32 = jnp.float32
    inv_freq = ROPE_THETA ** (-jnp.arange(0, HEAD_DIM, 2, dtype=f32) / HEAD_DIM)
    ang = positions.astype(f32)[..., None] * inv_freq
    cos = jnp.cos(ang)[:, :, None, None, :]
    sin = jnp.sin(ang)[:, :, None, None, :]
    c_act = jax.nn.silu(c)
    for l in range(DEPTH):
        mod = (c_act @ w_ada[l] + b_ada[l])[:, None, :]
        sh_m, sc_m, gt_m, sh_f, sc_f, gt_f = jnp.split(mod, N_MOD, axis=-1)
        h = rms_norm(x, g_norm_mix[l]) * (1 + sc_m) + sh_m
        x = x + gt_m * token_mixer(h, cos, sin, w_in[l], w_fg[l], b_fg[l], g_gla_out[l], g_q[l], g_k[l],
                                   w_gla_proj[l], w_attn_proj[l], w_out[l])
        h = rms_norm(x, g_norm_ffn[l]) * (1 + sc_f) + sh_f
        y = hierarchical_moe(h.reshape(B * S, D), w_route_group[l], b_route_group[l], w_route_expert[l],
                             b_route_expert[l], w_exp_gate[l], w_exp_up[l], w_exp_down[l])
        x = x + gt_f * y.reshape(B, S, D)
    return x
```

```python
import functools

import jax
import jax.numpy as jnp
from jax import lax
from jax.experimental import pallas as pl
from jax.experimental.pallas import tpu as pltpu

F32 = jnp.float32
BF16 = jnp.bfloat16

GLA_HEADS = 4
GLA_RANK = 16
GLA_TAU = 16.0
GLA_CHUNK = 64
DIL_GROUPS = ((128, 1), (512, 4), (2048, 16))
N_DIL = 3
ATT_HEADS = 4
HEAD_DIM = 128
ATT_BLOCK = 128
ROPE_THETA = 10000.0
N_GROUPS = 4
EXPERTS_PER_GROUP = 8
N_EXPERTS = N_GROUPS * EXPERTS_PER_GROUP
N_MOD = 6
EPS = 1e-6

LANES = 128
VMEM_LIMIT = 56 * 1024 * 1024
NEG_BIG = -1e30

NORM_ROWS = 512
MM_TM = 1024
MM_TN = 512
GLA_ROWS = 256
PREP_ROWS = 512
ATT_SPAN = 2048
MERGE_ROWS = 512
EXPERT_TM = 256
GATHER_ROWS = 256


def _params(*sem):
    return pltpu.CompilerParams(dimension_semantics=sem, vmem_limit_bytes=VMEM_LIMIT)


def _dot(a, b):
    return jnp.dot(a, b, preferred_element_type=F32)


def _dot_nt(a, b):
    return lax.dot_general(a, b, (((1,), (1,)), ((), ())), preferred_element_type=F32)


def _dot_tn(a, b):
    return lax.dot_general(a, b, (((0,), (0,)), ((), ())), preferred_element_type=F32)


def _split_bf16(x):
    hi = x.astype(BF16)
    lo = (x - hi.astype(F32)).astype(BF16)
    return hi, lo


def _sigmoid(x):
    return 1.0 / (1.0 + jnp.exp(-x))


def _grid_call(kernel, l, grid, in_specs, out_specs, out_shape, scratch=(), sem=None, name=None):
    sem = sem or ("arbitrary",) * len(grid)
    return pl.pallas_call(
        kernel,
        grid_spec=pltpu.PrefetchScalarGridSpec(
            num_scalar_prefetch=1, grid=grid, in_specs=in_specs, out_specs=out_specs,
            scratch_shapes=list(scratch)),
        out_shape=out_shape,
        compiler_params=_params(*sem),
        name=name,
    ), jnp.reshape(l, (1,)).astype(jnp.int32)


def _mod_kernel(c_ref, w_ref, b_ref, o_ref):
    c = c_ref[...]
    a = (c * _sigmoid(c)).astype(BF16)
    o_ref[...] = _dot(a, w_ref[...].astype(BF16)) + b_ref[...]


def _modulation(c, w_ada, b_ada):
    L, D, N = w_ada.shape
    B = c.shape[0]
    rows = 8
    c_pad = jnp.zeros((rows, D), F32).at[:B].set(c)
    tn = 1024
    return pl.pallas_call(
        _mod_kernel,
        grid=(L, N // tn),
        in_specs=[pl.BlockSpec((rows, D), lambda l, j: (0, 0)),
                  pl.BlockSpec((None, D, tn), lambda l, j: (l, 0, j)),
                  pl.BlockSpec((None, 1, tn), lambda l, j: (l, 0, j))],
        out_specs=pl.BlockSpec((None, rows, tn), lambda l, j: (l, 0, j)),
        out_shape=jax.ShapeDtypeStruct((L, rows, N), F32),
        compiler_params=_params("parallel", "parallel"),
        name="adaln_mod",
    )(c_pad, w_ada, b_ada.reshape(L, 1, N))


def _rope_kernel(pos_ref, freq_ref, cos_ref, sin_ref):
    ang = pos_ref[...].astype(F32) * freq_ref[...]
    lane = lax.broadcasted_iota(jnp.int32, ang.shape, 1)
    cos_ref[...] = jnp.cos(ang)
    sin_ref[...] = jnp.where(lane < HEAD_DIM // 2, -jnp.sin(ang), jnp.sin(ang))


def _rope_tables(positions):
    T = positions.size
    inv_freq = ROPE_THETA ** (-jnp.arange(0, HEAD_DIM, 2, dtype=F32) / HEAD_DIM)
    freq2 = jnp.concatenate([inv_freq, inv_freq]).reshape(1, HEAD_DIM)
    rows = 2048
    return pl.pallas_call(
        _rope_kernel,
        grid=(T // rows,),
        in_specs=[pl.BlockSpec((rows, 1), lambda i: (i, 0)),
                  pl.BlockSpec((1, HEAD_DIM), lambda i: (0, 0))],
        out_specs=[pl.BlockSpec((rows, HEAD_DIM), lambda i: (i, 0))] * 2,
        out_shape=[jax.ShapeDtypeStruct((T, HEAD_DIM), F32)] * 2,
        compiler_params=_params("parallel"),
        name="rope_tables",
    )(positions.reshape(T, 1), freq2)


def _ada_norm(x, g, scale, shift):
    y = x * lax.rsqrt(jnp.mean(x * x, axis=-1, keepdims=True) + EPS)
    return y * g * (1.0 + scale) + shift


def _norm_kernel(l_ref, x_ref, g_ref, sh_ref, sc_ref, o_ref):
    o_ref[...] = _ada_norm(x_ref[...], g_ref[...], sc_ref[...], sh_ref[...]).astype(o_ref.dtype)


def _mod_spec(which, rows, seq):
    return lambda i, l: (i * rows // seq, 0, which)


def _norm(x, l, g_all, mod3, which_shift, seq):
    T, D = x.shape
    rows = NORM_ROWS
    call, lidx = _grid_call(
        _norm_kernel, l, (T // rows,),
        [pl.BlockSpec((rows, D), lambda i, l: (i, 0)),
         pl.BlockSpec((None, 1, D), lambda i, l: (l[0], 0, 0)),
         pl.BlockSpec((None, 1, D), _mod_spec(which_shift, rows, seq)),
         pl.BlockSpec((None, 1, D), _mod_spec(which_shift + 1, rows, seq))],
        pl.BlockSpec((rows, D), lambda i, l: (i, 0)),
        jax.ShapeDtypeStruct((T, D), BF16), sem=("parallel",), name="ada_norm")
    return call(lidx, x, g_all, mod3, mod3)


def _mm_kernel(l_ref, a_ref, w_ref, *rest, epilogue, n_extra):
    extra = rest[:n_extra]
    o_ref = rest[n_extra]
    wbf_ref = rest[n_extra + 1]

    @pl.when(pl.program_id(1) == 0)
    def _():
        wbf_ref[...] = w_ref[...].astype(BF16)

    acc = _dot(a_ref[...], wbf_ref[...])
    if epilogue is not None:
        acc = epilogue(acc, *[e[...] for e in extra])
    o_ref[...] = acc.astype(o_ref.dtype)


def _matmul(a, w, l, *, col0, n, out_dtype, tn=MM_TN, tm=MM_TM, epilogue=None, extras=(),
            extra_specs=(), name="matmul"):
    M, K = a.shape
    assert col0 % tn == 0 and n % tn == 0 and M % tm == 0
    cb0 = col0 // tn
    kern = functools.partial(_mm_kernel, epilogue=epilogue, n_extra=len(extras))
    call, lidx = _grid_call(
        kern, l, (n // tn, M // tm),
        [pl.BlockSpec((tm, K), lambda j, i, l: (i, 0)),
         pl.BlockSpec((None, K, tn), lambda j, i, l: (l[0], 0, cb0 + j))] + list(extra_specs),
        pl.BlockSpec((tm, tn), lambda j, i, l: (i, j)),
        jax.ShapeDtypeStruct((M, n), out_dtype),
        scratch=[pltpu.VMEM((K, tn), BF16)], name=name)
    return call(lidx, a, w, *extras)


def _gla_kernel(l_ref, q_ref, k_ref, v_ref, r_ref, fg_ref, wfg_ref, bfg_ref, g_ref, o_ref,
                state_ref, *, head_k, head_v):
    rows = q_ref.shape[0]
    C = GLA_CHUNK

    @pl.when(pl.program_id(1) == 0)
    def _():
        state_ref[...] = jnp.zeros_like(state_ref)

    fh, fl = _split_bf16(fg_ref[...])
    wh, wl = _split_bf16(wfg_ref[...])
    z = _dot(fh, wh) + _dot(fl, wh) + _dot(fh, wl) + bfg_ref[...]
    log_a = (jnp.minimum(z, 0.0) - jnp.log(1.0 + jnp.exp(-jnp.abs(z)))) * (1.0 / GLA_TAU)

    ri = lax.broadcasted_iota(jnp.int32, (C, C), 0)
    ci = lax.broadcasted_iota(jnp.int32, (C, C), 1)
    causal = ri >= ci
    tri = causal.astype(BF16)
    scale = head_k ** -0.5

    for c in range(rows // C):
        rs = slice(c * C, (c + 1) * C)
        lh, ll = _split_bf16(log_a[rs])
        b = _dot(tri, lh) + _dot(tri, ll)
        b_last = b[C - 1:C]
        q = q_ref[rs, :].astype(F32) * scale
        k = k_ref[rs, :].astype(F32)
        q_dec = (q * jnp.exp(b)).astype(BF16)
        k_inv = (k * jnp.exp(-b)).astype(BF16)
        k_upd = (k * jnp.exp(b_last - b)).astype(BF16)
        decay = jnp.exp(b_last)
        for h in range(GLA_HEADS):
            ks = slice(h * head_k, (h + 1) * head_k)
            vs = slice(h * head_v, (h + 1) * head_v)
            v = v_ref[rs, vs]
            att = jnp.where(causal, _dot_nt(q_dec[:, ks], k_inv[:, ks]), 0.0).astype(BF16)
            st = state_ref[h]
            o = _dot(att, v) + _dot_nt(q_dec[:, ks], st.astype(BF16))
            state_ref[h] = st * decay[:, ks] + _dot_tn(v, k_upd[:, ks])
            o = o * lax.rsqrt(jnp.mean(o * o, axis=-1, keepdims=True) + EPS) * g_ref[:, vs]
            r = r_ref[rs, vs].astype(F32)
            o_ref[rs, vs] = (o * (r * _sigmoid(r))).astype(o_ref.dtype)


def _gla(proj, fg, l, wfg_pad, b_fg, g_gla_out, batch, seq):
    T = proj.shape[0]
    dk_all = wfg_pad.shape[-1]
    dv_all = g_gla_out.shape[-1]
    rows = GLA_ROWS
    nblk = seq // rows
    kern = functools.partial(_gla_kernel, head_k=dk_all // GLA_HEADS, head_v=dv_all // GLA_HEADS)
    row = lambda b, n, l: b * nblk + n
    call, lidx = _grid_call(
        kern, l, (batch, nblk),
        [pl.BlockSpec((rows, dk_all), lambda b, n, l: (row(b, n, l), 0)),
         pl.BlockSpec((rows, dk_all), lambda b, n, l: (row(b, n, l), 1)),
         pl.BlockSpec((rows, dv_all), lambda b, n, l: (row(b, n, l), 1)),
         pl.BlockSpec((rows, dv_all), lambda b, n, l: (row(b, n, l), 2)),
         pl.BlockSpec((rows, LANES), lambda b, n, l: (row(b, n, l), 0)),
         pl.BlockSpec((None, LANES, dk_all), lambda b, n, l: (l[0], 0, 0)),
         pl.BlockSpec((None, 1, dk_all), lambda b, n, l: (l[0], 0, 0)),
         pl.BlockSpec((None, 1, dv_all), lambda b, n, l: (l[0], 0, 0))],
        pl.BlockSpec((rows, dv_all), lambda b, n, l: (row(b, n, l), 0)),
        jax.ShapeDtypeStruct((T, dv_all), BF16),
        scratch=[pltpu.VMEM((GLA_HEADS, dv_all // GLA_HEADS, dk_all // GLA_HEADS), F32)],
        sem=("parallel", "arbitrary"), name="gla")
    return call(lidx, proj, proj, proj, proj, fg, wfg_pad, b_fg, g_gla_out)


def _qkprep_kernel(l_ref, q_ref, k_ref, v_ref, cos_ref, sin_ref, gq_ref, gk_ref, *out_refs):
    cos = cos_ref[...]
    sin = sin_ref[...]
    gw = ATT_HEADS * HEAD_DIM

    def prep(ref, g, scale):
        x = ref.astype(F32)
        y = x * lax.rsqrt(jnp.mean(x * x, axis=-1, keepdims=True) + EPS) * g
        y = y * cos + pltpu.roll(y, HEAD_DIM // 2, 1) * sin
        return y * scale

    for grp in range(N_DIL):
        for h in range(ATT_HEADS):
            src = slice(grp * gw + h * HEAD_DIM, grp * gw + (h + 1) * HEAD_DIM)
            dst = slice(h * HEAD_DIM, (h + 1) * HEAD_DIM)
            out_refs[grp][:, dst] = prep(q_ref[:, src], gq_ref[...], HEAD_DIM ** -0.5).astype(BF16)
            out_refs[N_DIL + grp][:, dst] = prep(k_ref[:, src], gk_ref[...], 1.0).astype(BF16)
        out_refs[2 * N_DIL + grp][...] = v_ref[:, grp * gw:(grp + 1) * gw]


def _qkprep(proj2, l, cos2, sin2, g_q, g_k):
    T = proj2.shape[0]
    rows = PREP_ROWS
    width = N_DIL * ATT_HEADS * HEAD_DIM
    gw = ATT_HEADS * HEAD_DIM
    call, lidx = _grid_call(
        _qkprep_kernel, l, (T // rows,),
        [pl.BlockSpec((rows, width), lambda i, l: (i, 0)),
         pl.BlockSpec((rows, width), lambda i, l: (i, 1)),
         pl.BlockSpec((rows, width), lambda i, l: (i, 2)),
         pl.BlockSpec((rows, HEAD_DIM), lambda i, l: (i, 0)),
         pl.BlockSpec((rows, HEAD_DIM), lambda i, l: (i, 0)),
         pl.BlockSpec((None, 1, HEAD_DIM), lambda i, l: (l[0], 0, 0)),
         pl.BlockSpec((None, 1, HEAD_DIM), lambda i, l: (l[0], 0, 0))],
        [pl.BlockSpec((rows, gw), lambda i, l: (i, 0))] * (3 * N_DIL),
        [jax.ShapeDtypeStruct((T, gw), BF16)] * (3 * N_DIL),
        sem=("parallel",), name="qk_prep")
    return call(lidx, proj2, proj2, proj2, cos2, sin2, g_q, g_k)


def _attn_kernel(q_ref, kc_ref, kp_ref, vc_ref, vp_ref, o_ref, lse_ref, *, dil):
    first = pl.program_id(1) == 0
    QB = q_ref.shape[0]
    A = ATT_BLOCK
    qi = lax.broadcasted_iota(jnp.int32, (A, 2 * A), 0)
    kj = lax.broadcasted_iota(jnp.int32, (A, 2 * A), 1) - A
    dist = qi - kj
    valid = (dist >= 0) & (dist <= A)
    bias = jnp.where(valid, 0.0, NEG_BIG).astype(F32)
    bias_first = jnp.where(first, jnp.where(valid & (kj >= 0), 0.0, NEG_BIG), bias).astype(F32)

    for p in range(dil):
        for h in range(ATT_HEADS):
            cs = slice((p * ATT_HEADS + h) * HEAD_DIM, (p * ATT_HEADS + h + 1) * HEAD_DIM)
            for j in range(QB // A):
                rs = slice(j * A, (j + 1) * A)
                if j == 0:
                    kcat = jnp.concatenate([kp_ref[:, cs], kc_ref[0:A, cs]], axis=0)
                    vcat = jnp.concatenate([vp_ref[:, cs], vc_ref[0:A, cs]], axis=0)
                    bb = bias_first
                else:
                    kcat = kc_ref[(j - 1) * A:(j + 1) * A, cs]
                    vcat = vc_ref[(j - 1) * A:(j + 1) * A, cs]
                    bb = bias
                s = _dot_nt(q_ref[rs, cs], kcat) + bb
                m = jnp.max(s, axis=-1, keepdims=True)
                e = jnp.exp(s - m)
                den = jnp.sum(e, axis=-1, keepdims=True)
                o = _dot(e.astype(BF16), vcat) / den
                o_ref[rs, cs] = o.astype(o_ref.dtype)
                lse_ref[rs, cs] = jnp.broadcast_to(m + jnp.log(den), (A, HEAD_DIM))


def _attn_group(q, k, v, dil, batch, seq):
    T, gw = q.shape
    A = ATT_BLOCK
    QB = ATT_SPAN // dil
    W = dil * gw
    view = lambda t: t.reshape(batch, seq // dil, W)
    cur = pl.BlockSpec((None, QB, W), lambda b, n: (b, n, 0))
    prev = pl.BlockSpec((None, A, W), lambda b, n: (b, jnp.maximum(n * (QB // A) - 1, 0), 0))
    o, lse = pl.pallas_call(
        functools.partial(_attn_kernel, dil=dil),
        grid=(batch, seq // ATT_SPAN),
        in_specs=[cur, cur, prev, cur, prev],
        out_specs=[cur, cur],
        out_shape=[jax.ShapeDtypeStruct((batch, seq // dil, W), BF16),
                   jax.ShapeDtypeStruct((batch, seq // dil, W), F32)],
        compiler_params=_params("parallel", "arbitrary"),
        name=f"dilated_attn_r{dil}",
    )(view(q), view(k), view(k), view(v), view(v))
    return o.reshape(T, gw), lse.reshape(T, gw)


def _merge_kernel(l_ref, ogla_ref, o0, o1, o2, l0, l1, l2, gg_ref, ga_ref, wg_ref, wa_ref,
                  out_ref, wgbf_ref, wabf_ref):
    @pl.when(pl.program_id(1) == 0)
    def _():
        wgbf_ref[...] = wg_ref[...].astype(BF16)
        wabf_ref[...] = wa_ref[...].astype(BF16)

    lses = [l0[...], l1[...], l2[...]]
    m = jnp.maximum(jnp.maximum(lses[0], lses[1]), lses[2])
    ws = [jnp.exp(x - m) for x in lses]
    den = ws[0] + ws[1] + ws[2]
    o_att = (ws[0] * o0[...].astype(F32) + ws[1] * o1[...].astype(F32)
             + ws[2] * o2[...].astype(F32)) / den
    a = _dot(ogla_ref[...], wgbf_ref[...])
    b = _dot(o_att.astype(BF16), wabf_ref[...])
    gg = _sigmoid(gg_ref[...].astype(F32))
    ga = _sigmoid(ga_ref[...].astype(F32))
    out_ref[...] = (gg * a + ga * b).astype(out_ref.dtype)


def _merge(o_gla, outs, lses, proj2, l, w_gla_proj, w_attn_proj, gate_col0):
    T, dv = o_gla.shape
    gw = outs[0].shape[1]
    D = w_gla_proj.shape[-1]
    tm, tn = MERGE_ROWS, MM_TN
    gcb = gate_col0 // tn
    dcb = D // tn
    grp = pl.BlockSpec((tm, gw), lambda j, i, l: (i, 0))
    call, lidx = _grid_call(
        _merge_kernel, l, (D // tn, T // tm),
        [pl.BlockSpec((tm, dv), lambda j, i, l: (i, 0))] + [grp] * 6 +
        [pl.BlockSpec((tm, tn), lambda j, i, l: (i, gcb + j)),
         pl.BlockSpec((tm, tn), lambda j, i, l: (i, gcb + dcb + j)),
         pl.BlockSpec((None, dv, tn), lambda j, i, l: (l[0], 0, j)),
         pl.BlockSpec((None, gw, tn), lambda j, i, l: (l[0], 0, j))],
        pl.BlockSpec((tm, tn), lambda j, i, l: (i, j)),
        jax.ShapeDtypeStruct((T, D), BF16),
        scratch=[pltpu.VMEM((dv, tn), BF16), pltpu.VMEM((gw, tn), BF16)], name="gated_merge")
    return call(lidx, o_gla, *outs, *lses, proj2, proj2, w_gla_proj, w_attn_proj)


def _router_kernel(l_ref, x_ref, g_ref, sh_ref, sc_ref, wr_ref, br_ref, h_ref, route_ref):
    h = _ada_norm(x_ref[...], g_ref[...], sc_ref[...], sh_ref[...])
    h_ref[...] = h
    lg = _dot(h.astype(BF16), wr_ref[...].astype(BF16)) + br_ref[...]
    lane = lax.broadcasted_iota(jnp.int32, lg.shape, 1).astype(F32)
    big = float(4 * LANES)

    def first_argmax(vals):
        mx = jnp.max(vals, axis=-1, keepdims=True)
        idx = jnp.min(jnp.where(vals == mx, lane, big), axis=-1, keepdims=True)
        return mx, idx

    gl = jnp.where(lane < N_GROUPS, lg, NEG_BIG)
    gmax, gidx = first_argmax(gl)
    g_weight = 1.0 / jnp.sum(jnp.exp(gl - gmax), axis=-1, keepdims=True)
    lo = N_GROUPS + gidx * EXPERTS_PER_GROUP
    el = jnp.where((lane >= lo) & (lane < lo + EXPERTS_PER_GROUP), lg, NEG_BIG)
    m1, i1 = first_argmax(el)
    m2, i2 = first_argmax(jnp.where(lane == i1, NEG_BIG, el))
    e2 = jnp.exp(m2 - m1)
    w1 = g_weight / (1.0 + e2)
    w2 = g_weight * e2 / (1.0 + e2)
    route = jnp.where(lane == 0, i1 - N_GROUPS,
                      jnp.where(lane == 1, i2 - N_GROUPS,
                                jnp.where(lane == 2, w1, jnp.where(lane == 3, w2, 0.0))))
    route_ref[...] = route


def _router(x, l, g_all, mod3, wr_pad, br_pad, seq):
    T, D = x.shape
    rows = NORM_ROWS
    call, lidx = _grid_call(
        _router_kernel, l, (T // rows,),
        [pl.BlockSpec((rows, D), lambda i, l: (i, 0)),
         pl.BlockSpec((None, 1, D), lambda i, l: (l[0], 0, 0)),
         pl.BlockSpec((None, 1, D), _mod_spec(3, rows, seq)),
         pl.BlockSpec((None, 1, D), _mod_spec(4, rows, seq)),
         pl.BlockSpec((None, D, LANES), lambda i, l: (l[0], 0, 0)),
         pl.BlockSpec((None, 1, LANES), lambda i, l: (l[0], 0, 0))],
        [pl.BlockSpec((rows, D), lambda i, l: (i, 0)),
         pl.BlockSpec((rows, LANES), lambda i, l: (i, 0))],
        [jax.ShapeDtypeStruct((T, D), F32), jax.ShapeDtypeStruct((T, LANES), F32)],
        sem=("parallel",), name="norm_router")
    return call(lidx, x, g_all, mod3, mod3, wr_pad, br_pad)


def _gather_kernel(idx_ref, src_ref, out_ref, sem_ref):
    rows = out_ref.shape[0]

    def row_copy(r):
        return pltpu.make_async_copy(src_ref.at[pl.ds(idx_ref[0, 0, r], 1)],
                                     out_ref.at[pl.ds(r, 1)], sem_ref)

    def start(r, c):
        row_copy(r).start()
        return c

    def wait(r, c):
        row_copy(r).wait()
        return c

    lax.fori_loop(0, rows, start, 0)
    lax.fori_loop(0, rows, wait, 0)


def _gather_rows(src, idx):
    n, = idx.shape
    D = src.shape[1]
    rows = GATHER_ROWS
    return pl.pallas_call(
        _gather_kernel,
        grid=(n // rows,),
        in_specs=[pl.BlockSpec((1, 1, rows), lambda i: (i, 0, 0), memory_space=pltpu.SMEM),
                  pl.BlockSpec(memory_space=pl.ANY)],
        out_specs=pl.BlockSpec((rows, D), lambda i: (i, 0)),
        out_shape=jax.ShapeDtypeStruct((n, D), src.dtype),
        scratch_shapes=[pltpu.SemaphoreType.DMA(())],
        compiler_params=_params("arbitrary"),
        name="gather_rows",
    )(idx.reshape(n // rows, 1, rows), src)


def _expert_kernel(meta_ref, xs_ref, gate_ref, wg_ref, wu_ref, wd_ref, ys_ref,
                   wgbf_ref, wubf_ref, wdbf_ref):
    i = pl.program_id(0)
    n_used = meta_ref[1]

    @pl.when(i < n_used)
    def _():
        prev = meta_ref[3 + jnp.maximum(i - 1, 0)]

        @pl.when((i == 0) | (meta_ref[3 + i] != prev))
        def _():
            wgbf_ref[...] = wg_ref[...].astype(BF16)
            wubf_ref[...] = wu_ref[...].astype(BF16)
            wdbf_ref[...] = wd_ref[...].astype(BF16)

        x = xs_ref[...].astype(BF16)
        a = _dot(x, wgbf_ref[...])
        u = _dot(x, wubf_ref[...])
        hid = (a * _sigmoid(a)) * u * gate_ref[...]
        ys_ref[...] = _dot(hid.astype(BF16), wdbf_ref[...])

    @pl.when(i >= n_used)
    def _():
        ys_ref[...] = jnp.zeros_like(ys_ref)


def _experts(xs, gate_s, meta, w_gate, w_up, w_down):
    P, D = xs.shape
    F = w_gate.shape[-1]
    tm = EXPERT_TM
    n_tiles = P // tm
    wspec = lambda shape: pl.BlockSpec((None, None) + shape, lambda i, m: (m[0], m[3 + i], 0, 0))
    return pl.pallas_call(
        _expert_kernel,
        grid_spec=pltpu.PrefetchScalarGridSpec(
            num_scalar_prefetch=1, grid=(n_tiles,),
            in_specs=[pl.BlockSpec((tm, D), lambda i, m: (i, 0)),
                      pl.BlockSpec((tm, 1), lambda i, m: (i, 0)),
                      wspec((D, F)), wspec((D, F)), wspec((F, D))],
            out_specs=pl.BlockSpec((tm, D), lambda i, m: (i, 0)),
            scratch_shapes=[pltpu.VMEM((D, F), BF16), pltpu.VMEM((D, F), BF16),
                            pltpu.VMEM((F, D), BF16)]),
        out_shape=jax.ShapeDtypeStruct((P, D), F32),
        compiler_params=_params("arbitrary"),
        name="grouped_experts",
    )(meta, xs, gate_s, w_gate, w_up, w_down)


def _combine_kernel(pos_ref, x_ref, gt_ref, ys_ref, o_ref, buf_ref, sem_ref):
    rows = x_ref.shape[0]

    def row_copy(r):
        return pltpu.make_async_copy(ys_ref.at[pl.ds(pos_ref[0, 0, r], 1)],
                                     buf_ref.at[pl.ds(r, 1)], sem_ref)

    def start(r, c):
        row_copy(r).start()
        return c

    def wait(r, c):
        row_copy(r).wait()
        return c

    lax.fori_loop(0, 2 * rows, start, 0)
    lax.fori_loop(0, 2 * rows, wait, 0)
    y = buf_ref[0:rows, :] + buf_ref[rows:2 * rows, :]
    o_ref[...] = x_ref[...] + gt_ref[...] * y


def _combine(x, ys, pos, mod3, seq):
    T, D = x.shape
    rows = GATHER_ROWS
    return pl.pallas_call(
        _combine_kernel,
        grid=(T // rows,),
        in_specs=[pl.BlockSpec((1, 1, 2 * rows), lambda i: (i, 0, 0), memory_space=pltpu.SMEM),
                  pl.BlockSpec((rows, D), lambda i: (i, 0)),
                  pl.BlockSpec((None, 1, D), lambda i: (i * rows // seq, 0, 5)),
                  pl.BlockSpec(memory_space=pl.ANY)],
        out_specs=pl.BlockSpec((rows, D), lambda i: (i, 0)),
        out_shape=jax.ShapeDtypeStruct((T, D), F32),
        scratch_shapes=[pltpu.VMEM((2 * rows, D), F32), pltpu.SemaphoreType.DMA(())],
        compiler_params=_params("arbitrary"),
        name="moe_combine",
    )(pos, x, mod3, ys)


def _dispatch_plan(route, l):
    T = route.shape[0]
    tm = EXPERT_TM
    n_tiles = (2 * T) // tm + N_EXPERTS
    P = n_tiles * tm
    e = route[:, 0:2].astype(jnp.int32).T.reshape(-1)
    w = route[:, 2:4].T.reshape(-1)
    tok = jnp.tile(jnp.arange(T, dtype=jnp.int32), 2)
    onehot = (e[:, None] == jnp.arange(N_EXPERTS, dtype=jnp.int32)[None, :]).astype(jnp.int32)
    csum = jnp.cumsum(onehot, axis=0)
    rank = jnp.sum((csum - onehot) * onehot, axis=1)
    counts = csum[-1]
    tiles_per = (counts + tm - 1) // tm
    tile_end = jnp.cumsum(tiles_per)
    off = (tile_end - tiles_per) * tm
    pos = off[e] + rank
    src = jnp.zeros((P,), jnp.int32).at[pos].set(tok)
    gate_s = jnp.zeros((P,), F32).at[pos].set(w)
    tile_ids = jnp.arange(n_tiles, dtype=jnp.int32)
    tile_expert = jnp.minimum(jnp.sum(tile_ids[:, None] >= tile_end[None, :], axis=1),
                              N_EXPERTS - 1).astype(jnp.int32)
    meta = jnp.concatenate([jnp.stack([l, tile_end[-1].astype(jnp.int32),
                                       jnp.int32(0)]), tile_expert])
    return src, gate_s.reshape(P, 1), meta, pos


def kernel(x, c, positions, w_ada, b_ada, g_norm_mix, g_norm_ffn, w_in, w_fg, b_fg, g_gla_out,
           g_q, g_k, w_gla_proj, w_attn_proj, w_out, w_route_group, b_route_group,
           w_route_expert, b_route_expert, w_exp_gate, w_exp_up, w_exp_down):
    B, S, D = x.shape
    L = w_ada.shape[0]
    T = B * S
    dk_all = w_fg.shape[-1]
    dv_all = g_gla_out.shape[-1]
    att_w = N_DIL * ATT_HEADS * HEAD_DIM
    gla_cols = 2 * dk_all + 2 * dv_all
    rest0 = gla_cols + GLA_RANK

    mod = _modulation(c, w_ada, b_ada)
    cos2, sin2 = _rope_tables(positions)

    w_rest = w_in[:, :, rest0:].astype(BF16)
    wfg_pad = jnp.zeros((L, LANES, dk_all), F32).at[:, :GLA_RANK].set(w_fg)
    wr_pad = jnp.zeros((L, D, LANES), F32)
    wr_pad = wr_pad.at[:, :, :N_GROUPS].set(w_route_group)
    wr_pad = wr_pad.at[:, :, N_GROUPS:N_GROUPS + N_EXPERTS].set(w_route_expert)
    br_pad = jnp.zeros((L, 1, LANES), F32)
    br_pad = br_pad.at[:, 0, :N_GROUPS].set(b_route_group)
    br_pad = br_pad.at[:, 0, N_GROUPS:N_GROUPS + N_EXPERTS].set(b_route_expert)
    g_mix = g_norm_mix.reshape(L, 1, D)
    g_ffn = g_norm_ffn.reshape(L, 1, D)
    b_fg3 = b_fg.reshape(L, 1, dk_all)
    g_gla3 = g_gla_out.reshape(L, 1, dv_all)
    g_q3 = g_q.reshape(L, 1, HEAD_DIM)
    g_k3 = g_k.reshape(L, 1, HEAD_DIM)

    def residual(acc, xres, gt):
        return xres + gt * acc

    def layer(l, xt):
        l = jnp.asarray(l, jnp.int32)
        mod3 = lax.dynamic_index_in_dim(mod, l, 0, keepdims=False).reshape(8, 1, N_MOD * D)
        h = _norm(xt, l, g_mix, mod3, 0, S)
        proj1 = _matmul(h, w_in, l, col0=0, n=gla_cols, out_dtype=BF16, name="proj_gla")
        fg = _matmul(h, w_in, l, col0=gla_cols, n=LANES, tn=LANES, out_dtype=F32, name="proj_fg")
        proj2 = _matmul(h, w_rest, l, col0=0, n=w_rest.shape[-1], out_dtype=BF16, name="proj_att")
        o_gla = _gla(proj1, fg, l, wfg_pad, b_fg3, g_gla3, B, S)
        prep = _qkprep(proj2, l, cos2, sin2, g_q3, g_k3)
        outs, lses = [], []
        for grp, (_, dil) in enumerate(DIL_GROUPS):
            o, lse = _attn_group(prep[grp], prep[N_DIL + grp], prep[2 * N_DIL + grp], dil, B, S)
            outs.append(o)
            lses.append(lse)
        merged = _merge(o_gla, outs, lses, proj2, l, w_gla_proj, w_attn_proj, 3 * att_w)
        gt_spec = pl.BlockSpec((None, 1, MM_TN),
                               lambda j, i, l: (i * MM_TM // S, 0, 2 * (D // MM_TN) + j))
        xt = _matmul(merged, w_out, l, col0=0, n=D, out_dtype=F32, epilogue=residual,
                     extras=(xt, mod3),
                     extra_specs=(pl.BlockSpec((MM_TM, MM_TN), lambda j, i, l: (i, j)), gt_spec),
                     name="out_proj")
        h2, route = _router(xt, l, g_ffn, mod3, wr_pad, br_pad, S)
        src, gate_s, meta, pos = _dispatch_plan(route, l)
        xs = _gather_rows(h2, src)
        ys = _experts(xs, gate_s, meta, w_exp_gate, w_exp_up, w_exp_down)
        rows = GATHER_ROWS
        pos_blk = pos.reshape(2, T // rows, rows).transpose(1, 0, 2).reshape(T // rows, 1, 2 * rows)
        return _combine(xt, ys, pos_blk, mod3, S)

    xt = lax.fori_loop(0, L, layer, x.reshape(T, D))
    return xt.reshape(B, S, D)
```

```python
import functools

import jax
import jax.numpy as jnp
from jax import lax
from jax.experimental import pallas as pl
from jax.experimental.pallas import tpu as pltpu

F32 = jnp.float32
BF16 = jnp.bfloat16

GLA_HEADS = 4
GLA_RANK = 16
GLA_TAU = 16.0
GLA_CHUNK = 64
DIL_GROUPS = ((128, 1), (512, 4), (2048, 16))
N_DIL = 3
ATT_HEADS = 4
HEAD_DIM = 128
ATT_BLOCK = 128
ROPE_THETA = 10000.0
N_GROUPS = 4
EXPERTS_PER_GROUP = 8
N_EXPERTS = N_GROUPS * EXPERTS_PER_GROUP
N_MOD = 6
EPS = 1e-6

LANES = 128
VMEM_LIMIT = 56 * 1024 * 1024
NEG_BIG = -1e30

NORM_ROWS = 512
MM_TM = 1024
MM_TN = 512
GLA_ROWS = 256
PREP_ROWS = 512
ATT_SPAN = 2048
MERGE_ROWS = 1024
EXPERT_TM = 256
COMBINE_ROWS = 256


def _params(*sem):
    return pltpu.CompilerParams(dimension_semantics=sem, vmem_limit_bytes=VMEM_LIMIT)


def _dot(a, b):
    return jnp.dot(a, b, preferred_element_type=F32)


def _dot_nt(a, b):
    return lax.dot_general(a, b, (((1,), (1,)), ((), ())), preferred_element_type=F32)


def _dot_tn(a, b):
    return lax.dot_general(a, b, (((0,), (0,)), ((), ())), preferred_element_type=F32)


def _split_bf16(x):
    hi = x.astype(BF16)
    lo = (x - hi.astype(F32)).astype(BF16)
    return hi, lo


def _sigmoid(x):
    return 1.0 / (1.0 + jnp.exp(-x))


def _grid_call(kernel, l, grid, in_specs, out_specs, out_shape, scratch=(), sem=None, name=None):
    sem = sem or ("arbitrary",) * len(grid)
    return pl.pallas_call(
        kernel,
        grid_spec=pltpu.PrefetchScalarGridSpec(
            num_scalar_prefetch=1, grid=grid, in_specs=in_specs, out_specs=out_specs,
            scratch_shapes=list(scratch)),
        out_shape=out_shape,
        compiler_params=_params(*sem),
        name=name,
    ), jnp.reshape(l, (1,)).astype(jnp.int32)


def _mod_kernel(c_ref, w_ref, b_ref, o_ref):
    c = c_ref[...]
    a = (c * _sigmoid(c)).astype(BF16)
    o_ref[...] = _dot(a, w_ref[...].astype(BF16)) + b_ref[...]


def _modulation(c, w_ada, b_ada):
    L, D, N = w_ada.shape
    B = c.shape[0]
    rows = 8
    c_pad = jnp.zeros((rows, D), F32).at[:B].set(c)
    tn = 1024
    return pl.pallas_call(
        _mod_kernel,
        grid=(L, N // tn),
        in_specs=[pl.BlockSpec((rows, D), lambda l, j: (0, 0)),
                  pl.BlockSpec((None, D, tn), lambda l, j: (l, 0, j)),
                  pl.BlockSpec((None, 1, tn), lambda l, j: (l, 0, j))],
        out_specs=pl.BlockSpec((None, rows, tn), lambda l, j: (l, 0, j)),
        out_shape=jax.ShapeDtypeStruct((L, rows, N), F32),
        compiler_params=_params("parallel", "parallel"),
        name="adaln_mod",
    )(c_pad, w_ada, b_ada.reshape(L, 1, N))


def _rope_kernel(pos_ref, freq_ref, cos_ref, sin_ref):
    ang = pos_ref[...].astype(F32) * freq_ref[...]
    lane = lax.broadcasted_iota(jnp.int32, ang.shape, 1)
    cos_ref[...] = jnp.cos(ang)
    sin_ref[...] = jnp.where(lane < HEAD_DIM // 2, -jnp.sin(ang), jnp.sin(ang))


def _rope_tables(positions):
    T = positions.size
    inv_freq = ROPE_THETA ** (-jnp.arange(0, HEAD_DIM, 2, dtype=F32) / HEAD_DIM)
    freq2 = jnp.concatenate([inv_freq, inv_freq]).reshape(1, HEAD_DIM)
    rows = 2048
    return pl.pallas_call(
        _rope_kernel,
        grid=(T // rows,),
        in_specs=[pl.BlockSpec((rows, 1), lambda i: (i, 0)),
                  pl.BlockSpec((1, HEAD_DIM), lambda i: (0, 0))],
        out_specs=[pl.BlockSpec((rows, HEAD_DIM), lambda i: (i, 0))] * 2,
        out_shape=[jax.ShapeDtypeStruct((T, HEAD_DIM), F32)] * 2,
        compiler_params=_params("parallel"),
        name="rope_tables",
    )(positions.reshape(T, 1), freq2)


def _ada_norm(x, g, scale, shift):
    y = x * lax.rsqrt(jnp.mean(x * x, axis=-1, keepdims=True) + EPS)
    return y * g * (1.0 + scale) + shift


def _norm_kernel(l_ref, x_ref, g_ref, sh_ref, sc_ref, o_ref):
    o_ref[...] = _ada_norm(x_ref[...], g_ref[...], sc_ref[...], sh_ref[...]).astype(o_ref.dtype)


def _mod_spec(which, rows, seq):
    return lambda i, l: (i * rows // seq, 0, which)


def _norm(x, l, g_all, mod3, which_shift, seq):
    T, D = x.shape
    rows = NORM_ROWS
    call, lidx = _grid_call(
        _norm_kernel, l, (T // rows,),
        [pl.BlockSpec((rows, D), lambda i, l: (i, 0)),
         pl.BlockSpec((None, 1, D), lambda i, l: (l[0], 0, 0)),
         pl.BlockSpec((None, 1, D), _mod_spec(which_shift, rows, seq)),
         pl.BlockSpec((None, 1, D), _mod_spec(which_shift + 1, rows, seq))],
        pl.BlockSpec((rows, D), lambda i, l: (i, 0)),
        jax.ShapeDtypeStruct((T, D), BF16), sem=("parallel",), name="ada_norm")
    return call(lidx, x, g_all, mod3, mod3)


def _mm_kernel(l_ref, a_ref, w_ref, *rest, epilogue, n_extra):
    extra = rest[:n_extra]
    o_ref = rest[n_extra]
    if w_ref.dtype == BF16:
        wbf_ref = w_ref
    else:
        wbf_ref = rest[n_extra + 1]

        @pl.when(pl.program_id(1) == 0)
        def _():
            wbf_ref[...] = w_ref[...].astype(BF16)

    acc = _dot(a_ref[...], wbf_ref[...])
    if epilogue is not None:
        acc = epilogue(acc, *[e[...] for e in extra])
    o_ref[...] = acc.astype(o_ref.dtype)


def _matmul(a, w, l, *, col0, n, out_dtype, tn=MM_TN, tm=MM_TM, epilogue=None, extras=(),
            extra_specs=(), name="matmul"):
    M, K = a.shape
    assert col0 % tn == 0 and n % tn == 0 and M % tm == 0
    cb0 = col0 // tn
    kern = functools.partial(_mm_kernel, epilogue=epilogue, n_extra=len(extras))
    call, lidx = _grid_call(
        kern, l, (n // tn, M // tm),
        [pl.BlockSpec((tm, K), lambda j, i, l: (i, 0)),
         pl.BlockSpec((None, K, tn), lambda j, i, l: (l[0], 0, cb0 + j))] + list(extra_specs),
        pl.BlockSpec((tm, tn), lambda j, i, l: (i, j)),
        jax.ShapeDtypeStruct((M, n), out_dtype),
        scratch=[] if w.dtype == BF16 else [pltpu.VMEM((K, tn), BF16)], name=name)
    return call(lidx, a, w, *extras)


def _gla_kernel(l_ref, q_ref, k_ref, v_ref, r_ref, fg_ref, wfg_ref, bfg_ref, g_ref, o_ref,
                state_ref, *, head_k, head_v):
    rows = q_ref.shape[0]
    C = GLA_CHUNK

    @pl.when(pl.program_id(1) == 0)
    def _():
        state_ref[...] = jnp.zeros_like(state_ref)

    fh, fl = _split_bf16(fg_ref[...])
    wh, wl = _split_bf16(wfg_ref[...])
    z = _dot(fh, wh) + _dot(fl, wh) + _dot(fh, wl) + bfg_ref[...]
    log_a = (jnp.minimum(z, 0.0) - jnp.log(1.0 + jnp.exp(-jnp.abs(z)))) * (1.0 / GLA_TAU)

    ri = lax.broadcasted_iota(jnp.int32, (C, C), 0)
    ci = lax.broadcasted_iota(jnp.int32, (C, C), 1)
    causal = ri >= ci
    tri = causal.astype(BF16)
    scale = head_k ** -0.5

    for c in range(rows // C):
        rs = slice(c * C, (c + 1) * C)
        lh, ll = _split_bf16(log_a[rs])
        b = _dot(tri, lh) + _dot(tri, ll)
        b_last = b[C - 1:C]
        q = q_ref[rs, :].astype(F32) * scale
        k = k_ref[rs, :].astype(F32)
        q_dec = (q * jnp.exp(b)).astype(BF16)
        k_inv = (k * jnp.exp(-b)).astype(BF16)
        k_upd = (k * jnp.exp(b_last - b)).astype(BF16)
        decay = jnp.exp(b_last)
        for h in range(GLA_HEADS):
            ks = slice(h * head_k, (h + 1) * head_k)
            vs = slice(h * head_v, (h + 1) * head_v)
            v = v_ref[rs, vs]
            att = jnp.where(causal, _dot_nt(q_dec[:, ks], k_inv[:, ks]), 0.0).astype(BF16)
            st = state_ref[h]
            o = _dot(att, v) + _dot_nt(q_dec[:, ks], st.astype(BF16))
            state_ref[h] = st * decay[:, ks] + _dot_tn(v, k_upd[:, ks])
            o = o * lax.rsqrt(jnp.mean(o * o, axis=-1, keepdims=True) + EPS) * g_ref[:, vs]
            r = r_ref[rs, vs].astype(F32)
            o_ref[rs, vs] = (o * (r * _sigmoid(r))).astype(o_ref.dtype)


def _gla(proj, fg, l, wfg_pad, b_fg, g_gla_out, batch, seq):
    T = proj.shape[0]
    dk_all = wfg_pad.shape[-1]
    dv_all = g_gla_out.shape[-1]
    rows = GLA_ROWS
    nblk = seq // rows
    kern = functools.partial(_gla_kernel, head_k=dk_all // GLA_HEADS, head_v=dv_all // GLA_HEADS)
    row = lambda b, n, l: b * nblk + n
    call, lidx = _grid_call(
        kern, l, (batch, nblk),
        [pl.BlockSpec((rows, dk_all), lambda b, n, l: (row(b, n, l), 0)),
         pl.BlockSpec((rows, dk_all), lambda b, n, l: (row(b, n, l), 1)),
         pl.BlockSpec((rows, dv_all), lambda b, n, l: (row(b, n, l), 1)),
         pl.BlockSpec((rows, dv_all), lambda b, n, l: (row(b, n, l), 2)),
         pl.BlockSpec((rows, LANES), lambda b, n, l: (row(b, n, l), 0)),
         pl.BlockSpec((None, LANES, dk_all), lambda b, n, l: (l[0], 0, 0)),
         pl.BlockSpec((None, 1, dk_all), lambda b, n, l: (l[0], 0, 0)),
         pl.BlockSpec((None, 1, dv_all), lambda b, n, l: (l[0], 0, 0))],
        pl.BlockSpec((rows, dv_all), lambda b, n, l: (row(b, n, l), 0)),
        jax.ShapeDtypeStruct((T, dv_all), BF16),
        scratch=[pltpu.VMEM((GLA_HEADS, dv_all // GLA_HEADS, dk_all // GLA_HEADS), F32)],
        sem=("parallel", "arbitrary"), name="gla")
    return call(lidx, proj, proj, proj, proj, fg, wfg_pad, b_fg, g_gla_out)


def _qkprep_kernel(l_ref, q_ref, k_ref, v_ref, cos_ref, sin_ref, gq_ref, gk_ref, *refs):
    out_refs, scr_ref = refs[:-1], refs[-1]
    cos = cos_ref[...]
    sin = sin_ref[...]
    gw = ATT_HEADS * HEAD_DIM
    rows = q_ref.shape[0]

    def prep(ref, g, scale):
        x = ref.astype(F32)
        y = x * lax.rsqrt(jnp.mean(x * x, axis=-1, keepdims=True) + EPS) * g
        y = y * cos + pltpu.roll(y, HEAD_DIM // 2, 1) * sin
        return y * scale

    def emit(dst_ref, dil):
        n = rows // dil
        for p in range(dil):
            for h in range(ATT_HEADS):
                cs = slice(p * gw + h * HEAD_DIM, p * gw + (h + 1) * HEAD_DIM)
                dst_ref[:, cs] = scr_ref[h, pl.ds(p, n, stride=dil), :].astype(BF16)

    for grp, (_, dil) in enumerate(DIL_GROUPS):
        for ref, g_ref, scale, out in ((q_ref, gq_ref, HEAD_DIM ** -0.5, out_refs[grp]),
                                       (k_ref, gk_ref, 1.0, out_refs[N_DIL + grp])):
            for h in range(ATT_HEADS):
                src = slice(grp * gw + h * HEAD_DIM, grp * gw + (h + 1) * HEAD_DIM)
                y = prep(ref[:, src], g_ref[...], scale)
                if dil == 1:
                    out[:, h * HEAD_DIM:(h + 1) * HEAD_DIM] = y.astype(BF16)
                else:
                    scr_ref[h] = y
            if dil > 1:
                emit(out, dil)
        if dil == 1:
            out_refs[2 * N_DIL + grp][...] = v_ref[:, grp * gw:(grp + 1) * gw]
        else:
            for h in range(ATT_HEADS):
                src = slice(grp * gw + h * HEAD_DIM, grp * gw + (h + 1) * HEAD_DIM)
                scr_ref[h] = v_ref[:, src].astype(F32)
            emit(out_refs[2 * N_DIL + grp], dil)


def _qkprep(proj2, l, cos2, sin2, g_q, g_k):
    T = proj2.shape[0]
    rows = PREP_ROWS
    width = N_DIL * ATT_HEADS * HEAD_DIM
    gw = ATT_HEADS * HEAD_DIM
    dils = [dil for _, dil in DIL_GROUPS] * 3
    call, lidx = _grid_call(
        _qkprep_kernel, l, (T // rows,),
        [pl.BlockSpec((rows, width), lambda i, l: (i, 0)),
         pl.BlockSpec((rows, width), lambda i, l: (i, 1)),
         pl.BlockSpec((rows, width), lambda i, l: (i, 2)),
         pl.BlockSpec((rows, HEAD_DIM), lambda i, l: (i, 0)),
         pl.BlockSpec((rows, HEAD_DIM), lambda i, l: (i, 0)),
         pl.BlockSpec((None, 1, HEAD_DIM), lambda i, l: (l[0], 0, 0)),
         pl.BlockSpec((None, 1, HEAD_DIM), lambda i, l: (l[0], 0, 0))],
        [pl.BlockSpec((rows // d, d * gw), lambda i, l: (i, 0)) for d in dils],
        [jax.ShapeDtypeStruct((T // d, d * gw), BF16) for d in dils],
        scratch=[pltpu.VMEM((ATT_HEADS, rows, HEAD_DIM), F32)],
        sem=("parallel",), name="qk_prep")
    return call(lidx, proj2, proj2, proj2, cos2, sin2, g_q, g_k)


def _attn_kernel(q_ref, kc_ref, kp_ref, vc_ref, vp_ref, o_ref, lse_ref, *, dil):
    first = pl.program_id(1) == 0
    QB = q_ref.shape[0]
    A = ATT_BLOCK
    qi = lax.broadcasted_iota(jnp.int32, (A, 2 * A), 0)
    kj = lax.broadcasted_iota(jnp.int32, (A, 2 * A), 1) - A
    dist = qi - kj
    valid = (dist >= 0) & (dist <= A)
    bias = jnp.where(valid, 0.0, NEG_BIG).astype(F32)
    bias_first = jnp.where(first, jnp.where(valid & (kj >= 0), 0.0, NEG_BIG), bias).astype(F32)

    for p in range(dil):
        for h in range(ATT_HEADS):
            cs = slice((p * ATT_HEADS + h) * HEAD_DIM, (p * ATT_HEADS + h + 1) * HEAD_DIM)
            for j in range(QB // A):
                rs = slice(j * A, (j + 1) * A)
                if j == 0:
                    kcat = jnp.concatenate([kp_ref[:, cs], kc_ref[0:A, cs]], axis=0)
                    vcat = jnp.concatenate([vp_ref[:, cs], vc_ref[0:A, cs]], axis=0)
                    bb = bias_first
                else:
                    kcat = kc_ref[(j - 1) * A:(j + 1) * A, cs]
                    vcat = vc_ref[(j - 1) * A:(j + 1) * A, cs]
                    bb = bias
                s = _dot_nt(q_ref[rs, cs], kcat) + bb
                m = jnp.max(s, axis=-1, keepdims=True)
                e = jnp.exp(s - m)
                den = jnp.sum(e, axis=-1, keepdims=True)
                o = _dot(e.astype(BF16), vcat) / den
                o_ref[rs, cs] = o.astype(o_ref.dtype)
                lse_ref[rs, cs] = jnp.broadcast_to(m + jnp.log(den), (A, HEAD_DIM))


def _attn_group(q, k, v, dil, batch, seq):
    rows_all, W = q.shape
    A = ATT_BLOCK
    QB = ATT_SPAN // dil
    nsp = seq // ATT_SPAN
    per = QB // A
    cur = pl.BlockSpec((QB, W), lambda b, n: (b * nsp + n, 0))
    prev = pl.BlockSpec((A, W), lambda b, n: (jnp.maximum((b * nsp + n) * per - 1, b * nsp * per), 0))
    return pl.pallas_call(
        functools.partial(_attn_kernel, dil=dil),
        grid=(batch, nsp),
        in_specs=[cur, cur, prev, cur, prev],
        out_specs=[cur, cur],
        out_shape=[jax.ShapeDtypeStruct((rows_all, W), BF16),
                   jax.ShapeDtypeStruct((rows_all, W), F32)],
        compiler_params=_params("parallel", "arbitrary"),
        name=f"dilated_attn_r{dil}",
    )(q, k, k, v, v)


def _attn_mix_kernel(o0, o1, o2, l0, l1, l2, out_ref, so_ref, sl_ref):
    rows, gw = out_ref.shape
    o_refs, l_refs = (o0, o1, o2), (l0, l1, l2)
    for h in range(ATT_HEADS):
        hs = slice(h * HEAD_DIM, (h + 1) * HEAD_DIM)
        outs, lses = [], []
        for g, (_, dil) in enumerate(DIL_GROUPS):
            if dil == 1:
                outs.append(o_refs[g][:, hs].astype(F32))
                lses.append(l_refs[g][:, hs])
                continue
            n = rows // dil
            for p in range(dil):
                cs = slice(p * gw + h * HEAD_DIM, p * gw + (h + 1) * HEAD_DIM)
                so_ref[g, pl.ds(p, n, stride=dil), :] = o_refs[g][:, cs].astype(F32)
                sl_ref[g, pl.ds(p, n, stride=dil), :] = l_refs[g][:, cs]
            outs.append(so_ref[g])
            lses.append(sl_ref[g])
        m = jnp.maximum(jnp.maximum(lses[0], lses[1]), lses[2])
        ws = [jnp.exp(x - m) for x in lses]
        den = ws[0] + ws[1] + ws[2]
        out_ref[:, hs] = ((ws[0] * outs[0] + ws[1] * outs[1] + ws[2] * outs[2]) / den
                          ).astype(out_ref.dtype)


def _attn_mix(outs, lses):
    gw = ATT_HEADS * HEAD_DIM
    T = outs[0].shape[0]
    rows = PREP_ROWS
    dils = [dil for _, dil in DIL_GROUPS]
    specs = [pl.BlockSpec((rows // d, d * gw), lambda i: (i, 0)) for d in dils]
    return pl.pallas_call(
        _attn_mix_kernel,
        grid=(T // rows,),
        in_specs=specs + specs,
        out_specs=pl.BlockSpec((rows, gw), lambda i: (i, 0)),
        out_shape=jax.ShapeDtypeStruct((T, gw), BF16),
        scratch_shapes=[pltpu.VMEM((N_DIL, rows, HEAD_DIM), F32),
                        pltpu.VMEM((N_DIL, rows, HEAD_DIM), F32)],
        compiler_params=_params("parallel"),
        name="attn_mix",
    )(*outs, *lses)


def _merge_kernel(l_ref, ogla_ref, oatt_ref, gg_ref, ga_ref, wg_ref, wa_ref,
                  out_ref, wgbf_ref, wabf_ref):
    @pl.when(pl.program_id(1) == 0)
    def _():
        wgbf_ref[...] = wg_ref[...].astype(BF16)
        wabf_ref[...] = wa_ref[...].astype(BF16)

    a = _dot(ogla_ref[...], wgbf_ref[...])
    b = _dot(oatt_ref[...], wabf_ref[...])
    gg = _sigmoid(gg_ref[...].astype(F32))
    ga = _sigmoid(ga_ref[...].astype(F32))
    out_ref[...] = (gg * a + ga * b).astype(out_ref.dtype)


def _merge(o_gla, o_att, proj2, l, w_gla_proj, w_attn_proj, gate_col0):
    T, dv = o_gla.shape
    gw = o_att.shape[1]
    D = w_gla_proj.shape[-1]
    tm, tn = MERGE_ROWS, MM_TN
    gcb = gate_col0 // tn
    dcb = D // tn
    call, lidx = _grid_call(
        _merge_kernel, l, (D // tn, T // tm),
        [pl.BlockSpec((tm, dv), lambda j, i, l: (i, 0)),
         pl.BlockSpec((tm, gw), lambda j, i, l: (i, 0)),
         pl.BlockSpec((tm, tn), lambda j, i, l: (i, gcb + j)),
         pl.BlockSpec((tm, tn), lambda j, i, l: (i, gcb + dcb + j)),
         pl.BlockSpec((None, dv, tn), lambda j, i, l: (l[0], 0, j)),
         pl.BlockSpec((None, gw, tn), lambda j, i, l: (l[0], 0, j))],
        pl.BlockSpec((tm, tn), lambda j, i, l: (i, j)),
        jax.ShapeDtypeStruct((T, D), BF16),
        scratch=[pltpu.VMEM((dv, tn), BF16), pltpu.VMEM((gw, tn), BF16)], name="gated_merge")
    return call(lidx, o_gla, o_att, proj2, proj2, w_gla_proj, w_attn_proj)


def _router_kernel(l_ref, x_ref, g_ref, sh_ref, sc_ref, wr_ref, br_ref, h_ref, route_ref):
    h = _ada_norm(x_ref[...], g_ref[...], sc_ref[...], sh_ref[...])
    h_ref[...] = h
    lg = _dot(h.astype(BF16), wr_ref[...].astype(BF16)) + br_ref[...]
    lane = lax.broadcasted_iota(jnp.int32, lg.shape, 1).astype(F32)
    big = float(4 * LANES)

    def first_argmax(vals):
        mx = jnp.max(vals, axis=-1, keepdims=True)
        idx = jnp.min(jnp.where(vals == mx, lane, big), axis=-1, keepdims=True)
        return mx, idx

    gl = jnp.where(lane < N_GROUPS, lg, NEG_BIG)
    gmax, gidx = first_argmax(gl)
    g_weight = 1.0 / jnp.sum(jnp.exp(gl - gmax), axis=-1, keepdims=True)
    lo = N_GROUPS + gidx * EXPERTS_PER_GROUP
    el = jnp.where((lane >= lo) & (lane < lo + EXPERTS_PER_GROUP), lg, NEG_BIG)
    m1, i1 = first_argmax(el)
    m2, i2 = first_argmax(jnp.where(lane == i1, NEG_BIG, el))
    e2 = jnp.exp(m2 - m1)
    w1 = g_weight / (1.0 + e2)
    w2 = g_weight * e2 / (1.0 + e2)
    route = jnp.where(lane == 0, i1 - N_GROUPS,
                      jnp.where(lane == 1, i2 - N_GROUPS,
                                jnp.where(lane == 2, w1, jnp.where(lane == 3, w2, 0.0))))
    route_ref[...] = route


def _router(x, l, g_all, mod3, wr_pad, br_pad, seq):
    T, D = x.shape
    rows = NORM_ROWS
    call, lidx = _grid_call(
        _router_kernel, l, (T // rows,),
        [pl.BlockSpec((rows, D), lambda i, l: (i, 0)),
         pl.BlockSpec((None, 1, D), lambda i, l: (l[0], 0, 0)),
         pl.BlockSpec((None, 1, D), _mod_spec(3, rows, seq)),
         pl.BlockSpec((None, 1, D), _mod_spec(4, rows, seq)),
         pl.BlockSpec((None, D, LANES), lambda i, l: (l[0], 0, 0)),
         pl.BlockSpec((None, 1, LANES), lambda i, l: (l[0], 0, 0))],
        [pl.BlockSpec((rows, D), lambda i, l: (i, 0)),
         pl.BlockSpec((rows, LANES), lambda i, l: (i, 0))],
        [jax.ShapeDtypeStruct((T, D), F32), jax.ShapeDtypeStruct((T, LANES), F32)],
        sem=("parallel",), name="norm_router")
    return call(lidx, x, g_all, mod3, mod3, wr_pad, br_pad)


def _expert_kernel(meta_ref, src_cur_ref, src_nxt_ref, dst_ref, h_ref, wg_ref, wu_ref, wd_ref,
                   ys_ref, xbuf_ref, obuf_ref, wgbf_ref, wubf_ref, wdbf_ref, gsem_ref, ssem_ref):
    i = pl.program_id(0)
    last = pl.num_programs(0) - 1
    n_used = meta_ref[1]
    slot = i % 2
    tm = xbuf_ref.shape[1]

    def gather_start(idx_ref, s):
        def body(r, c):
            pltpu.make_async_copy(h_ref.at[pl.ds(idx_ref[0, 0, r], 1)],
                                  xbuf_ref.at[s, pl.ds(r, 1)], gsem_ref.at[s]).start()
            return c
        lax.fori_loop(0, tm, body, 0, unroll=8)

    def gather_wait(s):
        pltpu.make_async_copy(h_ref.at[pl.ds(0, tm)], xbuf_ref.at[s], gsem_ref.at[s]).wait()

    def scatter_start(s):
        def body(r, c):
            pltpu.make_async_copy(obuf_ref.at[s, pl.ds(r, 1)],
                                  ys_ref.at[pl.ds(dst_ref[0, 0, r], 1)], ssem_ref.at[s]).start()
            return c
        lax.fori_loop(0, tm, body, 0, unroll=8)

    def scatter_wait(s):
        pltpu.make_async_copy(obuf_ref.at[s], ys_ref.at[pl.ds(0, tm)], ssem_ref.at[s]).wait()

    @pl.when(i == 0)
    def _():
        gather_start(src_cur_ref, 0)
        n_rows = ys_ref.shape[0]
        obuf_ref[0] = jnp.zeros((tm, obuf_ref.shape[2]), F32)
        spare = [pltpu.make_async_copy(obuf_ref.at[0], ys_ref.at[pl.ds(n_rows - (k + 1) * tm, tm)],
                                       ssem_ref.at[0]) for k in range(2)]
        for cp in spare:
            cp.start()
        for cp in spare:
            cp.wait()

    @pl.when(i + 1 < n_used)
    def _():
        gather_start(src_nxt_ref, 1 - slot)

    @pl.when((i >= 2) & (i - 2 < n_used))
    def _():
        scatter_wait(slot)

    @pl.when(i < n_used)
    def _():
        gather_wait(slot)
        prev = meta_ref[3 + jnp.maximum(i - 1, 0)]

        @pl.when((i == 0) | (meta_ref[3 + i] != prev))
        def _():
            wgbf_ref[...] = wg_ref[...].astype(BF16)
            wubf_ref[...] = wu_ref[...].astype(BF16)
            wdbf_ref[...] = wd_ref[...].astype(BF16)

        x = xbuf_ref[slot].astype(BF16)
        a = _dot(x, wgbf_ref[...])
        u = _dot(x, wubf_ref[...])
        hid = (a * _sigmoid(a)) * u
        obuf_ref[slot] = _dot(hid.astype(BF16), wdbf_ref[...])
        scatter_start(slot)

    @pl.when(i == last)
    def _():
        @pl.when((i >= 1) & (i - 1 < n_used))
        def _():
            scatter_wait(1 - slot)

        @pl.when(i < n_used)
        def _():
            scatter_wait(slot)


def _experts(h, src, dst, meta, w_gate, w_up, w_down, n_out):
    T, D = h.shape
    F = w_gate.shape[-1]
    n_tiles, _, tm = src.shape
    wspec = lambda shape: pl.BlockSpec((None, None) + shape, lambda i, m: (m[0], m[3 + i], 0, 0))
    idx_spec = lambda f: pl.BlockSpec((1, 1, tm), lambda i, m: (f(i), 0, 0), memory_space=pltpu.SMEM)
    return pl.pallas_call(
        _expert_kernel,
        grid_spec=pltpu.PrefetchScalarGridSpec(
            num_scalar_prefetch=1, grid=(n_tiles,),
            in_specs=[idx_spec(lambda i: i), idx_spec(lambda i: jnp.minimum(i + 1, n_tiles - 1)),
                      idx_spec(lambda i: i),
                      pl.BlockSpec(memory_space=pl.ANY),
                      wspec((D, F)), wspec((D, F)), wspec((F, D))],
            out_specs=pl.BlockSpec(memory_space=pl.ANY),
            scratch_shapes=[pltpu.VMEM((2, tm, D), F32), pltpu.VMEM((2, tm, D), F32),
                            pltpu.VMEM((D, F), BF16), pltpu.VMEM((D, F), BF16),
                            pltpu.VMEM((F, D), BF16),
                            pltpu.SemaphoreType.DMA((2,)), pltpu.SemaphoreType.DMA((2,))]),
        out_shape=jax.ShapeDtypeStruct((n_out, D), F32),
        compiler_params=_params("arbitrary"),
        name="grouped_experts",
    )(meta, src, src, dst, h, w_gate, w_up, w_down)


def _combine_kernel(x_ref, gt_ref, route_ref, y0_ref, y1_ref, o_ref):
    w1 = route_ref[:, 2:3]
    w2 = route_ref[:, 3:4]
    o_ref[...] = x_ref[...] + gt_ref[...] * (w1 * y0_ref[...] + w2 * y1_ref[...])


def _combine(x, ys, route, mod3, seq):
    T, D = x.shape
    rows = COMBINE_ROWS
    nblk = T // rows
    return pl.pallas_call(
        _combine_kernel,
        grid=(nblk,),
        in_specs=[pl.BlockSpec((rows, D), lambda i: (i, 0)),
                  pl.BlockSpec((None, 1, D), lambda i: (i * rows // seq, 0, 5)),
                  pl.BlockSpec((rows, LANES), lambda i: (i, 0)),
                  pl.BlockSpec((rows, D), lambda i: (i, 0)),
                  pl.BlockSpec((rows, D), lambda i: (nblk + i, 0))],
        out_specs=pl.BlockSpec((rows, D), lambda i: (i, 0)),
        out_shape=jax.ShapeDtypeStruct((T, D), F32),
        compiler_params=_params("parallel"),
        name="moe_combine",
    )(x, mod3, route, ys, ys)


def _dispatch_plan(route, l):
    T = route.shape[0]
    tm = EXPERT_TM
    n_tiles = (2 * T) // tm + N_EXPERTS
    P = n_tiles * tm
    e = route[:, 0:2].astype(jnp.int32).T.reshape(-1)
    onehot = (e[:, None] == jnp.arange(N_EXPERTS, dtype=jnp.int32)[None, :]).astype(jnp.int32)
    csum = jnp.cumsum(onehot, axis=0)
    rank = jnp.sum((csum - onehot) * onehot, axis=1)
    counts = csum[-1]
    tiles_per = (counts + tm - 1) // tm
    tile_end = jnp.cumsum(tiles_per)
    off = (tile_end - tiles_per) * tm
    pos = off[e] + rank
    asg = jnp.full((P,), -1, jnp.int32).at[pos].set(jnp.arange(2 * T, dtype=jnp.int32))
    slot = jnp.arange(P, dtype=jnp.int32)
    spare = 2 * T + ((slot // tm) % 2) * tm + slot % tm
    src = jnp.where(asg >= 0, asg % T, 0).reshape(n_tiles, 1, tm)
    dst = jnp.where(asg >= 0, asg, spare).reshape(n_tiles, 1, tm)
    tile_ids = jnp.arange(n_tiles, dtype=jnp.int32)
    tile_expert = jnp.minimum(jnp.sum(tile_ids[:, None] >= tile_end[None, :], axis=1),
                              N_EXPERTS - 1).astype(jnp.int32)
    meta = jnp.concatenate([jnp.stack([l, tile_end[-1].astype(jnp.int32),
                                       jnp.int32(0)]), tile_expert])
    return src, dst, meta


def kernel(x, c, positions, w_ada, b_ada, g_norm_mix, g_norm_ffn, w_in, w_fg, b_fg, g_gla_out,
           g_q, g_k, w_gla_proj, w_attn_proj, w_out, w_route_group, b_route_group,
           w_route_expert, b_route_expert, w_exp_gate, w_exp_up, w_exp_down):
    B, S, D = x.shape
    L = w_ada.shape[0]
    T = B * S
    dk_all = w_fg.shape[-1]
    dv_all = g_gla_out.shape[-1]
    att_w = N_DIL * ATT_HEADS * HEAD_DIM
    gla_cols = 2 * dk_all + 2 * dv_all
    rest0 = gla_cols + GLA_RANK

    mod = _modulation(c, w_ada, b_ada)
    cos2, sin2 = _rope_tables(positions)

    w_rest = w_in[:, :, rest0:].astype(BF16)
    wfg_pad = jnp.zeros((L, LANES, dk_all), F32).at[:, :GLA_RANK].set(w_fg)
    wr_pad = jnp.zeros((L, D, LANES), F32)
    wr_pad = wr_pad.at[:, :, :N_GROUPS].set(w_route_group)
    wr_pad = wr_pad.at[:, :, N_GROUPS:N_GROUPS + N_EXPERTS].set(w_route_expert)
    br_pad = jnp.zeros((L, 1, LANES), F32)
    br_pad = br_pad.at[:, 0, :N_GROUPS].set(b_route_group)
    br_pad = br_pad.at[:, 0, N_GROUPS:N_GROUPS + N_EXPERTS].set(b_route_expert)
    g_mix = g_norm_mix.reshape(L, 1, D)
    g_ffn = g_norm_ffn.reshape(L, 1, D)
    b_fg3 = b_fg.reshape(L, 1, dk_all)
    g_gla3 = g_gla_out.reshape(L, 1, dv_all)
    g_q3 = g_q.reshape(L, 1, HEAD_DIM)
    g_k3 = g_k.reshape(L, 1, HEAD_DIM)

    def residual(acc, xres, gt):
        return xres + gt * acc

    def layer(l, xt):
        l = jnp.asarray(l, jnp.int32)
        mod3 = lax.dynamic_index_in_dim(mod, l, 0, keepdims=False).reshape(8, 1, N_MOD * D)
        h = _norm(xt, l, g_mix, mod3, 0, S)
        proj1 = _matmul(h, w_in, l, col0=0, n=gla_cols, out_dtype=BF16, tn=2 * MM_TN,
                        name="proj_gla")
        fg = _matmul(h, w_in, l, col0=gla_cols, n=LANES, tn=LANES, out_dtype=F32, name="proj_fg")
        proj2 = _matmul(h, w_rest, l, col0=0, n=w_rest.shape[-1], out_dtype=BF16, tm=2 * MM_TM,
                        name="proj_att")
        o_gla = _gla(proj1, fg, l, wfg_pad, b_fg3, g_gla3, B, S)
        prep = _qkprep(proj2, l, cos2, sin2, g_q3, g_k3)
        outs, lses = [], []
        for grp, (_, dil) in enumerate(DIL_GROUPS):
            o, lse = _attn_group(prep[grp], prep[N_DIL + grp], prep[2 * N_DIL + grp], dil, B, S)
            outs.append(o)
            lses.append(lse)
        o_att = _attn_mix(outs, lses)
        merged = _merge(o_gla, o_att, proj2, l, w_gla_proj, w_attn_proj, 3 * att_w)
        gt_spec = pl.BlockSpec((None, 1, MM_TN),
                               lambda j, i, l: (i * MM_TM // S, 0, 2 * (D // MM_TN) + j))
        xt = _matmul(merged, w_out, l, col0=0, n=D, out_dtype=F32, epilogue=residual,
                     extras=(xt, mod3),
                     extra_specs=(pl.BlockSpec((MM_TM, MM_TN), lambda j, i, l: (i, j)), gt_spec),
                     name="out_proj")
        h2, route = _router(xt, l, g_ffn, mod3, wr_pad, br_pad, S)
        src, dst, meta = _dispatch_plan(route, l)
        ys = _experts(h2, src, dst, meta, w_exp_gate, w_exp_up, w_exp_down, 2 * T + 2 * EXPERT_TM)
        return _combine(xt, ys, route, mod3, S)

    xt = lax.fori_loop(0, L, layer, x.reshape(T, D))
    return xt.reshape(B, S, D)
```

```python
import functools

import jax
import jax.numpy as jnp
from jax import lax
from jax.experimental import pallas as pl
from jax.experimental.pallas import tpu as pltpu

F32 = jnp.float32
BF16 = jnp.bfloat16

GLA_HEADS = 4
GLA_RANK = 16
GLA_TAU = 16.0
GLA_CHUNK = 64
DIL_GROUPS = ((128, 1), (512, 4), (2048, 16))
N_DIL = 3
ATT_HEADS = 4
HEAD_DIM = 128
ATT_BLOCK = 128
ROPE_THETA = 10000.0
N_GROUPS = 4
EXPERTS_PER_GROUP = 8
N_EXPERTS = N_GROUPS * EXPERTS_PER_GROUP
N_MOD = 6
EPS = 1e-6

LANES = 128
VMEM_LIMIT = 56 * 1024 * 1024
NEG_BIG = -1e30

NORM_ROWS = 512
MM_TM = 1024
MM_TN = 512
GLA_ROWS = 256
PREP_ROWS = 512
ATT_SPAN = 2048
MERGE_ROWS = 1024
EXPERT_TM = 256
COMBINE_ROWS = 256


def _params(*sem):
    return pltpu.CompilerParams(dimension_semantics=sem, vmem_limit_bytes=VMEM_LIMIT)


def _dot(a, b):
    return jnp.dot(a, b, preferred_element_type=F32)


def _dot_nt(a, b):
    return lax.dot_general(a, b, (((1,), (1,)), ((), ())), preferred_element_type=F32)


def _dot_tn(a, b):
    return lax.dot_general(a, b, (((0,), (0,)), ((), ())), preferred_element_type=F32)


def _split_bf16(x):
    hi = x.astype(BF16)
    lo = (x - hi.astype(F32)).astype(BF16)
    return hi, lo


def _sigmoid(x):
    return 1.0 / (1.0 + jnp.exp(-x))


def _grid_call(kernel, l, grid, in_specs, out_specs, out_shape, scratch=(), sem=None, name=None):
    sem = sem or ("arbitrary",) * len(grid)
    return pl.pallas_call(
        kernel,
        grid_spec=pltpu.PrefetchScalarGridSpec(
            num_scalar_prefetch=1, grid=grid, in_specs=in_specs, out_specs=out_specs,
            scratch_shapes=list(scratch)),
        out_shape=out_shape,
        compiler_params=_params(*sem),
        name=name,
    ), jnp.reshape(l, (1,)).astype(jnp.int32)


def _mod_kernel(c_ref, w_ref, b_ref, o_ref):
    c = c_ref[...]
    a = (c * _sigmoid(c)).astype(BF16)
    o_ref[...] = _dot(a, w_ref[...].astype(BF16)) + b_ref[...]


def _modulation(c, w_ada, b_ada):
    L, D, N = w_ada.shape
    B = c.shape[0]
    rows = 8
    c_pad = jnp.zeros((rows, D), F32).at[:B].set(c)
    tn = 1024
    return pl.pallas_call(
        _mod_kernel,
        grid=(L, N // tn),
        in_specs=[pl.BlockSpec((rows, D), lambda l, j: (0, 0)),
                  pl.BlockSpec((None, D, tn), lambda l, j: (l, 0, j)),
                  pl.BlockSpec((None, 1, tn), lambda l, j: (l, 0, j))],
        out_specs=pl.BlockSpec((None, rows, tn), lambda l, j: (l, 0, j)),
        out_shape=jax.ShapeDtypeStruct((L, rows, N), F32),
        compiler_params=_params("parallel", "parallel"),
        name="adaln_mod",
    )(c_pad, w_ada, b_ada.reshape(L, 1, N))


def _rope_kernel(pos_ref, freq_ref, cos_ref, sin_ref):
    ang = pos_ref[...].astype(F32) * freq_ref[...]
    lane = lax.broadcasted_iota(jnp.int32, ang.shape, 1)
    cos_ref[...] = jnp.cos(ang)
    sin_ref[...] = jnp.where(lane < HEAD_DIM // 2, -jnp.sin(ang), jnp.sin(ang))


def _rope_tables(positions):
    T = positions.size
    inv_freq = ROPE_THETA ** (-jnp.arange(0, HEAD_DIM, 2, dtype=F32) / HEAD_DIM)
    freq2 = jnp.concatenate([inv_freq, inv_freq]).reshape(1, HEAD_DIM)
    rows = 2048
    return pl.pallas_call(
        _rope_kernel,
        grid=(T // rows,),
        in_specs=[pl.BlockSpec((rows, 1), lambda i: (i, 0)),
                  pl.BlockSpec((1, HEAD_DIM), lambda i: (0, 0))],
        out_specs=[pl.BlockSpec((rows, HEAD_DIM), lambda i: (i, 0))] * 2,
        out_shape=[jax.ShapeDtypeStruct((T, HEAD_DIM), F32)] * 2,
        compiler_params=_params("parallel"),
        name="rope_tables",
    )(positions.reshape(T, 1), freq2)


def _ada_norm(x, g, scale, shift):
    y = x * lax.rsqrt(jnp.mean(x * x, axis=-1, keepdims=True) + EPS)
    return y * g * (1.0 + scale) + shift


def _norm_kernel(l_ref, x_ref, g_ref, sh_ref, sc_ref, o_ref):
    o_ref[...] = _ada_norm(x_ref[...], g_ref[...], sc_ref[...], sh_ref[...]).astype(o_ref.dtype)


def _mod_spec(which, rows, seq):
    return lambda i, l: (i * rows // seq, 0, which)


def _norm(x, l, g_all, mod3, which_shift, seq):
    T, D = x.shape
    rows = NORM_ROWS
    call, lidx = _grid_call(
        _norm_kernel, l, (T // rows,),
        [pl.BlockSpec((rows, D), lambda i, l: (i, 0)),
         pl.BlockSpec((None, 1, D), lambda i, l: (l[0], 0, 0)),
         pl.BlockSpec((None, 1, D), _mod_spec(which_shift, rows, seq)),
         pl.BlockSpec((None, 1, D), _mod_spec(which_shift + 1, rows, seq))],
        pl.BlockSpec((rows, D), lambda i, l: (i, 0)),
        jax.ShapeDtypeStruct((T, D), BF16), sem=("parallel",), name="ada_norm")
    return call(lidx, x, g_all, mod3, mod3)


def _mm_kernel(l_ref, a_ref, w_ref, *rest, epilogue, n_extra, shift):
    n_w = 1 if shift else 0
    extra = rest[n_w:n_w + n_extra]
    o_ref = rest[n_w + n_extra]
    wbf_ref = rest[n_w + n_extra + 1]
    tn = o_ref.shape[1]

    @pl.when(pl.program_id(1) == 0)
    def _():
        if shift:
            wide = jnp.concatenate([w_ref[...], rest[0][...]], axis=1)
            wbf_ref[...] = wide[:, shift:shift + tn].astype(BF16)
        else:
            wbf_ref[...] = w_ref[...].astype(BF16)

    acc = _dot(a_ref[...], wbf_ref[...])
    if epilogue is not None:
        acc = epilogue(acc, *[e[...] for e in extra])
    o_ref[...] = acc.astype(o_ref.dtype)


def _matmul(a, w, l, *, col0, n, out_dtype, tn=MM_TN, tm=MM_TM, shift=0, epilogue=None,
            extras=(), extra_specs=(), name="matmul"):
    M, K = a.shape
    assert col0 % tn == 0 and n % tn == 0 and M % tm == 0 and 0 <= shift < LANES
    cb0 = col0 // tn
    kern = functools.partial(_mm_kernel, epilogue=epilogue, n_extra=len(extras), shift=shift)
    w_specs = [pl.BlockSpec((None, K, tn), lambda j, i, l: (l[0], 0, cb0 + j))]
    if shift:
        per = tn // LANES
        w_specs.append(pl.BlockSpec((None, K, LANES), lambda j, i, l: (l[0], 0, (cb0 + j + 1) * per)))
    call, lidx = _grid_call(
        kern, l, (n // tn, M // tm),
        [pl.BlockSpec((tm, K), lambda j, i, l: (i, 0))] + w_specs + list(extra_specs),
        pl.BlockSpec((tm, tn), lambda j, i, l: (i, j)),
        jax.ShapeDtypeStruct((M, n), out_dtype),
        scratch=[pltpu.VMEM((K, tn), BF16)], name=name)
    return call(lidx, a, *([w] * len(w_specs)), *extras)


def _gla_kernel(l_ref, q_ref, k_ref, v_ref, r_ref, fg_ref, wfg_ref, bfg_ref, g_ref, o_ref,
                state_ref, *, head_k, head_v):
    rows = q_ref.shape[0]
    C = GLA_CHUNK

    @pl.when(pl.program_id(1) == 0)
    def _():
        state_ref[...] = jnp.zeros_like(state_ref)

    fh, fl = _split_bf16(fg_ref[...])
    wh, wl = _split_bf16(wfg_ref[...])
    z = _dot(fh, wh) + _dot(fl, wh) + _dot(fh, wl) + bfg_ref[...]
    log_a = (jnp.minimum(z, 0.0) - jnp.log(1.0 + jnp.exp(-jnp.abs(z)))) * (1.0 / GLA_TAU)

    ri = lax.broadcasted_iota(jnp.int32, (C, C), 0)
    ci = lax.broadcasted_iota(jnp.int32, (C, C), 1)
    causal = ri >= ci
    tri = causal.astype(BF16)
    scale = head_k ** -0.5

    for c in range(rows // C):
        rs = slice(c * C, (c + 1) * C)
        lh, ll = _split_bf16(log_a[rs])
        b = _dot(tri, lh) + _dot(tri, ll)
        b_last = b[C - 1:C]
        q = q_ref[rs, :].astype(F32) * scale
        k = k_ref[rs, :].astype(F32)
        q_dec = (q * jnp.exp(b)).astype(BF16)
        k_inv = (k * jnp.exp(-b)).astype(BF16)
        k_upd = (k * jnp.exp(b_last - b)).astype(BF16)
        decay = jnp.exp(b_last)
        for h in range(GLA_HEADS):
            ks = slice(h * head_k, (h + 1) * head_k)
            vs = slice(h * head_v, (h + 1) * head_v)
            v = v_ref[rs, vs]
            att = jnp.where(causal, _dot_nt(q_dec[:, ks], k_inv[:, ks]), 0.0).astype(BF16)
            st = state_ref[h]
            o = _dot(att, v) + _dot_nt(q_dec[:, ks], st.astype(BF16))
            state_ref[h] = st * decay[:, ks] + _dot_tn(v, k_upd[:, ks])
            o = o * lax.rsqrt(jnp.mean(o * o, axis=-1, keepdims=True) + EPS) * g_ref[:, vs]
            r = r_ref[rs, vs].astype(F32)
            o_ref[rs, vs] = (o * (r * _sigmoid(r))).astype(o_ref.dtype)


def _gla(proj, fg, l, wfg_pad, b_fg, g_gla_out, batch, seq):
    T = proj.shape[0]
    dk_all = wfg_pad.shape[-1]
    dv_all = g_gla_out.shape[-1]
    rows = GLA_ROWS
    nblk = seq // rows
    kern = functools.partial(_gla_kernel, head_k=dk_all // GLA_HEADS, head_v=dv_all // GLA_HEADS)
    row = lambda b, n, l: b * nblk + n
    call, lidx = _grid_call(
        kern, l, (batch, nblk),
        [pl.BlockSpec((rows, dk_all), lambda b, n, l: (row(b, n, l), 0)),
         pl.BlockSpec((rows, dk_all), lambda b, n, l: (row(b, n, l), 1)),
         pl.BlockSpec((rows, dv_all), lambda b, n, l: (row(b, n, l), 1)),
         pl.BlockSpec((rows, dv_all), lambda b, n, l: (row(b, n, l), 2)),
         pl.BlockSpec((rows, LANES), lambda b, n, l: (row(b, n, l), 0)),
         pl.BlockSpec((None, LANES, dk_all), lambda b, n, l: (l[0], 0, 0)),
         pl.BlockSpec((None, 1, dk_all), lambda b, n, l: (l[0], 0, 0)),
         pl.BlockSpec((None, 1, dv_all), lambda b, n, l: (l[0], 0, 0))],
        pl.BlockSpec((rows, dv_all), lambda b, n, l: (row(b, n, l), 0)),
        jax.ShapeDtypeStruct((T, dv_all), BF16),
        scratch=[pltpu.VMEM((GLA_HEADS, dv_all // GLA_HEADS, dk_all // GLA_HEADS), F32)],
        sem=("parallel", "arbitrary"), name="gla")
    return call(lidx, proj, proj, proj, proj, fg, wfg_pad, b_fg, g_gla_out)


def _qkprep_kernel(l_ref, q_ref, k_ref, v_ref, cos_ref, sin_ref, gq_ref, gk_ref, *refs):
    out_refs, scr_ref = refs[:-1], refs[-1]
    cos = cos_ref[...]
    sin = sin_ref[...]
    gw = ATT_HEADS * HEAD_DIM
    rows = q_ref.shape[0]

    def prep(ref, g, scale):
        x = ref.astype(F32)
        y = x * lax.rsqrt(jnp.mean(x * x, axis=-1, keepdims=True) + EPS) * g
        y = y * cos + pltpu.roll(y, HEAD_DIM // 2, 1) * sin
        return y * scale

    def emit(dst_ref, dil):
        n = rows // dil
        for p in range(dil):
            for h in range(ATT_HEADS):
                cs = slice(p * gw + h * HEAD_DIM, p * gw + (h + 1) * HEAD_DIM)
                dst_ref[:, cs] = scr_ref[h, pl.ds(p, n, stride=dil), :].astype(BF16)

    for grp, (_, dil) in enumerate(DIL_GROUPS):
        for ref, g_ref, scale, out in ((q_ref, gq_ref, HEAD_DIM ** -0.5, out_refs[grp]),
                                       (k_ref, gk_ref, 1.0, out_refs[N_DIL + grp])):
            for h in range(ATT_HEADS):
                src = slice(grp * gw + h * HEAD_DIM, grp * gw + (h + 1) * HEAD_DIM)
                y = prep(ref[:, src], g_ref[...], scale)
                if dil == 1:
                    out[:, h * HEAD_DIM:(h + 1) * HEAD_DIM] = y.astype(BF16)
                else:
                    scr_ref[h] = y
            if dil > 1:
                emit(out, dil)
        if dil == 1:
            out_refs[2 * N_DIL + grp][...] = v_ref[:, grp * gw:(grp + 1) * gw]
        else:
            for h in range(ATT_HEADS):
                src = slice(grp * gw + h * HEAD_DIM, grp * gw + (h + 1) * HEAD_DIM)
                scr_ref[h] = v_ref[:, src].astype(F32)
            emit(out_refs[2 * N_DIL + grp], dil)


def _qkprep(proj2, l, cos2, sin2, g_q, g_k):
    T = proj2.shape[0]
    rows = PREP_ROWS
    width = N_DIL * ATT_HEADS * HEAD_DIM
    gw = ATT_HEADS * HEAD_DIM
    dils = [dil for _, dil in DIL_GROUPS] * 3
    call, lidx = _grid_call(
        _qkprep_kernel, l, (T // rows,),
        [pl.BlockSpec((rows, width), lambda i, l: (i, 0)),
         pl.BlockSpec((rows, width), lambda i, l: (i, 1)),
         pl.BlockSpec((rows, width), lambda i, l: (i, 2)),
         pl.BlockSpec((rows, HEAD_DIM), lambda i, l: (i, 0)),
         pl.BlockSpec((rows, HEAD_DIM), lambda i, l: (i, 0)),
         pl.BlockSpec((None, 1, HEAD_DIM), lambda i, l: (l[0], 0, 0)),
         pl.BlockSpec((None, 1, HEAD_DIM), lambda i, l: (l[0], 0, 0))],
        [pl.BlockSpec((rows // d, d * gw), lambda i, l: (i, 0)) for d in dils],
        [jax.ShapeDtypeStruct((T // d, d * gw), BF16) for d in dils],
        scratch=[pltpu.VMEM((ATT_HEADS, rows, HEAD_DIM), F32)],
        sem=("parallel",), name="qk_prep")
    return call(lidx, proj2, proj2, proj2, cos2, sin2, g_q, g_k)


def _attn_kernel(q_ref, kc_ref, kp_ref, vc_ref, vp_ref, o_ref, lse_ref, *, dil):
    first = pl.program_id(1) == 0
    QB = q_ref.shape[0]
    A = ATT_BLOCK
    qi = lax.broadcasted_iota(jnp.int32, (A, 2 * A), 0)
    kj = lax.broadcasted_iota(jnp.int32, (A, 2 * A), 1) - A
    dist = qi - kj
    valid = (dist >= 0) & (dist <= A)
    bias = jnp.where(valid, 0.0, NEG_BIG).astype(F32)
    bias_first = jnp.where(first, jnp.where(valid & (kj >= 0), 0.0, NEG_BIG), bias).astype(F32)

    for p in range(dil):
        for h in range(ATT_HEADS):
            cs = slice((p * ATT_HEADS + h) * HEAD_DIM, (p * ATT_HEADS + h + 1) * HEAD_DIM)
            for j in range(QB // A):
                rs = slice(j * A, (j + 1) * A)
                if j == 0:
                    kcat = jnp.concatenate([kp_ref[:, cs], kc_ref[0:A, cs]], axis=0)
                    vcat = jnp.concatenate([vp_ref[:, cs], vc_ref[0:A, cs]], axis=0)
                    bb = bias_first
                else:
                    kcat = kc_ref[(j - 1) * A:(j + 1) * A, cs]
                    vcat = vc_ref[(j - 1) * A:(j + 1) * A, cs]
                    bb = bias
                s = _dot_nt(q_ref[rs, cs], kcat) + bb
                m = jnp.max(s, axis=-1, keepdims=True)
                e = jnp.exp(s - m)
                den = jnp.sum(e, axis=-1, keepdims=True)
                o = _dot(e.astype(BF16), vcat) / den
                o_ref[rs, cs] = o.astype(o_ref.dtype)
                lse_ref[rs, cs] = jnp.broadcast_to(m + jnp.log(den), (A, HEAD_DIM))


def _attn_group(q, k, v, dil, batch, seq):
    rows_all, W = q.shape
    A = ATT_BLOCK
    QB = ATT_SPAN // dil
    nsp = seq // ATT_SPAN
    per = QB // A
    cur = pl.BlockSpec((QB, W), lambda b, n: (b * nsp + n, 0))
    prev = pl.BlockSpec((A, W), lambda b, n: (jnp.maximum((b * nsp + n) * per - 1, b * nsp * per), 0))
    return pl.pallas_call(
        functools.partial(_attn_kernel, dil=dil),
        grid=(batch, nsp),
        in_specs=[cur, cur, prev, cur, prev],
        out_specs=[cur, cur],
        out_shape=[jax.ShapeDtypeStruct((rows_all, W), BF16),
                   jax.ShapeDtypeStruct((rows_all, W), F32)],
        compiler_params=_params("parallel", "arbitrary"),
        name=f"dilated_attn_r{dil}",
    )(q, k, k, v, v)


def _attn_mix_kernel(o0, o1, o2, l0, l1, l2, out_ref, so_ref, sl_ref):
    rows, gw = out_ref.shape
    o_refs, l_refs = (o0, o1, o2), (l0, l1, l2)
    for h in range(ATT_HEADS):
        hs = slice(h * HEAD_DIM, (h + 1) * HEAD_DIM)
        outs, lses = [], []
        for g, (_, dil) in enumerate(DIL_GROUPS):
            if dil == 1:
                outs.append(o_refs[g][:, hs].astype(F32))
                lses.append(l_refs[g][:, hs])
                continue
            n = rows // dil
            for p in range(dil):
                cs = slice(p * gw + h * HEAD_DIM, p * gw + (h + 1) * HEAD_DIM)
                so_ref[g, pl.ds(p, n, stride=dil), :] = o_refs[g][:, cs].astype(F32)
                sl_ref[g, pl.ds(p, n, stride=dil), :] = l_refs[g][:, cs]
            outs.append(so_ref[g])
            lses.append(sl_ref[g])
        m = jnp.maximum(jnp.maximum(lses[0], lses[1]), lses[2])
        ws = [jnp.exp(x - m) for x in lses]
        den = ws[0] + ws[1] + ws[2]
        out_ref[:, hs] = ((ws[0] * outs[0] + ws[1] * outs[1] + ws[2] * outs[2]) / den
                          ).astype(out_ref.dtype)


def _attn_mix(outs, lses):
    gw = ATT_HEADS * HEAD_DIM
    T = outs[0].shape[0]
    rows = PREP_ROWS
    dils = [dil for _, dil in DIL_GROUPS]
    specs = [pl.BlockSpec((rows // d, d * gw), lambda i: (i, 0)) for d in dils]
    return pl.pallas_call(
        _attn_mix_kernel,
        grid=(T // rows,),
        in_specs=specs + specs,
        out_specs=pl.BlockSpec((rows, gw), lambda i: (i, 0)),
        out_shape=jax.ShapeDtypeStruct((T, gw), BF16),
        scratch_shapes=[pltpu.VMEM((N_DIL, rows, HEAD_DIM), F32),
                        pltpu.VMEM((N_DIL, rows, HEAD_DIM), F32)],
        compiler_params=_params("parallel"),
        name="attn_mix",
    )(*outs, *lses)


def _merge_kernel(l_ref, ogla_ref, oatt_ref, gg_ref, ga_ref, wg_ref, wa_ref,
                  out_ref, wgbf_ref, wabf_ref):
    @pl.when(pl.program_id(1) == 0)
    def _():
        wgbf_ref[...] = wg_ref[...].astype(BF16)
        wabf_ref[...] = wa_ref[...].astype(BF16)

    a = _dot(ogla_ref[...], wgbf_ref[...])
    b = _dot(oatt_ref[...], wabf_ref[...])
    gg = _sigmoid(gg_ref[...].astype(F32))
    ga = _sigmoid(ga_ref[...].astype(F32))
    out_ref[...] = (gg * a + ga * b).astype(out_ref.dtype)


def _merge(o_gla, o_att, proj2, l, w_gla_proj, w_attn_proj, gate_col0):
    T, dv = o_gla.shape
    gw = o_att.shape[1]
    D = w_gla_proj.shape[-1]
    tm, tn = MERGE_ROWS, MM_TN
    gcb = gate_col0 // tn
    dcb = D // tn
    call, lidx = _grid_call(
        _merge_kernel, l, (D // tn, T // tm),
        [pl.BlockSpec((tm, dv), lambda j, i, l: (i, 0)),
         pl.BlockSpec((tm, gw), lambda j, i, l: (i, 0)),
         pl.BlockSpec((tm, tn), lambda j, i, l: (i, gcb + j)),
         pl.BlockSpec((tm, tn), lambda j, i, l: (i, gcb + dcb + j)),
         pl.BlockSpec((None, dv, tn), lambda j, i, l: (l[0], 0, j)),
         pl.BlockSpec((None, gw, tn), lambda j, i, l: (l[0], 0, j))],
        pl.BlockSpec((tm, tn), lambda j, i, l: (i, j)),
        jax.ShapeDtypeStruct((T, D), BF16),
        scratch=[pltpu.VMEM((dv, tn), BF16), pltpu.VMEM((gw, tn), BF16)], name="gated_merge")
    return call(lidx, o_gla, o_att, proj2, proj2, w_gla_proj, w_attn_proj)


def _router_kernel(l_ref, x_ref, g_ref, sh_ref, sc_ref, wr_ref, br_ref, h_ref, route_ref):
    h = _ada_norm(x_ref[...], g_ref[...], sc_ref[...], sh_ref[...])
    h_ref[...] = h
    lg = _dot(h.astype(BF16), wr_ref[...].astype(BF16)) + br_ref[...]
    lane = lax.broadcasted_iota(jnp.int32, lg.shape, 1).astype(F32)
    big = float(4 * LANES)

    def first_argmax(vals):
        mx = jnp.max(vals, axis=-1, keepdims=True)
        idx = jnp.min(jnp.where(vals == mx, lane, big), axis=-1, keepdims=True)
        return mx, idx

    gl = jnp.where(lane < N_GROUPS, lg, NEG_BIG)
    gmax, gidx = first_argmax(gl)
    g_weight = 1.0 / jnp.sum(jnp.exp(gl - gmax), axis=-1, keepdims=True)
    lo = N_GROUPS + gidx * EXPERTS_PER_GROUP
    el = jnp.where((lane >= lo) & (lane < lo + EXPERTS_PER_GROUP), lg, NEG_BIG)
    m1, i1 = first_argmax(el)
    m2, i2 = first_argmax(jnp.where(lane == i1, NEG_BIG, el))
    e2 = jnp.exp(m2 - m1)
    w1 = g_weight / (1.0 + e2)
    w2 = g_weight * e2 / (1.0 + e2)
    route = jnp.where(lane == 0, i1 - N_GROUPS,
                      jnp.where(lane == 1, i2 - N_GROUPS,
                                jnp.where(lane == 2, w1, jnp.where(lane == 3, w2, 0.0))))
    route_ref[...] = route


def _router(x, l, g_all, mod3, wr_pad, br_pad, seq):
    T, D = x.shape
    rows = NORM_ROWS
    call, lidx = _grid_call(
        _router_kernel, l, (T // rows,),
        [pl.BlockSpec((rows, D), lambda i, l: (i, 0)),
         pl.BlockSpec((None, 1, D), lambda i, l: (l[0], 0, 0)),
         pl.BlockSpec((None, 1, D), _mod_spec(3, rows, seq)),
         pl.BlockSpec((None, 1, D), _mod_spec(4, rows, seq)),
         pl.BlockSpec((None, D, LANES), lambda i, l: (l[0], 0, 0)),
         pl.BlockSpec((None, 1, LANES), lambda i, l: (l[0], 0, 0))],
        [pl.BlockSpec((rows, D), lambda i, l: (i, 0)),
         pl.BlockSpec((rows, LANES), lambda i, l: (i, 0))],
        [jax.ShapeDtypeStruct((T, D), F32), jax.ShapeDtypeStruct((T, LANES), F32)],
        sem=("parallel",), name="norm_router")
    return call(lidx, x, g_all, mod3, mod3, wr_pad, br_pad)


def _expert_kernel(meta_ref, src_cur_ref, src_nxt_ref, dst_prev_ref, dst_cur_ref, h_ref,
                   wg_ref, wu_ref, wd_ref, ys_ref, xbuf_ref, xb_ref, obuf_ref,
                   wgbf_ref, wubf_ref, wdbf_ref, gsem_ref, ssem_ref):
    i = pl.program_id(0)
    last = pl.num_programs(0) - 1
    slot = i % 2
    other = 1 - slot
    tm = xbuf_ref.shape[1]

    def gather_copy(idx_ref, r, s):
        return pltpu.make_async_copy(h_ref.at[pl.ds(idx_ref[0, 0, r], 1)],
                                     xbuf_ref.at[s, pl.ds(r, 1)], gsem_ref.at[s])

    def scatter_copy(idx_ref, r, s):
        return pltpu.make_async_copy(obuf_ref.at[s, pl.ds(r, 1)],
                                     ys_ref.at[pl.ds(idx_ref[0, 0, r], 1)], ssem_ref.at[s])

    def gather_wait(s):
        pltpu.make_async_copy(h_ref.at[pl.ds(0, tm)], xbuf_ref.at[s], gsem_ref.at[s]).wait()

    def scatter_wait(s):
        pltpu.make_async_copy(obuf_ref.at[s], ys_ref.at[pl.ds(0, tm)], ssem_ref.at[s]).wait()

    @pl.when(i == 0)
    def _():
        def body(r, c):
            gather_copy(src_cur_ref, r, 0).start()
            return c
        lax.fori_loop(0, tm, body, 0, unroll=8)
        obuf_ref[1] = jnp.zeros((tm, obuf_ref.shape[2]), F32)
        fill = pltpu.make_async_copy(obuf_ref.at[1], ys_ref.at[pl.ds(ys_ref.shape[0] - 2 * tm, tm)],
                                     ssem_ref.at[0])
        fill.start()
        fill.wait()

    gather_wait(slot)
    xb_ref[...] = xbuf_ref[slot].astype(BF16)

    @pl.when(i >= 1)
    def _():
        scatter_wait(slot)

    @pl.when((i == 0) | (meta_ref[1 + i] != meta_ref[1 + jnp.maximum(i - 1, 0)]))
    def _():
        wgbf_ref[...] = wg_ref[...].astype(BF16)
        wubf_ref[...] = wu_ref[...].astype(BF16)
        wdbf_ref[...] = wd_ref[...].astype(BF16)

    for r in range(tm):
        gather_copy(src_nxt_ref, r, other).start(priority=r % 2)
    x = xb_ref[...]
    a = _dot(x, wgbf_ref[...])
    u = _dot(x, wubf_ref[...])
    hid = (a * _sigmoid(a)) * u
    obuf_ref[slot] = _dot(hid.astype(BF16), wdbf_ref[...])
    for r in range(tm):
        scatter_copy(dst_prev_ref, r, other).start(priority=r % 2)

    @pl.when(i == last)
    def _():
        def body(r, c):
            scatter_copy(dst_cur_ref, r, slot).start()
            return c
        lax.fori_loop(0, tm, body, 0, unroll=8)
        gather_wait(other)
        scatter_wait(other)
        scatter_wait(slot)


def _experts(h, src, dst, meta, w_gate, w_up, w_down, n_out):
    T, D = h.shape
    F = w_gate.shape[-1]
    n_tiles, _, tm = src.shape
    wspec = lambda shape: pl.BlockSpec((None, None) + shape, lambda i, m: (m[0], m[1 + i], 0, 0))
    idx_spec = lambda f: pl.BlockSpec((1, 1, tm), lambda i, m: (f(i), 0, 0), memory_space=pltpu.SMEM)
    return pl.pallas_call(
        _expert_kernel,
        grid_spec=pltpu.PrefetchScalarGridSpec(
            num_scalar_prefetch=1, grid=(n_tiles,),
            in_specs=[idx_spec(lambda i: i), idx_spec(lambda i: jnp.minimum(i + 1, n_tiles - 1)),
                      idx_spec(lambda i: i), idx_spec(lambda i: i + 1),
                      pl.BlockSpec(memory_space=pl.ANY),
                      wspec((D, F)), wspec((D, F)), wspec((F, D))],
            out_specs=pl.BlockSpec(memory_space=pl.ANY),
            scratch_shapes=[pltpu.VMEM((2, tm, D), F32), pltpu.VMEM((tm, D), BF16),
                            pltpu.VMEM((2, tm, D), F32),
                            pltpu.VMEM((D, F), BF16), pltpu.VMEM((D, F), BF16),
                            pltpu.VMEM((F, D), BF16),
                            pltpu.SemaphoreType.DMA((2,)), pltpu.SemaphoreType.DMA((2,))]),
        out_shape=jax.ShapeDtypeStruct((n_out, D), F32),
        compiler_params=_params("arbitrary"),
        name="grouped_experts",
    )(meta, src, src, dst, dst, h, w_gate, w_up, w_down)


def _combine_kernel(x_ref, gt_ref, route_ref, y0_ref, y1_ref, o_ref):
    w1 = route_ref[:, 2:3]
    w2 = route_ref[:, 3:4]
    o_ref[...] = x_ref[...] + gt_ref[...] * (w1 * y0_ref[...] + w2 * y1_ref[...])


def _combine(x, ys, route, mod3, seq):
    T, D = x.shape
    rows = COMBINE_ROWS
    nblk = T // rows
    return pl.pallas_call(
        _combine_kernel,
        grid=(nblk,),
        in_specs=[pl.BlockSpec((rows, D), lambda i: (i, 0)),
                  pl.BlockSpec((None, 1, D), lambda i: (i * rows // seq, 0, 5)),
                  pl.BlockSpec((rows, LANES), lambda i: (i, 0)),
                  pl.BlockSpec((rows, D), lambda i: (i, 0)),
                  pl.BlockSpec((rows, D), lambda i: (nblk + i, 0))],
        out_specs=pl.BlockSpec((rows, D), lambda i: (i, 0)),
        out_shape=jax.ShapeDtypeStruct((T, D), F32),
        compiler_params=_params("parallel"),
        name="moe_combine",
    )(x, mod3, route, ys, ys)


def _dispatch_plan(route, l):
    T = route.shape[0]
    tm = EXPERT_TM
    n_tiles = (2 * T) // tm + N_EXPERTS
    P = n_tiles * tm
    e = route[:, 0:2].astype(jnp.int32).T.reshape(-1)
    onehot = (e[:, None] == jnp.arange(N_EXPERTS, dtype=jnp.int32)[None, :]).astype(jnp.int32)
    csum = jnp.cumsum(onehot, axis=0)
    rank = jnp.sum((csum - onehot) * onehot, axis=1)
    counts = csum[-1]
    tiles_per = (counts + tm - 1) // tm
    tile_end = jnp.cumsum(tiles_per)
    off = (tile_end - tiles_per) * tm
    pos = off[e] + rank
    asg = jnp.full((P,), -1, jnp.int32).at[pos].set(jnp.arange(2 * T, dtype=jnp.int32))
    slot = jnp.arange(P, dtype=jnp.int32)
    spare = 2 * T + ((slot // tm) % 2) * tm + slot % tm
    src = jnp.where(asg >= 0, asg % T, 0).reshape(n_tiles, 1, tm)
    dst = jnp.where(asg >= 0, asg, spare)
    standin = 2 * T + tm + jnp.arange(tm, dtype=jnp.int32)
    dst = jnp.concatenate([standin, dst]).reshape(n_tiles + 1, 1, tm)
    tile_ids = jnp.arange(n_tiles, dtype=jnp.int32)
    tile_expert = jnp.minimum(jnp.sum(tile_ids[:, None] >= tile_end[None, :], axis=1),
                              N_EXPERTS - 1).astype(jnp.int32)
    meta = jnp.concatenate([l.reshape(1), tile_expert])
    return src, dst, meta


def kernel(x, c, positions, w_ada, b_ada, g_norm_mix, g_norm_ffn, w_in, w_fg, b_fg, g_gla_out,
           g_q, g_k, w_gla_proj, w_attn_proj, w_out, w_route_group, b_route_group,
           w_route_expert, b_route_expert, w_exp_gate, w_exp_up, w_exp_down):
    B, S, D = x.shape
    L = w_ada.shape[0]
    T = B * S
    dk_all = w_fg.shape[-1]
    dv_all = g_gla_out.shape[-1]
    att_w = N_DIL * ATT_HEADS * HEAD_DIM
    gla_cols = 2 * dk_all + 2 * dv_all

    mod = _modulation(c, w_ada, b_ada)
    cos2, sin2 = _rope_tables(positions)

    wfg_pad = jnp.zeros((L, LANES, dk_all), F32).at[:, :GLA_RANK].set(w_fg)
    wr_pad = jnp.zeros((L, D, LANES), F32)
    wr_pad = wr_pad.at[:, :, :N_GROUPS].set(w_route_group)
    wr_pad = wr_pad.at[:, :, N_GROUPS:N_GROUPS + N_EXPERTS].set(w_route_expert)
    br_pad = jnp.zeros((L, 1, LANES), F32)
    br_pad = br_pad.at[:, 0, :N_GROUPS].set(b_route_group)
    br_pad = br_pad.at[:, 0, N_GROUPS:N_GROUPS + N_EXPERTS].set(b_route_expert)
    g_mix = g_norm_mix.reshape(L, 1, D)
    g_ffn = g_norm_ffn.reshape(L, 1, D)
    b_fg3 = b_fg.reshape(L, 1, dk_all)
    g_gla3 = g_gla_out.reshape(L, 1, dv_all)
    g_q3 = g_q.reshape(L, 1, HEAD_DIM)
    g_k3 = g_k.reshape(L, 1, HEAD_DIM)

    def residual(acc, xres, gt):
        return xres + gt * acc

    def layer(l, xt):
        l = jnp.asarray(l, jnp.int32)
        mod3 = lax.dynamic_index_in_dim(mod, l, 0, keepdims=False).reshape(8, 1, N_MOD * D)
        h = _norm(xt, l, g_mix, mod3, 0, S)
        proj1 = _matmul(h, w_in, l, col0=0, n=gla_cols, out_dtype=BF16, tn=2 * MM_TN,
                        tm=2 * MM_TM, name="proj_gla")
        fg = _matmul(h, w_in, l, col0=gla_cols, n=LANES, tn=LANES, out_dtype=F32, name="proj_fg")
        proj2 = _matmul(h, w_in, l, col0=gla_cols, shift=GLA_RANK, n=3 * att_w + 2 * D,
                        out_dtype=BF16, tm=2 * MM_TM, name="proj_att")
        o_gla = _gla(proj1, fg, l, wfg_pad, b_fg3, g_gla3, B, S)
        prep = _qkprep(proj2, l, cos2, sin2, g_q3, g_k3)
        outs, lses = [], []
        for grp, (_, dil) in enumerate(DIL_GROUPS):
            o, lse = _attn_group(prep[grp], prep[N_DIL + grp], prep[2 * N_DIL + grp], dil, B, S)
            outs.append(o)
            lses.append(lse)
        o_att = _attn_mix(outs, lses)
        merged = _merge(o_gla, o_att, proj2, l, w_gla_proj, w_attn_proj, 3 * att_w)
        gt_spec = pl.BlockSpec((None, 1, MM_TN),
                               lambda j, i, l: (i * MM_TM // S, 0, 2 * (D // MM_TN) + j))
        xt = _matmul(merged, w_out, l, col0=0, n=D, out_dtype=F32, epilogue=residual,
                     extras=(xt, mod3),
                     extra_specs=(pl.BlockSpec((MM_TM, MM_TN), lambda j, i, l: (i, j)), gt_spec),
                     name="out_proj")
        h2, route = _router(xt, l, g_ffn, mod3, wr_pad, br_pad, S)
        src, dst, meta = _dispatch_plan(route, l)
        ys = _experts(h2, src, dst, meta, w_exp_gate, w_exp_up, w_exp_down, 2 * T + 2 * EXPERT_TM)
        return _combine(xt, ys, route, mod3, S)

    xt = lax.fori_loop(0, L, layer, x.reshape(T, D))
    return xt.reshape(B, S, D)
```

```python
import functools

import jax
import jax.numpy as jnp
from jax import lax
from jax.experimental import pallas as pl
from jax.experimental.pallas import tpu as pltpu

F32 = jnp.float32
BF16 = jnp.bfloat16

GLA_HEADS = 4
GLA_RANK = 16
GLA_TAU = 16.0
GLA_CHUNK = 64
DIL_GROUPS = ((128, 1), (512, 4), (2048, 16))
N_DIL = 3
ATT_HEADS = 4
HEAD_DIM = 128
ATT_BLOCK = 128
ROPE_THETA = 10000.0
N_GROUPS = 4
EXPERTS_PER_GROUP = 8
N_EXPERTS = N_GROUPS * EXPERTS_PER_GROUP
N_MOD = 6
EPS = 1e-6

LANES = 128
VMEM_LIMIT = 56 * 1024 * 1024
NEG_BIG = -1e30

NORM_ROWS = 512
MM_TM = 1024
MM_TN = 512
GLA_ROWS = 256
PREP_ROWS = 512
ATT_SPAN = 2048
MERGE_ROWS = 1024
EXPERT_TM = 256
COMBINE_ROWS = 256


def _params(*sem):
    return pltpu.CompilerParams(dimension_semantics=sem, vmem_limit_bytes=VMEM_LIMIT)


def _dot(a, b):
    return jnp.dot(a, b, preferred_element_type=F32)


def _dot_nt(a, b):
    return lax.dot_general(a, b, (((1,), (1,)), ((), ())), preferred_element_type=F32)


def _dot_tn(a, b):
    return lax.dot_general(a, b, (((0,), (0,)), ((), ())), preferred_element_type=F32)


def _split_bf16(x):
    hi = x.astype(BF16)
    lo = (x - hi.astype(F32)).astype(BF16)
    return hi, lo


def _sigmoid(x):
    return 1.0 / (1.0 + jnp.exp(-x))


def _grid_call(kernel, l, grid, in_specs, out_specs, out_shape, scratch=(), sem=None, name=None):
    sem = sem or ("arbitrary",) * len(grid)
    return pl.pallas_call(
        kernel,
        grid_spec=pltpu.PrefetchScalarGridSpec(
            num_scalar_prefetch=1, grid=grid, in_specs=in_specs, out_specs=out_specs,
            scratch_shapes=list(scratch)),
        out_shape=out_shape,
        compiler_params=_params(*sem),
        name=name,
    ), jnp.reshape(l, (1,)).astype(jnp.int32)


def _mod_kernel(c_ref, w_ref, b_ref, o_ref):
    c = c_ref[...]
    a = (c * _sigmoid(c)).astype(BF16)
    o_ref[...] = _dot(a, w_ref[...].astype(BF16)) + b_ref[...]


def _modulation(c, w_ada, b_ada):
    L, D, N = w_ada.shape
    B = c.shape[0]
    rows = 8
    c_pad = jnp.zeros((rows, D), F32).at[:B].set(c)
    tn = 1024
    return pl.pallas_call(
        _mod_kernel,
        grid=(L, N // tn),
        in_specs=[pl.BlockSpec((rows, D), lambda l, j: (0, 0)),
                  pl.BlockSpec((None, D, tn), lambda l, j: (l, 0, j)),
                  pl.BlockSpec((None, 1, tn), lambda l, j: (l, 0, j))],
        out_specs=pl.BlockSpec((None, rows, tn), lambda l, j: (l, 0, j)),
        out_shape=jax.ShapeDtypeStruct((L, rows, N), F32),
        compiler_params=_params("parallel", "parallel"),
        name="adaln_mod",
    )(c_pad, w_ada, b_ada.reshape(L, 1, N))


def _rope_kernel(pos_ref, freq_ref, cos_ref, sin_ref):
    ang = pos_ref[...].astype(F32) * freq_ref[...]
    lane = lax.broadcasted_iota(jnp.int32, ang.shape, 1)
    cos_ref[...] = jnp.cos(ang)
    sin_ref[...] = jnp.where(lane < HEAD_DIM // 2, -jnp.sin(ang), jnp.sin(ang))


def _rope_tables(positions):
    T = positions.size
    inv_freq = ROPE_THETA ** (-jnp.arange(0, HEAD_DIM, 2, dtype=F32) / HEAD_DIM)
    freq2 = jnp.concatenate([inv_freq, inv_freq]).reshape(1, HEAD_DIM)
    rows = 2048
    return pl.pallas_call(
        _rope_kernel,
        grid=(T // rows,),
        in_specs=[pl.BlockSpec((rows, 1), lambda i: (i, 0)),
                  pl.BlockSpec((1, HEAD_DIM), lambda i: (0, 0))],
        out_specs=[pl.BlockSpec((rows, HEAD_DIM), lambda i: (i, 0))] * 2,
        out_shape=[jax.ShapeDtypeStruct((T, HEAD_DIM), F32)] * 2,
        compiler_params=_params("parallel"),
        name="rope_tables",
    )(positions.reshape(T, 1), freq2)


def _ada_norm(x, g, scale, shift):
    y = x * lax.rsqrt(jnp.mean(x * x, axis=-1, keepdims=True) + EPS)
    return y * g * (1.0 + scale) + shift


def _norm_kernel(l_ref, x_ref, g_ref, sh_ref, sc_ref, o_ref):
    o_ref[...] = _ada_norm(x_ref[...], g_ref[...], sc_ref[...], sh_ref[...]).astype(o_ref.dtype)


def _mod_spec(which, rows, seq):
    return lambda i, l: (i * rows // seq, 0, which)


def _norm(x, l, g_all, mod3, which_shift, seq):
    T, D = x.shape
    rows = NORM_ROWS
    call, lidx = _grid_call(
        _norm_kernel, l, (T // rows,),
        [pl.BlockSpec((rows, D), lambda i, l: (i, 0)),
         pl.BlockSpec((None, 1, D), lambda i, l: (l[0], 0, 0)),
         pl.BlockSpec((None, 1, D), _mod_spec(which_shift, rows, seq)),
         pl.BlockSpec((None, 1, D), _mod_spec(which_shift + 1, rows, seq))],
        pl.BlockSpec((rows, D), lambda i, l: (i, 0)),
        jax.ShapeDtypeStruct((T, D), BF16), sem=("parallel",), name="ada_norm")
    return call(lidx, x, g_all, mod3, mod3)


def _mm_kernel(l_ref, a_ref, w_ref, *rest, epilogue, n_extra, shift):
    n_w = 1 if shift else 0
    extra = rest[n_w:n_w + n_extra]
    o_ref = rest[n_w + n_extra]
    wbf_ref = rest[n_w + n_extra + 1]
    tn = o_ref.shape[1]

    @pl.when(pl.program_id(1) == 0)
    def _():
        if shift:
            wide = jnp.concatenate([w_ref[...], rest[0][...]], axis=1)
            wbf_ref[...] = wide[:, shift:shift + tn].astype(BF16)
        else:
            wbf_ref[...] = w_ref[...].astype(BF16)

    acc = _dot(a_ref[...], wbf_ref[...])
    if epilogue is not None:
        acc = epilogue(acc, *[e[...] for e in extra])
    o_ref[...] = acc.astype(o_ref.dtype)


def _matmul(a, w, l, *, col0, n, out_dtype, tn=MM_TN, tm=MM_TM, shift=0, epilogue=None,
            extras=(), extra_specs=(), name="matmul"):
    M, K = a.shape
    assert col0 % tn == 0 and n % tn == 0 and M % tm == 0 and 0 <= shift < LANES
    cb0 = col0 // tn
    kern = functools.partial(_mm_kernel, epilogue=epilogue, n_extra=len(extras), shift=shift)
    w_specs = [pl.BlockSpec((None, K, tn), lambda j, i, l: (l[0], 0, cb0 + j))]
    if shift:
        per = tn // LANES
        w_specs.append(pl.BlockSpec((None, K, LANES), lambda j, i, l: (l[0], 0, (cb0 + j + 1) * per)))
    call, lidx = _grid_call(
        kern, l, (n // tn, M // tm),
        [pl.BlockSpec((tm, K), lambda j, i, l: (i, 0))] + w_specs + list(extra_specs),
        pl.BlockSpec((tm, tn), lambda j, i, l: (i, j)),
        jax.ShapeDtypeStruct((M, n), out_dtype),
        scratch=[pltpu.VMEM((K, tn), BF16)], name=name)
    return call(lidx, a, *([w] * len(w_specs)), *extras)


def _gla_kernel(l_ref, q_ref, k_ref, v_ref, r_ref, fg_ref, wfg_ref, bfg_ref, g_ref, o_ref,
                state_ref, *, head_k, head_v):
    rows = q_ref.shape[0]
    C = GLA_CHUNK

    @pl.when(pl.program_id(1) == 0)
    def _():
        state_ref[...] = jnp.zeros_like(state_ref)

    fh, fl = _split_bf16(fg_ref[...])
    wh, wl = _split_bf16(wfg_ref[...])
    z = _dot(fh, wh) + _dot(fl, wh) + _dot(fh, wl) + bfg_ref[...]
    log_a = (jnp.minimum(z, 0.0) - jnp.log(1.0 + jnp.exp(-jnp.abs(z)))) * (1.0 / GLA_TAU)

    ri = lax.broadcasted_iota(jnp.int32, (C, C), 0)
    ci = lax.broadcasted_iota(jnp.int32, (C, C), 1)
    causal = ri >= ci
    tri = causal.astype(BF16)
    scale = head_k ** -0.5

    for c in range(rows // C):
        rs = slice(c * C, (c + 1) * C)
        lh, ll = _split_bf16(log_a[rs])
        b = _dot(tri, lh) + _dot(tri, ll)
        b_last = b[C - 1:C]
        q = q_ref[rs, :].astype(F32) * scale
        k = k_ref[rs, :].astype(F32)
        q_dec = (q * jnp.exp(b)).astype(BF16)
        k_inv = (k * jnp.exp(-b)).astype(BF16)
        k_upd = (k * jnp.exp(b_last - b)).astype(BF16)
        decay = jnp.exp(b_last)
        for h in range(GLA_HEADS):
            ks = slice(h * head_k, (h + 1) * head_k)
            vs = slice(h * head_v, (h + 1) * head_v)
            v = v_ref[rs, vs]
            att = jnp.where(causal, _dot_nt(q_dec[:, ks], k_inv[:, ks]), 0.0).astype(BF16)
            st = state_ref[h]
            o = _dot(att, v) + _dot_nt(q_dec[:, ks], st.astype(BF16))
            state_ref[h] = st * decay[:, ks] + _dot_tn(v, k_upd[:, ks])
            o = o * lax.rsqrt(jnp.mean(o * o, axis=-1, keepdims=True) + EPS) * g_ref[:, vs]
            r = r_ref[rs, vs].astype(F32)
            o_ref[rs, vs] = (o * (r * _sigmoid(r))).astype(o_ref.dtype)


def _gla(proj, fg, l, wfg_pad, b_fg, g_gla_out, batch, seq):
    T = proj.shape[0]
    dk_all = wfg_pad.shape[-1]
    dv_all = g_gla_out.shape[-1]
    rows = GLA_ROWS
    nblk = seq // rows
    kern = functools.partial(_gla_kernel, head_k=dk_all // GLA_HEADS, head_v=dv_all // GLA_HEADS)
    row = lambda b, n, l: b * nblk + n
    call, lidx = _grid_call(
        kern, l, (batch, nblk),
        [pl.BlockSpec((rows, dk_all), lambda b, n, l: (row(b, n, l), 0)),
         pl.BlockSpec((rows, dk_all), lambda b, n, l: (row(b, n, l), 1)),
         pl.BlockSpec((rows, dv_all), lambda b, n, l: (row(b, n, l), 1)),
         pl.BlockSpec((rows, dv_all), lambda b, n, l: (row(b, n, l), 2)),
         pl.BlockSpec((rows, LANES), lambda b, n, l: (row(b, n, l), 0)),
         pl.BlockSpec((None, LANES, dk_all), lambda b, n, l: (l[0], 0, 0)),
         pl.BlockSpec((None, 1, dk_all), lambda b, n, l: (l[0], 0, 0)),
         pl.BlockSpec((None, 1, dv_all), lambda b, n, l: (l[0], 0, 0))],
        pl.BlockSpec((rows, dv_all), lambda b, n, l: (row(b, n, l), 0)),
        jax.ShapeDtypeStruct((T, dv_all), BF16),
        scratch=[pltpu.VMEM((GLA_HEADS, dv_all // GLA_HEADS, dk_all // GLA_HEADS), F32)],
        sem=("parallel", "arbitrary"), name="gla")
    return call(lidx, proj, proj, proj, proj, fg, wfg_pad, b_fg, g_gla_out)


def _qkprep_kernel(l_ref, q_ref, k_ref, v_ref, cos_ref, sin_ref, gq_ref, gk_ref, *refs):
    out_refs, scr_ref = refs[:-1], refs[-1]
    cos = cos_ref[...]
    sin = sin_ref[...]
    gw = ATT_HEADS * HEAD_DIM
    rows = q_ref.shape[0]

    def prep(ref, g, scale):
        x = ref.astype(F32)
        y = x * lax.rsqrt(jnp.mean(x * x, axis=-1, keepdims=True) + EPS) * g
        y = y * cos + pltpu.roll(y, HEAD_DIM // 2, 1) * sin
        return y * scale

    def emit(dst_ref, dil):
        n = rows // dil
        for p in range(dil):
            for h in range(ATT_HEADS):
                cs = slice(p * gw + h * HEAD_DIM, p * gw + (h + 1) * HEAD_DIM)
                dst_ref[:, cs] = scr_ref[h, pl.ds(p, n, stride=dil), :].astype(BF16)

    for grp, (_, dil) in enumerate(DIL_GROUPS):
        for ref, g_ref, scale, out in ((q_ref, gq_ref, HEAD_DIM ** -0.5, out_refs[grp]),
                                       (k_ref, gk_ref, 1.0, out_refs[N_DIL + grp])):
            for h in range(ATT_HEADS):
                src = slice(grp * gw + h * HEAD_DIM, grp * gw + (h + 1) * HEAD_DIM)
                y = prep(ref[:, src], g_ref[...], scale)
                if dil == 1:
                    out[:, h * HEAD_DIM:(h + 1) * HEAD_DIM] = y.astype(BF16)
                else:
                    scr_ref[h] = y
            if dil > 1:
                emit(out, dil)
        if dil == 1:
            out_refs[2 * N_DIL + grp][...] = v_ref[:, grp * gw:(grp + 1) * gw]
        else:
            for h in range(ATT_HEADS):
                src = slice(grp * gw + h * HEAD_DIM, grp * gw + (h + 1) * HEAD_DIM)
                scr_ref[h] = v_ref[:, src].astype(F32)
            emit(out_refs[2 * N_DIL + grp], dil)


def _qkprep(proj2, l, cos2, sin2, g_q, g_k):
    T = proj2.shape[0]
    rows = PREP_ROWS
    width = N_DIL * ATT_HEADS * HEAD_DIM
    gw = ATT_HEADS * HEAD_DIM
    dils = [dil for _, dil in DIL_GROUPS] * 3
    call, lidx = _grid_call(
        _qkprep_kernel, l, (T // rows,),
        [pl.BlockSpec((rows, width), lambda i, l: (i, 0)),
         pl.BlockSpec((rows, width), lambda i, l: (i, 1)),
         pl.BlockSpec((rows, width), lambda i, l: (i, 2)),
         pl.BlockSpec((rows, HEAD_DIM), lambda i, l: (i, 0)),
         pl.BlockSpec((rows, HEAD_DIM), lambda i, l: (i, 0)),
         pl.BlockSpec((None, 1, HEAD_DIM), lambda i, l: (l[0], 0, 0)),
         pl.BlockSpec((None, 1, HEAD_DIM), lambda i, l: (l[0], 0, 0))],
        [pl.BlockSpec((rows // d, d * gw), lambda i, l: (i, 0)) for d in dils],
        [jax.ShapeDtypeStruct((T // d, d * gw), BF16) for d in dils],
        scratch=[pltpu.VMEM((ATT_HEADS, rows, HEAD_DIM), F32)],
        sem=("parallel",), name="qk_prep")
    return call(lidx, proj2, proj2, proj2, cos2, sin2, g_q, g_k)


def _attn_kernel(q_ref, kc_ref, kp_ref, vc_ref, vp_ref, o_ref, lse_ref, *, dil):
    first = pl.program_id(1) == 0
    QB = q_ref.shape[0]
    A = ATT_BLOCK
    qi = lax.broadcasted_iota(jnp.int32, (A, 2 * A), 0)
    kj = lax.broadcasted_iota(jnp.int32, (A, 2 * A), 1) - A
    dist = qi - kj
    valid = (dist >= 0) & (dist <= A)
    bias = jnp.where(valid, 0.0, NEG_BIG).astype(F32)
    bias_first = jnp.where(first, jnp.where(valid & (kj >= 0), 0.0, NEG_BIG), bias).astype(F32)

    for p in range(dil):
        for h in range(ATT_HEADS):
            cs = slice((p * ATT_HEADS + h) * HEAD_DIM, (p * ATT_HEADS + h + 1) * HEAD_DIM)
            for j in range(QB // A):
                rs = slice(j * A, (j + 1) * A)
                if j == 0:
                    kcat = jnp.concatenate([kp_ref[:, cs], kc_ref[0:A, cs]], axis=0)
                    vcat = jnp.concatenate([vp_ref[:, cs], vc_ref[0:A, cs]], axis=0)
                    bb = bias_first
                else:
                    kcat = kc_ref[(j - 1) * A:(j + 1) * A, cs]
                    vcat = vc_ref[(j - 1) * A:(j + 1) * A, cs]
                    bb = bias
                s = _dot_nt(q_ref[rs, cs], kcat) + bb
                m = jnp.max(s, axis=-1, keepdims=True)
                e = jnp.exp(s - m)
                den = jnp.sum(e, axis=-1, keepdims=True)
                o = _dot(e.astype(BF16), vcat) / den
                o_ref[rs, cs] = o.astype(o_ref.dtype)
                lse_ref[rs, cs] = jnp.broadcast_to(m + jnp.log(den), (A, HEAD_DIM))


def _attn_group(q, k, v, dil, batch, seq):
    rows_all, W = q.shape
    A = ATT_BLOCK
    QB = ATT_SPAN // dil
    nsp = seq // ATT_SPAN
    per = QB // A
    cur = pl.BlockSpec((QB, W), lambda b, n: (b * nsp + n, 0))
    prev = pl.BlockSpec((A, W), lambda b, n: (jnp.maximum((b * nsp + n) * per - 1, b * nsp * per), 0))
    return pl.pallas_call(
        functools.partial(_attn_kernel, dil=dil),
        grid=(batch, nsp),
        in_specs=[cur, cur, prev, cur, prev],
        out_specs=[cur, cur],
        out_shape=[jax.ShapeDtypeStruct((rows_all, W), BF16),
                   jax.ShapeDtypeStruct((rows_all, W), F32)],
        compiler_params=_params("parallel", "arbitrary"),
        name=f"dilated_attn_r{dil}",
    )(q, k, k, v, v)


def _attn_mix_kernel(o0, o1, o2, l0, l1, l2, out_ref, so_ref, sl_ref):
    rows, gw = out_ref.shape
    o_refs, l_refs = (o0, o1, o2), (l0, l1, l2)
    for h in range(ATT_HEADS):
        hs = slice(h * HEAD_DIM, (h + 1) * HEAD_DIM)
        outs, lses = [], []
        for g, (_, dil) in enumerate(DIL_GROUPS):
            if dil == 1:
                outs.append(o_refs[g][:, hs].astype(F32))
                lses.append(l_refs[g][:, hs])
                continue
            n = rows // dil
            for p in range(dil):
                cs = slice(p * gw + h * HEAD_DIM, p * gw + (h + 1) * HEAD_DIM)
                so_ref[g, pl.ds(p, n, stride=dil), :] = o_refs[g][:, cs].astype(F32)
                sl_ref[g, pl.ds(p, n, stride=dil), :] = l_refs[g][:, cs]
            outs.append(so_ref[g])
            lses.append(sl_ref[g])
        m = jnp.maximum(jnp.maximum(lses[0], lses[1]), lses[2])
        ws = [jnp.exp(x - m) for x in lses]
        den = ws[0] + ws[1] + ws[2]
        out_ref[:, hs] = ((ws[0] * outs[0] + ws[1] * outs[1] + ws[2] * outs[2]) / den
                          ).astype(out_ref.dtype)


def _attn_mix(outs, lses):
    gw = ATT_HEADS * HEAD_DIM
    T = outs[0].shape[0]
    rows = PREP_ROWS
    dils = [dil for _, dil in DIL_GROUPS]
    specs = [pl.BlockSpec((rows // d, d * gw), lambda i: (i, 0)) for d in dils]
    return pl.pallas_call(
        _attn_mix_kernel,
        grid=(T // rows,),
        in_specs=specs + specs,
        out_specs=pl.BlockSpec((rows, gw), lambda i: (i, 0)),
        out_shape=jax.ShapeDtypeStruct((T, gw), BF16),
        scratch_shapes=[pltpu.VMEM((N_DIL, rows, HEAD_DIM), F32),
                        pltpu.VMEM((N_DIL, rows, HEAD_DIM), F32)],
        compiler_params=_params("parallel"),
        name="attn_mix",
    )(*outs, *lses)


def _merge_kernel(l_ref, ogla_ref, oatt_ref, gg_ref, ga_ref, wg_ref, wa_ref,
                  out_ref, wgbf_ref, wabf_ref):
    @pl.when(pl.program_id(1) == 0)
    def _():
        wgbf_ref[...] = wg_ref[...].astype(BF16)
        wabf_ref[...] = wa_ref[...].astype(BF16)

    a = _dot(ogla_ref[...], wgbf_ref[...])
    b = _dot(oatt_ref[...], wabf_ref[...])
    gg = _sigmoid(gg_ref[...].astype(F32))
    ga = _sigmoid(ga_ref[...].astype(F32))
    out_ref[...] = (gg * a + ga * b).astype(out_ref.dtype)


def _merge(o_gla, o_att, proj2, l, w_gla_proj, w_attn_proj, gate_col0):
    T, dv = o_gla.shape
    gw = o_att.shape[1]
    D = w_gla_proj.shape[-1]
    tm, tn = MERGE_ROWS, MM_TN
    gcb = gate_col0 // tn
    dcb = D // tn
    call, lidx = _grid_call(
        _merge_kernel, l, (D // tn, T // tm),
        [pl.BlockSpec((tm, dv), lambda j, i, l: (i, 0)),
         pl.BlockSpec((tm, gw), lambda j, i, l: (i, 0)),
         pl.BlockSpec((tm, tn), lambda j, i, l: (i, gcb + j)),
         pl.BlockSpec((tm, tn), lambda j, i, l: (i, gcb + dcb + j)),
         pl.BlockSpec((None, dv, tn), lambda j, i, l: (l[0], 0, j)),
         pl.BlockSpec((None, gw, tn), lambda j, i, l: (l[0], 0, j))],
        pl.BlockSpec((tm, tn), lambda j, i, l: (i, j)),
        jax.ShapeDtypeStruct((T, D), BF16),
        scratch=[pltpu.VMEM((dv, tn), BF16), pltpu.VMEM((gw, tn), BF16)], name="gated_merge")
    return call(lidx, o_gla, o_att, proj2, proj2, w_gla_proj, w_attn_proj)


def _router_kernel(l_ref, x_ref, g_ref, sh_ref, sc_ref, wr_ref, br_ref, h_ref, route_ref):
    h = _ada_norm(x_ref[...], g_ref[...], sc_ref[...], sh_ref[...])
    rows, d = h.shape
    for j in range(d // LANES):
        h_ref[pl.ds(j, rows, stride=d // LANES), :] = h[:, j * LANES:(j + 1) * LANES]
    lg = _dot(h.astype(BF16), wr_ref[...].astype(BF16)) + br_ref[...]
    lane = lax.broadcasted_iota(jnp.int32, lg.shape, 1).astype(F32)
    big = float(4 * LANES)

    def first_argmax(vals):
        mx = jnp.max(vals, axis=-1, keepdims=True)
        idx = jnp.min(jnp.where(vals == mx, lane, big), axis=-1, keepdims=True)
        return mx, idx

    gl = jnp.where(lane < N_GROUPS, lg, NEG_BIG)
    gmax, gidx = first_argmax(gl)
    g_weight = 1.0 / jnp.sum(jnp.exp(gl - gmax), axis=-1, keepdims=True)
    lo = N_GROUPS + gidx * EXPERTS_PER_GROUP
    el = jnp.where((lane >= lo) & (lane < lo + EXPERTS_PER_GROUP), lg, NEG_BIG)
    m1, i1 = first_argmax(el)
    m2, i2 = first_argmax(jnp.where(lane == i1, NEG_BIG, el))
    e2 = jnp.exp(m2 - m1)
    w1 = g_weight / (1.0 + e2)
    w2 = g_weight * e2 / (1.0 + e2)
    route = jnp.where(lane == 0, i1 - N_GROUPS,
                      jnp.where(lane == 1, i2 - N_GROUPS,
                                jnp.where(lane == 2, w1, jnp.where(lane == 3, w2, 0.0))))
    route_ref[...] = route


def _router(x, l, g_all, mod3, wr_pad, br_pad, seq):
    T, D = x.shape
    rows = NORM_ROWS
    call, lidx = _grid_call(
        _router_kernel, l, (T // rows,),
        [pl.BlockSpec((rows, D), lambda i, l: (i, 0)),
         pl.BlockSpec((None, 1, D), lambda i, l: (l[0], 0, 0)),
         pl.BlockSpec((None, 1, D), _mod_spec(3, rows, seq)),
         pl.BlockSpec((None, 1, D), _mod_spec(4, rows, seq)),
         pl.BlockSpec((None, D, LANES), lambda i, l: (l[0], 0, 0)),
         pl.BlockSpec((None, 1, LANES), lambda i, l: (l[0], 0, 0))],
        [pl.BlockSpec((rows * (D // LANES), LANES), lambda i, l: (i, 0)),
         pl.BlockSpec((rows, LANES), lambda i, l: (i, 0))],
        [jax.ShapeDtypeStruct((T * (D // LANES), LANES), F32),
         jax.ShapeDtypeStruct((T, LANES), F32)],
        sem=("parallel",), name="norm_router")
    return call(lidx, x, g_all, mod3, mod3, wr_pad, br_pad)


def _expert_kernel(meta_ref, src_cur_ref, src_nxt_ref, dst_ref, h_ref, wg_ref, wu_ref, wd_ref,
                   ys_ref, xbuf_ref, obuf_ref, wgbf_ref, wubf_ref, wdbf_ref, gsem_ref, ssem_ref):
    i = pl.program_id(0)
    n_used = meta_ref[1]
    slot = i % 2
    nj, tm = xbuf_ref.shape[1], xbuf_ref.shape[2]

    def gather_start(idx_ref, s):
        def body(r, c):
            pltpu.make_async_copy(h_ref.at[idx_ref[0, 0, r]], xbuf_ref.at[s, :, r, :],
                                  gsem_ref.at[s]).start()
            return c
        lax.fori_loop(0, tm, body, 0, unroll=8)

    def scatter_start(idx_ref, s):
        def body(r, c):
            pltpu.make_async_copy(obuf_ref.at[s, :, r, :], ys_ref.at[idx_ref[0, 0, r]],
                                  ssem_ref.at[s]).start()
            return c
        lax.fori_loop(0, tm, body, 0, unroll=8)

    def gather_wait(s):
        pltpu.make_async_copy(obuf_ref.at[s], xbuf_ref.at[s], gsem_ref.at[s]).wait()

    def scatter_wait(s):
        pltpu.make_async_copy(xbuf_ref.at[s], obuf_ref.at[s], ssem_ref.at[s]).wait()

    @pl.when(i == 0)
    def _():
        gather_start(src_cur_ref, 0)
        obuf_ref[1] = jnp.zeros(obuf_ref.shape[1:], F32)
        n_rows = ys_ref.shape[0]

        def fill(r, c):
            pltpu.make_async_copy(obuf_ref.at[1, :, r % tm, :], ys_ref.at[n_rows - 2 * tm + r],
                                  ssem_ref.at[1]).start()
            return c
        lax.fori_loop(0, 2 * tm, fill, 0, unroll=8)
        scatter_wait(1)
        scatter_wait(1)

    @pl.when(i < n_used)
    def _():
        gather_wait(slot)

        @pl.when(i + 1 < n_used)
        def _():
            gather_start(src_nxt_ref, 1 - slot)

        @pl.when(i >= 2)
        def _():
            scatter_wait(slot)

        @pl.when((i == 0) | (meta_ref[2 + i] != meta_ref[2 + jnp.maximum(i - 1, 0)]))
        def _():
            wgbf_ref[...] = wg_ref[...].astype(BF16)
            wubf_ref[...] = wu_ref[...].astype(BF16)
            wdbf_ref[...] = wd_ref[...].astype(BF16)

        x = jnp.concatenate([xbuf_ref[slot, j] for j in range(nj)], axis=1).astype(BF16)
        a = _dot(x, wgbf_ref[...])
        u = _dot(x, wubf_ref[...])
        hid = (a * _sigmoid(a)) * u
        y = _dot(hid.astype(BF16), wdbf_ref[...])
        for j in range(nj):
            obuf_ref[slot, j] = y[:, j * LANES:(j + 1) * LANES]
        scatter_start(dst_ref, slot)

        @pl.when(i == n_used - 1)
        def _():
            @pl.when(i >= 1)
            def _():
                scatter_wait(1 - slot)
            scatter_wait(slot)


def _experts(h3, src, dst, meta, w_gate, w_up, w_down, n_out):
    T, nj, _ = h3.shape
    D = nj * LANES
    F = w_gate.shape[-1]
    n_tiles, _, tm = src.shape
    wspec = lambda shape: pl.BlockSpec((None, None) + shape, lambda i, m: (m[0], m[2 + i], 0, 0))
    idx_spec = lambda f: pl.BlockSpec((1, 1, tm), lambda i, m: (f(i), 0, 0), memory_space=pltpu.SMEM)
    return pl.pallas_call(
        _expert_kernel,
        grid_spec=pltpu.PrefetchScalarGridSpec(
            num_scalar_prefetch=1, grid=(n_tiles,),
            in_specs=[idx_spec(lambda i: i), idx_spec(lambda i: jnp.minimum(i + 1, n_tiles - 1)),
                      idx_spec(lambda i: i),
                      pl.BlockSpec(memory_space=pl.ANY),
                      wspec((D, F)), wspec((D, F)), wspec((F, D))],
            out_specs=pl.BlockSpec(memory_space=pl.ANY),
            scratch_shapes=[pltpu.VMEM((2, nj, tm, LANES), F32), pltpu.VMEM((2, nj, tm, LANES), F32),
                            pltpu.VMEM((D, F), BF16), pltpu.VMEM((D, F), BF16),
                            pltpu.VMEM((F, D), BF16),
                            pltpu.SemaphoreType.DMA((2,)), pltpu.SemaphoreType.DMA((2,))]),
        out_shape=jax.ShapeDtypeStruct((n_out, nj, LANES), F32),
        compiler_params=_params("arbitrary"),
        name="grouped_experts",
    )(meta, src, src, dst, h3, w_gate, w_up, w_down)


def _combine_kernel(x_ref, gt_ref, route_ref, y0_ref, y1_ref, o_ref):
    rows, d = x_ref.shape
    nj = d // LANES
    w1 = jnp.broadcast_to(route_ref[:, 2:3], (rows, LANES))
    w2 = jnp.broadcast_to(route_ref[:, 3:4], (rows, LANES))
    for j in range(nj):
        js = slice(j * LANES, (j + 1) * LANES)
        y = w1 * y0_ref[pl.ds(j, rows, stride=nj), :] + w2 * y1_ref[pl.ds(j, rows, stride=nj), :]
        o_ref[:, js] = x_ref[:, js] + gt_ref[:, js] * y


def _combine(x, ys, route, mod3, seq):
    T, D = x.shape
    rows = COMBINE_ROWS
    nblk = T // rows
    nj = D // LANES
    return pl.pallas_call(
        _combine_kernel,
        grid=(nblk,),
        in_specs=[pl.BlockSpec((rows, D), lambda i: (i, 0)),
                  pl.BlockSpec((None, 1, D), lambda i: (i * rows // seq, 0, 5)),
                  pl.BlockSpec((rows, LANES), lambda i: (i, 0)),
                  pl.BlockSpec((rows * nj, LANES), lambda i: (i, 0)),
                  pl.BlockSpec((rows * nj, LANES), lambda i: (nblk + i, 0))],
        out_specs=pl.BlockSpec((rows, D), lambda i: (i, 0)),
        out_shape=jax.ShapeDtypeStruct((T, D), F32),
        compiler_params=_params("parallel"),
        name="moe_combine",
    )(x, mod3, route, ys, ys)


def _dispatch_plan(route, l):
    T = route.shape[0]
    tm = EXPERT_TM
    n_tiles = (2 * T) // tm + N_EXPERTS
    P = n_tiles * tm
    e = route[:, 0:2].astype(jnp.int32).T.reshape(-1)
    onehot = (e[:, None] == jnp.arange(N_EXPERTS, dtype=jnp.int32)[None, :]).astype(jnp.int32)
    csum = jnp.cumsum(onehot, axis=0)
    rank = jnp.sum((csum - onehot) * onehot, axis=1)
    counts = csum[-1]
    tiles_per = (counts + tm - 1) // tm
    tile_end = jnp.cumsum(tiles_per)
    off = (tile_end - tiles_per) * tm
    pos = off[e] + rank
    asg = jnp.full((P,), -1, jnp.int32).at[pos].set(jnp.arange(2 * T, dtype=jnp.int32))
    slot = jnp.arange(P, dtype=jnp.int32)
    spare = 2 * T + ((slot // tm) % 2) * tm + slot % tm
    src = jnp.where(asg >= 0, asg % T, 0).reshape(n_tiles, 1, tm)
    dst = jnp.where(asg >= 0, asg, spare).reshape(n_tiles, 1, tm)
    tile_ids = jnp.arange(n_tiles, dtype=jnp.int32)
    tile_expert = jnp.minimum(jnp.sum(tile_ids[:, None] >= tile_end[None, :], axis=1),
                              N_EXPERTS - 1).astype(jnp.int32)
    meta = jnp.concatenate([jnp.stack([l, tile_end[-1].astype(jnp.int32)]), tile_expert])
    return src, dst, meta


def kernel(x, c, positions, w_ada, b_ada, g_norm_mix, g_norm_ffn, w_in, w_fg, b_fg, g_gla_out,
           g_q, g_k, w_gla_proj, w_attn_proj, w_out, w_route_group, b_route_group,
           w_route_expert, b_route_expert, w_exp_gate, w_exp_up, w_exp_down):
    B, S, D = x.shape
    L = w_ada.shape[0]
    T = B * S
    dk_all = w_fg.shape[-1]
    dv_all = g_gla_out.shape[-1]
    att_w = N_DIL * ATT_HEADS * HEAD_DIM
    gla_cols = 2 * dk_all + 2 * dv_all

    mod = _modulation(c, w_ada, b_ada)
    cos2, sin2 = _rope_tables(positions)

    wfg_pad = jnp.zeros((L, LANES, dk_all), F32).at[:, :GLA_RANK].set(w_fg)
    wr_pad = jnp.zeros((L, D, LANES), F32)
    wr_pad = wr_pad.at[:, :, :N_GROUPS].set(w_route_group)
    wr_pad = wr_pad.at[:, :, N_GROUPS:N_GROUPS + N_EXPERTS].set(w_route_expert)
    br_pad = jnp.zeros((L, 1, LANES), F32)
    br_pad = br_pad.at[:, 0, :N_GROUPS].set(b_route_group)
    br_pad = br_pad.at[:, 0, N_GROUPS:N_GROUPS + N_EXPERTS].set(b_route_expert)
    g_mix = g_norm_mix.reshape(L, 1, D)
    g_ffn = g_norm_ffn.reshape(L, 1, D)
    b_fg3 = b_fg.reshape(L, 1, dk_all)
    g_gla3 = g_gla_out.reshape(L, 1, dv_all)
    g_q3 = g_q.reshape(L, 1, HEAD_DIM)
    g_k3 = g_k.reshape(L, 1, HEAD_DIM)

    def residual(acc, xres, gt):
        return xres + gt * acc

    def layer(l, xt):
        l = jnp.asarray(l, jnp.int32)
        mod3 = lax.dynamic_index_in_dim(mod, l, 0, keepdims=False).reshape(8, 1, N_MOD * D)
        h = _norm(xt, l, g_mix, mod3, 0, S)
        proj1 = _matmul(h, w_in, l, col0=0, n=gla_cols, out_dtype=BF16, tn=2 * MM_TN,
                        tm=2 * MM_TM, name="proj_gla")
        fg = _matmul(h, w_in, l, col0=gla_cols, n=LANES, tn=LANES, out_dtype=F32, name="proj_fg")
        proj2 = _matmul(h, w_in, l, col0=gla_cols, shift=GLA_RANK, n=3 * att_w + 2 * D,
                        out_dtype=BF16, tm=2 * MM_TM, name="proj_att")
        o_gla = _gla(proj1, fg, l, wfg_pad, b_fg3, g_gla3, B, S)
        prep = _qkprep(proj2, l, cos2, sin2, g_q3, g_k3)
        outs, lses = [], []
        for grp, (_, dil) in enumerate(DIL_GROUPS):
            o, lse = _attn_group(prep[grp], prep[N_DIL + grp], prep[2 * N_DIL + grp], dil, B, S)
            outs.append(o)
            lses.append(lse)
        o_att = _attn_mix(outs, lses)
        merged = _merge(o_gla, o_att, proj2, l, w_gla_proj, w_attn_proj, 3 * att_w)
        gt_spec = pl.BlockSpec((None, 1, MM_TN),
                               lambda j, i, l: (i * MM_TM // S, 0, 2 * (D // MM_TN) + j))
        xt = _matmul(merged, w_out, l, col0=0, n=D, out_dtype=F32, epilogue=residual,
                     extras=(xt, mod3),
                     extra_specs=(pl.BlockSpec((MM_TM, MM_TN), lambda j, i, l: (i, j)), gt_spec),
                     name="out_proj")
        h2, route = _router(xt, l, g_ffn, mod3, wr_pad, br_pad, S)
        src, dst, meta = _dispatch_plan(route, l)
        nj = D // LANES
        n_out = 2 * T + 2 * EXPERT_TM
        ys = _experts(h2.reshape(T, nj, LANES), src, dst, meta, w_exp_gate, w_exp_up, w_exp_down,
                      n_out)
        return _combine(xt, ys.reshape(n_out * nj, LANES), route, mod3, S)

    xt = lax.fori_loop(0, L, layer, x.reshape(T, D))
    return xt.reshape(B, S, D)
```

```python
import functools

import jax
import jax.numpy as jnp
from jax import lax
from jax.experimental import pallas as pl
from jax.experimental.pallas import tpu as pltpu

F32 = jnp.float32
BF16 = jnp.bfloat16

GLA_HEADS = 4
GLA_RANK = 16
GLA_TAU = 16.0
GLA_CHUNK = 64
DIL_GROUPS = ((128, 1), (512, 4), (2048, 16))
N_DIL = 3
ATT_HEADS = 4
HEAD_DIM = 128
ATT_BLOCK = 128
ROPE_THETA = 10000.0
N_GROUPS = 4
EXPERTS_PER_GROUP = 8
N_EXPERTS = N_GROUPS * EXPERTS_PER_GROUP
N_MOD = 6
EPS = 1e-6

LANES = 128
VMEM_LIMIT = 56 * 1024 * 1024
NEG_BIG = -1e30

NORM_ROWS = 512
MM_TM = 1024
MM_TN = 512
GLA_ROWS = 256
PREP_ROWS = 512
ATT_SPAN = 2048
MERGE_ROWS = 1024
EXPERT_TM = 256
COMBINE_ROWS = 256


def _params(*sem):
    return pltpu.CompilerParams(dimension_semantics=sem, vmem_limit_bytes=VMEM_LIMIT)


def _dot(a, b):
    return jnp.dot(a, b, preferred_element_type=F32)


def _dot_nt(a, b):
    return lax.dot_general(a, b, (((1,), (1,)), ((), ())), preferred_element_type=F32)


def _dot_tn(a, b):
    return lax.dot_general(a, b, (((0,), (0,)), ((), ())), preferred_element_type=F32)


def _split_bf16(x):
    hi = x.astype(BF16)
    lo = (x - hi.astype(F32)).astype(BF16)
    return hi, lo


def _sigmoid(x):
    return 1.0 / (1.0 + jnp.exp(-x))


def _grid_call(kernel, l, grid, in_specs, out_specs, out_shape, scratch=(), sem=None, name=None):
    sem = sem or ("arbitrary",) * len(grid)
    return pl.pallas_call(
        kernel,
        grid_spec=pltpu.PrefetchScalarGridSpec(
            num_scalar_prefetch=1, grid=grid, in_specs=in_specs, out_specs=out_specs,
            scratch_shapes=list(scratch)),
        out_shape=out_shape,
        compiler_params=_params(*sem),
        name=name,
    ), jnp.reshape(l, (1,)).astype(jnp.int32)


def _mod_kernel(c_ref, w_ref, b_ref, o_ref):
    c = c_ref[...]
    a = (c * _sigmoid(c)).astype(BF16)
    o_ref[...] = _dot(a, w_ref[...].astype(BF16)) + b_ref[...]


def _modulation(c, w_ada, b_ada):
    L, D, N = w_ada.shape
    B = c.shape[0]
    rows = 8
    c_pad = jnp.zeros((rows, D), F32).at[:B].set(c)
    tn = 1024
    return pl.pallas_call(
        _mod_kernel,
        grid=(L, N // tn),
        in_specs=[pl.BlockSpec((rows, D), lambda l, j: (0, 0)),
                  pl.BlockSpec((None, D, tn), lambda l, j: (l, 0, j)),
                  pl.BlockSpec((None, 1, tn), lambda l, j: (l, 0, j))],
        out_specs=pl.BlockSpec((None, rows, tn), lambda l, j: (l, 0, j)),
        out_shape=jax.ShapeDtypeStruct((L, rows, N), F32),
        compiler_params=_params("parallel", "parallel"),
        name="adaln_mod",
    )(c_pad, w_ada, b_ada.reshape(L, 1, N))


def _rope_kernel(pos_ref, freq_ref, cos_ref, sin_ref):
    ang = pos_ref[...].astype(F32) * freq_ref[...]
    lane = lax.broadcasted_iota(jnp.int32, ang.shape, 1)
    cos_ref[...] = jnp.cos(ang)
    sin_ref[...] = jnp.where(lane < HEAD_DIM // 2, -jnp.sin(ang), jnp.sin(ang))


def _rope_tables(positions):
    T = positions.size
    inv_freq = ROPE_THETA ** (-jnp.arange(0, HEAD_DIM, 2, dtype=F32) / HEAD_DIM)
    freq2 = jnp.concatenate([inv_freq, inv_freq]).reshape(1, HEAD_DIM)
    rows = 2048
    return pl.pallas_call(
        _rope_kernel,
        grid=(T // rows,),
        in_specs=[pl.BlockSpec((rows, 1), lambda i: (i, 0)),
                  pl.BlockSpec((1, HEAD_DIM), lambda i: (0, 0))],
        out_specs=[pl.BlockSpec((rows, HEAD_DIM), lambda i: (i, 0))] * 2,
        out_shape=[jax.ShapeDtypeStruct((T, HEAD_DIM), F32)] * 2,
        compiler_params=_params("parallel"),
        name="rope_tables",
    )(positions.reshape(T, 1), freq2)


def _ada_norm(x, g, scale, shift):
    y = x * lax.rsqrt(jnp.mean(x * x, axis=-1, keepdims=True) + EPS)
    return y * g * (1.0 + scale) + shift


def _norm_kernel(l_ref, x_ref, g_ref, sh_ref, sc_ref, o_ref):
    o_ref[...] = _ada_norm(x_ref[...], g_ref[...], sc_ref[...], sh_ref[...]).astype(o_ref.dtype)


def _mod_spec(which, rows, seq):
    return lambda i, l: (i * rows // seq, 0, which)


def _norm(x, l, g_all, mod3, which_shift, seq):
    T, D = x.shape
    rows = NORM_ROWS
    call, lidx = _grid_call(
        _norm_kernel, l, (T // rows,),
        [pl.BlockSpec((rows, D), lambda i, l: (i, 0)),
         pl.BlockSpec((None, 1, D), lambda i, l: (l[0], 0, 0)),
         pl.BlockSpec((None, 1, D), _mod_spec(which_shift, rows, seq)),
         pl.BlockSpec((None, 1, D), _mod_spec(which_shift + 1, rows, seq))],
        pl.BlockSpec((rows, D), lambda i, l: (i, 0)),
        jax.ShapeDtypeStruct((T, D), BF16), sem=("parallel",), name="ada_norm")
    return call(lidx, x, g_all, mod3, mod3)


def _mm_kernel(l_ref, a_ref, w_ref, *rest, epilogue, n_extra, shift):
    n_w = 1 if shift else 0
    extra = rest[n_w:n_w + n_extra]
    o_ref = rest[n_w + n_extra]
    wbf_ref = rest[n_w + n_extra + 1]
    tn = o_ref.shape[1]

    @pl.when(pl.program_id(1) == 0)
    def _():
        if shift:
            wide = jnp.concatenate([w_ref[...], rest[0][...]], axis=1)
            wbf_ref[...] = wide[:, shift:shift + tn].astype(BF16)
        else:
            wbf_ref[...] = w_ref[...].astype(BF16)

    acc = _dot(a_ref[...], wbf_ref[...])
    if epilogue is not None:
        acc = epilogue(acc, *[e[...] for e in extra])
    o_ref[...] = acc.astype(o_ref.dtype)


def _matmul(a, w, l, *, col0, n, out_dtype, tn=MM_TN, tm=MM_TM, shift=0, epilogue=None,
            extras=(), extra_specs=(), name="matmul"):
    M, K = a.shape
    assert col0 % tn == 0 and n % tn == 0 and M % tm == 0 and 0 <= shift < LANES
    cb0 = col0 // tn
    kern = functools.partial(_mm_kernel, epilogue=epilogue, n_extra=len(extras), shift=shift)
    w_specs = [pl.BlockSpec((None, K, tn), lambda j, i, l: (l[0], 0, cb0 + j))]
    if shift:
        per = tn // LANES
        w_specs.append(pl.BlockSpec((None, K, LANES), lambda j, i, l: (l[0], 0, (cb0 + j + 1) * per)))
    call, lidx = _grid_call(
        kern, l, (n // tn, M // tm),
        [pl.BlockSpec((tm, K), lambda j, i, l: (i, 0))] + w_specs + list(extra_specs),
        pl.BlockSpec((tm, tn), lambda j, i, l: (i, j)),
        jax.ShapeDtypeStruct((M, n), out_dtype),
        scratch=[pltpu.VMEM((K, tn), BF16)], name=name)
    return call(lidx, a, *([w] * len(w_specs)), *extras)


def _gla_kernel(l_ref, q_ref, k_ref, v_ref, r_ref, fg_ref, wfg_ref, bfg_ref, g_ref, o_ref,
                state_ref, *, head_k, head_v):
    rows = q_ref.shape[0]
    C = GLA_CHUNK

    @pl.when(pl.program_id(1) == 0)
    def _():
        state_ref[...] = jnp.zeros_like(state_ref)

    fh, fl = _split_bf16(fg_ref[...])
    wh, wl = _split_bf16(wfg_ref[...])
    z = _dot(fh, wh) + _dot(fl, wh) + _dot(fh, wl) + bfg_ref[...]
    log_a = (jnp.minimum(z, 0.0) - jnp.log(1.0 + jnp.exp(-jnp.abs(z)))) * (1.0 / GLA_TAU)

    ri = lax.broadcasted_iota(jnp.int32, (C, C), 0)
    ci = lax.broadcasted_iota(jnp.int32, (C, C), 1)
    causal = ri >= ci
    tri = causal.astype(BF16)
    scale = head_k ** -0.5

    for c in range(rows // C):
        rs = slice(c * C, (c + 1) * C)
        lh, ll = _split_bf16(log_a[rs])
        b = _dot(tri, lh) + _dot(tri, ll)
        b_last = b[C - 1:C]
        q = q_ref[rs, :].astype(F32) * scale
        k = k_ref[rs, :].astype(F32)
        q_dec = (q * jnp.exp(b)).astype(BF16)
        k_inv = (k * jnp.exp(-b)).astype(BF16)
        k_upd = (k * jnp.exp(b_last - b)).astype(BF16)
        decay = jnp.exp(b_last)
        for h in range(GLA_HEADS):
            ks = slice(h * head_k, (h + 1) * head_k)
            vs = slice(h * head_v, (h + 1) * head_v)
            v = v_ref[rs, vs]
            att = jnp.where(causal, _dot_nt(q_dec[:, ks], k_inv[:, ks]), 0.0).astype(BF16)
            st = state_ref[h]
            o = _dot(att, v) + _dot_nt(q_dec[:, ks], st.astype(BF16))
            state_ref[h] = st * decay[:, ks] + _dot_tn(v, k_upd[:, ks])
            o = o * lax.rsqrt(jnp.mean(o * o, axis=-1, keepdims=True) + EPS) * g_ref[:, vs]
            r = r_ref[rs, vs].astype(F32)
            o_ref[rs, vs] = (o * (r * _sigmoid(r))).astype(o_ref.dtype)


def _gla(proj, fg, l, wfg_pad, b_fg, g_gla_out, batch, seq):
    T = proj.shape[0]
    dk_all = wfg_pad.shape[-1]
    dv_all = g_gla_out.shape[-1]
    rows = GLA_ROWS
    nblk = seq // rows
    kern = functools.partial(_gla_kernel, head_k=dk_all // GLA_HEADS, head_v=dv_all // GLA_HEADS)
    row = lambda b, n, l: b * nblk + n
    call, lidx = _grid_call(
        kern, l, (batch, nblk),
        [pl.BlockSpec((rows, dk_all), lambda b, n, l: (row(b, n, l), 0)),
         pl.BlockSpec((rows, dk_all), lambda b, n, l: (row(b, n, l), 1)),
         pl.BlockSpec((rows, dv_all), lambda b, n, l: (row(b, n, l), 1)),
         pl.BlockSpec((rows, dv_all), lambda b, n, l: (row(b, n, l), 2)),
         pl.BlockSpec((rows, LANES), lambda b, n, l: (row(b, n, l), 0)),
         pl.BlockSpec((None, LANES, dk_all), lambda b, n, l: (l[0], 0, 0)),
         pl.BlockSpec((None, 1, dk_all), lambda b, n, l: (l[0], 0, 0)),
         pl.BlockSpec((None, 1, dv_all), lambda b, n, l: (l[0], 0, 0))],
        pl.BlockSpec((rows, dv_all), lambda b, n, l: (row(b, n, l), 0)),
        jax.ShapeDtypeStruct((T, dv_all), BF16),
        scratch=[pltpu.VMEM((GLA_HEADS, dv_all // GLA_HEADS, dk_all // GLA_HEADS), F32)],
        sem=("parallel", "arbitrary"), name="gla")
    return call(lidx, proj, proj, proj, proj, fg, wfg_pad, b_fg, g_gla_out)


def _qkprep_kernel(l_ref, q_ref, k_ref, v_ref, cos_ref, sin_ref, gq_ref, gk_ref, *refs):
    out_refs, scr_ref = refs[:-1], refs[-1]
    cos = cos_ref[...]
    sin = sin_ref[...]
    gw = ATT_HEADS * HEAD_DIM
    rows = q_ref.shape[0]

    def prep(ref, g, scale):
        x = ref.astype(F32)
        y = x * lax.rsqrt(jnp.mean(x * x, axis=-1, keepdims=True) + EPS) * g
        y = y * cos + pltpu.roll(y, HEAD_DIM // 2, 1) * sin
        return y * scale

    def emit(dst_ref, dil):
        n = rows // dil
        for p in range(dil):
            for h in range(ATT_HEADS):
                cs = slice(p * gw + h * HEAD_DIM, p * gw + (h + 1) * HEAD_DIM)
                dst_ref[:, cs] = scr_ref[h, pl.ds(p, n, stride=dil), :].astype(BF16)

    for grp, (_, dil) in enumerate(DIL_GROUPS):
        for ref, g_ref, scale, out in ((q_ref, gq_ref, HEAD_DIM ** -0.5, out_refs[grp]),
                                       (k_ref, gk_ref, 1.0, out_refs[N_DIL + grp])):
            for h in range(ATT_HEADS):
                src = slice(grp * gw + h * HEAD_DIM, grp * gw + (h + 1) * HEAD_DIM)
                y = prep(ref[:, src], g_ref[...], scale)
                if dil == 1:
                    out[:, h * HEAD_DIM:(h + 1) * HEAD_DIM] = y.astype(BF16)
                else:
                    scr_ref[h] = y
            if dil > 1:
                emit(out, dil)
        if dil == 1:
            out_refs[2 * N_DIL + grp][...] = v_ref[:, grp * gw:(grp + 1) * gw]
        else:
            for h in range(ATT_HEADS):
                src = slice(grp * gw + h * HEAD_DIM, grp * gw + (h + 1) * HEAD_DIM)
                scr_ref[h] = v_ref[:, src].astype(F32)
            emit(out_refs[2 * N_DIL + grp], dil)


def _qkprep(proj2, l, cos2, sin2, g_q, g_k):
    T = proj2.shape[0]
    rows = PREP_ROWS
    width = N_DIL * ATT_HEADS * HEAD_DIM
    gw = ATT_HEADS * HEAD_DIM
    dils = [dil for _, dil in DIL_GROUPS] * 3
    call, lidx = _grid_call(
        _qkprep_kernel, l, (T // rows,),
        [pl.BlockSpec((rows, width), lambda i, l: (i, 0)),
         pl.BlockSpec((rows, width), lambda i, l: (i, 1)),
         pl.BlockSpec((rows, width), lambda i, l: (i, 2)),
         pl.BlockSpec((rows, HEAD_DIM), lambda i, l: (i, 0)),
         pl.BlockSpec((rows, HEAD_DIM), lambda i, l: (i, 0)),
         pl.BlockSpec((None, 1, HEAD_DIM), lambda i, l: (l[0], 0, 0)),
         pl.BlockSpec((None, 1, HEAD_DIM), lambda i, l: (l[0], 0, 0))],
        [pl.BlockSpec((rows // d, d * gw), lambda i, l: (i, 0)) for d in dils],
        [jax.ShapeDtypeStruct((T // d, d * gw), BF16) for d in dils],
        scratch=[pltpu.VMEM((ATT_HEADS, rows, HEAD_DIM), F32)],
        sem=("parallel",), name="qk_prep")
    return call(lidx, proj2, proj2, proj2, cos2, sin2, g_q, g_k)


def _attn_kernel(q_ref, kc_ref, kp_ref, vc_ref, vp_ref, o_ref, lse_ref, *, dil):
    first = pl.program_id(1) == 0
    QB = q_ref.shape[0]
    A = ATT_BLOCK
    qi = lax.broadcasted_iota(jnp.int32, (A, 2 * A), 0)
    kj = lax.broadcasted_iota(jnp.int32, (A, 2 * A), 1) - A
    dist = qi - kj
    valid = (dist >= 0) & (dist <= A)
    bias = jnp.where(valid, 0.0, NEG_BIG).astype(F32)
    bias_first = jnp.where(first, jnp.where(valid & (kj >= 0), 0.0, NEG_BIG), bias).astype(F32)

    for p in range(dil):
        for h in range(ATT_HEADS):
            cs = slice((p * ATT_HEADS + h) * HEAD_DIM, (p * ATT_HEADS + h + 1) * HEAD_DIM)
            for j in range(QB // A):
                rs = slice(j * A, (j + 1) * A)
                if j == 0:
                    kcat = jnp.concatenate([kp_ref[:, cs], kc_ref[0:A, cs]], axis=0)
                    vcat = jnp.concatenate([vp_ref[:, cs], vc_ref[0:A, cs]], axis=0)
                    bb = bias_first
                else:
                    kcat = kc_ref[(j - 1) * A:(j + 1) * A, cs]
                    vcat = vc_ref[(j - 1) * A:(j + 1) * A, cs]
                    bb = bias
                s = _dot_nt(q_ref[rs, cs], kcat) + bb
                m = jnp.max(s, axis=-1, keepdims=True)
                e = jnp.exp(s - m)
                den = jnp.sum(e, axis=-1, keepdims=True)
                o = _dot(e.astype(BF16), vcat) / den
                o_ref[rs, cs] = o.astype(o_ref.dtype)
                lse_ref[rs, cs] = jnp.broadcast_to(m + jnp.log(den), (A, HEAD_DIM))


def _attn_group(q, k, v, dil, batch, seq):
    rows_all, W = q.shape
    A = ATT_BLOCK
    QB = ATT_SPAN // dil
    nsp = seq // ATT_SPAN
    per = QB // A
    cur = pl.BlockSpec((QB, W), lambda b, n: (b * nsp + n, 0))
    prev = pl.BlockSpec((A, W), lambda b, n: (jnp.maximum((b * nsp + n) * per - 1, b * nsp * per), 0))
    return pl.pallas_call(
        functools.partial(_attn_kernel, dil=dil),
        grid=(batch, nsp),
        in_specs=[cur, cur, prev, cur, prev],
        out_specs=[cur, cur],
        out_shape=[jax.ShapeDtypeStruct((rows_all, W), BF16),
                   jax.ShapeDtypeStruct((rows_all, W), F32)],
        compiler_params=_params("parallel", "arbitrary"),
        name=f"dilated_attn_r{dil}",
    )(q, k, k, v, v)


def _attn_mix_kernel(o0, o1, o2, l0, l1, l2, out_ref, so_ref, sl_ref):
    rows, gw = out_ref.shape
    o_refs, l_refs = (o0, o1, o2), (l0, l1, l2)
    for h in range(ATT_HEADS):
        hs = slice(h * HEAD_DIM, (h + 1) * HEAD_DIM)
        outs, lses = [], []
        for g, (_, dil) in enumerate(DIL_GROUPS):
            if dil == 1:
                outs.append(o_refs[g][:, hs].astype(F32))
                lses.append(l_refs[g][:, hs])
                continue
            n = rows // dil
            for p in range(dil):
                cs = slice(p * gw + h * HEAD_DIM, p * gw + (h + 1) * HEAD_DIM)
                so_ref[g, pl.ds(p, n, stride=dil), :] = o_refs[g][:, cs].astype(F32)
                sl_ref[g, pl.ds(p, n, stride=dil), :] = l_refs[g][:, cs]
            outs.append(so_ref[g])
            lses.append(sl_ref[g])
        m = jnp.maximum(jnp.maximum(lses[0], lses[1]), lses[2])
        ws = [jnp.exp(x - m) for x in lses]
        den = ws[0] + ws[1] + ws[2]
        out_ref[:, hs] = ((ws[0] * outs[0] + ws[1] * outs[1] + ws[2] * outs[2]) / den
                          ).astype(out_ref.dtype)


def _attn_mix(outs, lses):
    gw = ATT_HEADS * HEAD_DIM
    T = outs[0].shape[0]
    rows = PREP_ROWS
    dils = [dil for _, dil in DIL_GROUPS]
    specs = [pl.BlockSpec((rows // d, d * gw), lambda i: (i, 0)) for d in dils]
    return pl.pallas_call(
        _attn_mix_kernel,
        grid=(T // rows,),
        in_specs=specs + specs,
        out_specs=pl.BlockSpec((rows, gw), lambda i: (i, 0)),
        out_shape=jax.ShapeDtypeStruct((T, gw), BF16),
        scratch_shapes=[pltpu.VMEM((N_DIL, rows, HEAD_DIM), F32),
                        pltpu.VMEM((N_DIL, rows, HEAD_DIM), F32)],
        compiler_params=_params("parallel"),
        name="attn_mix",
    )(*outs, *lses)


def _merge_kernel(l_ref, ogla_ref, oatt_ref, gg_ref, ga_ref, wg_ref, wa_ref,
                  out_ref, wgbf_ref, wabf_ref):
    @pl.when(pl.program_id(1) == 0)
    def _():
        wgbf_ref[...] = wg_ref[...].astype(BF16)
        wabf_ref[...] = wa_ref[...].astype(BF16)

    a = _dot(ogla_ref[...], wgbf_ref[...])
    b = _dot(oatt_ref[...], wabf_ref[...])
    gg = _sigmoid(gg_ref[...].astype(F32))
    ga = _sigmoid(ga_ref[...].astype(F32))
    out_ref[...] = (gg * a + ga * b).astype(out_ref.dtype)


def _merge(o_gla, o_att, proj2, l, w_gla_proj, w_attn_proj, gate_col0):
    T, dv = o_gla.shape
    gw = o_att.shape[1]
    D = w_gla_proj.shape[-1]
    tm, tn = MERGE_ROWS, MM_TN
    gcb = gate_col0 // tn
    dcb = D // tn
    call, lidx = _grid_call(
        _merge_kernel, l, (D // tn, T // tm),
        [pl.BlockSpec((tm, dv), lambda j, i, l: (i, 0)),
         pl.BlockSpec((tm, gw), lambda j, i, l: (i, 0)),
         pl.BlockSpec((tm, tn), lambda j, i, l: (i, gcb + j)),
         pl.BlockSpec((tm, tn), lambda j, i, l: (i, gcb + dcb + j)),
         pl.BlockSpec((None, dv, tn), lambda j, i, l: (l[0], 0, j)),
         pl.BlockSpec((None, gw, tn), lambda j, i, l: (l[0], 0, j))],
        pl.BlockSpec((tm, tn), lambda j, i, l: (i, j)),
        jax.ShapeDtypeStruct((T, D), BF16),
        scratch=[pltpu.VMEM((dv, tn), BF16), pltpu.VMEM((gw, tn), BF16)], name="gated_merge")
    return call(lidx, o_gla, o_att, proj2, proj2, w_gla_proj, w_attn_proj)


def _router_kernel(l_ref, x_ref, g_ref, sh_ref, sc_ref, wr_ref, br_ref, h_ref, route_ref):
    h = _ada_norm(x_ref[...], g_ref[...], sc_ref[...], sh_ref[...])
    rows, d = h.shape
    for j in range(d // LANES):
        h_ref[pl.ds(j, rows, stride=d // LANES), :] = h[:, j * LANES:(j + 1) * LANES]
    lg = _dot(h.astype(BF16), wr_ref[...].astype(BF16)) + br_ref[...]
    lane = lax.broadcasted_iota(jnp.int32, lg.shape, 1).astype(F32)
    big = float(4 * LANES)

    def first_argmax(vals):
        mx = jnp.max(vals, axis=-1, keepdims=True)
        idx = jnp.min(jnp.where(vals == mx, lane, big), axis=-1, keepdims=True)
        return mx, idx

    gl = jnp.where(lane < N_GROUPS, lg, NEG_BIG)
    gmax, gidx = first_argmax(gl)
    g_weight = 1.0 / jnp.sum(jnp.exp(gl - gmax), axis=-1, keepdims=True)
    lo = N_GROUPS + gidx * EXPERTS_PER_GROUP
    el = jnp.where((lane >= lo) & (lane < lo + EXPERTS_PER_GROUP), lg, NEG_BIG)
    m1, i1 = first_argmax(el)
    m2, i2 = first_argmax(jnp.where(lane == i1, NEG_BIG, el))
    e2 = jnp.exp(m2 - m1)
    w1 = g_weight / (1.0 + e2)
    w2 = g_weight * e2 / (1.0 + e2)
    route = jnp.where(lane == 0, i1 - N_GROUPS,
                      jnp.where(lane == 1, i2 - N_GROUPS,
                                jnp.where(lane == 2, w1, jnp.where(lane == 3, w2, 0.0))))
    route_ref[...] = route


def _router(x, l, g_all, mod3, wr_pad, br_pad, seq):
    T, D = x.shape
    rows = NORM_ROWS
    call, lidx = _grid_call(
        _router_kernel, l, (T // rows,),
        [pl.BlockSpec((rows, D), lambda i, l: (i, 0)),
         pl.BlockSpec((None, 1, D), lambda i, l: (l[0], 0, 0)),
         pl.BlockSpec((None, 1, D), _mod_spec(3, rows, seq)),
         pl.BlockSpec((None, 1, D), _mod_spec(4, rows, seq)),
         pl.BlockSpec((None, D, LANES), lambda i, l: (l[0], 0, 0)),
         pl.BlockSpec((None, 1, LANES), lambda i, l: (l[0], 0, 0))],
        [pl.BlockSpec((rows * (D // LANES), LANES), lambda i, l: (i, 0)),
         pl.BlockSpec((rows, LANES), lambda i, l: (i, 0))],
        [jax.ShapeDtypeStruct((T * (D // LANES), LANES), F32),
         jax.ShapeDtypeStruct((T, LANES), F32)],
        sem=("parallel",), name="norm_router")
    return call(lidx, x, g_all, mod3, mod3, wr_pad, br_pad)


def _expert_kernel(meta_ref, src_one_ref, src_cur_ref, src_nxt_ref, dst_ref, h_ref,
                   wg_ref, wu_ref, wd_ref, ys_ref, xbuf_ref, obuf_ref,
                   wgbf_ref, wubf_ref, wdbf_ref, gsem_ref, ssem_ref):
    i = pl.program_id(0)
    n_used = meta_ref[1]
    slot = i % 2
    nj, tm = xbuf_ref.shape[1], xbuf_ref.shape[2]

    def gather_copy(idx_ref, r, s):
        return pltpu.make_async_copy(h_ref.at[idx_ref[0, 0, r]], xbuf_ref.at[s, :, r, :],
                                     gsem_ref.at[s])

    def scatter_copy(idx_ref, r, s):
        return pltpu.make_async_copy(obuf_ref.at[s, :, r, :], ys_ref.at[idx_ref[0, 0, r]],
                                     ssem_ref.at[s])

    def gather_start(idx_ref, s):
        def body(r, c):
            gather_copy(idx_ref, r, s).start()
            return c
        lax.fori_loop(0, tm, body, 0, unroll=8)

    def gather_wait(s):
        pltpu.make_async_copy(obuf_ref.at[0], xbuf_ref.at[s], gsem_ref.at[s]).wait()

    def scatter_wait(s):
        pltpu.make_async_copy(xbuf_ref.at[0], obuf_ref.at[s], ssem_ref.at[s]).wait()

    @pl.when(i == 0)
    def _():
        gather_start(src_cur_ref, 0)

        @pl.when(n_used > 1)
        def _():
            gather_start(src_one_ref, 1)
        obuf_ref[1] = jnp.zeros(obuf_ref.shape[1:], F32)
        n_rows = ys_ref.shape[0]

        def fill(r, c):
            pltpu.make_async_copy(obuf_ref.at[1, :, r % tm, :], ys_ref.at[n_rows - 2 * tm + r],
                                  ssem_ref.at[1]).start()
            return c
        lax.fori_loop(0, 2 * tm, fill, 0, unroll=8)
        scatter_wait(1)
        scatter_wait(1)

    @pl.when(i < n_used)
    def _():
        xslot = i % 3
        gather_wait(xslot)

        @pl.when(i >= 2)
        def _():
            scatter_wait(slot)

        @pl.when((i == 0) | (meta_ref[2 + i] != meta_ref[2 + jnp.maximum(i - 1, 0)]))
        def _():
            wgbf_ref[...] = wg_ref[...].astype(BF16)
            wubf_ref[...] = wu_ref[...].astype(BF16)
            wdbf_ref[...] = wd_ref[...].astype(BF16)

        x = jnp.concatenate([xbuf_ref[xslot, j] for j in range(nj)], axis=1).astype(BF16)
        a = _dot(x, wgbf_ref[...])
        u = _dot(x, wubf_ref[...])
        hid = (a * _sigmoid(a)) * u
        y = _dot(hid.astype(BF16), wdbf_ref[...])
        for j in range(nj):
            obuf_ref[slot, j] = y[:, j * LANES:(j + 1) * LANES]
        for r in range(tm):
            scatter_copy(dst_ref, r, slot).start()

        @pl.when(i + 2 < n_used)
        def _():
            nslot = (i + 2) % 3
            for r in range(tm):
                gather_copy(src_nxt_ref, r, nslot).start()

        @pl.when(i == n_used - 1)
        def _():
            @pl.when(i >= 1)
            def _():
                scatter_wait(1 - slot)
            scatter_wait(slot)


def _experts(h3, src, dst, meta, w_gate, w_up, w_down, n_out):
    T, nj, _ = h3.shape
    D = nj * LANES
    F = w_gate.shape[-1]
    n_tiles, _, tm = src.shape
    wspec = lambda shape: pl.BlockSpec((None, None) + shape, lambda i, m: (m[0], m[2 + i], 0, 0))
    idx_spec = lambda f: pl.BlockSpec((1, 1, tm), lambda i, m: (f(i), 0, 0), memory_space=pltpu.SMEM)
    return pl.pallas_call(
        _expert_kernel,
        grid_spec=pltpu.PrefetchScalarGridSpec(
            num_scalar_prefetch=1, grid=(n_tiles,),
            in_specs=[idx_spec(lambda i: min(1, n_tiles - 1)), idx_spec(lambda i: i),
                      idx_spec(lambda i: jnp.minimum(i + 2, n_tiles - 1)), idx_spec(lambda i: i),
                      pl.BlockSpec(memory_space=pl.ANY),
                      wspec((D, F)), wspec((D, F)), wspec((F, D))],
            out_specs=pl.BlockSpec(memory_space=pl.ANY),
            scratch_shapes=[pltpu.VMEM((3, nj, tm, LANES), F32), pltpu.VMEM((2, nj, tm, LANES), F32),
                            pltpu.VMEM((D, F), BF16), pltpu.VMEM((D, F), BF16),
                            pltpu.VMEM((F, D), BF16),
                            pltpu.SemaphoreType.DMA((3,)), pltpu.SemaphoreType.DMA((2,))]),
        out_shape=jax.ShapeDtypeStruct((n_out, nj, LANES), F32),
        compiler_params=_params("arbitrary"),
        name="grouped_experts",
    )(meta, src, src, src, dst, h3, w_gate, w_up, w_down)


def _combine_kernel(x_ref, gt_ref, route_ref, y0_ref, y1_ref, o_ref):
    rows, d = x_ref.shape
    nj = d // LANES
    w1 = jnp.broadcast_to(route_ref[:, 2:3], (rows, LANES))
    w2 = jnp.broadcast_to(route_ref[:, 3:4], (rows, LANES))
    for j in range(nj):
        js = slice(j * LANES, (j + 1) * LANES)
        y = w1 * y0_ref[pl.ds(j, rows, stride=nj), :] + w2 * y1_ref[pl.ds(j, rows, stride=nj), :]
        o_ref[:, js] = x_ref[:, js] + gt_ref[:, js] * y


def _combine(x, ys, route, mod3, seq):
    T, D = x.shape
    rows = COMBINE_ROWS
    nblk = T // rows
    nj = D // LANES
    return pl.pallas_call(
        _combine_kernel,
        grid=(nblk,),
        in_specs=[pl.BlockSpec((rows, D), lambda i: (i, 0)),
                  pl.BlockSpec((None, 1, D), lambda i: (i * rows // seq, 0, 5)),
                  pl.BlockSpec((rows, LANES), lambda i: (i, 0)),
                  pl.BlockSpec((rows * nj, LANES), lambda i: (i, 0)),
                  pl.BlockSpec((rows * nj, LANES), lambda i: (nblk + i, 0))],
        out_specs=pl.BlockSpec((rows, D), lambda i: (i, 0)),
        out_shape=jax.ShapeDtypeStruct((T, D), F32),
        compiler_params=_params("parallel"),
        name="moe_combine",
    )(x, mod3, route, ys, ys)


def _dispatch_plan(route, l):
    T = route.shape[0]
    tm = EXPERT_TM
    n_tiles = (2 * T) // tm + N_EXPERTS
    P = n_tiles * tm
    e = route[:, 0:2].astype(jnp.int32).T.reshape(-1)
    onehot = (e[:, None] == jnp.arange(N_EXPERTS, dtype=jnp.int32)[None, :]).astype(jnp.int32)
    csum = jnp.cumsum(onehot, axis=0)
    rank = jnp.sum((csum - onehot) * onehot, axis=1)
    counts = csum[-1]
    tiles_per = (counts + tm - 1) // tm
    tile_end = jnp.cumsum(tiles_per)
    off = (tile_end - tiles_per) * tm
    pos = off[e] + rank
    asg = jnp.full((P,), -1, jnp.int32).at[pos].set(jnp.arange(2 * T, dtype=jnp.int32))
    slot = jnp.arange(P, dtype=jnp.int32)
    spare = 2 * T + ((slot // tm) % 2) * tm + slot % tm
    src = jnp.where(asg >= 0, asg % T, 0).reshape(n_tiles, 1, tm)
    dst = jnp.where(asg >= 0, asg, spare).reshape(n_tiles, 1, tm)
    tile_ids = jnp.arange(n_tiles, dtype=jnp.int32)
    tile_expert = jnp.minimum(jnp.sum(tile_ids[:, None] >= tile_end[None, :], axis=1),
                              N_EXPERTS - 1).astype(jnp.int32)
    meta = jnp.concatenate([jnp.stack([l, tile_end[-1].astype(jnp.int32)]), tile_expert])
    return src, dst, meta


def kernel(x, c, positions, w_ada, b_ada, g_norm_mix, g_norm_ffn, w_in, w_fg, b_fg, g_gla_out,
           g_q, g_k, w_gla_proj, w_attn_proj, w_out, w_route_group, b_route_group,
           w_route_expert, b_route_expert, w_exp_gate, w_exp_up, w_exp_down):
    B, S, D = x.shape
    L = w_ada.shape[0]
    T = B * S
    dk_all = w_fg.shape[-1]
    dv_all = g_gla_out.shape[-1]
    att_w = N_DIL * ATT_HEADS * HEAD_DIM
    gla_cols = 2 * dk_all + 2 * dv_all

    mod = _modulation(c, w_ada, b_ada)
    cos2, sin2 = _rope_tables(positions)

    wfg_pad = jnp.zeros((L, LANES, dk_all), F32).at[:, :GLA_RANK].set(w_fg)
    wr_pad = jnp.zeros((L, D, LANES), F32)
    wr_pad = wr_pad.at[:, :, :N_GROUPS].set(w_route_group)
    wr_pad = wr_pad.at[:, :, N_GROUPS:N_GROUPS + N_EXPERTS].set(w_route_expert)
    br_pad = jnp.zeros((L, 1, LANES), F32)
    br_pad = br_pad.at[:, 0, :N_GROUPS].set(b_route_group)
    br_pad = br_pad.at[:, 0, N_GROUPS:N_GROUPS + N_EXPERTS].set(b_route_expert)
    g_mix = g_norm_mix.reshape(L, 1, D)
    g_ffn = g_norm_ffn.reshape(L, 1, D)
    b_fg3 = b_fg.reshape(L, 1, dk_all)
    g_gla3 = g_gla_out.reshape(L, 1, dv_all)
    g_q3 = g_q.reshape(L, 1, HEAD_DIM)
    g_k3 = g_k.reshape(L, 1, HEAD_DIM)

    def residual(acc, xres, gt):
        return xres + gt * acc

    def layer(l, xt):
        l = jnp.asarray(l, jnp.int32)
        mod3 = lax.dynamic_index_in_dim(mod, l, 0, keepdims=False).reshape(8, 1, N_MOD * D)
        h = _norm(xt, l, g_mix, mod3, 0, S)
        proj1 = _matmul(h, w_in, l, col0=0, n=gla_cols, out_dtype=BF16, tn=2 * MM_TN,
                        tm=2 * MM_TM, name="proj_gla")
        fg = _matmul(h, w_in, l, col0=gla_cols, n=LANES, tn=LANES, out_dtype=F32, name="proj_fg")
        proj2 = _matmul(h, w_in, l, col0=gla_cols, shift=GLA_RANK, n=3 * att_w + 2 * D,
                        out_dtype=BF16, tm=2 * MM_TM, name="proj_att")
        o_gla = _gla(proj1, fg, l, wfg_pad, b_fg3, g_gla3, B, S)
        prep = _qkprep(proj2, l, cos2, sin2, g_q3, g_k3)
        outs, lses = [], []
        for grp, (_, dil) in enumerate(DIL_GROUPS):
            o, lse = _attn_group(prep[grp], prep[N_DIL + grp], prep[2 * N_DIL + grp], dil, B, S)
            outs.append(o)
            lses.append(lse)
        o_att = _attn_mix(outs, lses)
        merged = _merge(o_gla, o_att, proj2, l, w_gla_proj, w_attn_proj, 3 * att_w)
        gt_spec = pl.BlockSpec((None, 1, MM_TN),
                               lambda j, i, l: (i * MM_TM // S, 0, 2 * (D // MM_TN) + j))
        xt = _matmul(merged, w_out, l, col0=0, n=D, out_dtype=F32, epilogue=residual,
                     extras=(xt, mod3),
                     extra_specs=(pl.BlockSpec((MM_TM, MM_TN), lambda j, i, l: (i, j)), gt_spec),
                     name="out_proj")
        h2, route = _router(xt, l, g_ffn, mod3, wr_pad, br_pad, S)
        src, dst, meta = _dispatch_plan(route, l)
        nj = D // LANES
        n_out = 2 * T + 2 * EXPERT_TM
        ys = _experts(h2.reshape(T, nj, LANES), src, dst, meta, w_exp_gate, w_exp_up, w_exp_down,
                      n_out)
        return _combine(xt, ys.reshape(n_out * nj, LANES), route, mod3, S)

    xt = lax.fori_loop(0, L, layer, x.reshape(T, D))
    return xt.reshape(B, S, D)
```

```python
import functools

import jax
import jax.numpy as jnp
from jax import lax
from jax.experimental import pallas as pl
from jax.experimental.pallas import tpu as pltpu

F32 = jnp.float32
BF16 = jnp.bfloat16

GLA_HEADS = 4
GLA_RANK = 16
GLA_TAU = 16.0
GLA_CHUNK = 64
DIL_GROUPS = ((128, 1), (512, 4), (2048, 16))
N_DIL = 3
ATT_HEADS = 4
HEAD_DIM = 128
ATT_BLOCK = 128
ROPE_THETA = 10000.0
N_GROUPS = 4
EXPERTS_PER_GROUP = 8
N_EXPERTS = N_GROUPS * EXPERTS_PER_GROUP
N_MOD = 6
EPS = 1e-6

LANES = 128
VMEM_LIMIT = 56 * 1024 * 1024
NEG_BIG = -1e30

NORM_ROWS = 512
MM_TM = 1024
MM_TN = 512
GLA_ROWS = 256
PREP_ROWS = 512
ATT_SPAN = 2048
MERGE_ROWS = 1024
EXPERT_TM = 256
COMBINE_ROWS = 256


def _params(*sem):
    return pltpu.CompilerParams(dimension_semantics=sem, vmem_limit_bytes=VMEM_LIMIT)


def _dot(a, b):
    return jnp.dot(a, b, preferred_element_type=F32)


def _dot_nt(a, b):
    return lax.dot_general(a, b, (((1,), (1,)), ((), ())), preferred_element_type=F32)


def _dot_tn(a, b):
    return lax.dot_general(a, b, (((0,), (0,)), ((), ())), preferred_element_type=F32)


def _split_bf16(x):
    hi = x.astype(BF16)
    lo = (x - hi.astype(F32)).astype(BF16)
    return hi, lo


def _sigmoid(x):
    return 1.0 / (1.0 + jnp.exp(-x))


def _grid_call(kernel, l, grid, in_specs, out_specs, out_shape, scratch=(), sem=None, name=None):
    sem = sem or ("arbitrary",) * len(grid)
    return pl.pallas_call(
        kernel,
        grid_spec=pltpu.PrefetchScalarGridSpec(
            num_scalar_prefetch=1, grid=grid, in_specs=in_specs, out_specs=out_specs,
            scratch_shapes=list(scratch)),
        out_shape=out_shape,
        compiler_params=_params(*sem),
        name=name,
    ), jnp.reshape(l, (1,)).astype(jnp.int32)


def _mod_kernel(c_ref, w_ref, b_ref, o_ref, act_ref):
    n_batch = c_ref.shape[0]
    tn = w_ref.shape[1]

    @pl.when((pl.program_id(0) == 0) & (pl.program_id(1) == 0))
    def _():
        c = c_ref[...]
        act_ref[...] = c * _sigmoid(c)

    rows = [[] for _ in range(n_batch)]
    for jb in range(tn // LANES):
        w = w_ref[:, jb * LANES:(jb + 1) * LANES]
        for b in range(n_batch):
            rows[b].append(jnp.sum(w * act_ref[b], axis=0, keepdims=True))
    out = jnp.concatenate([jnp.concatenate(r, axis=1) for r in rows], axis=0) + b_ref[...]
    o_ref[0:n_batch, :] = out
    o_ref[n_batch:, :] = jnp.zeros((o_ref.shape[0] - n_batch, tn), F32)


def _modulation(c, w_ada, b_ada):
    L, D, N = w_ada.shape
    B = c.shape[0]
    rows = 8
    assert B <= rows
    c_rep = jnp.broadcast_to(c[:, :, None], (B, D, LANES))
    tn = 512
    return pl.pallas_call(
        _mod_kernel,
        grid=(L, N // tn),
        in_specs=[pl.BlockSpec((B, D, LANES), lambda l, j: (0, 0, 0)),
                  pl.BlockSpec((None, D, tn), lambda l, j: (l, 0, j)),
                  pl.BlockSpec((None, 1, tn), lambda l, j: (l, 0, j))],
        out_specs=pl.BlockSpec((None, rows, tn), lambda l, j: (l, 0, j)),
        out_shape=jax.ShapeDtypeStruct((L, rows, N), F32),
        scratch_shapes=[pltpu.VMEM((B, D, LANES), F32)],
        compiler_params=_params("arbitrary", "arbitrary"),
        name="adaln_mod",
    )(c_rep, w_ada, b_ada.reshape(L, 1, N))


def _rope_kernel(pos_ref, freq_ref, cos_ref, sin_ref):
    ang = pos_ref[...].astype(F32) * freq_ref[...]
    lane = lax.broadcasted_iota(jnp.int32, ang.shape, 1)
    cos_ref[...] = jnp.cos(ang)
    sin_ref[...] = jnp.where(lane < HEAD_DIM // 2, -jnp.sin(ang), jnp.sin(ang))


def _rope_tables(positions):
    T = positions.size
    inv_freq = ROPE_THETA ** (-jnp.arange(0, HEAD_DIM, 2, dtype=F32) / HEAD_DIM)
    freq2 = jnp.concatenate([inv_freq, inv_freq]).reshape(1, HEAD_DIM)
    rows = 2048
    return pl.pallas_call(
        _rope_kernel,
        grid=(T // rows,),
        in_specs=[pl.BlockSpec((rows, 1), lambda i: (i, 0)),
                  pl.BlockSpec((1, HEAD_DIM), lambda i: (0, 0))],
        out_specs=[pl.BlockSpec((rows, HEAD_DIM), lambda i: (i, 0))] * 2,
        out_shape=[jax.ShapeDtypeStruct((T, HEAD_DIM), F32)] * 2,
        compiler_params=_params("parallel"),
        name="rope_tables",
    )(positions.reshape(T, 1), freq2)


def _ada_norm(x, g, scale, shift):
    y = x * lax.rsqrt(jnp.mean(x * x, axis=-1, keepdims=True) + EPS)
    return y * g * (1.0 + scale) + shift


def _norm_kernel(l_ref, x_ref, g_ref, sh_ref, sc_ref, o_ref):
    o_ref[...] = _ada_norm(x_ref[...], g_ref[...], sc_ref[...], sh_ref[...]).astype(o_ref.dtype)


def _mod_spec(which, rows, seq):
    return lambda i, l: (i * rows // seq, 0, which)


def _norm(x, l, g_all, mod3, which_shift, seq):
    T, D = x.shape
    rows = NORM_ROWS
    call, lidx = _grid_call(
        _norm_kernel, l, (T // rows,),
        [pl.BlockSpec((rows, D), lambda i, l: (i, 0)),
         pl.BlockSpec((None, 1, D), lambda i, l: (l[0], 0, 0)),
         pl.BlockSpec((None, 1, D), _mod_spec(which_shift, rows, seq)),
         pl.BlockSpec((None, 1, D), _mod_spec(which_shift + 1, rows, seq))],
        pl.BlockSpec((rows, D), lambda i, l: (i, 0)),
        jax.ShapeDtypeStruct((T, D), BF16), sem=("parallel",), name="ada_norm")
    return call(lidx, x, g_all, mod3, mod3)


def _mm_kernel(l_ref, a_ref, w_ref, *rest, epilogue, n_extra, shift):
    n_w = 1 if shift else 0
    extra = rest[n_w:n_w + n_extra]
    o_ref = rest[n_w + n_extra]
    wbf_ref = rest[n_w + n_extra + 1]
    tn = o_ref.shape[1]

    @pl.when(pl.program_id(1) == 0)
    def _():
        if shift:
            wide = jnp.concatenate([w_ref[...], rest[0][...]], axis=1)
            wbf_ref[...] = wide[:, shift:shift + tn].astype(BF16)
        else:
            wbf_ref[...] = w_ref[...].astype(BF16)

    acc = _dot(a_ref[...], wbf_ref[...])
    if epilogue is not None:
        acc = epilogue(acc, *[e[...] for e in extra])
    o_ref[...] = acc.astype(o_ref.dtype)


def _matmul(a, w, l, *, col0, n, out_dtype, tn=MM_TN, tm=MM_TM, shift=0, epilogue=None,
            extras=(), extra_specs=(), name="matmul"):
    M, K = a.shape
    assert col0 % tn == 0 and n % tn == 0 and M % tm == 0 and 0 <= shift < LANES
    cb0 = col0 // tn
    kern = functools.partial(_mm_kernel, epilogue=epilogue, n_extra=len(extras), shift=shift)
    w_specs = [pl.BlockSpec((None, K, tn), lambda j, i, l: (l[0], 0, cb0 + j))]
    if shift:
        per = tn // LANES
        w_specs.append(pl.BlockSpec((None, K, LANES), lambda j, i, l: (l[0], 0, (cb0 + j + 1) * per)))
    call, lidx = _grid_call(
        kern, l, (n // tn, M // tm),
        [pl.BlockSpec((tm, K), lambda j, i, l: (i, 0))] + w_specs + list(extra_specs),
        pl.BlockSpec((tm, tn), lambda j, i, l: (i, j)),
        jax.ShapeDtypeStruct((M, n), out_dtype),
        scratch=[pltpu.VMEM((K, tn), BF16)], name=name)
    return call(lidx, a, *([w] * len(w_specs)), *extras)


def _gla_kernel(l_ref, q_ref, k_ref, v_ref, r_ref, fg_ref, wfg_ref, bfg_ref, g_ref, o_ref,
                state_ref, *, head_k, head_v):
    rows = q_ref.shape[0]
    C = GLA_CHUNK

    @pl.when(pl.program_id(1) == 0)
    def _():
        state_ref[...] = jnp.zeros_like(state_ref)

    fh, fl = _split_bf16(fg_ref[...])
    wh, wl = _split_bf16(wfg_ref[...])
    z = _dot(fh, wh) + _dot(fl, wh) + _dot(fh, wl) + bfg_ref[...]
    log_a = (jnp.minimum(z, 0.0) - jnp.log(1.0 + jnp.exp(-jnp.abs(z)))) * (1.0 / GLA_TAU)

    ri = lax.broadcasted_iota(jnp.int32, (C, C), 0)
    ci = lax.broadcasted_iota(jnp.int32, (C, C), 1)
    causal = ri >= ci
    tri = causal.astype(BF16)
    scale = head_k ** -0.5

    for c in range(rows // C):
        rs = slice(c * C, (c + 1) * C)
        lh, ll = _split_bf16(log_a[rs])
        b = _dot(tri, lh) + _dot(tri, ll)
        b_last = b[C - 1:C]
        q = q_ref[rs, :].astype(F32) * scale
        k = k_ref[rs, :].astype(F32)
        q_dec = (q * jnp.exp(b)).astype(BF16)
        k_inv = (k * jnp.exp(-b)).astype(BF16)
        k_upd = (k * jnp.exp(b_last - b)).astype(BF16)
        decay = jnp.exp(b_last)
        for h in range(GLA_HEADS):
            ks = slice(h * head_k, (h + 1) * head_k)
            vs = slice(h * head_v, (h + 1) * head_v)
            v = v_ref[rs, vs]
            att = jnp.where(causal, _dot_nt(q_dec[:, ks], k_inv[:, ks]), 0.0).astype(BF16)
            st = state_ref[h]
            o = _dot(att, v) + _dot_nt(q_dec[:, ks], st.astype(BF16))
            state_ref[h] = st * decay[:, ks] + _dot_tn(v, k_upd[:, ks])
            o = o * lax.rsqrt(jnp.mean(o * o, axis=-1, keepdims=True) + EPS) * g_ref[:, vs]
            r = r_ref[rs, vs].astype(F32)
            o_ref[rs, vs] = (o * (r * _sigmoid(r))).astype(o_ref.dtype)


def _gla(proj, fg, l, wfg_pad, b_fg, g_gla_out, batch, seq):
    T = proj.shape[0]
    dk_all = wfg_pad.shape[-1]
    dv_all = g_gla_out.shape[-1]
    rows = GLA_ROWS
    nblk = seq // rows
    kern = functools.partial(_gla_kernel, head_k=dk_all // GLA_HEADS, head_v=dv_all // GLA_HEADS)
    row = lambda b, n, l: b * nblk + n
    call, lidx = _grid_call(
        kern, l, (batch, nblk),
        [pl.BlockSpec((rows, dk_all), lambda b, n, l: (row(b, n, l), 0)),
         pl.BlockSpec((rows, dk_all), lambda b, n, l: (row(b, n, l), 1)),
         pl.BlockSpec((rows, dv_all), lambda b, n, l: (row(b, n, l), 1)),
         pl.BlockSpec((rows, dv_all), lambda b, n, l: (row(b, n, l), 2)),
         pl.BlockSpec((rows, LANES), lambda b, n, l: (row(b, n, l), 0)),
         pl.BlockSpec((None, LANES, dk_all), lambda b, n, l: (l[0], 0, 0)),
         pl.BlockSpec((None, 1, dk_all), lambda b, n, l: (l[0], 0, 0)),
         pl.BlockSpec((None, 1, dv_all), lambda b, n, l: (l[0], 0, 0))],
        pl.BlockSpec((rows, dv_all), lambda b, n, l: (row(b, n, l), 0)),
        jax.ShapeDtypeStruct((T, dv_all), BF16),
        scratch=[pltpu.VMEM((GLA_HEADS, dv_all // GLA_HEADS, dk_all // GLA_HEADS), F32)],
        sem=("parallel", "arbitrary"), name="gla")
    return call(lidx, proj, proj, proj, proj, fg, wfg_pad, b_fg, g_gla_out)


def _qkprep_kernel(l_ref, q_ref, k_ref, v_ref, cos_ref, sin_ref, gq_ref, gk_ref, *refs):
    out_refs, scr_ref = refs[:-1], refs[-1]
    cos = cos_ref[...]
    sin = sin_ref[...]
    gw = ATT_HEADS * HEAD_DIM
    rows = q_ref.shape[0]

    def prep(ref, g, scale):
        x = ref.astype(F32)
        y = x * lax.rsqrt(jnp.mean(x * x, axis=-1, keepdims=True) + EPS) * g
        y = y * cos + pltpu.roll(y, HEAD_DIM // 2, 1) * sin
        return y * scale

    def emit(dst_ref, dil):
        n = rows // dil
        for p in range(dil):
            for h in range(ATT_HEADS):
                cs = slice(p * gw + h * HEAD_DIM, p * gw + (h + 1) * HEAD_DIM)
                dst_ref[:, cs] = scr_ref[h, pl.ds(p, n, stride=dil), :].astype(BF16)

    for grp, (_, dil) in enumerate(DIL_GROUPS):
        for ref, g_ref, scale, out in ((q_ref, gq_ref, HEAD_DIM ** -0.5, out_refs[grp]),
                                       (k_ref, gk_ref, 1.0, out_refs[N_DIL + grp])):
            for h in range(ATT_HEADS):
                src = slice(grp * gw + h * HEAD_DIM, grp * gw + (h + 1) * HEAD_DIM)
                y = prep(ref[:, src], g_ref[...], scale)
                if dil == 1:
                    out[:, h * HEAD_DIM:(h + 1) * HEAD_DIM] = y.astype(BF16)
                else:
                    scr_ref[h] = y
            if dil > 1:
                emit(out, dil)
        if dil == 1:
            out_refs[2 * N_DIL + grp][...] = v_ref[:, grp * gw:(grp + 1) * gw]
        else:
            for h in range(ATT_HEADS):
                src = slice(grp * gw + h * HEAD_DIM, grp * gw + (h + 1) * HEAD_DIM)
                scr_ref[h] = v_ref[:, src].astype(F32)
            emit(out_refs[2 * N_DIL + grp], dil)


def _qkprep(proj2, l, cos2, sin2, g_q, g_k):
    T = proj2.shape[0]
    rows = PREP_ROWS
    width = N_DIL * ATT_HEADS * HEAD_DIM
    gw = ATT_HEADS * HEAD_DIM
    dils = [dil for _, dil in DIL_GROUPS] * 3
    call, lidx = _grid_call(
        _qkprep_kernel, l, (T // rows,),
        [pl.BlockSpec((rows, width), lambda i, l: (i, 0)),
         pl.BlockSpec((rows, width), lambda i, l: (i, 1)),
         pl.BlockSpec((rows, width), lambda i, l: (i, 2)),
         pl.BlockSpec((rows, HEAD_DIM), lambda i, l: (i, 0)),
         pl.BlockSpec((rows, HEAD_DIM), lambda i, l: (i, 0)),
         pl.BlockSpec((None, 1, HEAD_DIM), lambda i, l: (l[0], 0, 0)),
         pl.BlockSpec((None, 1, HEAD_DIM), lambda i, l: (l[0], 0, 0))],
        [pl.BlockSpec((rows // d, d * gw), lambda i, l: (i, 0)) for d in dils],
        [jax.ShapeDtypeStruct((T // d, d * gw), BF16) for d in dils],
        scratch=[pltpu.VMEM((ATT_HEADS, rows, HEAD_DIM), F32)],
        sem=("parallel",), name="qk_prep")
    return call(lidx, proj2, proj2, proj2, cos2, sin2, g_q, g_k)


def _attn_kernel(q_ref, kc_ref, kp_ref, vc_ref, vp_ref, o_ref, lse_ref, *, dil):
    first = pl.program_id(1) == 0
    QB = q_ref.shape[0]
    A = ATT_BLOCK
    qi = lax.broadcasted_iota(jnp.int32, (A, 2 * A), 0)
    kj = lax.broadcasted_iota(jnp.int32, (A, 2 * A), 1) - A
    dist = qi - kj
    valid = (dist >= 0) & (dist <= A)
    bias = jnp.where(valid, 0.0, NEG_BIG).astype(F32)
    bias_first = jnp.where(first, jnp.where(valid & (kj >= 0), 0.0, NEG_BIG), bias).astype(F32)

    for p in range(dil):
        for h in range(ATT_HEADS):
            cs = slice((p * ATT_HEADS + h) * HEAD_DIM, (p * ATT_HEADS + h + 1) * HEAD_DIM)
            for j in range(QB // A):
                rs = slice(j * A, (j + 1) * A)
                if j == 0:
                    kcat = jnp.concatenate([kp_ref[:, cs], kc_ref[0:A, cs]], axis=0)
                    vcat = jnp.concatenate([vp_ref[:, cs], vc_ref[0:A, cs]], axis=0)
                    bb = bias_first
                else:
                    kcat = kc_ref[(j - 1) * A:(j + 1) * A, cs]
                    vcat = vc_ref[(j - 1) * A:(j + 1) * A, cs]
                    bb = bias
                s = _dot_nt(q_ref[rs, cs], kcat) + bb
                m = jnp.max(s, axis=-1, keepdims=True)
                e = jnp.exp(s - m)
                den = jnp.sum(e, axis=-1, keepdims=True)
                o = _dot(e.astype(BF16), vcat) / den
                o_ref[rs, cs] = o.astype(o_ref.dtype)
                lse_ref[rs, cs] = jnp.broadcast_to(m + jnp.log(den), (A, HEAD_DIM))


def _attn_group(q, k, v, dil, batch, seq):
    rows_all, W = q.shape
    A = ATT_BLOCK
    QB = ATT_SPAN // dil
    nsp = seq // ATT_SPAN
    per = QB // A
    cur = pl.BlockSpec((QB, W), lambda b, n: (b * nsp + n, 0))
    prev = pl.BlockSpec((A, W), lambda b, n: (jnp.maximum((b * nsp + n) * per - 1, b * nsp * per), 0))
    return pl.pallas_call(
        functools.partial(_attn_kernel, dil=dil),
        grid=(batch, nsp),
        in_specs=[cur, cur, prev, cur, prev],
        out_specs=[cur, cur],
        out_shape=[jax.ShapeDtypeStruct((rows_all, W), BF16),
                   jax.ShapeDtypeStruct((rows_all, W), F32)],
        compiler_params=_params("parallel", "arbitrary"),
        name=f"dilated_attn_r{dil}",
    )(q, k, k, v, v)


def _attn_mix_kernel(o0, o1, o2, l0, l1, l2, out_ref, so_ref, sl_ref):
    rows, gw = out_ref.shape
    o_refs, l_refs = (o0, o1, o2), (l0, l1, l2)
    for h in range(ATT_HEADS):
        hs = slice(h * HEAD_DIM, (h + 1) * HEAD_DIM)
        outs, lses = [], []
        for g, (_, dil) in enumerate(DIL_GROUPS):
            if dil == 1:
                outs.append(o_refs[g][:, hs].astype(F32))
                lses.append(l_refs[g][:, hs])
                continue
            n = rows // dil
            for p in range(dil):
                cs = slice(p * gw + h * HEAD_DIM, p * gw + (h + 1) * HEAD_DIM)
                so_ref[g, pl.ds(p, n, stride=dil), :] = o_refs[g][:, cs].astype(F32)
                sl_ref[g, pl.ds(p, n, stride=dil), :] = l_refs[g][:, cs]
            outs.append(so_ref[g])
            lses.append(sl_ref[g])
        m = jnp.maximum(jnp.maximum(lses[0], lses[1]), lses[2])
        ws = [jnp.exp(x - m) for x in lses]
        den = ws[0] + ws[1] + ws[2]
        out_ref[:, hs] = ((ws[0] * outs[0] + ws[1] * outs[1] + ws[2] * outs[2]) / den
                          ).astype(out_ref.dtype)


def _attn_mix(outs, lses):
    gw = ATT_HEADS * HEAD_DIM
    T = outs[0].shape[0]
    rows = PREP_ROWS
    dils = [dil for _, dil in DIL_GROUPS]
    specs = [pl.BlockSpec((rows // d, d * gw), lambda i: (i, 0)) for d in dils]
    return pl.pallas_call(
        _attn_mix_kernel,
        grid=(T // rows,),
        in_specs=specs + specs,
        out_specs=pl.BlockSpec((rows, gw), lambda i: (i, 0)),
        out_shape=jax.ShapeDtypeStruct((T, gw), BF16),
        scratch_shapes=[pltpu.VMEM((N_DIL, rows, HEAD_DIM), F32),
                        pltpu.VMEM((N_DIL, rows, HEAD_DIM), F32)],
        compiler_params=_params("parallel"),
        name="attn_mix",
    )(*outs, *lses)


def _merge_kernel(l_ref, ogla_ref, oatt_ref, gg_ref, ga_ref, wg_ref, wa_ref,
                  out_ref, wgbf_ref, wabf_ref):
    @pl.when(pl.program_id(1) == 0)
    def _():
        wgbf_ref[...] = wg_ref[...].astype(BF16)
        wabf_ref[...] = wa_ref[...].astype(BF16)

    a = _dot(ogla_ref[...], wgbf_ref[...])
    b = _dot(oatt_ref[...], wabf_ref[...])
    gg = _sigmoid(gg_ref[...].astype(F32))
    ga = _sigmoid(ga_ref[...].astype(F32))
    out_ref[...] = (gg * a + ga * b).astype(out_ref.dtype)


def _merge(o_gla, o_att, gates, l, w_gla_proj, w_attn_proj):
    T, dv = o_gla.shape
    gw = o_att.shape[1]
    D = w_gla_proj.shape[-1]
    tm, tn = MERGE_ROWS, 2 * MM_TN
    gcb = 0
    dcb = D // tn
    call, lidx = _grid_call(
        _merge_kernel, l, (D // tn, T // tm),
        [pl.BlockSpec((tm, dv), lambda j, i, l: (i, 0)),
         pl.BlockSpec((tm, gw), lambda j, i, l: (i, 0)),
         pl.BlockSpec((tm, tn), lambda j, i, l: (i, gcb + j)),
         pl.BlockSpec((tm, tn), lambda j, i, l: (i, gcb + dcb + j)),
         pl.BlockSpec((None, dv, tn), lambda j, i, l: (l[0], 0, j)),
         pl.BlockSpec((None, gw, tn), lambda j, i, l: (l[0], 0, j))],
        pl.BlockSpec((tm, tn), lambda j, i, l: (i, j)),
        jax.ShapeDtypeStruct((T, D), BF16),
        scratch=[pltpu.VMEM((dv, tn), BF16), pltpu.VMEM((gw, tn), BF16)], name="gated_merge")
    return call(lidx, o_gla, o_att, gates, gates, w_gla_proj, w_attn_proj)


def _router_kernel(l_ref, x_ref, g_ref, sh_ref, sc_ref, wr_ref, br_ref, h_ref, route_ref):
    h = _ada_norm(x_ref[...], g_ref[...], sc_ref[...], sh_ref[...])
    rows, d = h.shape
    for j in range(d // LANES):
        h_ref[pl.ds(j, rows, stride=d // LANES), :] = h[:, j * LANES:(j + 1) * LANES]
    lg = _dot(h.astype(BF16), wr_ref[...].astype(BF16)) + br_ref[...]
    lane = lax.broadcasted_iota(jnp.int32, lg.shape, 1).astype(F32)
    big = float(4 * LANES)

    def first_argmax(vals):
        mx = jnp.max(vals, axis=-1, keepdims=True)
        idx = jnp.min(jnp.where(vals == mx, lane, big), axis=-1, keepdims=True)
        return mx, idx

    gl = jnp.where(lane < N_GROUPS, lg, NEG_BIG)
    gmax, gidx = first_argmax(gl)
    g_weight = 1.0 / jnp.sum(jnp.exp(gl - gmax), axis=-1, keepdims=True)
    lo = N_GROUPS + gidx * EXPERTS_PER_GROUP
    el = jnp.where((lane >= lo) & (lane < lo + EXPERTS_PER_GROUP), lg, NEG_BIG)
    m1, i1 = first_argmax(el)
    m2, i2 = first_argmax(jnp.where(lane == i1, NEG_BIG, el))
    e2 = jnp.exp(m2 - m1)
    w1 = g_weight / (1.0 + e2)
    w2 = g_weight * e2 / (1.0 + e2)
    route = jnp.where(lane == 0, i1 - N_GROUPS,
                      jnp.where(lane == 1, i2 - N_GROUPS,
                                jnp.where(lane == 2, w1, jnp.where(lane == 3, w2, 0.0))))
    route_ref[...] = route


def _router(x, l, g_all, mod3, wr_pad, br_pad, seq):
    T, D = x.shape
    rows = NORM_ROWS
    call, lidx = _grid_call(
        _router_kernel, l, (T // rows,),
        [pl.BlockSpec((rows, D), lambda i, l: (i, 0)),
         pl.BlockSpec((None, 1, D), lambda i, l: (l[0], 0, 0)),
         pl.BlockSpec((None, 1, D), _mod_spec(3, rows, seq)),
         pl.BlockSpec((None, 1, D), _mod_spec(4, rows, seq)),
         pl.BlockSpec((None, D, LANES), lambda i, l: (l[0], 0, 0)),
         pl.BlockSpec((None, 1, LANES), lambda i, l: (l[0], 0, 0))],
        [pl.BlockSpec((rows * (D // LANES), LANES), lambda i, l: (i, 0)),
         pl.BlockSpec((rows, LANES), lambda i, l: (i, 0))],
        [jax.ShapeDtypeStruct((T * (D // LANES), LANES), F32),
         jax.ShapeDtypeStruct((T, LANES), F32)],
        sem=("parallel",), name="norm_router")
    return call(lidx, x, g_all, mod3, mod3, wr_pad, br_pad)


def _expert_kernel(meta_ref, src_one_ref, src_cur_ref, src_nxt_ref, dst_ref, h_ref,
                   wg_ref, wu_ref, wd_ref, ys_ref, xbuf_ref, obuf_ref,
                   wgbf_ref, wubf_ref, wdbf_ref, gsem_ref, ssem_ref):
    i = pl.program_id(0)
    n_used = meta_ref[1]
    slot = i % 2
    nj, tm = xbuf_ref.shape[1], xbuf_ref.shape[2]

    def gather_copy(idx_ref, r, s):
        return pltpu.make_async_copy(h_ref.at[idx_ref[0, 0, r]], xbuf_ref.at[s, :, r, :],
                                     gsem_ref.at[s])

    def scatter_copy(idx_ref, r, s):
        return pltpu.make_async_copy(obuf_ref.at[s, :, r, :], ys_ref.at[idx_ref[0, 0, r]],
                                     ssem_ref.at[s])

    def gather_start(idx_ref, s):
        def body(r, c):
            gather_copy(idx_ref, r, s).start()
            return c
        lax.fori_loop(0, tm, body, 0, unroll=8)

    def gather_wait(s):
        pltpu.make_async_copy(obuf_ref.at[0], xbuf_ref.at[s], gsem_ref.at[s]).wait()

    def scatter_wait(s):
        pltpu.make_async_copy(xbuf_ref.at[0], obuf_ref.at[s], ssem_ref.at[s]).wait()

    @pl.when(i == 0)
    def _():
        gather_start(src_cur_ref, 0)

        @pl.when(n_used > 1)
        def _():
            gather_start(src_one_ref, 1)
        obuf_ref[1] = jnp.zeros(obuf_ref.shape[1:], F32)
        n_rows = ys_ref.shape[0]

        def fill(r, c):
            pltpu.make_async_copy(obuf_ref.at[1, :, r % tm, :], ys_ref.at[n_rows - 2 * tm + r],
                                  ssem_ref.at[1]).start()
            return c
        lax.fori_loop(0, 2 * tm, fill, 0, unroll=8)
        scatter_wait(1)
        scatter_wait(1)

    @pl.when(i < n_used)
    def _():
        xslot = i % 3
        gather_wait(xslot)

        @pl.when(i >= 2)
        def _():
            scatter_wait(slot)

        @pl.when((i == 0) | (meta_ref[2 + i] != meta_ref[2 + jnp.maximum(i - 1, 0)]))
        def _():
            wgbf_ref[...] = wg_ref[...].astype(BF16)
            wubf_ref[...] = wu_ref[...].astype(BF16)
            wdbf_ref[...] = wd_ref[...].astype(BF16)

        x = jnp.concatenate([xbuf_ref[xslot, j] for j in range(nj)], axis=1).astype(BF16)
        a = _dot(x, wgbf_ref[...])
        u = _dot(x, wubf_ref[...])
        hid = (a * _sigmoid(a)) * u
        y = _dot(hid.astype(BF16), wdbf_ref[...])
        for j in range(nj):
            obuf_ref[slot, j] = y[:, j * LANES:(j + 1) * LANES]
        for r in range(tm):
            scatter_copy(dst_ref, r, slot).start(priority=r % 2)

        @pl.when(i + 2 < n_used)
        def _():
            nslot = (i + 2) % 3
            for r in range(tm):
                gather_copy(src_nxt_ref, r, nslot).start(priority=r % 2)

        @pl.when(i == n_used - 1)
        def _():
            @pl.when(i >= 1)
            def _():
                scatter_wait(1 - slot)
            scatter_wait(slot)


def _experts(h3, src, dst, meta, w_gate, w_up, w_down, n_out):
    T, nj, _ = h3.shape
    D = nj * LANES
    F = w_gate.shape[-1]
    n_tiles, _, tm = src.shape
    wspec = lambda shape: pl.BlockSpec((None, None) + shape, lambda i, m: (m[0], m[2 + i], 0, 0))
    idx_spec = lambda f: pl.BlockSpec((1, 1, tm), lambda i, m: (f(i), 0, 0), memory_space=pltpu.SMEM)
    return pl.pallas_call(
        _expert_kernel,
        grid_spec=pltpu.PrefetchScalarGridSpec(
            num_scalar_prefetch=1, grid=(n_tiles,),
            in_specs=[idx_spec(lambda i: min(1, n_tiles - 1)), idx_spec(lambda i: i),
                      idx_spec(lambda i: jnp.minimum(i + 2, n_tiles - 1)), idx_spec(lambda i: i),
                      pl.BlockSpec(memory_space=pl.ANY),
                      wspec((D, F)), wspec((D, F)), wspec((F, D))],
            out_specs=pl.BlockSpec(memory_space=pl.ANY),
            scratch_shapes=[pltpu.VMEM((3, nj, tm, LANES), F32), pltpu.VMEM((2, nj, tm, LANES), F32),
                            pltpu.VMEM((D, F), BF16), pltpu.VMEM((D, F), BF16),
                            pltpu.VMEM((F, D), BF16),
                            pltpu.SemaphoreType.DMA((3,)), pltpu.SemaphoreType.DMA((2,))]),
        out_shape=jax.ShapeDtypeStruct((n_out, nj, LANES), F32),
        compiler_params=_params("arbitrary"),
        name="grouped_experts",
    )(meta, src, src, src, dst, h3, w_gate, w_up, w_down)


def _combine_kernel(x_ref, gt_ref, route_ref, y0_ref, y1_ref, o_ref):
    rows, d = x_ref.shape
    nj = d // LANES
    w1 = jnp.broadcast_to(route_ref[:, 2:3], (rows, LANES))
    w2 = jnp.broadcast_to(route_ref[:, 3:4], (rows, LANES))
    for j in range(nj):
        js = slice(j * LANES, (j + 1) * LANES)
        y = w1 * y0_ref[pl.ds(j, rows, stride=nj), :] + w2 * y1_ref[pl.ds(j, rows, stride=nj), :]
        o_ref[:, js] = x_ref[:, js] + gt_ref[:, js] * y


def _combine(x, ys, route, mod3, seq):
    T, D = x.shape
    rows = COMBINE_ROWS
    nblk = T // rows
    nj = D // LANES
    return pl.pallas_call(
        _combine_kernel,
        grid=(nblk,),
        in_specs=[pl.BlockSpec((rows, D), lambda i: (i, 0)),
                  pl.BlockSpec((None, 1, D), lambda i: (i * rows // seq, 0, 5)),
                  pl.BlockSpec((rows, LANES), lambda i: (i, 0)),
                  pl.BlockSpec((rows * nj, LANES), lambda i: (i, 0)),
                  pl.BlockSpec((rows * nj, LANES), lambda i: (nblk + i, 0))],
        out_specs=pl.BlockSpec((rows, D), lambda i: (i, 0)),
        out_shape=jax.ShapeDtypeStruct((T, D), F32),
        compiler_params=_params("parallel"),
        name="moe_combine",
    )(x, mod3, route, ys, ys)


def _dispatch_plan(route, l):
    T = route.shape[0]
    tm = EXPERT_TM
    n_tiles = (2 * T) // tm + N_EXPERTS
    P = n_tiles * tm
    e = route[:, 0:2].astype(jnp.int32).T.reshape(-1)
    onehot = (e[:, None] == jnp.arange(N_EXPERTS, dtype=jnp.int32)[None, :]).astype(jnp.int32)
    csum = jnp.cumsum(onehot, axis=0)
    rank = jnp.sum((csum - onehot) * onehot, axis=1)
    counts = csum[-1]
    tiles_per = (counts + tm - 1) // tm
    tile_end = jnp.cumsum(tiles_per)
    off = (tile_end - tiles_per) * tm
    pos = off[e] + rank
    asg = jnp.full((P,), -1, jnp.int32).at[pos].set(jnp.arange(2 * T, dtype=jnp.int32))
    slot = jnp.arange(P, dtype=jnp.int32)
    spare = 2 * T + ((slot // tm) % 2) * tm + slot % tm
    src = jnp.where(asg >= 0, asg % T, 0).reshape(n_tiles, 1, tm)
    dst = jnp.where(asg >= 0, asg, spare).reshape(n_tiles, 1, tm)
    tile_ids = jnp.arange(n_tiles, dtype=jnp.int32)
    tile_expert = jnp.minimum(jnp.sum(tile_ids[:, None] >= tile_end[None, :], axis=1),
                              N_EXPERTS - 1).astype(jnp.int32)
    meta = jnp.concatenate([jnp.stack([l, tile_end[-1].astype(jnp.int32)]), tile_expert])
    return src, dst, meta


def kernel(x, c, positions, w_ada, b_ada, g_norm_mix, g_norm_ffn, w_in, w_fg, b_fg, g_gla_out,
           g_q, g_k, w_gla_proj, w_attn_proj, w_out, w_route_group, b_route_group,
           w_route_expert, b_route_expert, w_exp_gate, w_exp_up, w_exp_down):
    B, S, D = x.shape
    L = w_ada.shape[0]
    T = B * S
    dk_all = w_fg.shape[-1]
    dv_all = g_gla_out.shape[-1]
    att_w = N_DIL * ATT_HEADS * HEAD_DIM
    gla_cols = 2 * dk_all + 2 * dv_all

    mod = _modulation(c, w_ada, b_ada)
    cos2, sin2 = _rope_tables(positions)

    wfg_pad = jnp.zeros((L, LANES, dk_all), F32).at[:, :GLA_RANK].set(w_fg)
    wr_pad = jnp.zeros((L, D, LANES), F32)
    wr_pad = wr_pad.at[:, :, :N_GROUPS].set(w_route_group)
    wr_pad = wr_pad.at[:, :, N_GROUPS:N_GROUPS + N_EXPERTS].set(w_route_expert)
    br_pad = jnp.zeros((L, 1, LANES), F32)
    br_pad = br_pad.at[:, 0, :N_GROUPS].set(b_route_group)
    br_pad = br_pad.at[:, 0, N_GROUPS:N_GROUPS + N_EXPERTS].set(b_route_expert)
    w_gla_bf = w_gla_proj.astype(BF16)
    w_attn_bf = w_attn_proj.astype(BF16)
    w_out_bf = w_out.astype(BF16)
    g_mix = g_norm_mix.reshape(L, 1, D)
    g_ffn = g_norm_ffn.reshape(L, 1, D)
    b_fg3 = b_fg.reshape(L, 1, dk_all)
    g_gla3 = g_gla_out.reshape(L, 1, dv_all)
    g_q3 = g_q.reshape(L, 1, HEAD_DIM)
    g_k3 = g_k.reshape(L, 1, HEAD_DIM)

    def residual(acc, xres, gt):
        return xres + gt * acc

    def layer(l, xt):
        l = jnp.asarray(l, jnp.int32)
        mod3 = lax.dynamic_index_in_dim(mod, l, 0, keepdims=False).reshape(8, 1, N_MOD * D)
        h = _norm(xt, l, g_mix, mod3, 0, S)
        proj1 = _matmul(h, w_in, l, col0=0, n=gla_cols, out_dtype=BF16, tn=2 * MM_TN,
                        tm=2 * MM_TM, name="proj_gla")
        fg = _matmul(h, w_in, l, col0=gla_cols, n=LANES, tn=LANES, out_dtype=F32, name="proj_fg")
        proj2 = _matmul(h, w_in, l, col0=gla_cols, shift=GLA_RANK, n=3 * att_w,
                        out_dtype=BF16, tm=2 * MM_TM, name="proj_att")
        gates = _matmul(h, w_in, l, col0=gla_cols + 3 * att_w, shift=GLA_RANK, n=2 * D,
                        out_dtype=BF16, tm=2 * MM_TM, name="proj_gates")
        o_gla = _gla(proj1, fg, l, wfg_pad, b_fg3, g_gla3, B, S)
        prep = _qkprep(proj2, l, cos2, sin2, g_q3, g_k3)
        outs, lses = [], []
        for grp, (_, dil) in enumerate(DIL_GROUPS):
            o, lse = _attn_group(prep[grp], prep[N_DIL + grp], prep[2 * N_DIL + grp], dil, B, S)
            outs.append(o)
            lses.append(lse)
        o_att = _attn_mix(outs, lses)
        merged = _merge(o_gla, o_att, gates, l, w_gla_bf, w_attn_bf)
        tn_out = 2 * MM_TN
        gt_spec = pl.BlockSpec((None, 1, tn_out),
                               lambda j, i, l: (i * MM_TM // S, 0, 2 * (D // tn_out) + j))
        xt = _matmul(merged, w_out_bf, l, col0=0, n=D, out_dtype=F32, tn=tn_out, epilogue=residual,
                     extras=(xt, mod3),
                     extra_specs=(pl.BlockSpec((MM_TM, tn_out), lambda j, i, l: (i, j)), gt_spec),
                     name="out_proj")
        h2, route = _router(xt, l, g_ffn, mod3, wr_pad, br_pad, S)
        src, dst, meta = _dispatch_plan(route, l)
        nj = D // LANES
        n_out = 2 * T + 2 * EXPERT_TM
        ys = _experts(h2.reshape(T, nj, LANES), src, dst, meta, w_exp_gate, w_exp_up, w_exp_down,
                      n_out)
        return _combine(xt, ys.reshape(n_out * nj, LANES), route, mod3, S)

    xt = lax.fori_loop(0, L, layer, x.reshape(T, D))
    return xt.reshape(B, S, D)
```

```python
import functools

import jax
import jax.numpy as jnp
from jax import lax
from jax.experimental import pallas as pl
from jax.experimental.pallas import tpu as pltpu

F32 = jnp.float32
BF16 = jnp.bfloat16

GLA_HEADS = 4
GLA_RANK = 16
GLA_TAU = 16.0
GLA_CHUNK = 64
DIL_GROUPS = ((128, 1), (512, 4), (2048, 16))
N_DIL = 3
ATT_HEADS = 4
HEAD_DIM = 128
ATT_BLOCK = 128
ROPE_THETA = 10000.0
N_GROUPS = 4
EXPERTS_PER_GROUP = 8
N_EXPERTS = N_GROUPS * EXPERTS_PER_GROUP
N_MOD = 6
EPS = 1e-6

LANES = 128
VMEM_LIMIT = 56 * 1024 * 1024
NEG_BIG = -1e30

NORM_ROWS = 512
MM_TM = 1024
MM_TN = 512
GLA_ROWS = 256
PREP_ROWS = 512
ATT_SPAN = 2048
MERGE_ROWS = 1024
EXPERT_TM = 256
COMBINE_ROWS = 256


def _params(*sem):
    return pltpu.CompilerParams(dimension_semantics=sem, vmem_limit_bytes=VMEM_LIMIT)


def _dot(a, b):
    return jnp.dot(a, b, preferred_element_type=F32)


def _dot_nt(a, b):
    return lax.dot_general(a, b, (((1,), (1,)), ((), ())), preferred_element_type=F32)


def _dot_tn(a, b):
    return lax.dot_general(a, b, (((0,), (0,)), ((), ())), preferred_element_type=F32)


def _split_bf16(x):
    hi = x.astype(BF16)
    lo = (x - hi.astype(F32)).astype(BF16)
    return hi, lo


def _sigmoid(x):
    return 1.0 / (1.0 + jnp.exp(-x))


def _pack_rows(xb):
    bits = lax.bitcast_convert_type(xb, jnp.uint32)
    nw = xb.shape[1] // (2 * LANES)
    out = []
    for j in range(nw):
        lo = bits[:, j * LANES:(j + 1) * LANES] >> 16
        hi = bits[:, (nw + j) * LANES:(nw + j + 1) * LANES] & jnp.uint32(0xFFFF0000)
        out.append(lo | hi)
    return out


def _unpack_rows(words):
    lo = [lax.bitcast_convert_type(w << 16, F32) for w in words]
    hi = [lax.bitcast_convert_type(w & jnp.uint32(0xFFFF0000), F32) for w in words]
    return jnp.concatenate(lo + hi, axis=1)


def _grid_call(kernel, l, grid, in_specs, out_specs, out_shape, scratch=(), sem=None, name=None):
    sem = sem or ("arbitrary",) * len(grid)
    return pl.pallas_call(
        kernel,
        grid_spec=pltpu.PrefetchScalarGridSpec(
            num_scalar_prefetch=1, grid=grid, in_specs=in_specs, out_specs=out_specs,
            scratch_shapes=list(scratch)),
        out_shape=out_shape,
        compiler_params=_params(*sem),
        name=name,
    ), jnp.reshape(l, (1,)).astype(jnp.int32)


def _mod_kernel(c_ref, w_ref, b_ref, o_ref, act_ref):
    n_batch = c_ref.shape[0]
    tn = w_ref.shape[1]

    @pl.when((pl.program_id(0) == 0) & (pl.program_id(1) == 0))
    def _():
        c = c_ref[...]
        act_ref[...] = c * _sigmoid(c)

    rows = [[] for _ in range(n_batch)]
    for jb in range(tn // LANES):
        w = w_ref[:, jb * LANES:(jb + 1) * LANES]
        for b in range(n_batch):
            rows[b].append(jnp.sum(w * act_ref[b], axis=0, keepdims=True))
    out = jnp.concatenate([jnp.concatenate(r, axis=1) for r in rows], axis=0) + b_ref[...]
    o_ref[0:n_batch, :] = out
    o_ref[n_batch:, :] = jnp.zeros((o_ref.shape[0] - n_batch, tn), F32)


def _modulation(c, w_ada, b_ada):
    L, D, N = w_ada.shape
    B = c.shape[0]
    rows = 8
    assert B <= rows
    c_rep = jnp.broadcast_to(c[:, :, None], (B, D, LANES))
    tn = 512
    return pl.pallas_call(
        _mod_kernel,
        grid=(L, N // tn),
        in_specs=[pl.BlockSpec((B, D, LANES), lambda l, j: (0, 0, 0)),
                  pl.BlockSpec((None, D, tn), lambda l, j: (l, 0, j)),
                  pl.BlockSpec((None, 1, tn), lambda l, j: (l, 0, j))],
        out_specs=pl.BlockSpec((None, rows, tn), lambda l, j: (l, 0, j)),
        out_shape=jax.ShapeDtypeStruct((L, rows, N), F32),
        scratch_shapes=[pltpu.VMEM((B, D, LANES), F32)],
        compiler_params=_params("arbitrary", "arbitrary"),
        name="adaln_mod",
    )(c_rep, w_ada, b_ada.reshape(L, 1, N))


def _rope_kernel(pos_ref, freq_ref, cos_ref, sin_ref):
    ang = pos_ref[...].astype(F32) * freq_ref[...]
    lane = lax.broadcasted_iota(jnp.int32, ang.shape, 1)
    cos_ref[...] = jnp.cos(ang)
    sin_ref[...] = jnp.where(lane < HEAD_DIM // 2, -jnp.sin(ang), jnp.sin(ang))


def _rope_tables(positions):
    T = positions.size
    inv_freq = ROPE_THETA ** (-jnp.arange(0, HEAD_DIM, 2, dtype=F32) / HEAD_DIM)
    freq2 = jnp.concatenate([inv_freq, inv_freq]).reshape(1, HEAD_DIM)
    rows = 2048
    return pl.pallas_call(
        _rope_kernel,
        grid=(T // rows,),
        in_specs=[pl.BlockSpec((rows, 1), lambda i: (i, 0)),
                  pl.BlockSpec((1, HEAD_DIM), lambda i: (0, 0))],
        out_specs=[pl.BlockSpec((rows, HEAD_DIM), lambda i: (i, 0))] * 2,
        out_shape=[jax.ShapeDtypeStruct((T, HEAD_DIM), F32)] * 2,
        compiler_params=_params("parallel"),
        name="rope_tables",
    )(positions.reshape(T, 1), freq2)


def _ada_norm(x, g, scale, shift):
    y = x * lax.rsqrt(jnp.mean(x * x, axis=-1, keepdims=True) + EPS)
    return y * g * (1.0 + scale) + shift


def _norm_kernel(l_ref, x_ref, g_ref, sh_ref, sc_ref, o_ref):
    o_ref[...] = _ada_norm(x_ref[...], g_ref[...], sc_ref[...], sh_ref[...]).astype(o_ref.dtype)


def _mod_spec(which, rows, seq):
    return lambda i, l: (i * rows // seq, 0, which)


def _norm(x, l, g_all, mod3, which_shift, seq):
    T, D = x.shape
    rows = NORM_ROWS
    call, lidx = _grid_call(
        _norm_kernel, l, (T // rows,),
        [pl.BlockSpec((rows, D), lambda i, l: (i, 0)),
         pl.BlockSpec((None, 1, D), lambda i, l: (l[0], 0, 0)),
         pl.BlockSpec((None, 1, D), _mod_spec(which_shift, rows, seq)),
         pl.BlockSpec((None, 1, D), _mod_spec(which_shift + 1, rows, seq))],
        pl.BlockSpec((rows, D), lambda i, l: (i, 0)),
        jax.ShapeDtypeStruct((T, D), BF16), sem=("parallel",), name="ada_norm")
    return call(lidx, x, g_all, mod3, mod3)


def _mm_kernel(l_ref, a_ref, w_ref, *rest, epilogue, n_extra, shift):
    n_w = 1 if shift else 0
    extra = rest[n_w:n_w + n_extra]
    o_ref = rest[n_w + n_extra]
    wbf_ref = rest[n_w + n_extra + 1]
    tn = o_ref.shape[1]

    @pl.when(pl.program_id(1) == 0)
    def _():
        if shift:
            wide = jnp.concatenate([w_ref[...], rest[0][...]], axis=1)
            wbf_ref[...] = wide[:, shift:shift + tn].astype(BF16)
        else:
            wbf_ref[...] = w_ref[...].astype(BF16)

    acc = _dot(a_ref[...], wbf_ref[...])
    if epilogue is not None:
        acc = epilogue(acc, *[e[...] for e in extra])
    o_ref[...] = acc.astype(o_ref.dtype)


def _matmul(a, w, l, *, col0, n, out_dtype, tn=MM_TN, tm=MM_TM, shift=0, epilogue=None,
            extras=(), extra_specs=(), name="matmul"):
    M, K = a.shape
    assert col0 % tn == 0 and n % tn == 0 and M % tm == 0 and 0 <= shift < LANES
    cb0 = col0 // tn
    kern = functools.partial(_mm_kernel, epilogue=epilogue, n_extra=len(extras), shift=shift)
    w_specs = [pl.BlockSpec((None, K, tn), lambda j, i, l: (l[0], 0, cb0 + j))]
    if shift:
        per = tn // LANES
        w_specs.append(pl.BlockSpec((None, K, LANES), lambda j, i, l: (l[0], 0, (cb0 + j + 1) * per)))
    call, lidx = _grid_call(
        kern, l, (n // tn, M // tm),
        [pl.BlockSpec((tm, K), lambda j, i, l: (i, 0))] + w_specs + list(extra_specs),
        pl.BlockSpec((tm, tn), lambda j, i, l: (i, j)),
        jax.ShapeDtypeStruct((M, n), out_dtype),
        scratch=[pltpu.VMEM((K, tn), BF16)], name=name)
    return call(lidx, a, *([w] * len(w_specs)), *extras)


def _gla_kernel(l_ref, q_ref, k_ref, v_ref, r_ref, fg_ref, wfg_ref, bfg_ref, g_ref, o_ref,
                state_ref, *, head_k, head_v):
    rows = q_ref.shape[0]
    C = GLA_CHUNK

    @pl.when(pl.program_id(1) == 0)
    def _():
        state_ref[...] = jnp.zeros_like(state_ref)

    fh, fl = _split_bf16(fg_ref[...])
    wh, wl = _split_bf16(wfg_ref[...])
    z = _dot(fh, wh) + _dot(fl, wh) + _dot(fh, wl) + bfg_ref[...]
    log_a = (jnp.minimum(z, 0.0) - jnp.log(1.0 + jnp.exp(-jnp.abs(z)))) * (1.0 / GLA_TAU)

    ri = lax.broadcasted_iota(jnp.int32, (C, C), 0)
    ci = lax.broadcasted_iota(jnp.int32, (C, C), 1)
    causal = ri >= ci
    tri = causal.astype(BF16)
    scale = head_k ** -0.5

    for c in range(rows // C):
        rs = slice(c * C, (c + 1) * C)
        lh, ll = _split_bf16(log_a[rs])
        b = _dot(tri, lh) + _dot(tri, ll)
        b_last = b[C - 1:C]
        q = q_ref[rs, :].astype(F32) * scale
        k = k_ref[rs, :].astype(F32)
        q_dec = (q * jnp.exp(b)).astype(BF16)
        k_inv = (k * jnp.exp(-b)).astype(BF16)
        k_upd = (k * jnp.exp(b_last - b)).astype(BF16)
        decay = jnp.exp(b_last)
        for h in range(GLA_HEADS):
            ks = slice(h * head_k, (h + 1) * head_k)
            vs = slice(h * head_v, (h + 1) * head_v)
            v = v_ref[rs, vs]
            att = jnp.where(causal, _dot_nt(q_dec[:, ks], k_inv[:, ks]), 0.0).astype(BF16)
            st = state_ref[h]
            o = _dot(att, v) + _dot_nt(q_dec[:, ks], st.astype(BF16))
            state_ref[h] = st * decay[:, ks] + _dot_tn(v, k_upd[:, ks])
            o = o * lax.rsqrt(jnp.mean(o * o, axis=-1, keepdims=True) + EPS) * g_ref[:, vs]
            r = r_ref[rs, vs].astype(F32)
            o_ref[rs, vs] = (o * (r * _sigmoid(r))).astype(o_ref.dtype)


def _gla(proj, fg, l, wfg_pad, b_fg, g_gla_out, batch, seq):
    T = proj.shape[0]
    dk_all = wfg_pad.shape[-1]
    dv_all = g_gla_out.shape[-1]
    rows = GLA_ROWS
    nblk = seq // rows
    kern = functools.partial(_gla_kernel, head_k=dk_all // GLA_HEADS, head_v=dv_all // GLA_HEADS)
    row = lambda b, n, l: b * nblk + n
    call, lidx = _grid_call(
        kern, l, (batch, nblk),
        [pl.BlockSpec((rows, dk_all), lambda b, n, l: (row(b, n, l), 0)),
         pl.BlockSpec((rows, dk_all), lambda b, n, l: (row(b, n, l), 1)),
         pl.BlockSpec((rows, dv_all), lambda b, n, l: (row(b, n, l), 1)),
         pl.BlockSpec((rows, dv_all), lambda b, n, l: (row(b, n, l), 2)),
         pl.BlockSpec((rows, LANES), lambda b, n, l: (row(b, n, l), 0)),
         pl.BlockSpec((None, LANES, dk_all), lambda b, n, l: (l[0], 0, 0)),
         pl.BlockSpec((None, 1, dk_all), lambda b, n, l: (l[0], 0, 0)),
         pl.BlockSpec((None, 1, dv_all), lambda b, n, l: (l[0], 0, 0))],
        pl.BlockSpec((rows, dv_all), lambda b, n, l: (row(b, n, l), 0)),
        jax.ShapeDtypeStruct((T, dv_all), BF16),
        scratch=[pltpu.VMEM((GLA_HEADS, dv_all // GLA_HEADS, dk_all // GLA_HEADS), F32)],
        sem=("parallel", "arbitrary"), name="gla")
    return call(lidx, proj, proj, proj, proj, fg, wfg_pad, b_fg, g_gla_out)


def _qkprep_kernel(l_ref, q_ref, k_ref, v_ref, cos_ref, sin_ref, gq_ref, gk_ref, *refs):
    out_refs, scr_ref = refs[:-1], refs[-1]
    cos = cos_ref[...]
    sin = sin_ref[...]
    gw = ATT_HEADS * HEAD_DIM
    rows = q_ref.shape[0]

    def prep(ref, g, scale):
        x = ref.astype(F32)
        y = x * lax.rsqrt(jnp.mean(x * x, axis=-1, keepdims=True) + EPS) * g
        y = y * cos + pltpu.roll(y, HEAD_DIM // 2, 1) * sin
        return y * scale

    def emit(dst_ref, dil):
        n = rows // dil
        for p in range(dil):
            for h in range(ATT_HEADS):
                cs = slice(p * gw + h * HEAD_DIM, p * gw + (h + 1) * HEAD_DIM)
                dst_ref[:, cs] = scr_ref[h, pl.ds(p, n, stride=dil), :].astype(BF16)

    for grp, (_, dil) in enumerate(DIL_GROUPS):
        for ref, g_ref, scale, out in ((q_ref, gq_ref, HEAD_DIM ** -0.5, out_refs[grp]),
                                       (k_ref, gk_ref, 1.0, out_refs[N_DIL + grp])):
            for h in range(ATT_HEADS):
                src = slice(grp * gw + h * HEAD_DIM, grp * gw + (h + 1) * HEAD_DIM)
                y = prep(ref[:, src], g_ref[...], scale)
                if dil == 1:
                    out[:, h * HEAD_DIM:(h + 1) * HEAD_DIM] = y.astype(BF16)
                else:
                    scr_ref[h] = y
            if dil > 1:
                emit(out, dil)
        if dil == 1:
            out_refs[2 * N_DIL + grp][...] = v_ref[:, grp * gw:(grp + 1) * gw]
        else:
            for h in range(ATT_HEADS):
                src = slice(grp * gw + h * HEAD_DIM, grp * gw + (h + 1) * HEAD_DIM)
                scr_ref[h] = v_ref[:, src].astype(F32)
            emit(out_refs[2 * N_DIL + grp], dil)


def _qkprep(proj2, l, cos2, sin2, g_q, g_k):
    T = proj2.shape[0]
    rows = PREP_ROWS
    width = N_DIL * ATT_HEADS * HEAD_DIM
    gw = ATT_HEADS * HEAD_DIM
    dils = [dil for _, dil in DIL_GROUPS] * 3
    call, lidx = _grid_call(
        _qkprep_kernel, l, (T // rows,),
        [pl.BlockSpec((rows, width), lambda i, l: (i, 0)),
         pl.BlockSpec((rows, width), lambda i, l: (i, 1)),
         pl.BlockSpec((rows, width), lambda i, l: (i, 2)),
         pl.BlockSpec((rows, HEAD_DIM), lambda i, l: (i, 0)),
         pl.BlockSpec((rows, HEAD_DIM), lambda i, l: (i, 0)),
         pl.BlockSpec((None, 1, HEAD_DIM), lambda i, l: (l[0], 0, 0)),
         pl.BlockSpec((None, 1, HEAD_DIM), lambda i, l: (l[0], 0, 0))],
        [pl.BlockSpec((rows // d, d * gw), lambda i, l: (i, 0)) for d in dils],
        [jax.ShapeDtypeStruct((T // d, d * gw), BF16) for d in dils],
        scratch=[pltpu.VMEM((ATT_HEADS, rows, HEAD_DIM), F32)],
        sem=("parallel",), name="qk_prep")
    return call(lidx, proj2, proj2, proj2, cos2, sin2, g_q, g_k)


def _attn_kernel(q_ref, kc_ref, kp_ref, vc_ref, vp_ref, o_ref, lse_ref, *, dil):
    first = pl.program_id(1) == 0
    QB = q_ref.shape[0]
    A = ATT_BLOCK
    qi = lax.broadcasted_iota(jnp.int32, (A, 2 * A), 0)
    kj = lax.broadcasted_iota(jnp.int32, (A, 2 * A), 1) - A
    dist = qi - kj
    valid = (dist >= 0) & (dist <= A)
    bias = jnp.where(valid, 0.0, NEG_BIG).astype(F32)
    bias_first = jnp.where(first, jnp.where(valid & (kj >= 0), 0.0, NEG_BIG), bias).astype(F32)

    for p in range(dil):
        for h in range(ATT_HEADS):
            cs = slice((p * ATT_HEADS + h) * HEAD_DIM, (p * ATT_HEADS + h + 1) * HEAD_DIM)
            for j in range(QB // A):
                rs = slice(j * A, (j + 1) * A)
                if j == 0:
                    kcat = jnp.concatenate([kp_ref[:, cs], kc_ref[0:A, cs]], axis=0)
                    vcat = jnp.concatenate([vp_ref[:, cs], vc_ref[0:A, cs]], axis=0)
                    bb = bias_first
                else:
                    kcat = kc_ref[(j - 1) * A:(j + 1) * A, cs]
                    vcat = vc_ref[(j - 1) * A:(j + 1) * A, cs]
                    bb = bias
                s = _dot_nt(q_ref[rs, cs], kcat) + bb
                m = jnp.max(s, axis=-1, keepdims=True)
                e = jnp.exp(s - m)
                den = jnp.sum(e, axis=-1, keepdims=True)
                o = _dot(e.astype(BF16), vcat) / den
                o_ref[rs, cs] = o.astype(o_ref.dtype)
                lse_ref[rs, cs] = jnp.broadcast_to(m + jnp.log(den), (A, HEAD_DIM))


def _attn_group(q, k, v, dil, batch, seq):
    rows_all, W = q.shape
    A = ATT_BLOCK
    QB = ATT_SPAN // dil
    nsp = seq // ATT_SPAN
    per = QB // A
    cur = pl.BlockSpec((QB, W), lambda b, n: (b * nsp + n, 0))
    prev = pl.BlockSpec((A, W), lambda b, n: (jnp.maximum((b * nsp + n) * per - 1, b * nsp * per), 0))
    return pl.pallas_call(
        functools.partial(_attn_kernel, dil=dil),
        grid=(batch, nsp),
        in_specs=[cur, cur, prev, cur, prev],
        out_specs=[cur, cur],
        out_shape=[jax.ShapeDtypeStruct((rows_all, W), BF16),
                   jax.ShapeDtypeStruct((rows_all, W), F32)],
        compiler_params=_params("parallel", "arbitrary"),
        name=f"dilated_attn_r{dil}",
    )(q, k, k, v, v)


def _attn_mix_kernel(o0, o1, o2, l0, l1, l2, out_ref, so_ref, sl_ref):
    rows, gw = out_ref.shape
    o_refs, l_refs = (o0, o1, o2), (l0, l1, l2)
    for h in range(ATT_HEADS):
        hs = slice(h * HEAD_DIM, (h + 1) * HEAD_DIM)
        outs, lses = [], []
        for g, (_, dil) in enumerate(DIL_GROUPS):
            if dil == 1:
                outs.append(o_refs[g][:, hs].astype(F32))
                lses.append(l_refs[g][:, hs])
                continue
            n = rows // dil
            for p in range(dil):
                cs = slice(p * gw + h * HEAD_DIM, p * gw + (h + 1) * HEAD_DIM)
                so_ref[g, pl.ds(p, n, stride=dil), :] = o_refs[g][:, cs].astype(F32)
                sl_ref[g, pl.ds(p, n, stride=dil), :] = l_refs[g][:, cs]
            outs.append(so_ref[g])
            lses.append(sl_ref[g])
        m = jnp.maximum(jnp.maximum(lses[0], lses[1]), lses[2])
        ws = [jnp.exp(x - m) for x in lses]
        den = ws[0] + ws[1] + ws[2]
        out_ref[:, hs] = ((ws[0] * outs[0] + ws[1] * outs[1] + ws[2] * outs[2]) / den
                          ).astype(out_ref.dtype)


def _attn_mix(outs, lses):
    gw = ATT_HEADS * HEAD_DIM
    T = outs[0].shape[0]
    rows = PREP_ROWS
    dils = [dil for _, dil in DIL_GROUPS]
    specs = [pl.BlockSpec((rows // d, d * gw), lambda i: (i, 0)) for d in dils]
    return pl.pallas_call(
        _attn_mix_kernel,
        grid=(T // rows,),
        in_specs=specs + specs,
        out_specs=pl.BlockSpec((rows, gw), lambda i: (i, 0)),
        out_shape=jax.ShapeDtypeStruct((T, gw), BF16),
        scratch_shapes=[pltpu.VMEM((N_DIL, rows, HEAD_DIM), F32),
                        pltpu.VMEM((N_DIL, rows, HEAD_DIM), F32)],
        compiler_params=_params("parallel"),
        name="attn_mix",
    )(*outs, *lses)


def _merge_kernel(l_ref, ogla_ref, oatt_ref, gg_ref, ga_ref, wg_ref, wa_ref,
                  out_ref, wgbf_ref, wabf_ref):
    @pl.when(pl.program_id(1) == 0)
    def _():
        wgbf_ref[...] = wg_ref[...].astype(BF16)
        wabf_ref[...] = wa_ref[...].astype(BF16)

    a = _dot(ogla_ref[...], wgbf_ref[...])
    b = _dot(oatt_ref[...], wabf_ref[...])
    gg = _sigmoid(gg_ref[...].astype(F32))
    ga = _sigmoid(ga_ref[...].astype(F32))
    out_ref[...] = (gg * a + ga * b).astype(out_ref.dtype)


def _merge(o_gla, o_att, gates, l, w_gla_proj, w_attn_proj):
    T, dv = o_gla.shape
    gw = o_att.shape[1]
    D = w_gla_proj.shape[-1]
    tm, tn = MERGE_ROWS, 2 * MM_TN
    gcb = 0
    dcb = D // tn
    call, lidx = _grid_call(
        _merge_kernel, l, (D // tn, T // tm),
        [pl.BlockSpec((tm, dv), lambda j, i, l: (i, 0)),
         pl.BlockSpec((tm, gw), lambda j, i, l: (i, 0)),
         pl.BlockSpec((tm, tn), lambda j, i, l: (i, gcb + j)),
         pl.BlockSpec((tm, tn), lambda j, i, l: (i, gcb + dcb + j)),
         pl.BlockSpec((None, dv, tn), lambda j, i, l: (l[0], 0, j)),
         pl.BlockSpec((None, gw, tn), lambda j, i, l: (l[0], 0, j))],
        pl.BlockSpec((tm, tn), lambda j, i, l: (i, j)),
        jax.ShapeDtypeStruct((T, D), BF16),
        scratch=[pltpu.VMEM((dv, tn), BF16), pltpu.VMEM((gw, tn), BF16)], name="gated_merge")
    return call(lidx, o_gla, o_att, gates, gates, w_gla_proj, w_attn_proj)


def _router_kernel(l_ref, x_ref, g_ref, sh_ref, sc_ref, wr_ref, br_ref, h_ref, route_ref):
    h = _ada_norm(x_ref[...], g_ref[...], sc_ref[...], sh_ref[...])
    rows = h.shape[0]
    hb = h.astype(BF16)
    words = _pack_rows(hb.astype(F32))
    nw = len(words)
    for j in range(nw):
        h_ref[pl.ds(j, rows, stride=nw), :] = words[j]
    lg = _dot(hb, wr_ref[...].astype(BF16)) + br_ref[...]
    lane = lax.broadcasted_iota(jnp.int32, lg.shape, 1).astype(F32)
    big = float(4 * LANES)

    def first_argmax(vals):
        mx = jnp.max(vals, axis=-1, keepdims=True)
        idx = jnp.min(jnp.where(vals == mx, lane, big), axis=-1, keepdims=True)
        return mx, idx

    gl = jnp.where(lane < N_GROUPS, lg, NEG_BIG)
    gmax, gidx = first_argmax(gl)
    g_weight = 1.0 / jnp.sum(jnp.exp(gl - gmax), axis=-1, keepdims=True)
    lo = N_GROUPS + gidx * EXPERTS_PER_GROUP
    el = jnp.where((lane >= lo) & (lane < lo + EXPERTS_PER_GROUP), lg, NEG_BIG)
    m1, i1 = first_argmax(el)
    m2, i2 = first_argmax(jnp.where(lane == i1, NEG_BIG, el))
    e2 = jnp.exp(m2 - m1)
    w1 = g_weight / (1.0 + e2)
    w2 = g_weight * e2 / (1.0 + e2)
    route = jnp.where(lane == 0, i1 - N_GROUPS,
                      jnp.where(lane == 1, i2 - N_GROUPS,
                                jnp.where(lane == 2, w1, jnp.where(lane == 3, w2, 0.0))))
    route_ref[...] = route


def _router(x, l, g_all, mod3, wr_pad, br_pad, seq):
    T, D = x.shape
    rows = NORM_ROWS
    call, lidx = _grid_call(
        _router_kernel, l, (T // rows,),
        [pl.BlockSpec((rows, D), lambda i, l: (i, 0)),
         pl.BlockSpec((None, 1, D), lambda i, l: (l[0], 0, 0)),
         pl.BlockSpec((None, 1, D), _mod_spec(3, rows, seq)),
         pl.BlockSpec((None, 1, D), _mod_spec(4, rows, seq)),
         pl.BlockSpec((None, D, LANES), lambda i, l: (l[0], 0, 0)),
         pl.BlockSpec((None, 1, LANES), lambda i, l: (l[0], 0, 0))],
        [pl.BlockSpec((rows * (D // (2 * LANES)), LANES), lambda i, l: (i, 0)),
         pl.BlockSpec((rows, LANES), lambda i, l: (i, 0))],
        [jax.ShapeDtypeStruct((T * (D // (2 * LANES)), LANES), jnp.uint32),
         jax.ShapeDtypeStruct((T, LANES), F32)],
        sem=("parallel",), name="norm_router")
    return call(lidx, x, g_all, mod3, mod3, wr_pad, br_pad)


def _expert_kernel(meta_ref, src_one_ref, src_cur_ref, src_nxt_ref, dst_ref, h_ref,
                   wg_ref, wu_ref, wd_ref, ys_ref, xbuf_ref, obuf_ref,
                   wgbf_ref, wubf_ref, wdbf_ref, gsem_ref, ssem_ref):
    i = pl.program_id(0)
    n_used = meta_ref[1]
    slot = i % 2
    nj, tm = xbuf_ref.shape[1], xbuf_ref.shape[2]

    def gather_copy(idx_ref, r, s):
        return pltpu.make_async_copy(h_ref.at[idx_ref[0, 0, r]], xbuf_ref.at[s, :, r, :],
                                     gsem_ref.at[s])

    def scatter_copy(idx_ref, r, s):
        return pltpu.make_async_copy(obuf_ref.at[s, :, r, :], ys_ref.at[idx_ref[0, 0, r]],
                                     ssem_ref.at[s])

    def gather_start(idx_ref, s):
        def body(r, c):
            gather_copy(idx_ref, r, s).start()
            return c
        lax.fori_loop(0, tm, body, 0, unroll=8)

    def gather_wait(s):
        pltpu.make_async_copy(obuf_ref.at[0], xbuf_ref.at[s], gsem_ref.at[s]).wait()

    def scatter_wait(s):
        pltpu.make_async_copy(xbuf_ref.at[0], obuf_ref.at[s], ssem_ref.at[s]).wait()

    @pl.when(i == 0)
    def _():
        gather_start(src_cur_ref, 0)

        @pl.when(n_used > 1)
        def _():
            gather_start(src_one_ref, 1)
        obuf_ref[1] = jnp.zeros(obuf_ref.shape[1:], jnp.uint32)
        n_rows = ys_ref.shape[0]

        def fill(r, c):
            pltpu.make_async_copy(obuf_ref.at[1, :, r % tm, :], ys_ref.at[n_rows - 2 * tm + r],
                                  ssem_ref.at[1]).start()
            return c
        lax.fori_loop(0, 2 * tm, fill, 0, unroll=8)
        scatter_wait(1)
        scatter_wait(1)

    @pl.when(i < n_used)
    def _():
        xslot = i % 3
        gather_wait(xslot)

        @pl.when(i >= 2)
        def _():
            scatter_wait(slot)

        @pl.when((i == 0) | (meta_ref[2 + i] != meta_ref[2 + jnp.maximum(i - 1, 0)]))
        def _():
            wgbf_ref[...] = wg_ref[...].astype(BF16)
            wubf_ref[...] = wu_ref[...].astype(BF16)
            wdbf_ref[...] = wd_ref[...].astype(BF16)

        x = _unpack_rows([xbuf_ref[xslot, j] for j in range(nj)]).astype(BF16)
        a = _dot(x, wgbf_ref[...])
        u = _dot(x, wubf_ref[...])
        hid = (a * _sigmoid(a)) * u
        y = _dot(hid.astype(BF16), wdbf_ref[...])
        for j, word in enumerate(_pack_rows(y.astype(BF16).astype(F32))):
            obuf_ref[slot, j] = word
        for r in range(tm):
            scatter_copy(dst_ref, r, slot).start()

        @pl.when(i + 2 < n_used)
        def _():
            nslot = (i + 2) % 3
            for r in range(tm):
                gather_copy(src_nxt_ref, r, nslot).start()

        @pl.when(i == n_used - 1)
        def _():
            @pl.when(i >= 1)
            def _():
                scatter_wait(1 - slot)
            scatter_wait(slot)


def _experts(h3, src, dst, meta, w_gate, w_up, w_down, n_out):
    T, nj, _ = h3.shape
    D = 2 * nj * LANES
    F = w_gate.shape[-1]
    n_tiles, _, tm = src.shape
    wspec = lambda shape: pl.BlockSpec((None, None) + shape, lambda i, m: (m[0], m[2 + i], 0, 0))
    idx_spec = lambda f: pl.BlockSpec((1, 1, tm), lambda i, m: (f(i), 0, 0), memory_space=pltpu.SMEM)
    return pl.pallas_call(
        _expert_kernel,
        grid_spec=pltpu.PrefetchScalarGridSpec(
            num_scalar_prefetch=1, grid=(n_tiles,),
            in_specs=[idx_spec(lambda i: min(1, n_tiles - 1)), idx_spec(lambda i: i),
                      idx_spec(lambda i: jnp.minimum(i + 2, n_tiles - 1)), idx_spec(lambda i: i),
                      pl.BlockSpec(memory_space=pl.ANY),
                      wspec((D, F)), wspec((D, F)), wspec((F, D))],
            out_specs=pl.BlockSpec(memory_space=pl.ANY),
            scratch_shapes=[pltpu.VMEM((3, nj, tm, LANES), jnp.uint32),
                            pltpu.VMEM((2, nj, tm, LANES), jnp.uint32),
                            pltpu.VMEM((D, F), BF16), pltpu.VMEM((D, F), BF16),
                            pltpu.VMEM((F, D), BF16),
                            pltpu.SemaphoreType.DMA((3,)), pltpu.SemaphoreType.DMA((2,))]),
        out_shape=jax.ShapeDtypeStruct((n_out, nj, LANES), jnp.uint32),
        compiler_params=_params("arbitrary"),
        name="grouped_experts",
    )(meta, src, src, src, dst, h3, w_gate, w_up, w_down)


def _combine_kernel(x_ref, gt_ref, route_ref, y0_ref, y1_ref, o_ref):
    rows, d = x_ref.shape
    nw = d // (2 * LANES)
    w1 = jnp.broadcast_to(route_ref[:, 2:3], (rows, LANES))
    w2 = jnp.broadcast_to(route_ref[:, 3:4], (rows, LANES))
    for j in range(nw):
        u0 = y0_ref[pl.ds(j, rows, stride=nw), :]
        u1 = y1_ref[pl.ds(j, rows, stride=nw), :]
        for blk, half in ((j, lambda u: u << 16), (nw + j, lambda u: u & jnp.uint32(0xFFFF0000))):
            js = slice(blk * LANES, (blk + 1) * LANES)
            y = (w1 * lax.bitcast_convert_type(half(u0), F32)
                 + w2 * lax.bitcast_convert_type(half(u1), F32))
            o_ref[:, js] = x_ref[:, js] + gt_ref[:, js] * y


def _combine(x, ys, route, mod3, seq):
    T, D = x.shape
    rows = COMBINE_ROWS
    nblk = T // rows
    nj = D // (2 * LANES)
    return pl.pallas_call(
        _combine_kernel,
        grid=(nblk,),
        in_specs=[pl.BlockSpec((rows, D), lambda i: (i, 0)),
                  pl.BlockSpec((None, 1, D), lambda i: (i * rows // seq, 0, 5)),
                  pl.BlockSpec((rows, LANES), lambda i: (i, 0)),
                  pl.BlockSpec((rows * nj, LANES), lambda i: (i, 0)),
                  pl.BlockSpec((rows * nj, LANES), lambda i: (nblk + i, 0))],
        out_specs=pl.BlockSpec((rows, D), lambda i: (i, 0)),
        out_shape=jax.ShapeDtypeStruct((T, D), F32),
        compiler_params=_params("parallel"),
        name="moe_combine",
    )(x, mod3, route, ys, ys)


def _dispatch_plan(route, l):
    T = route.shape[0]
    tm = EXPERT_TM
    n_tiles = (2 * T) // tm + N_EXPERTS
    P = n_tiles * tm
    e = route[:, 0:2].astype(jnp.int32).T.reshape(-1)
    onehot = (e[:, None] == jnp.arange(N_EXPERTS, dtype=jnp.int32)[None, :]).astype(jnp.int32)
    csum = jnp.cumsum(onehot, axis=0)
    rank = jnp.sum((csum - onehot) * onehot, axis=1)
    counts = csum[-1]
    tiles_per = (counts + tm - 1) // tm
    tile_end = jnp.cumsum(tiles_per)
    off = (tile_end - tiles_per) * tm
    pos = off[e] + rank
    asg = jnp.full((P,), -1, jnp.int32).at[pos].set(jnp.arange(2 * T, dtype=jnp.int32))
    slot = jnp.arange(P, dtype=jnp.int32)
    spare = 2 * T + ((slot // tm) % 2) * tm + slot % tm
    src = jnp.where(asg >= 0, asg % T, 0).reshape(n_tiles, 1, tm)
    dst = jnp.where(asg >= 0, asg, spare).reshape(n_tiles, 1, tm)
    tile_ids = jnp.arange(n_tiles, dtype=jnp.int32)
    tile_expert = jnp.minimum(jnp.sum(tile_ids[:, None] >= tile_end[None, :], axis=1),
                              N_EXPERTS - 1).astype(jnp.int32)
    meta = jnp.concatenate([jnp.stack([l, tile_end[-1].astype(jnp.int32)]), tile_expert])
    return src, dst, meta


def kernel(x, c, positions, w_ada, b_ada, g_norm_mix, g_norm_ffn, w_in, w_fg, b_fg, g_gla_out,
           g_q, g_k, w_gla_proj, w_attn_proj, w_out, w_route_group, b_route_group,
           w_route_expert, b_route_expert, w_exp_gate, w_exp_up, w_exp_down):
    B, S, D = x.shape
    L = w_ada.shape[0]
    T = B * S
    dk_all = w_fg.shape[-1]
    dv_all = g_gla_out.shape[-1]
    att_w = N_DIL * ATT_HEADS * HEAD_DIM
    gla_cols = 2 * dk_all + 2 * dv_all

    mod = _modulation(c, w_ada, b_ada)
    cos2, sin2 = _rope_tables(positions)

    wfg_pad = jnp.zeros((L, LANES, dk_all), F32).at[:, :GLA_RANK].set(w_fg)
    wr_pad = jnp.zeros((L, D, LANES), F32)
    wr_pad = wr_pad.at[:, :, :N_GROUPS].set(w_route_group)
    wr_pad = wr_pad.at[:, :, N_GROUPS:N_GROUPS + N_EXPERTS].set(w_route_expert)
    br_pad = jnp.zeros((L, 1, LANES), F32)
    br_pad = br_pad.at[:, 0, :N_GROUPS].set(b_route_group)
    br_pad = br_pad.at[:, 0, N_GROUPS:N_GROUPS + N_EXPERTS].set(b_route_expert)
    w_gla_bf = w_gla_proj.astype(BF16)
    w_attn_bf = w_attn_proj.astype(BF16)
    w_out_bf = w_out.astype(BF16)
    g_mix = g_norm_mix.reshape(L, 1, D)
    g_ffn = g_norm_ffn.reshape(L, 1, D)
    b_fg3 = b_fg.reshape(L, 1, dk_all)
    g_gla3 = g_gla_out.reshape(L, 1, dv_all)
    g_q3 = g_q.reshape(L, 1, HEAD_DIM)
    g_k3 = g_k.reshape(L, 1, HEAD_DIM)

    def residual(acc, xres, gt):
        return xres + gt * acc

    def layer(l, xt):
        l = jnp.asarray(l, jnp.int32)
        mod3 = lax.dynamic_index_in_dim(mod, l, 0, keepdims=False).reshape(8, 1, N_MOD * D)
        h = _norm(xt, l, g_mix, mod3, 0, S)
        proj1 = _matmul(h, w_in, l, col0=0, n=gla_cols, out_dtype=BF16, tn=2 * MM_TN,
                        tm=2 * MM_TM, name="proj_gla")
        fg = _matmul(h, w_in, l, col0=gla_cols, n=LANES, tn=LANES, out_dtype=F32, name="proj_fg")
        proj2 = _matmul(h, w_in, l, col0=gla_cols, shift=GLA_RANK, n=3 * att_w,
                        out_dtype=BF16, tm=2 * MM_TM, name="proj_att")
        gates = _matmul(h, w_in, l, col0=gla_cols + 3 * att_w, shift=GLA_RANK, n=2 * D,
                        out_dtype=BF16, tm=2 * MM_TM, name="proj_gates")
        o_gla = _gla(proj1, fg, l, wfg_pad, b_fg3, g_gla3, B, S)
        prep = _qkprep(proj2, l, cos2, sin2, g_q3, g_k3)
        outs, lses = [], []
        for grp, (_, dil) in enumerate(DIL_GROUPS):
            o, lse = _attn_group(prep[grp], prep[N_DIL + grp], prep[2 * N_DIL + grp], dil, B, S)
            outs.append(o)
            lses.append(lse)
        o_att = _attn_mix(outs, lses)
        merged = _merge(o_gla, o_att, gates, l, w_gla_bf, w_attn_bf)
        tn_out = 2 * MM_TN
        gt_spec = pl.BlockSpec((None, 1, tn_out),
                               lambda j, i, l: (i * MM_TM // S, 0, 2 * (D // tn_out) + j))
        xt = _matmul(merged, w_out_bf, l, col0=0, n=D, out_dtype=F32, tn=tn_out, epilogue=residual,
                     extras=(xt, mod3),
                     extra_specs=(pl.BlockSpec((MM_TM, tn_out), lambda j, i, l: (i, j)), gt_spec),
                     name="out_proj")
        h2, route = _router(xt, l, g_ffn, mod3, wr_pad, br_pad, S)
        src, dst, meta = _dispatch_plan(route, l)
        nj = D // (2 * LANES)
        n_out = 2 * T + 2 * EXPERT_TM
        ys = _experts(h2.reshape(T, nj, LANES), src, dst, meta, w_exp_gate, w_exp_up, w_exp_down,
                      n_out)
        return _combine(xt, ys.reshape(n_out * nj, LANES), route, mod3, S)

    xt = lax.fori_loop(0, L, layer, x.reshape(T, D))
    return xt.reshape(B, S, D)
```

```python
import functools

import jax
import jax.numpy as jnp
from jax import lax
from jax.experimental import pallas as pl
from jax.experimental.pallas import tpu as pltpu

F32 = jnp.float32
BF16 = jnp.bfloat16

GLA_HEADS = 4
GLA_RANK = 16
GLA_TAU = 16.0
GLA_CHUNK = 64
DIL_GROUPS = ((128, 1), (512, 4), (2048, 16))
N_DIL = 3
ATT_HEADS = 4
HEAD_DIM = 128
ATT_BLOCK = 128
ROPE_THETA = 10000.0
N_GROUPS = 4
EXPERTS_PER_GROUP = 8
N_EXPERTS = N_GROUPS * EXPERTS_PER_GROUP
N_MOD = 6
EPS = 1e-6

LANES = 128
VMEM_LIMIT = 56 * 1024 * 1024
NEG_BIG = -1e30

NORM_ROWS = 512
MM_TM = 1024
MM_TN = 512
GLA_ROWS = 256
PREP_ROWS = 512
ATT_SPAN = 2048
MERGE_ROWS = 1024
EXPERT_TM = 256
COMBINE_ROWS = 512


def _params(*sem):
    return pltpu.CompilerParams(dimension_semantics=sem, vmem_limit_bytes=VMEM_LIMIT)


def _dot(a, b):
    return jnp.dot(a, b, preferred_element_type=F32)


def _dot_nt(a, b):
    return lax.dot_general(a, b, (((1,), (1,)), ((), ())), preferred_element_type=F32)


def _dot_tn(a, b):
    return lax.dot_general(a, b, (((0,), (0,)), ((), ())), preferred_element_type=F32)


def _split_bf16(x):
    hi = x.astype(BF16)
    lo = (x - hi.astype(F32)).astype(BF16)
    return hi, lo


def _sigmoid(x):
    return 1.0 / (1.0 + jnp.exp(-x))


def _pack_rows(xb):
    bits = lax.bitcast_convert_type(xb, jnp.uint32)
    nw = xb.shape[1] // (2 * LANES)
    out = []
    for j in range(nw):
        lo = bits[:, j * LANES:(j + 1) * LANES] >> 16
        hi = bits[:, (nw + j) * LANES:(nw + j + 1) * LANES] & jnp.uint32(0xFFFF0000)
        out.append(lo | hi)
    return out


def _unpack_rows(words):
    lo = [lax.bitcast_convert_type(w << 16, F32) for w in words]
    hi = [lax.bitcast_convert_type(w & jnp.uint32(0xFFFF0000), F32) for w in words]
    return jnp.concatenate(lo + hi, axis=1)


def _grid_call(kernel, l, grid, in_specs, out_specs, out_shape, scratch=(), sem=None, name=None):
    sem = sem or ("arbitrary",) * len(grid)
    return pl.pallas_call(
        kernel,
        grid_spec=pltpu.PrefetchScalarGridSpec(
            num_scalar_prefetch=1, grid=grid, in_specs=in_specs, out_specs=out_specs,
            scratch_shapes=list(scratch)),
        out_shape=out_shape,
        compiler_params=_params(*sem),
        name=name,
    ), jnp.reshape(l, (1,)).astype(jnp.int32)


def _mod_kernel(c_ref, w_ref, b_ref, o_ref, act_ref):
    n_batch = c_ref.shape[0]
    tn = w_ref.shape[1]

    @pl.when((pl.program_id(0) == 0) & (pl.program_id(1) == 0))
    def _():
        c = c_ref[...]
        act_ref[...] = c * _sigmoid(c)

    rows = [[] for _ in range(n_batch)]
    for jb in range(tn // LANES):
        w = w_ref[:, jb * LANES:(jb + 1) * LANES]
        for b in range(n_batch):
            rows[b].append(jnp.sum(w * act_ref[b], axis=0, keepdims=True))
    out = jnp.concatenate([jnp.concatenate(r, axis=1) for r in rows], axis=0) + b_ref[...]
    o_ref[0:n_batch, :] = out
    o_ref[n_batch:, :] = jnp.zeros((o_ref.shape[0] - n_batch, tn), F32)


def _modulation(c, w_ada, b_ada):
    L, D, N = w_ada.shape
    B = c.shape[0]
    rows = 8
    assert B <= rows
    c_rep = jnp.broadcast_to(c[:, :, None], (B, D, LANES))
    tn = 512
    return pl.pallas_call(
        _mod_kernel,
        grid=(L, N // tn),
        in_specs=[pl.BlockSpec((B, D, LANES), lambda l, j: (0, 0, 0)),
                  pl.BlockSpec((None, D, tn), lambda l, j: (l, 0, j)),
                  pl.BlockSpec((None, 1, tn), lambda l, j: (l, 0, j))],
        out_specs=pl.BlockSpec((None, rows, tn), lambda l, j: (l, 0, j)),
        out_shape=jax.ShapeDtypeStruct((L, rows, N), F32),
        scratch_shapes=[pltpu.VMEM((B, D, LANES), F32)],
        compiler_params=_params("arbitrary", "arbitrary"),
        name="adaln_mod",
    )(c_rep, w_ada, b_ada.reshape(L, 1, N))


def _rope_kernel(pos_ref, freq_ref, cos_ref, sin_ref):
    ang = pos_ref[...].astype(F32) * freq_ref[...]
    lane = lax.broadcasted_iota(jnp.int32, ang.shape, 1)
    cos_ref[...] = jnp.cos(ang)
    sin_ref[...] = jnp.where(lane < HEAD_DIM // 2, -jnp.sin(ang), jnp.sin(ang))


def _rope_tables(positions):
    T = positions.size
    inv_freq = ROPE_THETA ** (-jnp.arange(0, HEAD_DIM, 2, dtype=F32) / HEAD_DIM)
    freq2 = jnp.concatenate([inv_freq, inv_freq]).reshape(1, HEAD_DIM)
    rows = 2048
    return pl.pallas_call(
        _rope_kernel,
        grid=(T // rows,),
        in_specs=[pl.BlockSpec((rows, 1), lambda i: (i, 0)),
                  pl.BlockSpec((1, HEAD_DIM), lambda i: (0, 0))],
        out_specs=[pl.BlockSpec((rows, HEAD_DIM), lambda i: (i, 0))] * 2,
        out_shape=[jax.ShapeDtypeStruct((T, HEAD_DIM), F32)] * 2,
        compiler_params=_params("parallel"),
        name="rope_tables",
    )(positions.reshape(T, 1), freq2)


def _ada_norm(x, g, scale, shift):
    y = x * lax.rsqrt(jnp.mean(x * x, axis=-1, keepdims=True) + EPS)
    return y * g * (1.0 + scale) + shift


def _norm_kernel(l_ref, x_ref, g_ref, sh_ref, sc_ref, o_ref):
    o_ref[...] = _ada_norm(x_ref[...], g_ref[...], sc_ref[...], sh_ref[...]).astype(o_ref.dtype)


def _mod_spec(which, rows, seq):
    return lambda i, l: (i * rows // seq, 0, which)


def _norm(x, l, g_all, mod3, which_shift, seq):
    T, D = x.shape
    rows = NORM_ROWS
    call, lidx = _grid_call(
        _norm_kernel, l, (T // rows,),
        [pl.BlockSpec((rows, D), lambda i, l: (i, 0)),
         pl.BlockSpec((None, 1, D), lambda i, l: (l[0], 0, 0)),
         pl.BlockSpec((None, 1, D), _mod_spec(which_shift, rows, seq)),
         pl.BlockSpec((None, 1, D), _mod_spec(which_shift + 1, rows, seq))],
        pl.BlockSpec((rows, D), lambda i, l: (i, 0)),
        jax.ShapeDtypeStruct((T, D), BF16), sem=("parallel",), name="ada_norm")
    return call(lidx, x, g_all, mod3, mod3)


def _mm_kernel(l_ref, a_ref, w_ref, *rest, epilogue, n_extra, shift):
    n_w = 1 if shift else 0
    extra = rest[n_w:n_w + n_extra]
    o_ref = rest[n_w + n_extra]
    wbf_ref = rest[n_w + n_extra + 1]
    tn = o_ref.shape[1]

    @pl.when(pl.program_id(1) == 0)
    def _():
        if shift:
            wide = jnp.concatenate([w_ref[...], rest[0][...]], axis=1)
            wbf_ref[...] = wide[:, shift:shift + tn].astype(BF16)
        else:
            wbf_ref[...] = w_ref[...].astype(BF16)

    acc = _dot(a_ref[...], wbf_ref[...])
    if epilogue is not None:
        acc = epilogue(acc, *[e[...] for e in extra])
    o_ref[...] = acc.astype(o_ref.dtype)


def _matmul(a, w, l, *, col0, n, out_dtype, tn=MM_TN, tm=MM_TM, shift=0, epilogue=None,
            extras=(), extra_specs=(), name="matmul"):
    M, K = a.shape
    assert col0 % tn == 0 and n % tn == 0 and M % tm == 0 and 0 <= shift < LANES
    cb0 = col0 // tn
    kern = functools.partial(_mm_kernel, epilogue=epilogue, n_extra=len(extras), shift=shift)
    w_specs = [pl.BlockSpec((None, K, tn), lambda j, i, l: (l[0], 0, cb0 + j))]
    if shift:
        per = tn // LANES
        w_specs.append(pl.BlockSpec((None, K, LANES), lambda j, i, l: (l[0], 0, (cb0 + j + 1) * per)))
    call, lidx = _grid_call(
        kern, l, (n // tn, M // tm),
        [pl.BlockSpec((tm, K), lambda j, i, l: (i, 0))] + w_specs + list(extra_specs),
        pl.BlockSpec((tm, tn), lambda j, i, l: (i, j)),
        jax.ShapeDtypeStruct((M, n), out_dtype),
        scratch=[pltpu.VMEM((K, tn), BF16)], name=name)
    return call(lidx, a, *([w] * len(w_specs)), *extras)


def _gla_kernel(l_ref, q_ref, k_ref, v_ref, r_ref, fg_ref, wfg_ref, bfg_ref, g_ref, o_ref,
                state_ref, *, head_k, head_v):
    rows = q_ref.shape[0]
    C = GLA_CHUNK

    @pl.when(pl.program_id(1) == 0)
    def _():
        state_ref[...] = jnp.zeros_like(state_ref)

    fh, fl = _split_bf16(fg_ref[...])
    wh, wl = _split_bf16(wfg_ref[...])
    z = _dot(fh, wh) + _dot(fl, wh) + _dot(fh, wl) + bfg_ref[...]
    log_a = (jnp.minimum(z, 0.0) - jnp.log(1.0 + jnp.exp(-jnp.abs(z)))) * (1.0 / GLA_TAU)

    ri = lax.broadcasted_iota(jnp.int32, (C, C), 0)
    ci = lax.broadcasted_iota(jnp.int32, (C, C), 1)
    causal = ri >= ci
    tri = causal.astype(BF16)
    scale = head_k ** -0.5

    for c in range(rows // C):
        rs = slice(c * C, (c + 1) * C)
        lh, ll = _split_bf16(log_a[rs])
        b = _dot(tri, lh) + _dot(tri, ll)
        b_last = b[C - 1:C]
        q = q_ref[rs, :].astype(F32) * scale
        k = k_ref[rs, :].astype(F32)
        q_dec = (q * jnp.exp(b)).astype(BF16)
        k_inv = (k * jnp.exp(-b)).astype(BF16)
        k_upd = (k * jnp.exp(b_last - b)).astype(BF16)
        decay = jnp.exp(b_last)
        for h in range(GLA_HEADS):
            ks = slice(h * head_k, (h + 1) * head_k)
            vs = slice(h * head_v, (h + 1) * head_v)
            v = v_ref[rs, vs]
            att = jnp.where(causal, _dot_nt(q_dec[:, ks], k_inv[:, ks]), 0.0).astype(BF16)
            st = state_ref[h]
            o = _dot(att, v) + _dot_nt(q_dec[:, ks], st.astype(BF16))
            state_ref[h] = st * decay[:, ks] + _dot_tn(v, k_upd[:, ks])
            o = o * lax.rsqrt(jnp.mean(o * o, axis=-1, keepdims=True) + EPS) * g_ref[:, vs]
            r = r_ref[rs, vs].astype(F32)
            o_ref[rs, vs] = (o * (r * _sigmoid(r))).astype(o_ref.dtype)


def _gla(proj, fg, l, wfg_pad, b_fg, g_gla_out, batch, seq):
    T = proj.shape[0]
    dk_all = wfg_pad.shape[-1]
    dv_all = g_gla_out.shape[-1]
    rows = GLA_ROWS
    nblk = seq // rows
    kern = functools.partial(_gla_kernel, head_k=dk_all // GLA_HEADS, head_v=dv_all // GLA_HEADS)
    row = lambda b, n, l: b * nblk + n
    call, lidx = _grid_call(
        kern, l, (batch, nblk),
        [pl.BlockSpec((rows, dk_all), lambda b, n, l: (row(b, n, l), 0)),
         pl.BlockSpec((rows, dk_all), lambda b, n, l: (row(b, n, l), 1)),
         pl.BlockSpec((rows, dv_all), lambda b, n, l: (row(b, n, l), 1)),
         pl.BlockSpec((rows, dv_all), lambda b, n, l: (row(b, n, l), 2)),
         pl.BlockSpec((rows, LANES), lambda b, n, l: (row(b, n, l), 0)),
         pl.BlockSpec((None, LANES, dk_all), lambda b, n, l: (l[0], 0, 0)),
         pl.BlockSpec((None, 1, dk_all), lambda b, n, l: (l[0], 0, 0)),
         pl.BlockSpec((None, 1, dv_all), lambda b, n, l: (l[0], 0, 0))],
        pl.BlockSpec((rows, dv_all), lambda b, n, l: (row(b, n, l), 0)),
        jax.ShapeDtypeStruct((T, dv_all), BF16),
        scratch=[pltpu.VMEM((GLA_HEADS, dv_all // GLA_HEADS, dk_all // GLA_HEADS), F32)],
        sem=("parallel", "arbitrary"), name="gla")
    return call(lidx, proj, proj, proj, proj, fg, wfg_pad, b_fg, g_gla_out)


def _qkprep_kernel(l_ref, q_ref, k_ref, v_ref, cos_ref, sin_ref, gq_ref, gk_ref, *refs):
    out_refs, scr_ref = refs[:-1], refs[-1]
    cos = cos_ref[...]
    sin = sin_ref[...]
    gw = ATT_HEADS * HEAD_DIM
    rows = q_ref.shape[0]

    def tables(g_ref, scale):
        g = jnp.broadcast_to(g_ref[...], cos.shape)
        return g * cos * scale, pltpu.roll(g, HEAD_DIM // 2, 1) * sin * scale

    def prep(ref, tab):
        x = ref.astype(F32)
        n = x * lax.rsqrt(jnp.mean(x * x, axis=-1, keepdims=True) + EPS)
        return n * tab[0] + pltpu.roll(n, HEAD_DIM // 2, 1) * tab[1]

    def emit(dst_ref, dil):
        n = rows // dil
        for p in range(dil):
            for h in range(ATT_HEADS):
                cs = slice(p * gw + h * HEAD_DIM, p * gw + (h + 1) * HEAD_DIM)
                dst_ref[:, cs] = scr_ref[h, pl.ds(p, n, stride=dil), :].astype(BF16)

    tab_q = tables(gq_ref, HEAD_DIM ** -0.5)
    tab_k = tables(gk_ref, 1.0)
    for grp, (_, dil) in enumerate(DIL_GROUPS):
        for ref, tab, out in ((q_ref, tab_q, out_refs[grp]), (k_ref, tab_k, out_refs[N_DIL + grp])):
            for h in range(ATT_HEADS):
                src = slice(grp * gw + h * HEAD_DIM, grp * gw + (h + 1) * HEAD_DIM)
                y = prep(ref[:, src], tab)
                if dil == 1:
                    out[:, h * HEAD_DIM:(h + 1) * HEAD_DIM] = y.astype(BF16)
                else:
                    scr_ref[h] = y
            if dil > 1:
                emit(out, dil)
        if dil == 1:
            out_refs[2 * N_DIL + grp][...] = v_ref[:, grp * gw:(grp + 1) * gw]
        else:
            for h in range(ATT_HEADS):
                src = slice(grp * gw + h * HEAD_DIM, grp * gw + (h + 1) * HEAD_DIM)
                scr_ref[h] = v_ref[:, src].astype(F32)
            emit(out_refs[2 * N_DIL + grp], dil)


def _qkprep(proj2, l, cos2, sin2, g_q, g_k):
    T = proj2.shape[0]
    rows = PREP_ROWS
    width = N_DIL * ATT_HEADS * HEAD_DIM
    gw = ATT_HEADS * HEAD_DIM
    dils = [dil for _, dil in DIL_GROUPS] * 3
    call, lidx = _grid_call(
        _qkprep_kernel, l, (T // rows,),
        [pl.BlockSpec((rows, width), lambda i, l: (i, 0)),
         pl.BlockSpec((rows, width), lambda i, l: (i, 1)),
         pl.BlockSpec((rows, width), lambda i, l: (i, 2)),
         pl.BlockSpec((rows, HEAD_DIM), lambda i, l: (i, 0)),
         pl.BlockSpec((rows, HEAD_DIM), lambda i, l: (i, 0)),
         pl.BlockSpec((None, 1, HEAD_DIM), lambda i, l: (l[0], 0, 0)),
         pl.BlockSpec((None, 1, HEAD_DIM), lambda i, l: (l[0], 0, 0))],
        [pl.BlockSpec((rows // d, d * gw), lambda i, l: (i, 0)) for d in dils],
        [jax.ShapeDtypeStruct((T // d, d * gw), BF16) for d in dils],
        scratch=[pltpu.VMEM((ATT_HEADS, rows, HEAD_DIM), F32)],
        sem=("parallel",), name="qk_prep")
    return call(lidx, proj2, proj2, proj2, cos2, sin2, g_q, g_k)


def _attn_kernel(q_ref, kc_ref, kp_ref, vc_ref, vp_ref, o_ref, lse_ref, *, dil):
    first = pl.program_id(1) == 0
    QB = q_ref.shape[0]
    A = ATT_BLOCK
    qi = lax.broadcasted_iota(jnp.int32, (A, 2 * A), 0)
    kj = lax.broadcasted_iota(jnp.int32, (A, 2 * A), 1) - A
    dist = qi - kj
    valid = (dist >= 0) & (dist <= A)
    bias = jnp.where(valid, 0.0, NEG_BIG).astype(F32)
    bias_first = jnp.where(first, jnp.where(valid & (kj >= 0), 0.0, NEG_BIG), bias).astype(F32)

    for p in range(dil):
        for h in range(ATT_HEADS):
            cs = slice((p * ATT_HEADS + h) * HEAD_DIM, (p * ATT_HEADS + h + 1) * HEAD_DIM)
            for j in range(QB // A):
                rs = slice(j * A, (j + 1) * A)
                if j == 0:
                    kcat = jnp.concatenate([kp_ref[:, cs], kc_ref[0:A, cs]], axis=0)
                    vcat = jnp.concatenate([vp_ref[:, cs], vc_ref[0:A, cs]], axis=0)
                    bb = bias_first
                else:
                    kcat = kc_ref[(j - 1) * A:(j + 1) * A, cs]
                    vcat = vc_ref[(j - 1) * A:(j + 1) * A, cs]
                    bb = bias
                s = _dot_nt(q_ref[rs, cs], kcat) + bb
                m = jnp.max(s, axis=-1, keepdims=True)
                e = jnp.exp(s - m)
                den = jnp.sum(e, axis=-1, keepdims=True)
                o = _dot(e.astype(BF16), vcat) / den
                o_ref[rs, cs] = o.astype(o_ref.dtype)
                lse_ref[rs, cs] = jnp.broadcast_to(m + jnp.log(den), (A, HEAD_DIM))


def _attn_group(q, k, v, dil, batch, seq):
    rows_all, W = q.shape
    A = ATT_BLOCK
    QB = ATT_SPAN // dil
    nsp = seq // ATT_SPAN
    per = QB // A
    cur = pl.BlockSpec((QB, W), lambda b, n: (b * nsp + n, 0))
    prev = pl.BlockSpec((A, W), lambda b, n: (jnp.maximum((b * nsp + n) * per - 1, b * nsp * per), 0))
    return pl.pallas_call(
        functools.partial(_attn_kernel, dil=dil),
        grid=(batch, nsp),
        in_specs=[cur, cur, prev, cur, prev],
        out_specs=[cur, cur],
        out_shape=[jax.ShapeDtypeStruct((rows_all, W), BF16),
                   jax.ShapeDtypeStruct((rows_all, W), F32)],
        compiler_params=_params("parallel", "arbitrary"),
        name=f"dilated_attn_r{dil}",
    )(q, k, k, v, v)


def _attn_mix_kernel(o0, o1, o2, l0, l1, l2, out_ref, so_ref, sl_ref):
    rows, gw = out_ref.shape
    o_refs, l_refs = (o0, o1, o2), (l0, l1, l2)
    for h in range(ATT_HEADS):
        hs = slice(h * HEAD_DIM, (h + 1) * HEAD_DIM)
        outs, lses = [], []
        for g, (_, dil) in enumerate(DIL_GROUPS):
            if dil == 1:
                outs.append(o_refs[g][:, hs].astype(F32))
                lses.append(l_refs[g][:, hs])
                continue
            n = rows // dil
            for p in range(dil):
                cs = slice(p * gw + h * HEAD_DIM, p * gw + (h + 1) * HEAD_DIM)
                so_ref[g, pl.ds(p, n, stride=dil), :] = o_refs[g][:, cs].astype(F32)
                sl_ref[g, pl.ds(p, n, stride=dil), :] = l_refs[g][:, cs]
            outs.append(so_ref[g])
            lses.append(sl_ref[g])
        m = jnp.maximum(jnp.maximum(lses[0], lses[1]), lses[2])
        ws = [jnp.exp(x - m) for x in lses]
        den = ws[0] + ws[1] + ws[2]
        out_ref[:, hs] = ((ws[0] * outs[0] + ws[1] * outs[1] + ws[2] * outs[2]) / den
                          ).astype(out_ref.dtype)


def _attn_mix(outs, lses):
    gw = ATT_HEADS * HEAD_DIM
    T = outs[0].shape[0]
    rows = PREP_ROWS
    dils = [dil for _, dil in DIL_GROUPS]
    specs = [pl.BlockSpec((rows // d, d * gw), lambda i: (i, 0)) for d in dils]
    return pl.pallas_call(
        _attn_mix_kernel,
        grid=(T // rows,),
        in_specs=specs + specs,
        out_specs=pl.BlockSpec((rows, gw), lambda i: (i, 0)),
        out_shape=jax.ShapeDtypeStruct((T, gw), BF16),
        scratch_shapes=[pltpu.VMEM((N_DIL, rows, HEAD_DIM), F32),
                        pltpu.VMEM((N_DIL, rows, HEAD_DIM), F32)],
        compiler_params=_params("parallel"),
        name="attn_mix",
    )(*outs, *lses)


def _merge_kernel(l_ref, ogla_ref, oatt_ref, gg_ref, ga_ref, wg_ref, wa_ref,
                  out_ref, wgbf_ref, wabf_ref):
    @pl.when(pl.program_id(1) == 0)
    def _():
        wgbf_ref[...] = wg_ref[...].astype(BF16)
        wabf_ref[...] = wa_ref[...].astype(BF16)

    a = _dot(ogla_ref[...], wgbf_ref[...])
    b = _dot(oatt_ref[...], wabf_ref[...])
    gg = _sigmoid(gg_ref[...].astype(F32))
    ga = _sigmoid(ga_ref[...].astype(F32))
    out_ref[...] = (gg * a + ga * b).astype(out_ref.dtype)


def _merge(o_gla, o_att, gates, l, w_gla_proj, w_attn_proj):
    T, dv = o_gla.shape
    gw = o_att.shape[1]
    D = w_gla_proj.shape[-1]
    tm, tn = MERGE_ROWS, 2 * MM_TN
    gcb = 0
    dcb = D // tn
    call, lidx = _grid_call(
        _merge_kernel, l, (D // tn, T // tm),
        [pl.BlockSpec((tm, dv), lambda j, i, l: (i, 0)),
         pl.BlockSpec((tm, gw), lambda j, i, l: (i, 0)),
         pl.BlockSpec((tm, tn), lambda j, i, l: (i, gcb + j)),
         pl.BlockSpec((tm, tn), lambda j, i, l: (i, gcb + dcb + j)),
         pl.BlockSpec((None, dv, tn), lambda j, i, l: (l[0], 0, j)),
         pl.BlockSpec((None, gw, tn), lambda j, i, l: (l[0], 0, j))],
        pl.BlockSpec((tm, tn), lambda j, i, l: (i, j)),
        jax.ShapeDtypeStruct((T, D), BF16),
        scratch=[pltpu.VMEM((dv, tn), BF16), pltpu.VMEM((gw, tn), BF16)], name="gated_merge")
    return call(lidx, o_gla, o_att, gates, gates, w_gla_proj, w_attn_proj)


def _router_kernel(l_ref, x_ref, g_ref, sh_ref, sc_ref, wr_ref, br_ref, h_ref, route_ref):
    h = _ada_norm(x_ref[...], g_ref[...], sc_ref[...], sh_ref[...])
    rows = h.shape[0]
    hb = h.astype(BF16)
    words = _pack_rows(hb.astype(F32))
    nw = len(words)
    for j in range(nw):
        h_ref[pl.ds(j, rows, stride=nw), :] = words[j]
    lg = _dot(hb, wr_ref[...].astype(BF16)) + br_ref[...]
    lane = lax.broadcasted_iota(jnp.int32, lg.shape, 1).astype(F32)
    big = float(4 * LANES)

    def first_argmax(vals):
        mx = jnp.max(vals, axis=-1, keepdims=True)
        idx = jnp.min(jnp.where(vals == mx, lane, big), axis=-1, keepdims=True)
        return mx, idx

    gl = jnp.where(lane < N_GROUPS, lg, NEG_BIG)
    gmax, gidx = first_argmax(gl)
    g_weight = 1.0 / jnp.sum(jnp.exp(gl - gmax), axis=-1, keepdims=True)
    lo = N_GROUPS + gidx * EXPERTS_PER_GROUP
    el = jnp.where((lane >= lo) & (lane < lo + EXPERTS_PER_GROUP), lg, NEG_BIG)
    m1, i1 = first_argmax(el)
    m2, i2 = first_argmax(jnp.where(lane == i1, NEG_BIG, el))
    e2 = jnp.exp(m2 - m1)
    w1 = g_weight / (1.0 + e2)
    w2 = g_weight * e2 / (1.0 + e2)
    route = jnp.where(lane == 0, i1 - N_GROUPS,
                      jnp.where(lane == 1, i2 - N_GROUPS,
                                jnp.where(lane == 2, w1, jnp.where(lane == 3, w2, 0.0))))
    route_ref[...] = route


def _router(x, l, g_all, mod3, wr_pad, br_pad, seq):
    T, D = x.shape
    rows = NORM_ROWS
    call, lidx = _grid_call(
        _router_kernel, l, (T // rows,),
        [pl.BlockSpec((rows, D), lambda i, l: (i, 0)),
         pl.BlockSpec((None, 1, D), lambda i, l: (l[0], 0, 0)),
         pl.BlockSpec((None, 1, D), _mod_spec(3, rows, seq)),
         pl.BlockSpec((None, 1, D), _mod_spec(4, rows, seq)),
         pl.BlockSpec((None, D, LANES), lambda i, l: (l[0], 0, 0)),
         pl.BlockSpec((None, 1, LANES), lambda i, l: (l[0], 0, 0))],
        [pl.BlockSpec((rows * (D // (2 * LANES)), LANES), lambda i, l: (i, 0)),
         pl.BlockSpec((rows, LANES), lambda i, l: (i, 0))],
        [jax.ShapeDtypeStruct((T * (D // (2 * LANES)), LANES), jnp.uint32),
         jax.ShapeDtypeStruct((T, LANES), F32)],
        sem=("parallel",), name="norm_router")
    return call(lidx, x, g_all, mod3, mod3, wr_pad, br_pad)


def _expert_kernel(meta_ref, src_one_ref, src_cur_ref, src_nxt_ref, dst_ref, h_ref,
                   wg_ref, wu_ref, wd_ref, ys_ref, xbuf_ref, obuf_ref, wgf_ref, wuf_ref, wdf_ref,
                   wgbf_ref, wubf_ref, wdbf_ref, gsem_ref, ssem_ref, wsem_ref):
    i = pl.program_id(0)
    n_tiles = pl.num_programs(0)
    n_used = meta_ref[1]
    slot = i % 2
    nj, tm = xbuf_ref.shape[1], xbuf_ref.shape[2]
    layer = meta_ref[0]
    expert = meta_ref[2 + i]
    next_expert = meta_ref[2 + n_tiles + i]
    wset = meta_ref[2 + 2 * n_tiles + i]

    def weight_copies(e, s):
        return [pltpu.make_async_copy(w.at[layer, e], buf.at[s], wsem_ref.at[s])
                for w, buf in ((wg_ref, wgf_ref), (wu_ref, wuf_ref), (wd_ref, wdf_ref))]

    def gather_copy(idx_ref, r, s):
        return pltpu.make_async_copy(h_ref.at[idx_ref[0, 0, r]], xbuf_ref.at[s, :, r, :],
                                     gsem_ref.at[s])

    def scatter_copy(idx_ref, r, s):
        return pltpu.make_async_copy(obuf_ref.at[s, :, r, :], ys_ref.at[idx_ref[0, 0, r]],
                                     ssem_ref.at[s])

    def gather_start(idx_ref, s):
        def body(r, c):
            gather_copy(idx_ref, r, s).start()
            return c
        lax.fori_loop(0, tm, body, 0, unroll=8)

    def gather_wait(s):
        pltpu.make_async_copy(obuf_ref.at[0], xbuf_ref.at[s], gsem_ref.at[s]).wait()

    def scatter_wait(s):
        pltpu.make_async_copy(xbuf_ref.at[0], obuf_ref.at[s], ssem_ref.at[s]).wait()

    @pl.when(i == 0)
    def _():
        for cp in weight_copies(expert, wset):
            cp.start()
        gather_start(src_cur_ref, 0)

        @pl.when(n_used > 1)
        def _():
            gather_start(src_one_ref, 1)
        obuf_ref[1] = jnp.zeros(obuf_ref.shape[1:], jnp.uint32)
        n_rows = ys_ref.shape[0]

        def fill(r, c):
            pltpu.make_async_copy(obuf_ref.at[1, :, r % tm, :], ys_ref.at[n_rows - 2 * tm + r],
                                  ssem_ref.at[1]).start()
            return c
        lax.fori_loop(0, 2 * tm, fill, 0, unroll=8)
        scatter_wait(1)
        scatter_wait(1)

    @pl.when(i < n_used)
    def _():
        xslot = i % 3
        gather_wait(xslot)

        @pl.when(i >= 2)
        def _():
            scatter_wait(slot)

        @pl.when((i == 0) | (expert != meta_ref[2 + jnp.maximum(i - 1, 0)]))
        def _():
            for cp in weight_copies(expert, wset):
                cp.wait()
            wgbf_ref[...] = wgf_ref[wset].astype(BF16)
            wubf_ref[...] = wuf_ref[wset].astype(BF16)
            wdbf_ref[...] = wdf_ref[wset].astype(BF16)

            @pl.when(next_expert >= 0)
            def _():
                for cp in weight_copies(next_expert, 1 - wset):
                    cp.start()

        x = _unpack_rows([xbuf_ref[xslot, j] for j in range(nj)]).astype(BF16)
        a = _dot(x, wgbf_ref[...])
        u = _dot(x, wubf_ref[...])
        hid = (a * _sigmoid(a)) * u
        y = _dot(hid.astype(BF16), wdbf_ref[...])
        for j, word in enumerate(_pack_rows(y.astype(BF16).astype(F32))):
            obuf_ref[slot, j] = word
        for r in range(tm):
            scatter_copy(dst_ref, r, slot).start()

        @pl.when(i + 2 < n_used)
        def _():
            nslot = (i + 2) % 3
            for r in range(tm):
                gather_copy(src_nxt_ref, r, nslot).start()

        @pl.when(i == n_used - 1)
        def _():
            @pl.when(i >= 1)
            def _():
                scatter_wait(1 - slot)
            scatter_wait(slot)


def _experts(h3, src, dst, meta, w_gate, w_up, w_down, n_out):
    T, nj, _ = h3.shape
    D = 2 * nj * LANES
    F = w_gate.shape[-1]
    n_tiles, _, tm = src.shape
    idx_spec = lambda f: pl.BlockSpec((1, 1, tm), lambda i, m: (f(i), 0, 0), memory_space=pltpu.SMEM)
    hbm = pl.BlockSpec(memory_space=pl.ANY)
    return pl.pallas_call(
        _expert_kernel,
        grid_spec=pltpu.PrefetchScalarGridSpec(
            num_scalar_prefetch=1, grid=(n_tiles,),
            in_specs=[idx_spec(lambda i: min(1, n_tiles - 1)), idx_spec(lambda i: i),
                      idx_spec(lambda i: jnp.minimum(i + 2, n_tiles - 1)), idx_spec(lambda i: i),
                      hbm, hbm, hbm, hbm],
            out_specs=hbm,
            scratch_shapes=[pltpu.VMEM((3, nj, tm, LANES), jnp.uint32),
                            pltpu.VMEM((2, nj, tm, LANES), jnp.uint32),
                            pltpu.VMEM((2, D, F), F32), pltpu.VMEM((2, D, F), F32),
                            pltpu.VMEM((2, F, D), F32),
                            pltpu.VMEM((D, F), BF16), pltpu.VMEM((D, F), BF16),
                            pltpu.VMEM((F, D), BF16),
                            pltpu.SemaphoreType.DMA((3,)), pltpu.SemaphoreType.DMA((2,)),
                            pltpu.SemaphoreType.DMA((2,))]),
        out_shape=jax.ShapeDtypeStruct((n_out, nj, LANES), jnp.uint32),
        compiler_params=_params("arbitrary"),
        name="grouped_experts",
    )(meta, src, src, src, dst, h3, w_gate, w_up, w_down)


def _combine_kernel(x_ref, gt_ref, route_ref, y0_ref, y1_ref, g_ref, sh_ref, sc_ref, o_ref, h_ref):
    rows, d = x_ref.shape
    nw = d // (2 * LANES)
    w1 = jnp.broadcast_to(route_ref[:, 2:3], (rows, LANES))
    w2 = jnp.broadcast_to(route_ref[:, 3:4], (rows, LANES))
    for j in range(nw):
        u0 = y0_ref[pl.ds(j, rows, stride=nw), :]
        u1 = y1_ref[pl.ds(j, rows, stride=nw), :]
        for blk, half in ((j, lambda u: u << 16), (nw + j, lambda u: u & jnp.uint32(0xFFFF0000))):
            js = slice(blk * LANES, (blk + 1) * LANES)
            y = (w1 * lax.bitcast_convert_type(half(u0), F32)
                 + w2 * lax.bitcast_convert_type(half(u1), F32))
            o_ref[:, js] = x_ref[:, js] + gt_ref[:, js] * y
    h_ref[...] = _ada_norm(o_ref[...], g_ref[...], sc_ref[...], sh_ref[...]).astype(h_ref.dtype)


def _combine(x, ys, route, mod3, g_next, mod3_next, seq):
    T, D = x.shape
    rows = COMBINE_ROWS
    nblk = T // rows
    nj = D // (2 * LANES)
    batch_of = lambda i: i * rows // seq
    return pl.pallas_call(
        _combine_kernel,
        grid=(nblk,),
        in_specs=[pl.BlockSpec((rows, D), lambda i: (i, 0)),
                  pl.BlockSpec((None, 1, D), lambda i: (batch_of(i), 0, 5)),
                  pl.BlockSpec((rows, LANES), lambda i: (i, 0)),
                  pl.BlockSpec((rows * nj, LANES), lambda i: (i, 0)),
                  pl.BlockSpec((rows * nj, LANES), lambda i: (nblk + i, 0)),
                  pl.BlockSpec((1, D), lambda i: (0, 0)),
                  pl.BlockSpec((None, 1, D), lambda i: (batch_of(i), 0, 0)),
                  pl.BlockSpec((None, 1, D), lambda i: (batch_of(i), 0, 1))],
        out_specs=[pl.BlockSpec((rows, D), lambda i: (i, 0)),
                   pl.BlockSpec((rows, D), lambda i: (i, 0))],
        out_shape=[jax.ShapeDtypeStruct((T, D), F32), jax.ShapeDtypeStruct((T, D), BF16)],
        compiler_params=_params("parallel"),
        name="moe_combine",
    )(x, mod3, route, ys, ys, g_next, mod3_next, mod3_next)


def _dispatch_plan(route, l):
    T = route.shape[0]
    tm = EXPERT_TM
    n_tiles = (2 * T) // tm + N_EXPERTS
    P = n_tiles * tm
    e = route[:, 0:2].astype(jnp.int32).T.reshape(-1)
    onehot = (e[:, None] == jnp.arange(N_EXPERTS, dtype=jnp.int32)[None, :]).astype(jnp.int32)
    csum = jnp.cumsum(onehot, axis=0)
    rank = jnp.sum((csum - onehot) * onehot, axis=1)
    counts = csum[-1]
    tiles_per = (counts + tm - 1) // tm
    tile_end = jnp.cumsum(tiles_per)
    off = (tile_end - tiles_per) * tm
    pos = off[e] + rank
    asg = jnp.full((P,), -1, jnp.int32).at[pos].set(jnp.arange(2 * T, dtype=jnp.int32))
    slot = jnp.arange(P, dtype=jnp.int32)
    spare = 2 * T + ((slot // tm) % 2) * tm + slot % tm
    src = jnp.where(asg >= 0, asg % T, 0).reshape(n_tiles, 1, tm)
    dst = jnp.where(asg >= 0, asg, spare).reshape(n_tiles, 1, tm)
    tile_ids = jnp.arange(n_tiles, dtype=jnp.int32)
    tile_expert = jnp.minimum(jnp.sum(tile_ids[:, None] >= tile_end[None, :], axis=1),
                              N_EXPERTS - 1).astype(jnp.int32)
    ids = jnp.arange(N_EXPERTS, dtype=jnp.int32)
    used = counts > 0
    later = lax.cummin(jnp.where(used, ids, N_EXPERTS)[::-1])[::-1]
    nxt = jnp.concatenate([later[1:], jnp.full((1,), N_EXPERTS, jnp.int32)])
    nxt = jnp.where(nxt < N_EXPERTS, nxt, -1).astype(jnp.int32)
    wset = ((jnp.cumsum(used.astype(jnp.int32)) - 1) % 2).astype(jnp.int32)
    meta = jnp.concatenate([jnp.stack([l, tile_end[-1].astype(jnp.int32)]), tile_expert,
                            nxt[tile_expert], wset[tile_expert]])
    return src, dst, meta


def kernel(x, c, positions, w_ada, b_ada, g_norm_mix, g_norm_ffn, w_in, w_fg, b_fg, g_gla_out,
           g_q, g_k, w_gla_proj, w_attn_proj, w_out, w_route_group, b_route_group,
           w_route_expert, b_route_expert, w_exp_gate, w_exp_up, w_exp_down):
    B, S, D = x.shape
    L = w_ada.shape[0]
    T = B * S
    dk_all = w_fg.shape[-1]
    dv_all = g_gla_out.shape[-1]
    att_w = N_DIL * ATT_HEADS * HEAD_DIM
    gla_cols = 2 * dk_all + 2 * dv_all

    mod = _modulation(c, w_ada, b_ada)
    cos2, sin2 = _rope_tables(positions)

    wfg_pad = jnp.zeros((L, LANES, dk_all), F32).at[:, :GLA_RANK].set(w_fg)
    wr_pad = jnp.zeros((L, D, LANES), F32)
    wr_pad = wr_pad.at[:, :, :N_GROUPS].set(w_route_group)
    wr_pad = wr_pad.at[:, :, N_GROUPS:N_GROUPS + N_EXPERTS].set(w_route_expert)
    br_pad = jnp.zeros((L, 1, LANES), F32)
    br_pad = br_pad.at[:, 0, :N_GROUPS].set(b_route_group)
    br_pad = br_pad.at[:, 0, N_GROUPS:N_GROUPS + N_EXPERTS].set(b_route_expert)
    g_mix = g_norm_mix.reshape(L, 1, D)
    g_ffn = g_norm_ffn.reshape(L, 1, D)
    b_fg3 = b_fg.reshape(L, 1, dk_all)
    g_gla3 = g_gla_out.reshape(L, 1, dv_all)
    g_q3 = g_q.reshape(L, 1, HEAD_DIM)
    g_k3 = g_k.reshape(L, 1, HEAD_DIM)

    def residual(acc, xres, gt):
        return xres + gt * acc

    def mod_of(l):
        return lax.dynamic_index_in_dim(mod, l, 0, keepdims=False).reshape(8, 1, N_MOD * D)

    def layer(l, carry):
        xt, h = carry
        l = jnp.asarray(l, jnp.int32)
        mod3 = mod_of(l)
        proj1 = _matmul(h, w_in, l, col0=0, n=gla_cols, out_dtype=BF16, tn=2 * MM_TN,
                        tm=2 * MM_TM, name="proj_gla")
        fg = _matmul(h, w_in, l, col0=gla_cols, n=LANES, tn=LANES, out_dtype=F32, name="proj_fg")
        proj2 = _matmul(h, w_in, l, col0=gla_cols, shift=GLA_RANK, n=3 * att_w,
                        out_dtype=BF16, tm=2 * MM_TM, name="proj_att")
        gates = _matmul(h, w_in, l, col0=gla_cols + 3 * att_w, shift=GLA_RANK, n=2 * D,
                        out_dtype=BF16, tm=2 * MM_TM, name="proj_gates")
        o_gla = _gla(proj1, fg, l, wfg_pad, b_fg3, g_gla3, B, S)
        prep = _qkprep(proj2, l, cos2, sin2, g_q3, g_k3)
        outs, lses = [], []
        for grp, (_, dil) in enumerate(DIL_GROUPS):
            o, lse = _attn_group(prep[grp], prep[N_DIL + grp], prep[2 * N_DIL + grp], dil, B, S)
            outs.append(o)
            lses.append(lse)
        o_att = _attn_mix(outs, lses)
        merged = _merge(o_gla, o_att, gates, l, w_gla_proj, w_attn_proj)
        tn_out = 2 * MM_TN
        gt_spec = pl.BlockSpec((None, 1, tn_out),
                               lambda j, i, l: (i * MM_TM // S, 0, 2 * (D // tn_out) + j))
        xt = _matmul(merged, w_out, l, col0=0, n=D, out_dtype=F32, tn=tn_out, epilogue=residual,
                     extras=(xt, mod3),
                     extra_specs=(pl.BlockSpec((MM_TM, tn_out), lambda j, i, l: (i, j)), gt_spec),
                     name="out_proj")
        h2, route = _router(xt, l, g_ffn, mod3, wr_pad, br_pad, S)
        src, dst, meta = _dispatch_plan(route, l)
        nj = D // (2 * LANES)
        n_out = 2 * T + 2 * EXPERT_TM
        ys = _experts(h2.reshape(T, nj, LANES), src, dst, meta, w_exp_gate, w_exp_up, w_exp_down,
                      n_out)
        l_next = jnp.minimum(l + 1, L - 1)
        g_next = lax.dynamic_index_in_dim(g_mix, l_next, 0, keepdims=False)
        return tuple(_combine(xt, ys.reshape(n_out * nj, LANES), route, mod3, g_next,
                              mod_of(l_next), S))

    x0 = x.reshape(T, D)
    zero = jnp.int32(0)
    h0 = _norm(x0, zero, g_mix, mod_of(zero), 0, S)
    xt, _ = lax.fori_loop(0, L, layer, (x0, h0))
    return xt.reshape(B, S, D)
```

```python
import functools

import jax
import jax.numpy as jnp
from jax import lax
from jax.experimental import pallas as pl
from jax.experimental.pallas import tpu as pltpu

F32 = jnp.float32
BF16 = jnp.bfloat16

GLA_HEADS = 4
GLA_RANK = 16
GLA_TAU = 16.0
GLA_CHUNK = 64
DIL_GROUPS = ((128, 1), (512, 4), (2048, 16))
N_DIL = 3
ATT_HEADS = 4
HEAD_DIM = 128
ATT_BLOCK = 128
ROPE_THETA = 10000.0
N_GROUPS = 4
EXPERTS_PER_GROUP = 8
N_EXPERTS = N_GROUPS * EXPERTS_PER_GROUP
N_MOD = 6
EPS = 1e-6

LANES = 128
VMEM_LIMIT = 56 * 1024 * 1024
NEG_BIG = -1e30

NORM_ROWS = 512
MM_TM = 1024
MM_TN = 512
GLA_ROWS = 512
PREP_ROWS = 512
ATT_SPAN = 2048
MERGE_ROWS = 1024
EXPERT_TM = 256
COMBINE_ROWS = 512


def _params(*sem):
    return pltpu.CompilerParams(dimension_semantics=sem, vmem_limit_bytes=VMEM_LIMIT)


def _dot(a, b):
    return jnp.dot(a, b, preferred_element_type=F32)


def _dot_nt(a, b):
    return lax.dot_general(a, b, (((1,), (1,)), ((), ())), preferred_element_type=F32)


def _dot_tn(a, b):
    return lax.dot_general(a, b, (((0,), (0,)), ((), ())), preferred_element_type=F32)


def _split_bf16(x):
    hi = x.astype(BF16)
    lo = (x - hi.astype(F32)).astype(BF16)
    return hi, lo


def _sigmoid(x):
    return 1.0 / (1.0 + jnp.exp(-x))


def _pack_rows(xb):
    bits = lax.bitcast_convert_type(xb, jnp.uint32)
    nw = xb.shape[1] // (2 * LANES)
    out = []
    for j in range(nw):
        lo = bits[:, j * LANES:(j + 1) * LANES] >> 16
        hi = bits[:, (nw + j) * LANES:(nw + j + 1) * LANES] & jnp.uint32(0xFFFF0000)
        out.append(lo | hi)
    return out


def _unpack_rows(words):
    lo = [lax.bitcast_convert_type(w << 16, F32) for w in words]
    hi = [lax.bitcast_convert_type(w & jnp.uint32(0xFFFF0000), F32) for w in words]
    return jnp.concatenate(lo + hi, axis=1)


def _grid_call(kernel, l, grid, in_specs, out_specs, out_shape, scratch=(), sem=None, name=None):
    sem = sem or ("arbitrary",) * len(grid)
    return pl.pallas_call(
        kernel,
        grid_spec=pltpu.PrefetchScalarGridSpec(
            num_scalar_prefetch=1, grid=grid, in_specs=in_specs, out_specs=out_specs,
            scratch_shapes=list(scratch)),
        out_shape=out_shape,
        compiler_params=_params(*sem),
        name=name,
    ), jnp.reshape(l, (1,)).astype(jnp.int32)


def _mod_kernel(c_ref, w_ref, b_ref, o_ref, act_ref):
    n_batch = c_ref.shape[0]
    tn = w_ref.shape[1]

    @pl.when((pl.program_id(0) == 0) & (pl.program_id(1) == 0))
    def _():
        c = c_ref[...]
        act_ref[...] = c * _sigmoid(c)

    rows = [[] for _ in range(n_batch)]
    for jb in range(tn // LANES):
        w = w_ref[:, jb * LANES:(jb + 1) * LANES]
        for b in range(n_batch):
            rows[b].append(jnp.sum(w * act_ref[b], axis=0, keepdims=True))
    out = jnp.concatenate([jnp.concatenate(r, axis=1) for r in rows], axis=0) + b_ref[...]
    o_ref[0:n_batch, :] = out
    o_ref[n_batch:, :] = jnp.zeros((o_ref.shape[0] - n_batch, tn), F32)


def _modulation(c, w_ada, b_ada):
    L, D, N = w_ada.shape
    B = c.shape[0]
    rows = 8
    assert B <= rows
    c_rep = jnp.broadcast_to(c[:, :, None], (B, D, LANES))
    tn = 512
    return pl.pallas_call(
        _mod_kernel,
        grid=(L, N // tn),
        in_specs=[pl.BlockSpec((B, D, LANES), lambda l, j: (0, 0, 0)),
                  pl.BlockSpec((None, D, tn), lambda l, j: (l, 0, j)),
                  pl.BlockSpec((None, 1, tn), lambda l, j: (l, 0, j))],
        out_specs=pl.BlockSpec((None, rows, tn), lambda l, j: (l, 0, j)),
        out_shape=jax.ShapeDtypeStruct((L, rows, N), F32),
        scratch_shapes=[pltpu.VMEM((B, D, LANES), F32)],
        compiler_params=_params("arbitrary", "arbitrary"),
        name="adaln_mod",
    )(c_rep, w_ada, b_ada.reshape(L, 1, N))


def _rope_kernel(pos_ref, freq_ref, cos_ref, sin_ref):
    ang = pos_ref[...].astype(F32) * freq_ref[...]
    lane = lax.broadcasted_iota(jnp.int32, ang.shape, 1)
    cos_ref[...] = jnp.cos(ang)
    sin_ref[...] = jnp.where(lane < HEAD_DIM // 2, -jnp.sin(ang), jnp.sin(ang))


def _rope_tables(positions):
    T = positions.size
    inv_freq = ROPE_THETA ** (-jnp.arange(0, HEAD_DIM, 2, dtype=F32) / HEAD_DIM)
    freq2 = jnp.concatenate([inv_freq, inv_freq]).reshape(1, HEAD_DIM)
    rows = 2048
    return pl.pallas_call(
        _rope_kernel,
        grid=(T // rows,),
        in_specs=[pl.BlockSpec((rows, 1), lambda i: (i, 0)),
                  pl.BlockSpec((1, HEAD_DIM), lambda i: (0, 0))],
        out_specs=[pl.BlockSpec((rows, HEAD_DIM), lambda i: (i, 0))] * 2,
        out_shape=[jax.ShapeDtypeStruct((T, HEAD_DIM), F32)] * 2,
        compiler_params=_params("parallel"),
        name="rope_tables",
    )(positions.reshape(T, 1), freq2)


def _ada_norm(x, g, scale, shift):
    y = x * lax.rsqrt(jnp.mean(x * x, axis=-1, keepdims=True) + EPS)
    return y * g * (1.0 + scale) + shift


def _norm_kernel(l_ref, x_ref, g_ref, sh_ref, sc_ref, o_ref):
    o_ref[...] = _ada_norm(x_ref[...], g_ref[...], sc_ref[...], sh_ref[...]).astype(o_ref.dtype)


def _mod_spec(which, rows, seq):
    return lambda i, l: (i * rows // seq, 0, which)


def _norm(x, l, g_all, mod3, which_shift, seq):
    T, D = x.shape
    rows = NORM_ROWS
    call, lidx = _grid_call(
        _norm_kernel, l, (T // rows,),
        [pl.BlockSpec((rows, D), lambda i, l: (i, 0)),
         pl.BlockSpec((None, 1, D), lambda i, l: (l[0], 0, 0)),
         pl.BlockSpec((None, 1, D), _mod_spec(which_shift, rows, seq)),
         pl.BlockSpec((None, 1, D), _mod_spec(which_shift + 1, rows, seq))],
        pl.BlockSpec((rows, D), lambda i, l: (i, 0)),
        jax.ShapeDtypeStruct((T, D), BF16), sem=("parallel",), name="ada_norm")
    return call(lidx, x, g_all, mod3, mod3)


def _mm_kernel(l_ref, a_ref, w_ref, *rest, epilogue, n_extra, shift):
    n_w = 1 if shift else 0
    extra = rest[n_w:n_w + n_extra]
    o_ref = rest[n_w + n_extra]
    wbf_ref = rest[n_w + n_extra + 1]
    tn = o_ref.shape[1]

    @pl.when(pl.program_id(1) == 0)
    def _():
        if shift:
            wide = jnp.concatenate([w_ref[...], rest[0][...]], axis=1)
            wbf_ref[...] = wide[:, shift:shift + tn].astype(BF16)
        else:
            wbf_ref[...] = w_ref[...].astype(BF16)

    acc = _dot(a_ref[...], wbf_ref[...])
    if epilogue is not None:
        acc = epilogue(acc, *[e[...] for e in extra])
    o_ref[...] = acc.astype(o_ref.dtype)


def _matmul(a, w, l, *, col0, n, out_dtype, tn=MM_TN, tm=MM_TM, shift=0, epilogue=None,
            extras=(), extra_specs=(), name="matmul"):
    M, K = a.shape
    assert col0 % tn == 0 and n % tn == 0 and M % tm == 0 and 0 <= shift < LANES
    cb0 = col0 // tn
    kern = functools.partial(_mm_kernel, epilogue=epilogue, n_extra=len(extras), shift=shift)
    w_specs = [pl.BlockSpec((None, K, tn), lambda j, i, l: (l[0], 0, cb0 + j))]
    if shift:
        per = tn // LANES
        w_specs.append(pl.BlockSpec((None, K, LANES), lambda j, i, l: (l[0], 0, (cb0 + j + 1) * per)))
    call, lidx = _grid_call(
        kern, l, (n // tn, M // tm),
        [pl.BlockSpec((tm, K), lambda j, i, l: (i, 0))] + w_specs + list(extra_specs),
        pl.BlockSpec((tm, tn), lambda j, i, l: (i, j)),
        jax.ShapeDtypeStruct((M, n), out_dtype),
        scratch=[pltpu.VMEM((K, tn), BF16)], name=name)
    return call(lidx, a, *([w] * len(w_specs)), *extras)


def _gla_kernel(l_ref, q_ref, k_ref, v_ref, r_ref, fg_ref, wfg_ref, bfg_ref, g_ref, o_ref,
                state_ref, *, head_k, head_v):
    rows = q_ref.shape[0]
    C = GLA_CHUNK

    @pl.when(pl.program_id(1) == 0)
    def _():
        state_ref[...] = jnp.zeros_like(state_ref)

    fh, fl = _split_bf16(fg_ref[...])
    wh, wl = _split_bf16(wfg_ref[...])
    z = _dot(fh, wh) + _dot(fl, wh) + _dot(fh, wl) + bfg_ref[...]
    log_a = (jnp.minimum(z, 0.0) - jnp.log(1.0 + jnp.exp(-jnp.abs(z)))) * (1.0 / GLA_TAU)

    ri = lax.broadcasted_iota(jnp.int32, (C, C), 0)
    ci = lax.broadcasted_iota(jnp.int32, (C, C), 1)
    causal = ri >= ci
    tri = causal.astype(BF16)
    scale = head_k ** -0.5

    for c in range(rows // C):
        rs = slice(c * C, (c + 1) * C)
        lh, ll = _split_bf16(log_a[rs])
        b = _dot(tri, lh) + _dot(tri, ll)
        b_last = b[C - 1:C]
        q = q_ref[rs, :].astype(F32) * scale
        k = k_ref[rs, :].astype(F32)
        q_dec = (q * jnp.exp(b)).astype(BF16)
        k_inv = (k * jnp.exp(-b)).astype(BF16)
        k_upd = (k * jnp.exp(b_last - b)).astype(BF16)
        decay = jnp.exp(b_last)
        for h in range(GLA_HEADS):
            ks = slice(h * head_k, (h + 1) * head_k)
            vs = slice(h * head_v, (h + 1) * head_v)
            v = v_ref[rs, vs]
            att = jnp.where(causal, _dot_nt(q_dec[:, ks], k_inv[:, ks]), 0.0).astype(BF16)
            st = state_ref[h]
            o = _dot(att, v) + _dot_nt(q_dec[:, ks], st.astype(BF16))
            state_ref[h] = st * decay[:, ks] + _dot_tn(v, k_upd[:, ks])
            o = o * lax.rsqrt(jnp.mean(o * o, axis=-1, keepdims=True) + EPS) * g_ref[:, vs]
            r = r_ref[rs, vs].astype(F32)
            o_ref[rs, vs] = (o * (r * _sigmoid(r))).astype(o_ref.dtype)


def _gla(proj, fg, l, wfg_pad, b_fg, g_gla_out, batch, seq):
    T = proj.shape[0]
    dk_all = wfg_pad.shape[-1]
    dv_all = g_gla_out.shape[-1]
    rows = GLA_ROWS
    nblk = seq // rows
    kern = functools.partial(_gla_kernel, head_k=dk_all // GLA_HEADS, head_v=dv_all // GLA_HEADS)
    row = lambda b, n, l: b * nblk + n
    call, lidx = _grid_call(
        kern, l, (batch, nblk),
        [pl.BlockSpec((rows, dk_all), lambda b, n, l: (row(b, n, l), 0)),
         pl.BlockSpec((rows, dk_all), lambda b, n, l: (row(b, n, l), 1)),
         pl.BlockSpec((rows, dv_all), lambda b, n, l: (row(b, n, l), 1)),
         pl.BlockSpec((rows, dv_all), lambda b, n, l: (row(b, n, l), 2)),
         pl.BlockSpec((rows, LANES), lambda b, n, l: (row(b, n, l), 0)),
         pl.BlockSpec((None, LANES, dk_all), lambda b, n, l: (l[0], 0, 0)),
         pl.BlockSpec((None, 1, dk_all), lambda b, n, l: (l[0], 0, 0)),
         pl.BlockSpec((None, 1, dv_all), lambda b, n, l: (l[0], 0, 0))],
        pl.BlockSpec((rows, dv_all), lambda b, n, l: (row(b, n, l), 0)),
        jax.ShapeDtypeStruct((T, dv_all), BF16),
        scratch=[pltpu.VMEM((GLA_HEADS, dv_all // GLA_HEADS, dk_all // GLA_HEADS), F32)],
        sem=("parallel", "arbitrary"), name="gla")
    return call(lidx, proj, proj, proj, proj, fg, wfg_pad, b_fg, g_gla_out)


def _qkprep_kernel(l_ref, q_ref, k_ref, v_ref, cos_ref, sin_ref, gq_ref, gk_ref, *refs):
    out_refs, scr_ref = refs[:-1], refs[-1]
    cos = cos_ref[...]
    sin = sin_ref[...]
    gw = ATT_HEADS * HEAD_DIM
    rows = q_ref.shape[0]

    def tables(g_ref, scale):
        g = jnp.broadcast_to(g_ref[...], cos.shape)
        return g * cos * scale, pltpu.roll(g, HEAD_DIM // 2, 1) * sin * scale

    def prep(ref, tab):
        x = ref.astype(F32)
        n = x * lax.rsqrt(jnp.mean(x * x, axis=-1, keepdims=True) + EPS)
        return n * tab[0] + pltpu.roll(n, HEAD_DIM // 2, 1) * tab[1]

    def emit(dst_ref, dil):
        n = rows // dil
        for p in range(dil):
            for h in range(ATT_HEADS):
                cs = slice(p * gw + h * HEAD_DIM, p * gw + (h + 1) * HEAD_DIM)
                dst_ref[:, cs] = scr_ref[h, pl.ds(p, n, stride=dil), :].astype(BF16)

    tab_q = tables(gq_ref, HEAD_DIM ** -0.5)
    tab_k = tables(gk_ref, 1.0)
    for grp, (_, dil) in enumerate(DIL_GROUPS):
        for ref, tab, out in ((q_ref, tab_q, out_refs[grp]), (k_ref, tab_k, out_refs[N_DIL + grp])):
            for h in range(ATT_HEADS):
                src = slice(grp * gw + h * HEAD_DIM, grp * gw + (h + 1) * HEAD_DIM)
                y = prep(ref[:, src], tab)
                if dil == 1:
                    out[:, h * HEAD_DIM:(h + 1) * HEAD_DIM] = y.astype(BF16)
                else:
                    scr_ref[h] = y
            if dil > 1:
                emit(out, dil)
        if dil == 1:
            out_refs[2 * N_DIL + grp][...] = v_ref[:, grp * gw:(grp + 1) * gw]
        else:
            for h in range(ATT_HEADS):
                src = slice(grp * gw + h * HEAD_DIM, grp * gw + (h + 1) * HEAD_DIM)
                scr_ref[h] = v_ref[:, src].astype(F32)
            emit(out_refs[2 * N_DIL + grp], dil)


def _qkprep(proj2, l, cos2, sin2, g_q, g_k):
    T = proj2.shape[0]
    rows = PREP_ROWS
    width = N_DIL * ATT_HEADS * HEAD_DIM
    gw = ATT_HEADS * HEAD_DIM
    dils = [dil for _, dil in DIL_GROUPS] * 3
    call, lidx = _grid_call(
        _qkprep_kernel, l, (T // rows,),
        [pl.BlockSpec((rows, width), lambda i, l: (i, 0)),
         pl.BlockSpec((rows, width), lambda i, l: (i, 1)),
         pl.BlockSpec((rows, width), lambda i, l: (i, 2)),
         pl.BlockSpec((rows, HEAD_DIM), lambda i, l: (i, 0)),
         pl.BlockSpec((rows, HEAD_DIM), lambda i, l: (i, 0)),
         pl.BlockSpec((None, 1, HEAD_DIM), lambda i, l: (l[0], 0, 0)),
         pl.BlockSpec((None, 1, HEAD_DIM), lambda i, l: (l[0], 0, 0))],
        [pl.BlockSpec((rows // d, d * gw), lambda i, l: (i, 0)) for d in dils],
        [jax.ShapeDtypeStruct((T // d, d * gw), BF16) for d in dils],
        scratch=[pltpu.VMEM((ATT_HEADS, rows, HEAD_DIM), F32)],
        sem=("parallel",), name="qk_prep")
    return call(lidx, proj2, proj2, proj2, cos2, sin2, g_q, g_k)


def _attn_kernel(q_ref, kc_ref, kp_ref, vc_ref, vp_ref, o_ref, lse_ref, *, dil):
    first = pl.program_id(1) == 0
    QB = q_ref.shape[0]
    A = ATT_BLOCK
    qi = lax.broadcasted_iota(jnp.int32, (A, 2 * A), 0)
    kj = lax.broadcasted_iota(jnp.int32, (A, 2 * A), 1) - A
    dist = qi - kj
    valid = (dist >= 0) & (dist <= A)
    bias = jnp.where(valid, 0.0, NEG_BIG).astype(F32)
    bias_first = jnp.where(first, jnp.where(valid & (kj >= 0), 0.0, NEG_BIG), bias).astype(F32)

    for p in range(dil):
        for h in range(ATT_HEADS):
            cs = slice((p * ATT_HEADS + h) * HEAD_DIM, (p * ATT_HEADS + h + 1) * HEAD_DIM)
            for j in range(QB // A):
                rs = slice(j * A, (j + 1) * A)
                if j == 0:
                    kcat = jnp.concatenate([kp_ref[:, cs], kc_ref[0:A, cs]], axis=0)
                    vcat = jnp.concatenate([vp_ref[:, cs], vc_ref[0:A, cs]], axis=0)
                    bb = bias_first
                else:
                    kcat = kc_ref[(j - 1) * A:(j + 1) * A, cs]
                    vcat = vc_ref[(j - 1) * A:(j + 1) * A, cs]
                    bb = bias
                s = _dot_nt(q_ref[rs, cs], kcat) + bb
                m = jnp.max(s, axis=-1, keepdims=True)
                e = jnp.exp(s - m)
                den = jnp.sum(e, axis=-1, keepdims=True)
                o = _dot(e.astype(BF16), vcat) / den
                o_ref[rs, cs] = o.astype(o_ref.dtype)
                lse_ref[rs, cs] = jnp.broadcast_to(m + jnp.log(den), (A, HEAD_DIM))


def _attn_group(q, k, v, dil, batch, seq):
    rows_all, W = q.shape
    A = ATT_BLOCK
    QB = ATT_SPAN // dil
    nsp = seq // ATT_SPAN
    per = QB // A
    cur = pl.BlockSpec((QB, W), lambda b, n: (b * nsp + n, 0))
    prev = pl.BlockSpec((A, W), lambda b, n: (jnp.maximum((b * nsp + n) * per - 1, b * nsp * per), 0))
    return pl.pallas_call(
        functools.partial(_attn_kernel, dil=dil),
        grid=(batch, nsp),
        in_specs=[cur, cur, prev, cur, prev],
        out_specs=[cur, cur],
        out_shape=[jax.ShapeDtypeStruct((rows_all, W), BF16),
                   jax.ShapeDtypeStruct((rows_all, W), F32)],
        compiler_params=_params("parallel", "arbitrary"),
        name=f"dilated_attn_r{dil}",
    )(q, k, k, v, v)


def _attn_mix_kernel(o0, o1, o2, l0, l1, l2, out_ref, so_ref, sl_ref):
    rows, gw = out_ref.shape
    o_refs, l_refs = (o0, o1, o2), (l0, l1, l2)
    for h in range(ATT_HEADS):
        hs = slice(h * HEAD_DIM, (h + 1) * HEAD_DIM)
        outs, lses = [], []
        for g, (_, dil) in enumerate(DIL_GROUPS):
            if dil == 1:
                outs.append(o_refs[g][:, hs].astype(F32))
                lses.append(l_refs[g][:, hs])
                continue
            n = rows // dil
            for p in range(dil):
                cs = slice(p * gw + h * HEAD_DIM, p * gw + (h + 1) * HEAD_DIM)
                so_ref[g, pl.ds(p, n, stride=dil), :] = o_refs[g][:, cs].astype(F32)
                sl_ref[g, pl.ds(p, n, stride=dil), :] = l_refs[g][:, cs]
            outs.append(so_ref[g])
            lses.append(sl_ref[g])
        m = jnp.maximum(jnp.maximum(lses[0], lses[1]), lses[2])
        ws = [jnp.exp(x - m) for x in lses]
        den = ws[0] + ws[1] + ws[2]
        out_ref[:, hs] = ((ws[0] * outs[0] + ws[1] * outs[1] + ws[2] * outs[2]) / den
                          ).astype(out_ref.dtype)


def _attn_mix(outs, lses):
    gw = ATT_HEADS * HEAD_DIM
    T = outs[0].shape[0]
    rows = PREP_ROWS
    dils = [dil for _, dil in DIL_GROUPS]
    specs = [pl.BlockSpec((rows // d, d * gw), lambda i: (i, 0)) for d in dils]
    return pl.pallas_call(
        _attn_mix_kernel,
        grid=(T // rows,),
        in_specs=specs + specs,
        out_specs=pl.BlockSpec((rows, gw), lambda i: (i, 0)),
        out_shape=jax.ShapeDtypeStruct((T, gw), BF16),
        scratch_shapes=[pltpu.VMEM((N_DIL, rows, HEAD_DIM), F32),
                        pltpu.VMEM((N_DIL, rows, HEAD_DIM), F32)],
        compiler_params=_params("parallel"),
        name="attn_mix",
    )(*outs, *lses)


def _merge_kernel(l_ref, ogla_ref, oatt_ref, gg_ref, ga_ref, wg_ref, wa_ref,
                  out_ref, wgbf_ref, wabf_ref):
    @pl.when(pl.program_id(1) == 0)
    def _():
        wgbf_ref[...] = wg_ref[...].astype(BF16)
        wabf_ref[...] = wa_ref[...].astype(BF16)

    a = _dot(ogla_ref[...], wgbf_ref[...])
    b = _dot(oatt_ref[...], wabf_ref[...])
    gg = _sigmoid(gg_ref[...].astype(F32))
    ga = _sigmoid(ga_ref[...].astype(F32))
    out_ref[...] = (gg * a + ga * b).astype(out_ref.dtype)


def _merge(o_gla, o_att, gates, l, w_gla_proj, w_attn_proj):
    T, dv = o_gla.shape
    gw = o_att.shape[1]
    D = w_gla_proj.shape[-1]
    tm, tn = MERGE_ROWS, 2 * MM_TN
    gcb = 0
    dcb = D // tn
    call, lidx = _grid_call(
        _merge_kernel, l, (D // tn, T // tm),
        [pl.BlockSpec((tm, dv), lambda j, i, l: (i, 0)),
         pl.BlockSpec((tm, gw), lambda j, i, l: (i, 0)),
         pl.BlockSpec((tm, tn), lambda j, i, l: (i, gcb + j)),
         pl.BlockSpec((tm, tn), lambda j, i, l: (i, gcb + dcb + j)),
         pl.BlockSpec((None, dv, tn), lambda j, i, l: (l[0], 0, j)),
         pl.BlockSpec((None, gw, tn), lambda j, i, l: (l[0], 0, j))],
        pl.BlockSpec((tm, tn), lambda j, i, l: (i, j)),
        jax.ShapeDtypeStruct((T, D), BF16),
        scratch=[pltpu.VMEM((dv, tn), BF16), pltpu.VMEM((gw, tn), BF16)], name="gated_merge")
    return call(lidx, o_gla, o_att, gates, gates, w_gla_proj, w_attn_proj)


def _router_kernel(l_ref, x_ref, g_ref, sh_ref, sc_ref, wr_ref, br_ref, h_ref, route_ref):
    h = _ada_norm(x_ref[...], g_ref[...], sc_ref[...], sh_ref[...])
    rows = h.shape[0]
    hb = h.astype(BF16)
    words = _pack_rows(hb.astype(F32))
    nw = len(words)
    for j in range(nw):
        h_ref[pl.ds(j, rows, stride=nw), :] = words[j]
    lg = _dot(hb, wr_ref[...].astype(BF16)) + br_ref[...]
    lane = lax.broadcasted_iota(jnp.int32, lg.shape, 1).astype(F32)
    big = float(4 * LANES)

    def first_argmax(vals):
        mx = jnp.max(vals, axis=-1, keepdims=True)
        idx = jnp.min(jnp.where(vals == mx, lane, big), axis=-1, keepdims=True)
        return mx, idx

    gl = jnp.where(lane < N_GROUPS, lg, NEG_BIG)
    gmax, gidx = first_argmax(gl)
    g_weight = 1.0 / jnp.sum(jnp.exp(gl - gmax), axis=-1, keepdims=True)
    lo = N_GROUPS + gidx * EXPERTS_PER_GROUP
    el = jnp.where((lane >= lo) & (lane < lo + EXPERTS_PER_GROUP), lg, NEG_BIG)
    m1, i1 = first_argmax(el)
    m2, i2 = first_argmax(jnp.where(lane == i1, NEG_BIG, el))
    e2 = jnp.exp(m2 - m1)
    w1 = g_weight / (1.0 + e2)
    w2 = g_weight * e2 / (1.0 + e2)
    route = jnp.where(lane == 0, i1 - N_GROUPS,
                      jnp.where(lane == 1, i2 - N_GROUPS,
                                jnp.where(lane == 2, w1, jnp.where(lane == 3, w2, 0.0))))
    route_ref[...] = route


def _router(x, l, g_all, mod3, wr_pad, br_pad, seq):
    T, D = x.shape
    rows = NORM_ROWS
    call, lidx = _grid_call(
        _router_kernel, l, (T // rows,),
        [pl.BlockSpec((rows, D), lambda i, l: (i, 0)),
         pl.BlockSpec((None, 1, D), lambda i, l: (l[0], 0, 0)),
         pl.BlockSpec((None, 1, D), _mod_spec(3, rows, seq)),
         pl.BlockSpec((None, 1, D), _mod_spec(4, rows, seq)),
         pl.BlockSpec((None, D, LANES), lambda i, l: (l[0], 0, 0)),
         pl.BlockSpec((None, 1, LANES), lambda i, l: (l[0], 0, 0))],
        [pl.BlockSpec((rows * (D // (2 * LANES)), LANES), lambda i, l: (i, 0)),
         pl.BlockSpec((rows, LANES), lambda i, l: (i, 0))],
        [jax.ShapeDtypeStruct((T * (D // (2 * LANES)), LANES), jnp.uint32),
         jax.ShapeDtypeStruct((T, LANES), F32)],
        sem=("parallel",), name="norm_router")
    return call(lidx, x, g_all, mod3, mod3, wr_pad, br_pad)


def _expert_kernel(meta_ref, src_one_ref, src_cur_ref, src_nxt_ref, dst_ref, h_ref,
                   wg_ref, wu_ref, wd_ref, ys_ref, xbuf_ref, obuf_ref, wgf_ref, wuf_ref, wdf_ref,
                   wgbf_ref, wubf_ref, wdbf_ref, gsem_ref, ssem_ref, wsem_ref):
    i = pl.program_id(0)
    n_tiles = pl.num_programs(0)
    n_used = meta_ref[1]
    slot = i % 2
    nj, tm = xbuf_ref.shape[1], xbuf_ref.shape[2]
    layer = meta_ref[0]
    expert = meta_ref[2 + i]
    next_expert = meta_ref[2 + n_tiles + i]
    wset = meta_ref[2 + 2 * n_tiles + i]

    def weight_copies(e, s):
        return [pltpu.make_async_copy(w.at[layer, e], buf.at[s], wsem_ref.at[s])
                for w, buf in ((wg_ref, wgf_ref), (wu_ref, wuf_ref), (wd_ref, wdf_ref))]

    def gather_copy(idx_ref, r, s):
        return pltpu.make_async_copy(h_ref.at[idx_ref[0, 0, r]], xbuf_ref.at[s, :, r, :],
                                     gsem_ref.at[s])

    def scatter_copy(idx_ref, r, s):
        return pltpu.make_async_copy(obuf_ref.at[s, :, r, :], ys_ref.at[idx_ref[0, 0, r]],
                                     ssem_ref.at[s])

    def gather_start(idx_ref, s):
        def body(r, c):
            gather_copy(idx_ref, r, s).start()
            return c
        lax.fori_loop(0, tm, body, 0, unroll=8)

    def gather_wait(s):
        pltpu.make_async_copy(obuf_ref.at[0], xbuf_ref.at[s], gsem_ref.at[s]).wait()

    def scatter_wait(s):
        pltpu.make_async_copy(xbuf_ref.at[0], obuf_ref.at[s], ssem_ref.at[s]).wait()

    @pl.when(i == 0)
    def _():
        for cp in weight_copies(expert, wset):
            cp.start()
        gather_start(src_cur_ref, 0)

        @pl.when(n_used > 1)
        def _():
            gather_start(src_one_ref, 1)
        obuf_ref[1] = jnp.zeros(obuf_ref.shape[1:], jnp.uint32)
        n_rows = ys_ref.shape[0]

        def fill(r, c):
            pltpu.make_async_copy(obuf_ref.at[1, :, r % tm, :], ys_ref.at[n_rows - 2 * tm + r],
                                  ssem_ref.at[1]).start()
            return c
        lax.fori_loop(0, 2 * tm, fill, 0, unroll=8)
        scatter_wait(1)
        scatter_wait(1)

    @pl.when(i < n_used)
    def _():
        xslot = i % 3
        gather_wait(xslot)

        @pl.when(i >= 2)
        def _():
            scatter_wait(slot)

        @pl.when((i == 0) | (expert != meta_ref[2 + jnp.maximum(i - 1, 0)]))
        def _():
            for cp in weight_copies(expert, wset):
                cp.wait()
            wgbf_ref[...] = wgf_ref[wset].astype(BF16)
            wubf_ref[...] = wuf_ref[wset].astype(BF16)
            wdbf_ref[...] = wdf_ref[wset].astype(BF16)

            @pl.when(next_expert >= 0)
            def _():
                for cp in weight_copies(next_expert, 1 - wset):
                    cp.start()

        x = _unpack_rows([xbuf_ref[xslot, j] for j in range(nj)]).astype(BF16)
        a = _dot(x, wgbf_ref[...])
        u = _dot(x, wubf_ref[...])
        hid = (a * _sigmoid(a)) * u
        y = _dot(hid.astype(BF16), wdbf_ref[...])
        for j, word in enumerate(_pack_rows(y.astype(BF16).astype(F32))):
            obuf_ref[slot, j] = word
        for r in range(tm):
            scatter_copy(dst_ref, r, slot).start()

        @pl.when(i + 2 < n_used)
        def _():
            nslot = (i + 2) % 3
            for r in range(tm):
                gather_copy(src_nxt_ref, r, nslot).start()

        @pl.when(i == n_used - 1)
        def _():
            @pl.when(i >= 1)
            def _():
                scatter_wait(1 - slot)
            scatter_wait(slot)


def _experts(h3, src, dst, meta, w_gate, w_up, w_down, n_out):
    T, nj, _ = h3.shape
    D = 2 * nj * LANES
    F = w_gate.shape[-1]
    n_tiles, _, tm = src.shape
    idx_spec = lambda f: pl.BlockSpec((1, 1, tm), lambda i, m: (f(i), 0, 0), memory_space=pltpu.SMEM)
    hbm = pl.BlockSpec(memory_space=pl.ANY)
    return pl.pallas_call(
        _expert_kernel,
        grid_spec=pltpu.PrefetchScalarGridSpec(
            num_scalar_prefetch=1, grid=(n_tiles,),
            in_specs=[idx_spec(lambda i: min(1, n_tiles - 1)), idx_spec(lambda i: i),
                      idx_spec(lambda i: jnp.minimum(i + 2, n_tiles - 1)), idx_spec(lambda i: i),
                      hbm, hbm, hbm, hbm],
            out_specs=hbm,
            scratch_shapes=[pltpu.VMEM((3, nj, tm, LANES), jnp.uint32),
                            pltpu.VMEM((2, nj, tm, LANES), jnp.uint32),
                            pltpu.VMEM((2, D, F), F32), pltpu.VMEM((2, D, F), F32),
                            pltpu.VMEM((2, F, D), F32),
                            pltpu.VMEM((D, F), BF16), pltpu.VMEM((D, F), BF16),
                            pltpu.VMEM((F, D), BF16),
                            pltpu.SemaphoreType.DMA((3,)), pltpu.SemaphoreType.DMA((2,)),
                            pltpu.SemaphoreType.DMA((2,))]),
        out_shape=jax.ShapeDtypeStruct((n_out, nj, LANES), jnp.uint32),
        compiler_params=_params("arbitrary"),
        name="grouped_experts",
    )(meta, src, src, src, dst, h3, w_gate, w_up, w_down)


def _combine_kernel(x_ref, gt_ref, route_ref, y0_ref, y1_ref, g_ref, sh_ref, sc_ref, o_ref, h_ref):
    rows, d = x_ref.shape
    nw = d // (2 * LANES)
    w1 = jnp.broadcast_to(route_ref[:, 2:3], (rows, LANES))
    w2 = jnp.broadcast_to(route_ref[:, 3:4], (rows, LANES))
    for j in range(nw):
        u0 = y0_ref[pl.ds(j, rows, stride=nw), :]
        u1 = y1_ref[pl.ds(j, rows, stride=nw), :]
        for blk, half in ((j, lambda u: u << 16), (nw + j, lambda u: u & jnp.uint32(0xFFFF0000))):
            js = slice(blk * LANES, (blk + 1) * LANES)
            y = (w1 * lax.bitcast_convert_type(half(u0), F32)
                 + w2 * lax.bitcast_convert_type(half(u1), F32))
            o_ref[:, js] = x_ref[:, js] + gt_ref[:, js] * y
    h_ref[...] = _ada_norm(o_ref[...], g_ref[...], sc_ref[...], sh_ref[...]).astype(h_ref.dtype)


def _combine(x, ys, route, mod3, g_next, mod3_next, seq):
    T, D = x.shape
    rows = COMBINE_ROWS
    nblk = T // rows
    nj = D // (2 * LANES)
    batch_of = lambda i: i * rows // seq
    return pl.pallas_call(
        _combine_kernel,
        grid=(nblk,),
        in_specs=[pl.BlockSpec((rows, D), lambda i: (i, 0)),
                  pl.BlockSpec((None, 1, D), lambda i: (batch_of(i), 0, 5)),
                  pl.BlockSpec((rows, LANES), lambda i: (i, 0)),
                  pl.BlockSpec((rows * nj, LANES), lambda i: (i, 0)),
                  pl.BlockSpec((rows * nj, LANES), lambda i: (nblk + i, 0)),
                  pl.BlockSpec((1, D), lambda i: (0, 0)),
                  pl.BlockSpec((None, 1, D), lambda i: (batch_of(i), 0, 0)),
                  pl.BlockSpec((None, 1, D), lambda i: (batch_of(i), 0, 1))],
        out_specs=[pl.BlockSpec((rows, D), lambda i: (i, 0)),
                   pl.BlockSpec((rows, D), lambda i: (i, 0))],
        out_shape=[jax.ShapeDtypeStruct((T, D), F32), jax.ShapeDtypeStruct((T, D), BF16)],
        compiler_params=_params("parallel"),
        name="moe_combine",
    )(x, mod3, route, ys, ys, g_next, mod3_next, mod3_next)


def _dispatch_plan(route, l):
    T = route.shape[0]
    tm = EXPERT_TM
    n_tiles = (2 * T) // tm + N_EXPERTS
    P = n_tiles * tm
    e = route[:, 0:2].astype(jnp.int32).T.reshape(-1)
    onehot = (e[:, None] == jnp.arange(N_EXPERTS, dtype=jnp.int32)[None, :]).astype(jnp.int32)
    csum = jnp.cumsum(onehot, axis=0)
    counts = csum[-1]
    tiles_per = (counts + tm - 1) // tm
    tile_end = jnp.cumsum(tiles_per)
    off = (tile_end - tiles_per) * tm
    pos = jnp.sum((csum - onehot + off[None, :]) * onehot, axis=1)
    asg = jnp.full((P,), -1, jnp.int32).at[pos].set(jnp.arange(2 * T, dtype=jnp.int32))
    slot = jnp.arange(P, dtype=jnp.int32)
    spare = 2 * T + ((slot // tm) % 2) * tm + slot % tm
    src = jnp.where(asg >= 0, asg % T, 0).reshape(n_tiles, 1, tm)
    dst = jnp.where(asg >= 0, asg, spare).reshape(n_tiles, 1, tm)
    tile_ids = jnp.arange(n_tiles, dtype=jnp.int32)
    tile_expert = jnp.minimum(jnp.sum(tile_ids[:, None] >= tile_end[None, :], axis=1),
                              N_EXPERTS - 1).astype(jnp.int32)
    ids = jnp.arange(N_EXPERTS, dtype=jnp.int32)
    used = counts > 0
    later = lax.cummin(jnp.where(used, ids, N_EXPERTS)[::-1])[::-1]
    nxt = jnp.concatenate([later[1:], jnp.full((1,), N_EXPERTS, jnp.int32)])
    nxt = jnp.where(nxt < N_EXPERTS, nxt, -1).astype(jnp.int32)
    wset = ((jnp.cumsum(used.astype(jnp.int32)) - 1) % 2).astype(jnp.int32)
    of_tile = tile_expert[:, None] == ids[None, :]
    nxt_tile = jnp.sum(jnp.where(of_tile, nxt[None, :], 0), axis=1)
    wset_tile = jnp.sum(jnp.where(of_tile, wset[None, :], 0), axis=1)
    meta = jnp.concatenate([jnp.stack([l, tile_end[-1].astype(jnp.int32)]), tile_expert,
                            nxt_tile, wset_tile])
    return src, dst, meta


def kernel(x, c, positions, w_ada, b_ada, g_norm_mix, g_norm_ffn, w_in, w_fg, b_fg, g_gla_out,
           g_q, g_k, w_gla_proj, w_attn_proj, w_out, w_route_group, b_route_group,
           w_route_expert, b_route_expert, w_exp_gate, w_exp_up, w_exp_down):
    B, S, D = x.shape
    L = w_ada.shape[0]
    T = B * S
    dk_all = w_fg.shape[-1]
    dv_all = g_gla_out.shape[-1]
    att_w = N_DIL * ATT_HEADS * HEAD_DIM
    gla_cols = 2 * dk_all + 2 * dv_all

    mod = _modulation(c, w_ada, b_ada)
    cos2, sin2 = _rope_tables(positions)

    wfg_pad = jnp.zeros((L, LANES, dk_all), F32).at[:, :GLA_RANK].set(w_fg)
    wr_pad = jnp.zeros((L, D, LANES), F32)
    wr_pad = wr_pad.at[:, :, :N_GROUPS].set(w_route_group)
    wr_pad = wr_pad.at[:, :, N_GROUPS:N_GROUPS + N_EXPERTS].set(w_route_expert)
    br_pad = jnp.zeros((L, 1, LANES), F32)
    br_pad = br_pad.at[:, 0, :N_GROUPS].set(b_route_group)
    br_pad = br_pad.at[:, 0, N_GROUPS:N_GROUPS + N_EXPERTS].set(b_route_expert)
    g_mix = g_norm_mix.reshape(L, 1, D)
    g_ffn = g_norm_ffn.reshape(L, 1, D)
    b_fg3 = b_fg.reshape(L, 1, dk_all)
    g_gla3 = g_gla_out.reshape(L, 1, dv_all)
    g_q3 = g_q.reshape(L, 1, HEAD_DIM)
    g_k3 = g_k.reshape(L, 1, HEAD_DIM)

    def residual(acc, xres, gt):
        return xres + gt * acc

    def mod_of(l):
        return lax.dynamic_index_in_dim(mod, l, 0, keepdims=False).reshape(8, 1, N_MOD * D)

    def layer(l, carry):
        xt, h = carry
        l = jnp.asarray(l, jnp.int32)
        mod3 = mod_of(l)
        proj1 = _matmul(h, w_in, l, col0=0, n=gla_cols, out_dtype=BF16, tn=2 * MM_TN,
                        tm=2 * MM_TM, name="proj_gla")
        fg = _matmul(h, w_in, l, col0=gla_cols, n=LANES, tn=LANES, out_dtype=F32, name="proj_fg")
        proj2 = _matmul(h, w_in, l, col0=gla_cols, shift=GLA_RANK, n=3 * att_w,
                        out_dtype=BF16, tm=2 * MM_TM, name="proj_att")
        gates = _matmul(h, w_in, l, col0=gla_cols + 3 * att_w, shift=GLA_RANK, n=2 * D,
                        out_dtype=BF16, tm=2 * MM_TM, name="proj_gates")
        o_gla = _gla(proj1, fg, l, wfg_pad, b_fg3, g_gla3, B, S)
        prep = _qkprep(proj2, l, cos2, sin2, g_q3, g_k3)
        outs, lses = [], []
        for grp, (_, dil) in enumerate(DIL_GROUPS):
            o, lse = _attn_group(prep[grp], prep[N_DIL + grp], prep[2 * N_DIL + grp], dil, B, S)
            outs.append(o)
            lses.append(lse)
        o_att = _attn_mix(outs, lses)
        merged = _merge(o_gla, o_att, gates, l, w_gla_proj, w_attn_proj)
        tn_out = 2 * MM_TN
        gt_spec = pl.BlockSpec((None, 1, tn_out),
                               lambda j, i, l: (i * MM_TM // S, 0, 2 * (D // tn_out) + j))
        xt = _matmul(merged, w_out, l, col0=0, n=D, out_dtype=F32, tn=tn_out, epilogue=residual,
                     extras=(xt, mod3),
                     extra_specs=(pl.BlockSpec((MM_TM, tn_out), lambda j, i, l: (i, j)), gt_spec),
                     name="out_proj")
        h2, route = _router(xt, l, g_ffn, mod3, wr_pad, br_pad, S)
        src, dst, meta = _dispatch_plan(route, l)
        nj = D // (2 * LANES)
        n_out = 2 * T + 2 * EXPERT_TM
        ys = _experts(h2.reshape(T, nj, LANES), src, dst, meta, w_exp_gate, w_exp_up, w_exp_down,
                      n_out)
        l_next = jnp.minimum(l + 1, L - 1)
        g_next = lax.dynamic_index_in_dim(g_mix, l_next, 0, keepdims=False)
        return tuple(_combine(xt, ys.reshape(n_out * nj, LANES), route, mod3, g_next,
                              mod_of(l_next), S))

    x0 = x.reshape(T, D)
    zero = jnp.int32(0)
    h0 = _norm(x0, zero, g_mix, mod_of(zero), 0, S)
    xt, _ = lax.fori_loop(0, L, layer, (x0, h0))
    return xt.reshape(B, S, D)
```

```python
import functools

import jax
import jax.numpy as jnp
from jax import lax
from jax.experimental import pallas as pl
from jax.experimental.pallas import tpu as pltpu

F32 = jnp.float32
BF16 = jnp.bfloat16

GLA_HEADS = 4
GLA_RANK = 16
GLA_TAU = 16.0
GLA_CHUNK = 64
DIL_GROUPS = ((128, 1), (512, 4), (2048, 16))
N_DIL = 3
ATT_HEADS = 4
HEAD_DIM = 128
ATT_BLOCK = 128
ROPE_THETA = 10000.0
N_GROUPS = 4
EXPERTS_PER_GROUP = 8
N_EXPERTS = N_GROUPS * EXPERTS_PER_GROUP
N_MOD = 6
EPS = 1e-6

LANES = 128
VMEM_LIMIT = 56 * 1024 * 1024
NEG_BIG = -1e30

NORM_ROWS = 512
MM_TM = 1024
MM_TN = 512
GLA_ROWS = 512
PREP_ROWS = 512
ATT_SPAN = 2048
MERGE_ROWS = 1024
EXPERT_TM = 256
COMBINE_ROWS = 512
INVERT_BLOCK = 4096


def _params(*sem):
    return pltpu.CompilerParams(dimension_semantics=sem, vmem_limit_bytes=VMEM_LIMIT)


def _dot(a, b):
    return jnp.dot(a, b, preferred_element_type=F32)


def _dot_nt(a, b):
    return lax.dot_general(a, b, (((1,), (1,)), ((), ())), preferred_element_type=F32)


def _dot_tn(a, b):
    return lax.dot_general(a, b, (((0,), (0,)), ((), ())), preferred_element_type=F32)


def _split_bf16(x):
    hi = x.astype(BF16)
    lo = (x - hi.astype(F32)).astype(BF16)
    return hi, lo


def _sigmoid(x):
    return 1.0 / (1.0 + jnp.exp(-x))


def _pack_rows(xb):
    bits = lax.bitcast_convert_type(xb, jnp.uint32)
    nw = xb.shape[1] // (2 * LANES)
    out = []
    for j in range(nw):
        lo = bits[:, j * LANES:(j + 1) * LANES] >> 16
        hi = bits[:, (nw + j) * LANES:(nw + j + 1) * LANES] & jnp.uint32(0xFFFF0000)
        out.append(lo | hi)
    return out


def _unpack_rows(words):
    lo = [lax.bitcast_convert_type(w << 16, F32) for w in words]
    hi = [lax.bitcast_convert_type(w & jnp.uint32(0xFFFF0000), F32) for w in words]
    return jnp.concatenate(lo + hi, axis=1)


def _grid_call(kernel, l, grid, in_specs, out_specs, out_shape, scratch=(), sem=None, name=None):
    sem = sem or ("arbitrary",) * len(grid)
    return pl.pallas_call(
        kernel,
        grid_spec=pltpu.PrefetchScalarGridSpec(
            num_scalar_prefetch=1, grid=grid, in_specs=in_specs, out_specs=out_specs,
            scratch_shapes=list(scratch)),
        out_shape=out_shape,
        compiler_params=_params(*sem),
        name=name,
    ), jnp.reshape(l, (1,)).astype(jnp.int32)


def _mod_kernel(c_ref, w_ref, b_ref, o_ref, act_ref):
    n_batch = c_ref.shape[0]
    tn = w_ref.shape[1]

    @pl.when((pl.program_id(0) == 0) & (pl.program_id(1) == 0))
    def _():
        c = c_ref[...]
        act_ref[...] = c * _sigmoid(c)

    rows = [[] for _ in range(n_batch)]
    for jb in range(tn // LANES):
        w = w_ref[:, jb * LANES:(jb + 1) * LANES]
        for b in range(n_batch):
            rows[b].append(jnp.sum(w * act_ref[b], axis=0, keepdims=True))
    out = jnp.concatenate([jnp.concatenate(r, axis=1) for r in rows], axis=0) + b_ref[...]
    o_ref[0:n_batch, :] = out
    o_ref[n_batch:, :] = jnp.zeros((o_ref.shape[0] - n_batch, tn), F32)


def _modulation(c, w_ada, b_ada):
    L, D, N = w_ada.shape
    B = c.shape[0]
    rows = 8
    assert B <= rows
    c_rep = jnp.broadcast_to(c[:, :, None], (B, D, LANES))
    tn = 512
    return pl.pallas_call(
        _mod_kernel,
        grid=(L, N // tn),
        in_specs=[pl.BlockSpec((B, D, LANES), lambda l, j: (0, 0, 0)),
                  pl.BlockSpec((None, D, tn), lambda l, j: (l, 0, j)),
                  pl.BlockSpec((None, 1, tn), lambda l, j: (l, 0, j))],
        out_specs=pl.BlockSpec((None, rows, tn), lambda l, j: (l, 0, j)),
        out_shape=jax.ShapeDtypeStruct((L, rows, N), F32),
        scratch_shapes=[pltpu.VMEM((B, D, LANES), F32)],
        compiler_params=_params("arbitrary", "arbitrary"),
        name="adaln_mod",
    )(c_rep, w_ada, b_ada.reshape(L, 1, N))


def _rope_kernel(pos_ref, freq_ref, cos_ref, sin_ref):
    ang = pos_ref[...].astype(F32) * freq_ref[...]
    lane = lax.broadcasted_iota(jnp.int32, ang.shape, 1)
    cos_ref[...] = jnp.cos(ang)
    sin_ref[...] = jnp.where(lane < HEAD_DIM // 2, -jnp.sin(ang), jnp.sin(ang))


def _rope_tables(positions):
    T = positions.size
    inv_freq = ROPE_THETA ** (-jnp.arange(0, HEAD_DIM, 2, dtype=F32) / HEAD_DIM)
    freq2 = jnp.concatenate([inv_freq, inv_freq]).reshape(1, HEAD_DIM)
    rows = 2048
    return pl.pallas_call(
        _rope_kernel,
        grid=(T // rows,),
        in_specs=[pl.BlockSpec((rows, 1), lambda i: (i, 0)),
                  pl.BlockSpec((1, HEAD_DIM), lambda i: (0, 0))],
        out_specs=[pl.BlockSpec((rows, HEAD_DIM), lambda i: (i, 0))] * 2,
        out_shape=[jax.ShapeDtypeStruct((T, HEAD_DIM), F32)] * 2,
        compiler_params=_params("parallel"),
        name="rope_tables",
    )(positions.reshape(T, 1), freq2)


def _ada_norm(x, g, scale, shift):
    y = x * lax.rsqrt(jnp.mean(x * x, axis=-1, keepdims=True) + EPS)
    return y * g * (1.0 + scale) + shift


def _norm_kernel(l_ref, x_ref, g_ref, sh_ref, sc_ref, o_ref):
    o_ref[...] = _ada_norm(x_ref[...], g_ref[...], sc_ref[...], sh_ref[...]).astype(o_ref.dtype)


def _mod_spec(which, rows, seq):
    return lambda i, l: (i * rows // seq, 0, which)


def _norm(x, l, g_all, mod3, which_shift, seq):
    T, D = x.shape
    rows = NORM_ROWS
    call, lidx = _grid_call(
        _norm_kernel, l, (T // rows,),
        [pl.BlockSpec((rows, D), lambda i, l: (i, 0)),
         pl.BlockSpec((None, 1, D), lambda i, l: (l[0], 0, 0)),
         pl.BlockSpec((None, 1, D), _mod_spec(which_shift, rows, seq)),
         pl.BlockSpec((None, 1, D), _mod_spec(which_shift + 1, rows, seq))],
        pl.BlockSpec((rows, D), lambda i, l: (i, 0)),
        jax.ShapeDtypeStruct((T, D), BF16), sem=("parallel",), name="ada_norm")
    return call(lidx, x, g_all, mod3, mod3)


def _mm_kernel(l_ref, a_ref, w_ref, *rest, epilogue, n_extra, shift):
    n_w = 1 if shift else 0
    extra = rest[n_w:n_w + n_extra]
    o_ref = rest[n_w + n_extra]
    wbf_ref = rest[n_w + n_extra + 1]
    tn = o_ref.shape[1]

    @pl.when(pl.program_id(1) == 0)
    def _():
        if shift:
            wide = jnp.concatenate([w_ref[...], rest[0][...]], axis=1)
            wbf_ref[...] = wide[:, shift:shift + tn].astype(BF16)
        else:
            wbf_ref[...] = w_ref[...].astype(BF16)

    acc = _dot(a_ref[...], wbf_ref[...])
    if epilogue is not None:
        acc = epilogue(acc, *[e[...] for e in extra])
    o_ref[...] = acc.astype(o_ref.dtype)


def _matmul(a, w, l, *, col0, n, out_dtype, tn=MM_TN, tm=MM_TM, shift=0, epilogue=None,
            extras=(), extra_specs=(), name="matmul"):
    M, K = a.shape
    assert col0 % tn == 0 and n % tn == 0 and M % tm == 0 and 0 <= shift < LANES
    cb0 = col0 // tn
    kern = functools.partial(_mm_kernel, epilogue=epilogue, n_extra=len(extras), shift=shift)
    w_specs = [pl.BlockSpec((None, K, tn), lambda j, i, l: (l[0], 0, cb0 + j))]
    if shift:
        per = tn // LANES
        w_specs.append(pl.BlockSpec((None, K, LANES), lambda j, i, l: (l[0], 0, (cb0 + j + 1) * per)))
    call, lidx = _grid_call(
        kern, l, (n // tn, M // tm),
        [pl.BlockSpec((tm, K), lambda j, i, l: (i, 0))] + w_specs + list(extra_specs),
        pl.BlockSpec((tm, tn), lambda j, i, l: (i, j)),
        jax.ShapeDtypeStruct((M, n), out_dtype),
        scratch=[pltpu.VMEM((K, tn), BF16)], name=name)
    return call(lidx, a, *([w] * len(w_specs)), *extras)


def _gla_kernel(l_ref, q_ref, k_ref, v_ref, r_ref, fg_ref, wfg_ref, bfg_ref, g_ref, o_ref,
                state_ref, *, head_k, head_v):
    rows = q_ref.shape[0]
    C = GLA_CHUNK

    @pl.when(pl.program_id(1) == 0)
    def _():
        state_ref[...] = jnp.zeros_like(state_ref)

    fh, fl = _split_bf16(fg_ref[...])
    wh, wl = _split_bf16(wfg_ref[...])
    z = _dot(fh, wh) + _dot(fl, wh) + _dot(fh, wl) + bfg_ref[...]
    log_a = (jnp.minimum(z, 0.0) - jnp.log(1.0 + jnp.exp(-jnp.abs(z)))) * (1.0 / GLA_TAU)

    ri = lax.broadcasted_iota(jnp.int32, (C, C), 0)
    ci = lax.broadcasted_iota(jnp.int32, (C, C), 1)
    causal = ri >= ci
    tri = causal.astype(BF16)
    scale = head_k ** -0.5

    for c in range(rows // C):
        rs = slice(c * C, (c + 1) * C)
        lh, ll = _split_bf16(log_a[rs])
        b = _dot(tri, lh) + _dot(tri, ll)
        b_last = b[C - 1:C]
        q = q_ref[rs, :].astype(F32) * scale
        k = k_ref[rs, :].astype(F32)
        q_dec = (q * jnp.exp(b)).astype(BF16)
        k_inv = (k * jnp.exp(-b)).astype(BF16)
        k_upd = (k * jnp.exp(b_last - b)).astype(BF16)
        decay = jnp.exp(b_last)
        for h in range(GLA_HEADS):
            ks = slice(h * head_k, (h + 1) * head_k)
            vs = slice(h * head_v, (h + 1) * head_v)
            v = v_ref[rs, vs]
            att = jnp.where(causal, _dot_nt(q_dec[:, ks], k_inv[:, ks]), 0.0).astype(BF16)
            st = state_ref[h]
            o = _dot(att, v) + _dot_nt(q_dec[:, ks], st.astype(BF16))
            state_ref[h] = st * decay[:, ks] + _dot_tn(v, k_upd[:, ks])
            o = o * lax.rsqrt(jnp.mean(o * o, axis=-1, keepdims=True) + EPS) * g_ref[:, vs]
            r = r_ref[rs, vs].astype(F32)
            o_ref[rs, vs] = (o * (r * _sigmoid(r))).astype(o_ref.dtype)


def _gla(proj, fg, l, wfg_pad, b_fg, g_gla_out, batch, seq):
    T = proj.shape[0]
    dk_all = wfg_pad.shape[-1]
    dv_all = g_gla_out.shape[-1]
    rows = GLA_ROWS
    nblk = seq // rows
    kern = functools.partial(_gla_kernel, head_k=dk_all // GLA_HEADS, head_v=dv_all // GLA_HEADS)
    row = lambda b, n, l: b * nblk + n
    call, lidx = _grid_call(
        kern, l, (batch, nblk),
        [pl.BlockSpec((rows, dk_all), lambda b, n, l: (row(b, n, l), 0)),
         pl.BlockSpec((rows, dk_all), lambda b, n, l: (row(b, n, l), 1)),
         pl.BlockSpec((rows, dv_all), lambda b, n, l: (row(b, n, l), 1)),
         pl.BlockSpec((rows, dv_all), lambda b, n, l: (row(b, n, l), 2)),
         pl.BlockSpec((rows, LANES), lambda b, n, l: (row(b, n, l), 0)),
         pl.BlockSpec((None, LANES, dk_all), lambda b, n, l: (l[0], 0, 0)),
         pl.BlockSpec((None, 1, dk_all), lambda b, n, l: (l[0], 0, 0)),
         pl.BlockSpec((None, 1, dv_all), lambda b, n, l: (l[0], 0, 0))],
        pl.BlockSpec((rows, dv_all), lambda b, n, l: (row(b, n, l), 0)),
        jax.ShapeDtypeStruct((T, dv_all), BF16),
        scratch=[pltpu.VMEM((GLA_HEADS, dv_all // GLA_HEADS, dk_all // GLA_HEADS), F32)],
        sem=("parallel", "arbitrary"), name="gla")
    return call(lidx, proj, proj, proj, proj, fg, wfg_pad, b_fg, g_gla_out)


def _qkprep_kernel(l_ref, q_ref, k_ref, v_ref, cos_ref, sin_ref, gq_ref, gk_ref, *refs):
    out_refs, scr_ref = refs[:-1], refs[-1]
    cos = cos_ref[...]
    sin = sin_ref[...]
    gw = ATT_HEADS * HEAD_DIM
    rows = q_ref.shape[0]

    def tables(g_ref, scale):
        g = jnp.broadcast_to(g_ref[...], cos.shape)
        return g * cos * scale, pltpu.roll(g, HEAD_DIM // 2, 1) * sin * scale

    def prep(ref, tab):
        x = ref.astype(F32)
        n = x * lax.rsqrt(jnp.mean(x * x, axis=-1, keepdims=True) + EPS)
        return n * tab[0] + pltpu.roll(n, HEAD_DIM // 2, 1) * tab[1]

    def emit(dst_ref, dil):
        n = rows // dil
        for p in range(dil):
            for h in range(ATT_HEADS):
                cs = slice(p * gw + h * HEAD_DIM, p * gw + (h + 1) * HEAD_DIM)
                dst_ref[:, cs] = scr_ref[h, pl.ds(p, n, stride=dil), :].astype(BF16)

    tab_q = tables(gq_ref, HEAD_DIM ** -0.5)
    tab_k = tables(gk_ref, 1.0)
    for grp, (_, dil) in enumerate(DIL_GROUPS):
        for ref, tab, out in ((q_ref, tab_q, out_refs[grp]), (k_ref, tab_k, out_refs[N_DIL + grp])):
            for h in range(ATT_HEADS):
                src = slice(grp * gw + h * HEAD_DIM, grp * gw + (h + 1) * HEAD_DIM)
                y = prep(ref[:, src], tab)
                if dil == 1:
                    out[:, h * HEAD_DIM:(h + 1) * HEAD_DIM] = y.astype(BF16)
                else:
                    scr_ref[h] = y
            if dil > 1:
                emit(out, dil)
        if dil == 1:
            out_refs[2 * N_DIL + grp][...] = v_ref[:, grp * gw:(grp + 1) * gw]
        else:
            for h in range(ATT_HEADS):
                src = slice(grp * gw + h * HEAD_DIM, grp * gw + (h + 1) * HEAD_DIM)
                scr_ref[h] = v_ref[:, src].astype(F32)
            emit(out_refs[2 * N_DIL + grp], dil)


def _qkprep(proj2, l, cos2, sin2, g_q, g_k):
    T = proj2.shape[0]
    rows = PREP_ROWS
    width = N_DIL * ATT_HEADS * HEAD_DIM
    gw = ATT_HEADS * HEAD_DIM
    dils = [dil for _, dil in DIL_GROUPS] * 3
    call, lidx = _grid_call(
        _qkprep_kernel, l, (T // rows,),
        [pl.BlockSpec((rows, width), lambda i, l: (i, 0)),
         pl.BlockSpec((rows, width), lambda i, l: (i, 1)),
         pl.BlockSpec((rows, width), lambda i, l: (i, 2)),
         pl.BlockSpec((rows, HEAD_DIM), lambda i, l: (i, 0)),
         pl.BlockSpec((rows, HEAD_DIM), lambda i, l: (i, 0)),
         pl.BlockSpec((None, 1, HEAD_DIM), lambda i, l: (l[0], 0, 0)),
         pl.BlockSpec((None, 1, HEAD_DIM), lambda i, l: (l[0], 0, 0))],
        [pl.BlockSpec((rows // d, d * gw), lambda i, l: (i, 0)) for d in dils],
        [jax.ShapeDtypeStruct((T // d, d * gw), BF16) for d in dils],
        scratch=[pltpu.VMEM((ATT_HEADS, rows, HEAD_DIM), F32)],
        sem=("parallel",), name="qk_prep")
    return call(lidx, proj2, proj2, proj2, cos2, sin2, g_q, g_k)


def _attn_kernel(q_ref, kc_ref, kp_ref, vc_ref, vp_ref, o_ref, lse_ref, *, dil):
    first = pl.program_id(1) == 0
    QB = q_ref.shape[0]
    A = ATT_BLOCK
    qi = lax.broadcasted_iota(jnp.int32, (A, 2 * A), 0)
    kj = lax.broadcasted_iota(jnp.int32, (A, 2 * A), 1) - A
    dist = qi - kj
    valid = (dist >= 0) & (dist <= A)
    bias = jnp.where(valid, 0.0, NEG_BIG).astype(F32)
    bias_first = jnp.where(first, jnp.where(valid & (kj >= 0), 0.0, NEG_BIG), bias).astype(F32)

    for p in range(dil):
        for h in range(ATT_HEADS):
            cs = slice((p * ATT_HEADS + h) * HEAD_DIM, (p * ATT_HEADS + h + 1) * HEAD_DIM)
            for j in range(QB // A):
                rs = slice(j * A, (j + 1) * A)
                if j == 0:
                    kcat = jnp.concatenate([kp_ref[:, cs], kc_ref[0:A, cs]], axis=0)
                    vcat = jnp.concatenate([vp_ref[:, cs], vc_ref[0:A, cs]], axis=0)
                    bb = bias_first
                else:
                    kcat = kc_ref[(j - 1) * A:(j + 1) * A, cs]
                    vcat = vc_ref[(j - 1) * A:(j + 1) * A, cs]
                    bb = bias
                s = _dot_nt(q_ref[rs, cs], kcat) + bb
                m = jnp.max(s, axis=-1, keepdims=True)
                e = jnp.exp(s - m)
                den = jnp.sum(e, axis=-1, keepdims=True)
                o = _dot(e.astype(BF16), vcat) / den
                o_ref[rs, cs] = o.astype(o_ref.dtype)
                lse_ref[rs, cs] = jnp.broadcast_to(m + jnp.log(den), (A, HEAD_DIM))


def _attn_group(q, k, v, dil, batch, seq):
    rows_all, W = q.shape
    A = ATT_BLOCK
    QB = ATT_SPAN // dil
    nsp = seq // ATT_SPAN
    per = QB // A
    cur = pl.BlockSpec((QB, W), lambda b, n: (b * nsp + n, 0))
    prev = pl.BlockSpec((A, W), lambda b, n: (jnp.maximum((b * nsp + n) * per - 1, b * nsp * per), 0))
    return pl.pallas_call(
        functools.partial(_attn_kernel, dil=dil),
        grid=(batch, nsp),
        in_specs=[cur, cur, prev, cur, prev],
        out_specs=[cur, cur],
        out_shape=[jax.ShapeDtypeStruct((rows_all, W), BF16),
                   jax.ShapeDtypeStruct((rows_all, W), F32)],
        compiler_params=_params("parallel", "arbitrary"),
        name=f"dilated_attn_r{dil}",
    )(q, k, k, v, v)


def _attn_mix_kernel(o0, o1, o2, l0, l1, l2, out_ref, so_ref, sl_ref):
    rows, gw = out_ref.shape
    o_refs, l_refs = (o0, o1, o2), (l0, l1, l2)
    for h in range(ATT_HEADS):
        hs = slice(h * HEAD_DIM, (h + 1) * HEAD_DIM)
        outs, lses = [], []
        for g, (_, dil) in enumerate(DIL_GROUPS):
            if dil == 1:
                outs.append(o_refs[g][:, hs].astype(F32))
                lses.append(l_refs[g][:, hs])
                continue
            n = rows // dil
            for p in range(dil):
                cs = slice(p * gw + h * HEAD_DIM, p * gw + (h + 1) * HEAD_DIM)
                so_ref[g, pl.ds(p, n, stride=dil), :] = o_refs[g][:, cs].astype(F32)
                sl_ref[g, pl.ds(p, n, stride=dil), :] = l_refs[g][:, cs]
            outs.append(so_ref[g])
            lses.append(sl_ref[g])
        m = jnp.maximum(jnp.maximum(lses[0], lses[1]), lses[2])
        ws = [jnp.exp(x - m) for x in lses]
        den = ws[0] + ws[1] + ws[2]
        out_ref[:, hs] = ((ws[0] * outs[0] + ws[1] * outs[1] + ws[2] * outs[2]) / den
                          ).astype(out_ref.dtype)


def _attn_mix(outs, lses):
    gw = ATT_HEADS * HEAD_DIM
    T = outs[0].shape[0]
    rows = PREP_ROWS
    dils = [dil for _, dil in DIL_GROUPS]
    specs = [pl.BlockSpec((rows // d, d * gw), lambda i: (i, 0)) for d in dils]
    return pl.pallas_call(
        _attn_mix_kernel,
        grid=(T // rows,),
        in_specs=specs + specs,
        out_specs=pl.BlockSpec((rows, gw), lambda i: (i, 0)),
        out_shape=jax.ShapeDtypeStruct((T, gw), BF16),
        scratch_shapes=[pltpu.VMEM((N_DIL, rows, HEAD_DIM), F32),
                        pltpu.VMEM((N_DIL, rows, HEAD_DIM), F32)],
        compiler_params=_params("parallel"),
        name="attn_mix",
    )(*outs, *lses)


def _merge_kernel(l_ref, ogla_ref, oatt_ref, gg_ref, ga_ref, wg_ref, wa_ref,
                  out_ref, wgbf_ref, wabf_ref):
    @pl.when(pl.program_id(1) == 0)
    def _():
        wgbf_ref[...] = wg_ref[...].astype(BF16)
        wabf_ref[...] = wa_ref[...].astype(BF16)

    a = _dot(ogla_ref[...], wgbf_ref[...])
    b = _dot(oatt_ref[...], wabf_ref[...])
    gg = _sigmoid(gg_ref[...].astype(F32))
    ga = _sigmoid(ga_ref[...].astype(F32))
    out_ref[...] = (gg * a + ga * b).astype(out_ref.dtype)


def _merge(o_gla, o_att, gates, l, w_gla_proj, w_attn_proj):
    T, dv = o_gla.shape
    gw = o_att.shape[1]
    D = w_gla_proj.shape[-1]
    tm, tn = MERGE_ROWS, 2 * MM_TN
    gcb = 0
    dcb = D // tn
    call, lidx = _grid_call(
        _merge_kernel, l, (D // tn, T // tm),
        [pl.BlockSpec((tm, dv), lambda j, i, l: (i, 0)),
         pl.BlockSpec((tm, gw), lambda j, i, l: (i, 0)),
         pl.BlockSpec((tm, tn), lambda j, i, l: (i, gcb + j)),
         pl.BlockSpec((tm, tn), lambda j, i, l: (i, gcb + dcb + j)),
         pl.BlockSpec((None, dv, tn), lambda j, i, l: (l[0], 0, j)),
         pl.BlockSpec((None, gw, tn), lambda j, i, l: (l[0], 0, j))],
        pl.BlockSpec((tm, tn), lambda j, i, l: (i, j)),
        jax.ShapeDtypeStruct((T, D), BF16),
        scratch=[pltpu.VMEM((dv, tn), BF16), pltpu.VMEM((gw, tn), BF16)], name="gated_merge")
    return call(lidx, o_gla, o_att, gates, gates, w_gla_proj, w_attn_proj)


def _router_kernel(l_ref, x_ref, g_ref, sh_ref, sc_ref, wr_ref, br_ref, h_ref, route_ref):
    h = _ada_norm(x_ref[...], g_ref[...], sc_ref[...], sh_ref[...])
    rows = h.shape[0]
    hb = h.astype(BF16)
    words = _pack_rows(hb.astype(F32))
    nw = len(words)
    for j in range(nw):
        h_ref[pl.ds(j, rows, stride=nw), :] = words[j]
    lg = _dot(hb, wr_ref[...].astype(BF16)) + br_ref[...]
    lane = lax.broadcasted_iota(jnp.int32, lg.shape, 1).astype(F32)
    big = float(4 * LANES)

    def first_argmax(vals):
        mx = jnp.max(vals, axis=-1, keepdims=True)
        idx = jnp.min(jnp.where(vals == mx, lane, big), axis=-1, keepdims=True)
        return mx, idx

    gl = jnp.where(lane < N_GROUPS, lg, NEG_BIG)
    gmax, gidx = first_argmax(gl)
    g_weight = 1.0 / jnp.sum(jnp.exp(gl - gmax), axis=-1, keepdims=True)
    lo = N_GROUPS + gidx * EXPERTS_PER_GROUP
    el = jnp.where((lane >= lo) & (lane < lo + EXPERTS_PER_GROUP), lg, NEG_BIG)
    m1, i1 = first_argmax(el)
    m2, i2 = first_argmax(jnp.where(lane == i1, NEG_BIG, el))
    e2 = jnp.exp(m2 - m1)
    w1 = g_weight / (1.0 + e2)
    w2 = g_weight * e2 / (1.0 + e2)
    route = jnp.where(lane == 0, i1 - N_GROUPS,
                      jnp.where(lane == 1, i2 - N_GROUPS,
                                jnp.where(lane == 2, w1, jnp.where(lane == 3, w2, 0.0))))
    route_ref[...] = route


def _router(x, l, g_all, mod3, wr_pad, br_pad, seq):
    T, D = x.shape
    rows = NORM_ROWS
    call, lidx = _grid_call(
        _router_kernel, l, (T // rows,),
        [pl.BlockSpec((rows, D), lambda i, l: (i, 0)),
         pl.BlockSpec((None, 1, D), lambda i, l: (l[0], 0, 0)),
         pl.BlockSpec((None, 1, D), _mod_spec(3, rows, seq)),
         pl.BlockSpec((None, 1, D), _mod_spec(4, rows, seq)),
         pl.BlockSpec((None, D, LANES), lambda i, l: (l[0], 0, 0)),
         pl.BlockSpec((None, 1, LANES), lambda i, l: (l[0], 0, 0))],
        [pl.BlockSpec((rows * (D // (2 * LANES)), LANES), lambda i, l: (i, 0)),
         pl.BlockSpec((rows, LANES), lambda i, l: (i, 0))],
        [jax.ShapeDtypeStruct((T * (D // (2 * LANES)), LANES), jnp.uint32),
         jax.ShapeDtypeStruct((T, LANES), F32)],
        sem=("parallel",), name="norm_router")
    return call(lidx, x, g_all, mod3, mod3, wr_pad, br_pad)


def _expert_kernel(meta_ref, src_one_ref, src_cur_ref, src_nxt_ref, dst_ref, h_ref,
                   wg_ref, wu_ref, wd_ref, ys_ref, xbuf_ref, obuf_ref, wgf_ref, wuf_ref, wdf_ref,
                   wgbf_ref, wubf_ref, wdbf_ref, gsem_ref, ssem_ref, wsem_ref):
    i = pl.program_id(0)
    n_tiles = pl.num_programs(0)
    n_used = meta_ref[1]
    slot = i % 2
    nj, tm = xbuf_ref.shape[1], xbuf_ref.shape[2]
    layer = meta_ref[0]
    expert = meta_ref[2 + i]
    next_expert = meta_ref[2 + n_tiles + i]
    wset = meta_ref[2 + 2 * n_tiles + i]

    def weight_copies(e, s):
        return [pltpu.make_async_copy(w.at[layer, e], buf.at[s], wsem_ref.at[s])
                for w, buf in ((wg_ref, wgf_ref), (wu_ref, wuf_ref), (wd_ref, wdf_ref))]

    def gather_copy(idx_ref, r, s):
        return pltpu.make_async_copy(h_ref.at[idx_ref[0, 0, r]], xbuf_ref.at[s, :, r, :],
                                     gsem_ref.at[s])

    def scatter_copy(idx_ref, r, s):
        return pltpu.make_async_copy(obuf_ref.at[s, :, r, :], ys_ref.at[idx_ref[0, 0, r]],
                                     ssem_ref.at[s])

    def gather_start(idx_ref, s):
        def body(r, c):
            gather_copy(idx_ref, r, s).start()
            return c
        lax.fori_loop(0, tm, body, 0, unroll=8)

    def gather_wait(s):
        pltpu.make_async_copy(obuf_ref.at[0], xbuf_ref.at[s], gsem_ref.at[s]).wait()

    def scatter_wait(s):
        pltpu.make_async_copy(xbuf_ref.at[0], obuf_ref.at[s], ssem_ref.at[s]).wait()

    @pl.when(i == 0)
    def _():
        for cp in weight_copies(expert, wset):
            cp.start()
        gather_start(src_cur_ref, 0)

        @pl.when(n_used > 1)
        def _():
            gather_start(src_one_ref, 1)
        obuf_ref[1] = jnp.zeros(obuf_ref.shape[1:], jnp.uint32)
        n_rows = ys_ref.shape[0]

        def fill(r, c):
            pltpu.make_async_copy(obuf_ref.at[1, :, r % tm, :], ys_ref.at[n_rows - 2 * tm + r],
                                  ssem_ref.at[1]).start()
            return c
        lax.fori_loop(0, 2 * tm, fill, 0, unroll=8)
        scatter_wait(1)
        scatter_wait(1)

    @pl.when(i < n_used)
    def _():
        xslot = i % 3
        gather_wait(xslot)

        @pl.when(i >= 2)
        def _():
            scatter_wait(slot)

        @pl.when((i == 0) | (expert != meta_ref[2 + jnp.maximum(i - 1, 0)]))
        def _():
            for cp in weight_copies(expert, wset):
                cp.wait()
            wgbf_ref[...] = wgf_ref[wset].astype(BF16)
            wubf_ref[...] = wuf_ref[wset].astype(BF16)
            wdbf_ref[...] = wdf_ref[wset].astype(BF16)

            @pl.when(next_expert >= 0)
            def _():
                for cp in weight_copies(next_expert, 1 - wset):
                    cp.start()

        x = _unpack_rows([xbuf_ref[xslot, j] for j in range(nj)]).astype(BF16)
        a = _dot(x, wgbf_ref[...])
        u = _dot(x, wubf_ref[...])
        hid = (a * _sigmoid(a)) * u
        y = _dot(hid.astype(BF16), wdbf_ref[...])
        for j, word in enumerate(_pack_rows(y.astype(BF16).astype(F32))):
            obuf_ref[slot, j] = word
        for r in range(tm):
            scatter_copy(dst_ref, r, slot).start()

        @pl.when(i + 2 < n_used)
        def _():
            nslot = (i + 2) % 3
            for r in range(tm):
                gather_copy(src_nxt_ref, r, nslot).start()

        @pl.when(i == n_used - 1)
        def _():
            @pl.when(i >= 1)
            def _():
                scatter_wait(1 - slot)
            scatter_wait(slot)


def _experts(h3, src, dst, meta, w_gate, w_up, w_down, n_out):
    T, nj, _ = h3.shape
    D = 2 * nj * LANES
    F = w_gate.shape[-1]
    n_tiles, _, tm = src.shape
    idx_spec = lambda f: pl.BlockSpec((1, 1, tm), lambda i, m: (f(i), 0, 0), memory_space=pltpu.SMEM)
    hbm = pl.BlockSpec(memory_space=pl.ANY)
    return pl.pallas_call(
        _expert_kernel,
        grid_spec=pltpu.PrefetchScalarGridSpec(
            num_scalar_prefetch=1, grid=(n_tiles,),
            in_specs=[idx_spec(lambda i: min(1, n_tiles - 1)), idx_spec(lambda i: i),
                      idx_spec(lambda i: jnp.minimum(i + 2, n_tiles - 1)), idx_spec(lambda i: i),
                      hbm, hbm, hbm, hbm],
            out_specs=hbm,
            scratch_shapes=[pltpu.VMEM((3, nj, tm, LANES), jnp.uint32),
                            pltpu.VMEM((2, nj, tm, LANES), jnp.uint32),
                            pltpu.VMEM((2, D, F), F32), pltpu.VMEM((2, D, F), F32),
                            pltpu.VMEM((2, F, D), F32),
                            pltpu.VMEM((D, F), BF16), pltpu.VMEM((D, F), BF16),
                            pltpu.VMEM((F, D), BF16),
                            pltpu.SemaphoreType.DMA((3,)), pltpu.SemaphoreType.DMA((2,)),
                            pltpu.SemaphoreType.DMA((2,))]),
        out_shape=jax.ShapeDtypeStruct((n_out, nj, LANES), jnp.uint32),
        compiler_params=_params("arbitrary"),
        name="grouped_experts",
    )(meta, src, src, src, dst, h3, w_gate, w_up, w_down)


def _combine_kernel(x_ref, gt_ref, route_ref, y0_ref, y1_ref, g_ref, sh_ref, sc_ref, o_ref, h_ref):
    rows, d = x_ref.shape
    nw = d // (2 * LANES)
    w1 = jnp.broadcast_to(route_ref[:, 2:3], (rows, LANES))
    w2 = jnp.broadcast_to(route_ref[:, 3:4], (rows, LANES))
    for j in range(nw):
        u0 = y0_ref[pl.ds(j, rows, stride=nw), :]
        u1 = y1_ref[pl.ds(j, rows, stride=nw), :]
        for blk, half in ((j, lambda u: u << 16), (nw + j, lambda u: u & jnp.uint32(0xFFFF0000))):
            js = slice(blk * LANES, (blk + 1) * LANES)
            y = (w1 * lax.bitcast_convert_type(half(u0), F32)
                 + w2 * lax.bitcast_convert_type(half(u1), F32))
            o_ref[:, js] = x_ref[:, js] + gt_ref[:, js] * y
    h_ref[...] = _ada_norm(o_ref[...], g_ref[...], sc_ref[...], sh_ref[...]).astype(h_ref.dtype)


def _combine(x, ys, route, mod3, g_next, mod3_next, seq):
    T, D = x.shape
    rows = COMBINE_ROWS
    nblk = T // rows
    nj = D // (2 * LANES)
    batch_of = lambda i: i * rows // seq
    return pl.pallas_call(
        _combine_kernel,
        grid=(nblk,),
        in_specs=[pl.BlockSpec((rows, D), lambda i: (i, 0)),
                  pl.BlockSpec((None, 1, D), lambda i: (batch_of(i), 0, 5)),
                  pl.BlockSpec((rows, LANES), lambda i: (i, 0)),
                  pl.BlockSpec((rows * nj, LANES), lambda i: (i, 0)),
                  pl.BlockSpec((rows * nj, LANES), lambda i: (nblk + i, 0)),
                  pl.BlockSpec((1, D), lambda i: (0, 0)),
                  pl.BlockSpec((None, 1, D), lambda i: (batch_of(i), 0, 0)),
                  pl.BlockSpec((None, 1, D), lambda i: (batch_of(i), 0, 1))],
        out_specs=[pl.BlockSpec((rows, D), lambda i: (i, 0)),
                   pl.BlockSpec((rows, D), lambda i: (i, 0))],
        out_shape=[jax.ShapeDtypeStruct((T, D), F32), jax.ShapeDtypeStruct((T, D), BF16)],
        compiler_params=_params("parallel"),
        name="moe_combine",
    )(x, mod3, route, ys, ys, g_next, mod3_next, mod3_next)


def _invert_kernel(pos_ref, asg_ref, fill_ref, sem_ref):
    i = pl.program_id(0)
    blk = pos_ref.shape[0]

    @pl.when(i == 0)
    def _():
        fill_ref[...] = jnp.full(fill_ref.shape, -1, jnp.int32)
        cp = pltpu.make_async_copy(fill_ref, asg_ref, sem_ref)
        cp.start()
        cp.wait()

    base = i * blk

    def body(a, c):
        asg_ref[pos_ref[a]] = base + a
        return c
    lax.fori_loop(0, blk, body, 0, unroll=8)


def _invert_slots(pos, n_slots):
    n, = pos.shape
    blk = INVERT_BLOCK
    return pl.pallas_call(
        _invert_kernel,
        grid=(n // blk,),
        in_specs=[pl.BlockSpec((blk,), lambda i: (i,), memory_space=pltpu.SMEM)],
        out_specs=pl.BlockSpec((n_slots,), lambda i: (0,), memory_space=pltpu.SMEM),
        out_shape=jax.ShapeDtypeStruct((n_slots,), jnp.int32),
        scratch_shapes=[pltpu.VMEM((n_slots,), jnp.int32), pltpu.SemaphoreType.DMA(())],
        compiler_params=_params("arbitrary"),
        name="invert_slots",
    )(pos)


def _dispatch_plan(route, l):
    T = route.shape[0]
    tm = EXPERT_TM
    n_tiles = (2 * T) // tm + N_EXPERTS
    P = n_tiles * tm
    e = route[:, 0:2].astype(jnp.int32).T.reshape(-1)
    onehot = (e[:, None] == jnp.arange(N_EXPERTS, dtype=jnp.int32)[None, :]).astype(jnp.int32)
    csum = jnp.cumsum(onehot, axis=0)
    counts = csum[-1]
    tiles_per = (counts + tm - 1) // tm
    tile_end = jnp.cumsum(tiles_per)
    off = (tile_end - tiles_per) * tm
    pos = jnp.sum((csum - onehot + off[None, :]) * onehot, axis=1)
    asg = _invert_slots(pos.astype(jnp.int32), P)
    slot = jnp.arange(P, dtype=jnp.int32)
    spare = 2 * T + ((slot // tm) % 2) * tm + slot % tm
    src = jnp.where(asg >= 0, asg % T, 0).reshape(n_tiles, 1, tm)
    dst = jnp.where(asg >= 0, asg, spare).reshape(n_tiles, 1, tm)
    tile_ids = jnp.arange(n_tiles, dtype=jnp.int32)
    tile_expert = jnp.minimum(jnp.sum(tile_ids[:, None] >= tile_end[None, :], axis=1),
                              N_EXPERTS - 1).astype(jnp.int32)
    ids = jnp.arange(N_EXPERTS, dtype=jnp.int32)
    used = counts > 0
    later = lax.cummin(jnp.where(used, ids, N_EXPERTS)[::-1])[::-1]
    nxt = jnp.concatenate([later[1:], jnp.full((1,), N_EXPERTS, jnp.int32)])
    nxt = jnp.where(nxt < N_EXPERTS, nxt, -1).astype(jnp.int32)
    wset = ((jnp.cumsum(used.astype(jnp.int32)) - 1) % 2).astype(jnp.int32)
    of_tile = tile_expert[:, None] == ids[None, :]
    nxt_tile = jnp.sum(jnp.where(of_tile, nxt[None, :], 0), axis=1)
    wset_tile = jnp.sum(jnp.where(of_tile, wset[None, :], 0), axis=1)
    meta = jnp.concatenate([jnp.stack([l, tile_end[-1].astype(jnp.int32)]), tile_expert,
                            nxt_tile, wset_tile])
    return src, dst, meta


def kernel(x, c, positions, w_ada, b_ada, g_norm_mix, g_norm_ffn, w_in, w_fg, b_fg, g_gla_out,
           g_q, g_k, w_gla_proj, w_attn_proj, w_out, w_route_group, b_route_group,
           w_route_expert, b_route_expert, w_exp_gate, w_exp_up, w_exp_down):
    B, S, D = x.shape
    L = w_ada.shape[0]
    T = B * S
    dk_all = w_fg.shape[-1]
    dv_all = g_gla_out.shape[-1]
    att_w = N_DIL * ATT_HEADS * HEAD_DIM
    gla_cols = 2 * dk_all + 2 * dv_all

    mod = _modulation(c, w_ada, b_ada)
    cos2, sin2 = _rope_tables(positions)

    wfg_pad = jnp.pad(w_fg, ((0, 0), (0, LANES - GLA_RANK), (0, 0)))
    n_route = N_GROUPS + N_EXPERTS
    wr_pad = jnp.pad(jnp.concatenate([w_route_group, w_route_expert], axis=-1),
                     ((0, 0), (0, 0), (0, LANES - n_route)))
    br_pad = jnp.pad(jnp.concatenate([b_route_group, b_route_expert], axis=-1),
                     ((0, 0), (0, LANES - n_route))).reshape(L, 1, LANES)
    g_mix = g_norm_mix.reshape(L, 1, D)
    g_ffn = g_norm_ffn.reshape(L, 1, D)
    b_fg3 = b_fg.reshape(L, 1, dk_all)
    g_gla3 = g_gla_out.reshape(L, 1, dv_all)
    g_q3 = g_q.reshape(L, 1, HEAD_DIM)
    g_k3 = g_k.reshape(L, 1, HEAD_DIM)

    def residual(acc, xres, gt):
        return xres + gt * acc

    def mod_of(l):
        return lax.dynamic_index_in_dim(mod, l, 0, keepdims=False).reshape(8, 1, N_MOD * D)

    def layer(l, carry):
        xt, h = carry
        l = jnp.asarray(l, jnp.int32)
        mod3 = mod_of(l)
        proj1 = _matmul(h, w_in, l, col0=0, n=gla_cols, out_dtype=BF16, tn=2 * MM_TN,
                        tm=2 * MM_TM, name="proj_gla")
        fg = _matmul(h, w_in, l, col0=gla_cols, n=LANES, tn=LANES, out_dtype=F32, name="proj_fg")
        proj2 = _matmul(h, w_in, l, col0=gla_cols, shift=GLA_RANK, n=3 * att_w,
                        out_dtype=BF16, tm=2 * MM_TM, name="proj_att")
        gates = _matmul(h, w_in, l, col0=gla_cols + 3 * att_w, shift=GLA_RANK, n=2 * D,
                        out_dtype=BF16, tm=2 * MM_TM, name="proj_gates")
        o_gla = _gla(proj1, fg, l, wfg_pad, b_fg3, g_gla3, B, S)
        prep = _qkprep(proj2, l, cos2, sin2, g_q3, g_k3)
        outs, lses = [], []
        for grp, (_, dil) in enumerate(DIL_GROUPS):
            o, lse = _attn_group(prep[grp], prep[N_DIL + grp], prep[2 * N_DIL + grp], dil, B, S)
            outs.append(o)
            lses.append(lse)
        o_att = _attn_mix(outs, lses)
        merged = _merge(o_gla, o_att, gates, l, w_gla_proj, w_attn_proj)
        tn_out = 2 * MM_TN
        gt_spec = pl.BlockSpec((None, 1, tn_out),
                               lambda j, i, l: (i * MM_TM // S, 0, 2 * (D // tn_out) + j))
        xt = _matmul(merged, w_out, l, col0=0, n=D, out_dtype=F32, tn=tn_out, epilogue=residual,
                     extras=(xt, mod3),
                     extra_specs=(pl.BlockSpec((MM_TM, tn_out), lambda j, i, l: (i, j)), gt_spec),
                     name="out_proj")
        h2, route = _router(xt, l, g_ffn, mod3, wr_pad, br_pad, S)
        src, dst, meta = _dispatch_plan(route, l)
        nj = D // (2 * LANES)
        n_out = 2 * T + 2 * EXPERT_TM
        ys = _experts(h2.reshape(T, nj, LANES), src, dst, meta, w_exp_gate, w_exp_up, w_exp_down,
                      n_out)
        l_next = jnp.minimum(l + 1, L - 1)
        g_next = lax.dynamic_index_in_dim(g_mix, l_next, 0, keepdims=False)
        return tuple(_combine(xt, ys.reshape(n_out * nj, LANES), route, mod3, g_next,
                              mod_of(l_next), S))

    x0 = x.reshape(T, D)
    zero = jnp.int32(0)
    h0 = _norm(x0, zero, g_mix, mod_of(zero), 0, S)
    xt, _ = lax.fori_loop(0, L, layer, (x0, h0))
    return xt.reshape(B, S, D)
```

```python
import functools

import jax
import jax.numpy as jnp
from jax import lax
from jax.experimental import pallas as pl
from jax.experimental.pallas import tpu as pltpu

F32 = jnp.float32
BF16 = jnp.bfloat16

GLA_HEADS = 4
GLA_RANK = 16
GLA_TAU = 16.0
GLA_CHUNK = 64
DIL_GROUPS = ((128, 1), (512, 4), (2048, 16))
N_DIL = 3
ATT_HEADS = 4
HEAD_DIM = 128
ATT_BLOCK = 128
ROPE_THETA = 10000.0
N_GROUPS = 4
EXPERTS_PER_GROUP = 8
N_EXPERTS = N_GROUPS * EXPERTS_PER_GROUP
N_MOD = 6
EPS = 1e-6

LANES = 128
VMEM_LIMIT = 56 * 1024 * 1024
NEG_BIG = -1e30

MOD_TN = 2048
NORM_ROWS = 512
MM_TM = 1024
MM_TN = 512
GLA_ROWS = 512
PREP_ROWS = 512
ATT_SPAN = 2048
MERGE_ROWS = 1024
EXPERT_TM = 256
COMBINE_ROWS = 512
INVERT_BLOCK = 4096


def _params(*sem):
    return pltpu.CompilerParams(dimension_semantics=sem, vmem_limit_bytes=VMEM_LIMIT)


def _dot(a, b):
    return jnp.dot(a, b, preferred_element_type=F32)


def _dot_nt(a, b):
    return lax.dot_general(a, b, (((1,), (1,)), ((), ())), preferred_element_type=F32)


def _dot_tn(a, b):
    return lax.dot_general(a, b, (((0,), (0,)), ((), ())), preferred_element_type=F32)


def _split_bf16(x):
    hi = x.astype(BF16)
    lo = (x - hi.astype(F32)).astype(BF16)
    return hi, lo


def _sigmoid(x):
    return 1.0 / (1.0 + jnp.exp(-x))


def _pack_rows(xb):
    bits = lax.bitcast_convert_type(xb, jnp.uint32)
    nw = xb.shape[1] // (2 * LANES)
    out = []
    for j in range(nw):
        lo = bits[:, j * LANES:(j + 1) * LANES] >> 16
        hi = bits[:, (nw + j) * LANES:(nw + j + 1) * LANES] & jnp.uint32(0xFFFF0000)
        out.append(lo | hi)
    return out


def _unpack_rows(words):
    lo = [lax.bitcast_convert_type(w << 16, F32) for w in words]
    hi = [lax.bitcast_convert_type(w & jnp.uint32(0xFFFF0000), F32) for w in words]
    return jnp.concatenate(lo + hi, axis=1)


def _grid_call(kernel, l, grid, in_specs, out_specs, out_shape, scratch=(), sem=None, name=None):
    sem = sem or ("arbitrary",) * len(grid)
    return pl.pallas_call(
        kernel,
        grid_spec=pltpu.PrefetchScalarGridSpec(
            num_scalar_prefetch=1, grid=grid, in_specs=in_specs, out_specs=out_specs,
            scratch_shapes=list(scratch)),
        out_shape=out_shape,
        compiler_params=_params(*sem),
        name=name,
    ), jnp.reshape(l, (1,)).astype(jnp.int32)


def _mod_kernel(c_ref, w_ref, b_ref, o_ref, act_ref):
    n_batch = c_ref.shape[0]
    tn = w_ref.shape[1]

    @pl.when((pl.program_id(0) == 0) & (pl.program_id(1) == 0))
    def _():
        c = c_ref[...]
        act_ref[...] = c * _sigmoid(c)

    rows = [[] for _ in range(n_batch)]
    for jb in range(tn // LANES):
        w = w_ref[:, jb * LANES:(jb + 1) * LANES]
        for b in range(n_batch):
            rows[b].append(jnp.sum(w * act_ref[b], axis=0, keepdims=True))
    out = jnp.concatenate([jnp.concatenate(r, axis=1) for r in rows], axis=0) + b_ref[...]
    o_ref[0:n_batch, :] = out
    o_ref[n_batch:, :] = jnp.zeros((o_ref.shape[0] - n_batch, tn), F32)


def _modulation(c, w_ada, b_ada):
    L, D, N = w_ada.shape
    B = c.shape[0]
    rows = 8
    assert B <= rows
    c_rep = jnp.broadcast_to(c[:, :, None], (B, D, LANES))
    tn = MOD_TN
    return pl.pallas_call(
        _mod_kernel,
        grid=(L, N // tn),
        in_specs=[pl.BlockSpec((B, D, LANES), lambda l, j: (0, 0, 0)),
                  pl.BlockSpec((None, D, tn), lambda l, j: (l, 0, j)),
                  pl.BlockSpec((None, 1, tn), lambda l, j: (l, 0, j))],
        out_specs=pl.BlockSpec((None, rows, tn), lambda l, j: (l, 0, j)),
        out_shape=jax.ShapeDtypeStruct((L, rows, N), F32),
        scratch_shapes=[pltpu.VMEM((B, D, LANES), F32)],
        compiler_params=_params("arbitrary", "arbitrary"),
        name="adaln_mod",
    )(c_rep, w_ada, b_ada.reshape(L, 1, N))


def _rope_kernel(pos_ref, freq_ref, cos_ref, sin_ref):
    ang = pos_ref[...].astype(F32) * freq_ref[...]
    lane = lax.broadcasted_iota(jnp.int32, ang.shape, 1)
    cos_ref[...] = jnp.cos(ang)
    sin_ref[...] = jnp.where(lane < HEAD_DIM // 2, -jnp.sin(ang), jnp.sin(ang))


def _rope_tables(positions):
    T = positions.size
    inv_freq = ROPE_THETA ** (-jnp.arange(0, HEAD_DIM, 2, dtype=F32) / HEAD_DIM)
    freq2 = jnp.concatenate([inv_freq, inv_freq]).reshape(1, HEAD_DIM)
    rows = 2048
    return pl.pallas_call(
        _rope_kernel,
        grid=(T // rows,),
        in_specs=[pl.BlockSpec((rows, 1), lambda i: (i, 0)),
                  pl.BlockSpec((1, HEAD_DIM), lambda i: (0, 0))],
        out_specs=[pl.BlockSpec((rows, HEAD_DIM), lambda i: (i, 0))] * 2,
        out_shape=[jax.ShapeDtypeStruct((T, HEAD_DIM), F32)] * 2,
        compiler_params=_params("parallel"),
        name="rope_tables",
    )(positions.reshape(T, 1), freq2)


def _ada_norm(x, g, scale, shift):
    y = x * lax.rsqrt(jnp.mean(x * x, axis=-1, keepdims=True) + EPS)
    return y * g * (1.0 + scale) + shift


def _norm_kernel(l_ref, x_ref, g_ref, sh_ref, sc_ref, o_ref, xcopy_ref):
    x = x_ref[...]
    o_ref[...] = _ada_norm(x, g_ref[...], sc_ref[...], sh_ref[...]).astype(o_ref.dtype)
    xcopy_ref[...] = x


def _mod_spec(which, rows, seq):
    return lambda i, l: (i * rows // seq, 0, which)


def _norm(x, l, g_all, mod3, which_shift, seq):
    T, D = x.shape
    rows = NORM_ROWS
    call, lidx = _grid_call(
        _norm_kernel, l, (T // rows,),
        [pl.BlockSpec((rows, D), lambda i, l: (i, 0)),
         pl.BlockSpec((None, 1, D), lambda i, l: (l[0], 0, 0)),
         pl.BlockSpec((None, 1, D), _mod_spec(which_shift, rows, seq)),
         pl.BlockSpec((None, 1, D), _mod_spec(which_shift + 1, rows, seq))],
        [pl.BlockSpec((rows, D), lambda i, l: (i, 0))] * 2,
        [jax.ShapeDtypeStruct((T, D), BF16), jax.ShapeDtypeStruct((T, D), x.dtype)],
        sem=("parallel",), name="ada_norm")
    return call(lidx, x, g_all, mod3, mod3)


def _mm_kernel(l_ref, a_ref, w_ref, *rest, epilogue, n_extra, shift):
    n_w = 1 if shift else 0
    extra = rest[n_w:n_w + n_extra]
    o_ref = rest[n_w + n_extra]
    wbf_ref = rest[n_w + n_extra + 1]
    tn = o_ref.shape[1]

    @pl.when(pl.program_id(1) == 0)
    def _():
        if shift:
            wide = jnp.concatenate([w_ref[...], rest[0][...]], axis=1)
            wbf_ref[...] = wide[:, shift:shift + tn].astype(BF16)
        else:
            wbf_ref[...] = w_ref[...].astype(BF16)

    acc = _dot(a_ref[...], wbf_ref[...])
    if epilogue is not None:
        acc = epilogue(acc, *[e[...] for e in extra])
    o_ref[...] = acc.astype(o_ref.dtype)


def _matmul(a, w, l, *, col0, n, out_dtype, tn=MM_TN, tm=MM_TM, shift=0, epilogue=None,
            extras=(), extra_specs=(), name="matmul"):
    M, K = a.shape
    assert col0 % tn == 0 and n % tn == 0 and M % tm == 0 and 0 <= shift < LANES
    cb0 = col0 // tn
    kern = functools.partial(_mm_kernel, epilogue=epilogue, n_extra=len(extras), shift=shift)
    w_specs = [pl.BlockSpec((None, K, tn), lambda j, i, l: (l[0], 0, cb0 + j))]
    if shift:
        per = tn // LANES
        w_specs.append(pl.BlockSpec((None, K, LANES), lambda j, i, l: (l[0], 0, (cb0 + j + 1) * per)))
    call, lidx = _grid_call(
        kern, l, (n // tn, M // tm),
        [pl.BlockSpec((tm, K), lambda j, i, l: (i, 0))] + w_specs + list(extra_specs),
        pl.BlockSpec((tm, tn), lambda j, i, l: (i, j)),
        jax.ShapeDtypeStruct((M, n), out_dtype),
        scratch=[pltpu.VMEM((K, tn), BF16)], name=name)
    return call(lidx, a, *([w] * len(w_specs)), *extras)


def _gla_kernel(l_ref, q_ref, k_ref, v_ref, r_ref, fg_ref, wfg_ref, bfg_ref, g_ref, o_ref,
                state_ref, *, head_k, head_v):
    rows = q_ref.shape[0]
    C = GLA_CHUNK

    @pl.when(pl.program_id(1) == 0)
    def _():
        state_ref[...] = jnp.zeros_like(state_ref)

    fh, fl = _split_bf16(fg_ref[...])
    wh, wl = _split_bf16(wfg_ref[...])
    z = _dot(fh, wh) + _dot(fl, wh) + _dot(fh, wl) + bfg_ref[...]
    log_a = (jnp.minimum(z, 0.0) - jnp.log(1.0 + jnp.exp(-jnp.abs(z)))) * (1.0 / GLA_TAU)

    ri = lax.broadcasted_iota(jnp.int32, (C, C), 0)
    ci = lax.broadcasted_iota(jnp.int32, (C, C), 1)
    causal = ri >= ci
    tri = causal.astype(BF16)
    scale = head_k ** -0.5

    for c in range(rows // C):
        rs = slice(c * C, (c + 1) * C)
        lh, ll = _split_bf16(log_a[rs])
        b = _dot(tri, lh) + _dot(tri, ll)
        b_last = b[C - 1:C]
        q = q_ref[rs, :].astype(F32) * scale
        k = k_ref[rs, :].astype(F32)
        q_dec = (q * jnp.exp(b)).astype(BF16)
        k_inv = (k * jnp.exp(-b)).astype(BF16)
        k_upd = (k * jnp.exp(b_last - b)).astype(BF16)
        decay = jnp.exp(b_last)
        for h in range(GLA_HEADS):
            ks = slice(h * head_k, (h + 1) * head_k)
            vs = slice(h * head_v, (h + 1) * head_v)
            v = v_ref[rs, vs]
            att = jnp.where(causal, _dot_nt(q_dec[:, ks], k_inv[:, ks]), 0.0).astype(BF16)
            st = state_ref[h]
            o = _dot(att, v) + _dot_nt(q_dec[:, ks], st.astype(BF16))
            state_ref[h] = st * decay[:, ks] + _dot_tn(v, k_upd[:, ks])
            o = o * lax.rsqrt(jnp.mean(o * o, axis=-1, keepdims=True) + EPS) * g_ref[:, vs]
            r = r_ref[rs, vs].astype(F32)
            o_ref[rs, vs] = (o * (r * _sigmoid(r))).astype(o_ref.dtype)


def _gla(proj, fg, l, wfg_pad, b_fg, g_gla_out, batch, seq):
    T = proj.shape[0]
    dk_all = wfg_pad.shape[-1]
    dv_all = g_gla_out.shape[-1]
    rows = GLA_ROWS
    nblk = seq // rows
    kern = functools.partial(_gla_kernel, head_k=dk_all // GLA_HEADS, head_v=dv_all // GLA_HEADS)
    row = lambda b, n, l: b * nblk + n
    call, lidx = _grid_call(
        kern, l, (batch, nblk),
        [pl.BlockSpec((rows, dk_all), lambda b, n, l: (row(b, n, l), 0)),
         pl.BlockSpec((rows, dk_all), lambda b, n, l: (row(b, n, l), 1)),
         pl.BlockSpec((rows, dv_all), lambda b, n, l: (row(b, n, l), 1)),
         pl.BlockSpec((rows, dv_all), lambda b, n, l: (row(b, n, l), 2)),
         pl.BlockSpec((rows, LANES), lambda b, n, l: (row(b, n, l), 0)),
         pl.BlockSpec((None, LANES, dk_all), lambda b, n, l: (l[0], 0, 0)),
         pl.BlockSpec((None, 1, dk_all), lambda b, n, l: (l[0], 0, 0)),
         pl.BlockSpec((None, 1, dv_all), lambda b, n, l: (l[0], 0, 0))],
        pl.BlockSpec((rows, dv_all), lambda b, n, l: (row(b, n, l), 0)),
        jax.ShapeDtypeStruct((T, dv_all), BF16),
        scratch=[pltpu.VMEM((GLA_HEADS, dv_all // GLA_HEADS, dk_all // GLA_HEADS), F32)],
        sem=("parallel", "arbitrary"), name="gla")
    return call(lidx, proj, proj, proj, proj, fg, wfg_pad, b_fg, g_gla_out)


def _qkprep_kernel(l_ref, q_ref, k_ref, v_ref, cos_ref, sin_ref, gq_ref, gk_ref, *refs):
    out_refs, scr_ref = refs[:-1], refs[-1]
    cos = cos_ref[...]
    sin = sin_ref[...]
    gw = ATT_HEADS * HEAD_DIM
    rows = q_ref.shape[0]

    def tables(g_ref, scale):
        g = jnp.broadcast_to(g_ref[...], cos.shape)
        return g * cos * scale, pltpu.roll(g, HEAD_DIM // 2, 1) * sin * scale

    def prep(ref, tab):
        x = ref.astype(F32)
        n = x * lax.rsqrt(jnp.mean(x * x, axis=-1, keepdims=True) + EPS)
        return n * tab[0] + pltpu.roll(n, HEAD_DIM // 2, 1) * tab[1]

    def emit(dst_ref, dil):
        n = rows // dil
        for p in range(dil):
            for h in range(ATT_HEADS):
                cs = slice(p * gw + h * HEAD_DIM, p * gw + (h + 1) * HEAD_DIM)
                dst_ref[:, cs] = scr_ref[h, pl.ds(p, n, stride=dil), :].astype(BF16)

    tab_q = tables(gq_ref, HEAD_DIM ** -0.5)
    tab_k = tables(gk_ref, 1.0)
    for grp, (_, dil) in enumerate(DIL_GROUPS):
        for ref, tab, out in ((q_ref, tab_q, out_refs[grp]), (k_ref, tab_k, out_refs[N_DIL + grp])):
            for h in range(ATT_HEADS):
                src = slice(grp * gw + h * HEAD_DIM, grp * gw + (h + 1) * HEAD_DIM)
                y = prep(ref[:, src], tab)
                if dil == 1:
                    out[:, h * HEAD_DIM:(h + 1) * HEAD_DIM] = y.astype(BF16)
                else:
                    scr_ref[h] = y
            if dil > 1:
                emit(out, dil)
        if dil == 1:
            out_refs[2 * N_DIL + grp][...] = v_ref[:, grp * gw:(grp + 1) * gw]
        else:
            for h in range(ATT_HEADS):
                src = slice(grp * gw + h * HEAD_DIM, grp * gw + (h + 1) * HEAD_DIM)
                scr_ref[h] = v_ref[:, src].astype(F32)
            emit(out_refs[2 * N_DIL + grp], dil)


def _qkprep(proj2, l, cos2, sin2, g_q, g_k):
    T = proj2.shape[0]
    rows = PREP_ROWS
    width = N_DIL * ATT_HEADS * HEAD_DIM
    gw = ATT_HEADS * HEAD_DIM
    dils = [dil for _, dil in DIL_GROUPS] * 3
    call, lidx = _grid_call(
        _qkprep_kernel, l, (T // rows,),
        [pl.BlockSpec((rows, width), lambda i, l: (i, 0)),
         pl.BlockSpec((rows, width), lambda i, l: (i, 1)),
         pl.BlockSpec((rows, width), lambda i, l: (i, 2)),
         pl.BlockSpec((rows, HEAD_DIM), lambda i, l: (i, 0)),
         pl.BlockSpec((rows, HEAD_DIM), lambda i, l: (i, 0)),
         pl.BlockSpec((None, 1, HEAD_DIM), lambda i, l: (l[0], 0, 0)),
         pl.BlockSpec((None, 1, HEAD_DIM), lambda i, l: (l[0], 0, 0))],
        [pl.BlockSpec((rows // d, d * gw), lambda i, l: (i, 0)) for d in dils],
        [jax.ShapeDtypeStruct((T // d, d * gw), BF16) for d in dils],
        scratch=[pltpu.VMEM((ATT_HEADS, rows, HEAD_DIM), F32)],
        sem=("parallel",), name="qk_prep")
    return call(lidx, proj2, proj2, proj2, cos2, sin2, g_q, g_k)


def _attn_kernel(q_ref, kc_ref, kp_ref, vc_ref, vp_ref, o_ref, lse_ref, *, dil):
    first = pl.program_id(1) == 0
    QB = q_ref.shape[0]
    A = ATT_BLOCK
    qi = lax.broadcasted_iota(jnp.int32, (A, 2 * A), 0)
    kj = lax.broadcasted_iota(jnp.int32, (A, 2 * A), 1) - A
    dist = qi - kj
    valid = (dist >= 0) & (dist <= A)
    bias = jnp.where(valid, 0.0, NEG_BIG).astype(F32)
    bias_first = jnp.where(first, jnp.where(valid & (kj >= 0), 0.0, NEG_BIG), bias).astype(F32)

    for p in range(dil):
        for h in range(ATT_HEADS):
            cs = slice((p * ATT_HEADS + h) * HEAD_DIM, (p * ATT_HEADS + h + 1) * HEAD_DIM)
            for j in range(QB // A):
                rs = slice(j * A, (j + 1) * A)
                if j == 0:
                    kcat = jnp.concatenate([kp_ref[:, cs], kc_ref[0:A, cs]], axis=0)
                    vcat = jnp.concatenate([vp_ref[:, cs], vc_ref[0:A, cs]], axis=0)
                    bb = bias_first
                else:
                    kcat = kc_ref[(j - 1) * A:(j + 1) * A, cs]
                    vcat = vc_ref[(j - 1) * A:(j + 1) * A, cs]
                    bb = bias
                s = _dot_nt(q_ref[rs, cs], kcat) + bb
                m = jnp.max(s, axis=-1, keepdims=True)
                e = jnp.exp(s - m)
                den = jnp.sum(e, axis=-1, keepdims=True)
                o = _dot(e.astype(BF16), vcat) / den
                o_ref[rs, cs] = o.astype(o_ref.dtype)
                lse_ref[rs, cs] = jnp.broadcast_to(m + jnp.log(den), (A, HEAD_DIM))


def _attn_group(q, k, v, dil, batch, seq):
    rows_all, W = q.shape
    A = ATT_BLOCK
    QB = ATT_SPAN // dil
    nsp = seq // ATT_SPAN
    per = QB // A
    cur = pl.BlockSpec((QB, W), lambda b, n: (b * nsp + n, 0))
    prev = pl.BlockSpec((A, W), lambda b, n: (jnp.maximum((b * nsp + n) * per - 1, b * nsp * per), 0))
    return pl.pallas_call(
        functools.partial(_attn_kernel, dil=dil),
        grid=(batch, nsp),
        in_specs=[cur, cur, prev, cur, prev],
        out_specs=[cur, cur],
        out_shape=[jax.ShapeDtypeStruct((rows_all, W), BF16),
                   jax.ShapeDtypeStruct((rows_all, W), F32)],
        compiler_params=_params("parallel", "arbitrary"),
        name=f"dilated_attn_r{dil}",
    )(q, k, k, v, v)


def _attn_mix_kernel(o0, o1, o2, l0, l1, l2, out_ref, so_ref, sl_ref):
    rows, gw = out_ref.shape
    o_refs, l_refs = (o0, o1, o2), (l0, l1, l2)
    for h in range(ATT_HEADS):
        hs = slice(h * HEAD_DIM, (h + 1) * HEAD_DIM)
        outs, lses = [], []
        for g, (_, dil) in enumerate(DIL_GROUPS):
            if dil == 1:
                outs.append(o_refs[g][:, hs].astype(F32))
                lses.append(l_refs[g][:, hs])
                continue
            n = rows // dil
            for p in range(dil):
                cs = slice(p * gw + h * HEAD_DIM, p * gw + (h + 1) * HEAD_DIM)
                so_ref[g, pl.ds(p, n, stride=dil), :] = o_refs[g][:, cs].astype(F32)
                sl_ref[g, pl.ds(p, n, stride=dil), :] = l_refs[g][:, cs]
            outs.append(so_ref[g])
            lses.append(sl_ref[g])
        m = jnp.maximum(jnp.maximum(lses[0], lses[1]), lses[2])
        ws = [jnp.exp(x - m) for x in lses]
        den = ws[0] + ws[1] + ws[2]
        out_ref[:, hs] = ((ws[0] * outs[0] + ws[1] * outs[1] + ws[2] * outs[2]) / den
                          ).astype(out_ref.dtype)


def _attn_mix(outs, lses):
    gw = ATT_HEADS * HEAD_DIM
    T = outs[0].shape[0]
    rows = PREP_ROWS
    dils = [dil for _, dil in DIL_GROUPS]
    specs = [pl.BlockSpec((rows // d, d * gw), lambda i: (i, 0)) for d in dils]
    return pl.pallas_call(
        _attn_mix_kernel,
        grid=(T // rows,),
        in_specs=specs + specs,
        out_specs=pl.BlockSpec((rows, gw), lambda i: (i, 0)),
        out_shape=jax.ShapeDtypeStruct((T, gw), BF16),
        scratch_shapes=[pltpu.VMEM((N_DIL, rows, HEAD_DIM), F32),
                        pltpu.VMEM((N_DIL, rows, HEAD_DIM), F32)],
        compiler_params=_params("parallel"),
        name="attn_mix",
    )(*outs, *lses)


def _merge_kernel(l_ref, ogla_ref, oatt_ref, gg_ref, ga_ref, wg_ref, wa_ref,
                  out_ref, wgbf_ref, wabf_ref):
    @pl.when(pl.program_id(1) == 0)
    def _():
        wgbf_ref[...] = wg_ref[...].astype(BF16)
        wabf_ref[...] = wa_ref[...].astype(BF16)

    a = _dot(ogla_ref[...], wgbf_ref[...])
    b = _dot(oatt_ref[...], wabf_ref[...])
    gg = _sigmoid(gg_ref[...].astype(F32))
    ga = _sigmoid(ga_ref[...].astype(F32))
    out_ref[...] = (gg * a + ga * b).astype(out_ref.dtype)


def _merge(o_gla, o_att, gates, l, w_gla_proj, w_attn_proj):
    T, dv = o_gla.shape
    gw = o_att.shape[1]
    D = w_gla_proj.shape[-1]
    tm, tn = MERGE_ROWS, 2 * MM_TN
    gcb = 0
    dcb = D // tn
    call, lidx = _grid_call(
        _merge_kernel, l, (D // tn, T // tm),
        [pl.BlockSpec((tm, dv), lambda j, i, l: (i, 0)),
         pl.BlockSpec((tm, gw), lambda j, i, l: (i, 0)),
         pl.BlockSpec((tm, tn), lambda j, i, l: (i, gcb + j)),
         pl.BlockSpec((tm, tn), lambda j, i, l: (i, gcb + dcb + j)),
         pl.BlockSpec((None, dv, tn), lambda j, i, l: (l[0], 0, j)),
         pl.BlockSpec((None, gw, tn), lambda j, i, l: (l[0], 0, j))],
        pl.BlockSpec((tm, tn), lambda j, i, l: (i, j)),
        jax.ShapeDtypeStruct((T, D), BF16),
        scratch=[pltpu.VMEM((dv, tn), BF16), pltpu.VMEM((gw, tn), BF16)], name="gated_merge")
    return call(lidx, o_gla, o_att, gates, gates, w_gla_proj, w_attn_proj)


def _router_kernel(l_ref, x_ref, g_ref, sh_ref, sc_ref, wr_ref, br_ref, h_ref, route_ref):
    h = _ada_norm(x_ref[...], g_ref[...], sc_ref[...], sh_ref[...])
    rows = h.shape[0]
    hb = h.astype(BF16)
    words = _pack_rows(hb.astype(F32))
    nw = len(words)
    for j in range(nw):
        h_ref[pl.ds(j, rows, stride=nw), :] = words[j]
    lg = _dot(hb, wr_ref[...].astype(BF16)) + br_ref[...]
    lane = lax.broadcasted_iota(jnp.int32, lg.shape, 1).astype(F32)
    big = float(4 * LANES)

    def first_argmax(vals):
        mx = jnp.max(vals, axis=-1, keepdims=True)
        idx = jnp.min(jnp.where(vals == mx, lane, big), axis=-1, keepdims=True)
        return mx, idx

    gl = jnp.where(lane < N_GROUPS, lg, NEG_BIG)
    gmax, gidx = first_argmax(gl)
    g_weight = 1.0 / jnp.sum(jnp.exp(gl - gmax), axis=-1, keepdims=True)
    lo = N_GROUPS + gidx * EXPERTS_PER_GROUP
    el = jnp.where((lane >= lo) & (lane < lo + EXPERTS_PER_GROUP), lg, NEG_BIG)
    m1, i1 = first_argmax(el)
    m2, i2 = first_argmax(jnp.where(lane == i1, NEG_BIG, el))
    e2 = jnp.exp(m2 - m1)
    w1 = g_weight / (1.0 + e2)
    w2 = g_weight * e2 / (1.0 + e2)
    route = jnp.where(lane == 0, i1 - N_GROUPS,
                      jnp.where(lane == 1, i2 - N_GROUPS,
                                jnp.where(lane == 2, w1, jnp.where(lane == 3, w2, 0.0))))
    route_ref[...] = route


def _router(x, l, g_all, mod3, wr_pad, br_pad, seq):
    T, D = x.shape
    rows = NORM_ROWS
    call, lidx = _grid_call(
        _router_kernel, l, (T // rows,),
        [pl.BlockSpec((rows, D), lambda i, l: (i, 0)),
         pl.BlockSpec((None, 1, D), lambda i, l: (l[0], 0, 0)),
         pl.BlockSpec((None, 1, D), _mod_spec(3, rows, seq)),
         pl.BlockSpec((None, 1, D), _mod_spec(4, rows, seq)),
         pl.BlockSpec((None, D, LANES), lambda i, l: (l[0], 0, 0)),
         pl.BlockSpec((None, 1, LANES), lambda i, l: (l[0], 0, 0))],
        [pl.BlockSpec((rows * (D // (2 * LANES)), LANES), lambda i, l: (i, 0)),
         pl.BlockSpec((rows, LANES), lambda i, l: (i, 0))],
        [jax.ShapeDtypeStruct((T * (D // (2 * LANES)), LANES), jnp.uint32),
         jax.ShapeDtypeStruct((T, LANES), F32)],
        sem=("parallel",), name="norm_router")
    return call(lidx, x, g_all, mod3, mod3, wr_pad, br_pad)


def _expert_kernel(meta_ref, src_one_ref, src_cur_ref, src_nxt_ref, dst_ref, h_ref,
                   wg_ref, wu_ref, wd_ref, ys_ref, xbuf_ref, obuf_ref, wgf_ref, wuf_ref, wdf_ref,
                   wgbf_ref, wubf_ref, wdbf_ref, gsem_ref, ssem_ref, wsem_ref):
    i = pl.program_id(0)
    n_tiles = pl.num_programs(0)
    n_used = meta_ref[1]
    slot = i % 2
    nj, tm = xbuf_ref.shape[1], xbuf_ref.shape[2]
    layer = meta_ref[0]
    expert = meta_ref[2 + i]
    next_expert = meta_ref[2 + n_tiles + i]
    wset = meta_ref[2 + 2 * n_tiles + i]

    def weight_copies(e, s):
        return [pltpu.make_async_copy(w.at[layer, e], buf.at[s], wsem_ref.at[s])
                for w, buf in ((wg_ref, wgf_ref), (wu_ref, wuf_ref), (wd_ref, wdf_ref))]

    def gather_copy(idx_ref, r, s):
        return pltpu.make_async_copy(h_ref.at[idx_ref[0, 0, r]], xbuf_ref.at[s, :, r, :],
                                     gsem_ref.at[s])

    def scatter_copy(idx_ref, r, s):
        return pltpu.make_async_copy(obuf_ref.at[s, :, r, :], ys_ref.at[idx_ref[0, 0, r]],
                                     ssem_ref.at[s])

    def gather_start(idx_ref, s):
        def body(r, c):
            gather_copy(idx_ref, r, s).start()
            return c
        lax.fori_loop(0, tm, body, 0, unroll=8)

    def gather_wait(s):
        pltpu.make_async_copy(obuf_ref.at[0], xbuf_ref.at[s], gsem_ref.at[s]).wait()

    def scatter_wait(s):
        pltpu.make_async_copy(xbuf_ref.at[0], obuf_ref.at[s], ssem_ref.at[s]).wait()

    @pl.when(i == 0)
    def _():
        for cp in weight_copies(expert, wset):
            cp.start()
        gather_start(src_cur_ref, 0)

        @pl.when(n_used > 1)
        def _():
            gather_start(src_one_ref, 1)
        obuf_ref[1] = jnp.zeros(obuf_ref.shape[1:], jnp.uint32)
        n_rows = ys_ref.shape[0]

        def fill(r, c):
            pltpu.make_async_copy(obuf_ref.at[1, :, r % tm, :], ys_ref.at[n_rows - 2 * tm + r],
                                  ssem_ref.at[1]).start()
            return c
        lax.fori_loop(0, 2 * tm, fill, 0, unroll=8)
        scatter_wait(1)
        scatter_wait(1)

    @pl.when(i < n_used)
    def _():
        xslot = i % 3
        gather_wait(xslot)

        @pl.when(i >= 2)
        def _():
            scatter_wait(slot)

        @pl.when((i == 0) | (expert != meta_ref[2 + jnp.maximum(i - 1, 0)]))
        def _():
            for cp in weight_copies(expert, wset):
                cp.wait()
            wgbf_ref[...] = wgf_ref[wset].astype(BF16)
            wubf_ref[...] = wuf_ref[wset].astype(BF16)
            wdbf_ref[...] = wdf_ref[wset].astype(BF16)

            @pl.when(next_expert >= 0)
            def _():
                for cp in weight_copies(next_expert, 1 - wset):
                    cp.start()

        x = _unpack_rows([xbuf_ref[xslot, j] for j in range(nj)]).astype(BF16)
        a = _dot(x, wgbf_ref[...])
        u = _dot(x, wubf_ref[...])
        hid = (a * _sigmoid(a)) * u
        y = _dot(hid.astype(BF16), wdbf_ref[...])
        for j, word in enumerate(_pack_rows(y.astype(BF16).astype(F32))):
            obuf_ref[slot, j] = word
        for r in range(tm):
            scatter_copy(dst_ref, r, slot).start()

        @pl.when(i + 2 < n_used)
        def _():
            nslot = (i + 2) % 3
            for r in range(tm):
                gather_copy(src_nxt_ref, r, nslot).start()

        @pl.when(i == n_used - 1)
        def _():
            @pl.when(i >= 1)
            def _():
                scatter_wait(1 - slot)
            scatter_wait(slot)


def _experts(h3, src, dst, meta, w_gate, w_up, w_down, n_out):
    T, nj, _ = h3.shape
    D = 2 * nj * LANES
    F = w_gate.shape[-1]
    n_tiles, _, tm = src.shape
    idx_spec = lambda f: pl.BlockSpec((1, 1, tm), lambda i, m: (f(i), 0, 0), memory_space=pltpu.SMEM)
    hbm = pl.BlockSpec(memory_space=pl.ANY)
    return pl.pallas_call(
        _expert_kernel,
        grid_spec=pltpu.PrefetchScalarGridSpec(
            num_scalar_prefetch=1, grid=(n_tiles,),
            in_specs=[idx_spec(lambda i: min(1, n_tiles - 1)), idx_spec(lambda i: i),
                      idx_spec(lambda i: jnp.minimum(i + 2, n_tiles - 1)), idx_spec(lambda i: i),
                      hbm, hbm, hbm, hbm],
            out_specs=hbm,
            scratch_shapes=[pltpu.VMEM((3, nj, tm, LANES), jnp.uint32),
                            pltpu.VMEM((2, nj, tm, LANES), jnp.uint32),
                            pltpu.VMEM((2, D, F), F32), pltpu.VMEM((2, D, F), F32),
                            pltpu.VMEM((2, F, D), F32),
                            pltpu.VMEM((D, F), BF16), pltpu.VMEM((D, F), BF16),
                            pltpu.VMEM((F, D), BF16),
                            pltpu.SemaphoreType.DMA((3,)), pltpu.SemaphoreType.DMA((2,)),
                            pltpu.SemaphoreType.DMA((2,))]),
        out_shape=jax.ShapeDtypeStruct((n_out, nj, LANES), jnp.uint32),
        compiler_params=_params("arbitrary"),
        name="grouped_experts",
    )(meta, src, src, src, dst, h3, w_gate, w_up, w_down)


def _combine_kernel(x_ref, gt_ref, route_ref, y0_ref, y1_ref, g_ref, sh_ref, sc_ref, o_ref, h_ref):
    rows, d = x_ref.shape
    nw = d // (2 * LANES)
    w1 = jnp.broadcast_to(route_ref[:, 2:3], (rows, LANES))
    w2 = jnp.broadcast_to(route_ref[:, 3:4], (rows, LANES))
    for j in range(nw):
        u0 = y0_ref[pl.ds(j, rows, stride=nw), :]
        u1 = y1_ref[pl.ds(j, rows, stride=nw), :]
        for blk, half in ((j, lambda u: u << 16), (nw + j, lambda u: u & jnp.uint32(0xFFFF0000))):
            js = slice(blk * LANES, (blk + 1) * LANES)
            y = (w1 * lax.bitcast_convert_type(half(u0), F32)
                 + w2 * lax.bitcast_convert_type(half(u1), F32))
            o_ref[:, js] = x_ref[:, js] + gt_ref[:, js] * y
    h_ref[...] = _ada_norm(o_ref[...], g_ref[...], sc_ref[...], sh_ref[...]).astype(h_ref.dtype)


def _combine(x, ys, route, mod3, g_next, mod3_next, seq):
    T, D = x.shape
    rows = COMBINE_ROWS
    nblk = T // rows
    nj = D // (2 * LANES)
    batch_of = lambda i: i * rows // seq
    return pl.pallas_call(
        _combine_kernel,
        grid=(nblk,),
        in_specs=[pl.BlockSpec((rows, D), lambda i: (i, 0)),
                  pl.BlockSpec((None, 1, D), lambda i: (batch_of(i), 0, 5)),
                  pl.BlockSpec((rows, LANES), lambda i: (i, 0)),
                  pl.BlockSpec((rows * nj, LANES), lambda i: (i, 0)),
                  pl.BlockSpec((rows * nj, LANES), lambda i: (nblk + i, 0)),
                  pl.BlockSpec((1, D), lambda i: (0, 0)),
                  pl.BlockSpec((None, 1, D), lambda i: (batch_of(i), 0, 0)),
                  pl.BlockSpec((None, 1, D), lambda i: (batch_of(i), 0, 1))],
        out_specs=[pl.BlockSpec((rows, D), lambda i: (i, 0)),
                   pl.BlockSpec((rows, D), lambda i: (i, 0))],
        out_shape=[jax.ShapeDtypeStruct((T, D), F32), jax.ShapeDtypeStruct((T, D), BF16)],
        compiler_params=_params("parallel"),
        name="moe_combine",
    )(x, mod3, route, ys, ys, g_next, mod3_next, mod3_next)


def _invert_kernel(pos_ref, asg_ref, fill_ref, sem_ref):
    i = pl.program_id(0)
    blk = pos_ref.shape[0]

    @pl.when(i == 0)
    def _():
        fill_ref[...] = jnp.full(fill_ref.shape, -1, jnp.int32)
        cp = pltpu.make_async_copy(fill_ref, asg_ref, sem_ref)
        cp.start()
        cp.wait()

    base = i * blk

    def body(a, c):
        asg_ref[pos_ref[a]] = base + a
        return c
    lax.fori_loop(0, blk, body, 0, unroll=8)


def _invert_slots(pos, n_slots):
    n, = pos.shape
    blk = INVERT_BLOCK
    return pl.pallas_call(
        _invert_kernel,
        grid=(n // blk,),
        in_specs=[pl.BlockSpec((blk,), lambda i: (i,), memory_space=pltpu.SMEM)],
        out_specs=pl.BlockSpec((n_slots,), lambda i: (0,), memory_space=pltpu.SMEM),
        out_shape=jax.ShapeDtypeStruct((n_slots,), jnp.int32),
        scratch_shapes=[pltpu.VMEM((n_slots,), jnp.int32), pltpu.SemaphoreType.DMA(())],
        compiler_params=_params("arbitrary"),
        name="invert_slots",
    )(pos)


def _dispatch_plan(route, l):
    T = route.shape[0]
    tm = EXPERT_TM
    n_tiles = (2 * T) // tm + N_EXPERTS
    P = n_tiles * tm
    e = route[:, 0:2].astype(jnp.int32).T.reshape(-1)
    onehot = (e[:, None] == jnp.arange(N_EXPERTS, dtype=jnp.int32)[None, :]).astype(jnp.int32)
    csum = jnp.cumsum(onehot, axis=0)
    counts = csum[-1]
    tiles_per = (counts + tm - 1) // tm
    tile_end = jnp.cumsum(tiles_per)
    off = (tile_end - tiles_per) * tm
    pos = jnp.sum((csum - onehot + off[None, :]) * onehot, axis=1)
    asg = _invert_slots(pos.astype(jnp.int32), P)
    slot = jnp.arange(P, dtype=jnp.int32)
    spare = 2 * T + ((slot // tm) % 2) * tm + slot % tm
    src = jnp.where(asg >= 0, asg % T, 0).reshape(n_tiles, 1, tm)
    dst = jnp.where(asg >= 0, asg, spare).reshape(n_tiles, 1, tm)
    tile_ids = jnp.arange(n_tiles, dtype=jnp.int32)
    tile_expert = jnp.minimum(jnp.sum(tile_ids[:, None] >= tile_end[None, :], axis=1),
                              N_EXPERTS - 1).astype(jnp.int32)
    ids = jnp.arange(N_EXPERTS, dtype=jnp.int32)
    used = counts > 0
    later = lax.cummin(jnp.where(used, ids, N_EXPERTS)[::-1])[::-1]
    nxt = jnp.concatenate([later[1:], jnp.full((1,), N_EXPERTS, jnp.int32)])
    nxt = jnp.where(nxt < N_EXPERTS, nxt, -1).astype(jnp.int32)
    wset = ((jnp.cumsum(used.astype(jnp.int32)) - 1) % 2).astype(jnp.int32)
    of_tile = tile_expert[:, None] == ids[None, :]
    nxt_tile = jnp.sum(jnp.where(of_tile, nxt[None, :], 0), axis=1)
    wset_tile = jnp.sum(jnp.where(of_tile, wset[None, :], 0), axis=1)
    meta = jnp.concatenate([jnp.stack([l, tile_end[-1].astype(jnp.int32)]), tile_expert,
                            nxt_tile, wset_tile])
    return src, dst, meta


def kernel(x, c, positions, w_ada, b_ada, g_norm_mix, g_norm_ffn, w_in, w_fg, b_fg, g_gla_out,
           g_q, g_k, w_gla_proj, w_attn_proj, w_out, w_route_group, b_route_group,
           w_route_expert, b_route_expert, w_exp_gate, w_exp_up, w_exp_down):
    B, S, D = x.shape
    L = w_ada.shape[0]
    T = B * S
    dk_all = w_fg.shape[-1]
    dv_all = g_gla_out.shape[-1]
    att_w = N_DIL * ATT_HEADS * HEAD_DIM
    gla_cols = 2 * dk_all + 2 * dv_all

    mod = _modulation(c, w_ada, b_ada)
    cos2, sin2 = _rope_tables(positions)

    wfg_pad = jnp.pad(w_fg, ((0, 0), (0, LANES - GLA_RANK), (0, 0)))
    n_route = N_GROUPS + N_EXPERTS
    wr_pad = jnp.pad(jnp.concatenate([w_route_group, w_route_expert], axis=-1),
                     ((0, 0), (0, 0), (0, LANES - n_route)))
    br_pad = jnp.pad(jnp.concatenate([b_route_group, b_route_expert], axis=-1),
                     ((0, 0), (0, LANES - n_route))).reshape(L, 1, LANES)
    g_mix = g_norm_mix.reshape(L, 1, D)
    g_ffn = g_norm_ffn.reshape(L, 1, D)
    b_fg3 = b_fg.reshape(L, 1, dk_all)
    g_gla3 = g_gla_out.reshape(L, 1, dv_all)
    g_q3 = g_q.reshape(L, 1, HEAD_DIM)
    g_k3 = g_k.reshape(L, 1, HEAD_DIM)

    def residual(acc, xres, gt):
        return xres + gt * acc

    def mod_of(l):
        return lax.dynamic_index_in_dim(mod, l, 0, keepdims=False).reshape(8, 1, N_MOD * D)

    def layer(l, carry):
        xt, h = carry
        l = jnp.asarray(l, jnp.int32)
        mod3 = mod_of(l)
        proj1 = _matmul(h, w_in, l, col0=0, n=gla_cols, out_dtype=BF16, tn=2 * MM_TN,
                        tm=2 * MM_TM, name="proj_gla")
        fg = _matmul(h, w_in, l, col0=gla_cols, n=LANES, tn=LANES, out_dtype=F32, name="proj_fg")
        proj2 = _matmul(h, w_in, l, col0=gla_cols, shift=GLA_RANK, n=3 * att_w,
                        out_dtype=BF16, tm=2 * MM_TM, name="proj_att")
        gates = _matmul(h, w_in, l, col0=gla_cols + 3 * att_w, shift=GLA_RANK, n=2 * D,
                        out_dtype=BF16, tm=2 * MM_TM, name="proj_gates")
        o_gla = _gla(proj1, fg, l, wfg_pad, b_fg3, g_gla3, B, S)
        prep = _qkprep(proj2, l, cos2, sin2, g_q3, g_k3)
        outs, lses = [], []
        for grp, (_, dil) in enumerate(DIL_GROUPS):
            o, lse = _attn_group(prep[grp], prep[N_DIL + grp], prep[2 * N_DIL + grp], dil, B, S)
            outs.append(o)
            lses.append(lse)
        o_att = _attn_mix(outs, lses)
        merged = _merge(o_gla, o_att, gates, l, w_gla_proj, w_attn_proj)
        tn_out = 2 * MM_TN
        gt_spec = pl.BlockSpec((None, 1, tn_out),
                               lambda j, i, l: (i * MM_TM // S, 0, 2 * (D // tn_out) + j))
        xt = _matmul(merged, w_out, l, col0=0, n=D, out_dtype=F32, tn=tn_out, epilogue=residual,
                     extras=(xt, mod3),
                     extra_specs=(pl.BlockSpec((MM_TM, tn_out), lambda j, i, l: (i, j)), gt_spec),
                     name="out_proj")
        h2, route = _router(xt, l, g_ffn, mod3, wr_pad, br_pad, S)
        src, dst, meta = _dispatch_plan(route, l)
        nj = D // (2 * LANES)
        n_out = 2 * T + 2 * EXPERT_TM
        ys = _experts(h2.reshape(T, nj, LANES), src, dst, meta, w_exp_gate, w_exp_up, w_exp_down,
                      n_out)
        l_next = jnp.minimum(l + 1, L - 1)
        g_next = lax.dynamic_index_in_dim(g_mix, l_next, 0, keepdims=False)
        return tuple(_combine(xt, ys.reshape(n_out * nj, LANES), route, mod3, g_next,
                              mod_of(l_next), S))

    x0 = x.reshape(T, D)
    zero = jnp.int32(0)
    h0, x_init = _norm(x0, zero, g_mix, mod_of(zero), 0, S)
    xt, _ = lax.fori_loop(0, L, layer, (x_init, h0))
    return xt.reshape(B, S, D)
```

```python
import functools

import jax
import jax.numpy as jnp
from jax import lax
from jax.experimental import pallas as pl
from jax.experimental.pallas import tpu as pltpu

F32 = jnp.float32
BF16 = jnp.bfloat16

GLA_HEADS = 4
GLA_RANK = 16
GLA_TAU = 16.0
GLA_CHUNK = 64
DIL_GROUPS = ((128, 1), (512, 4), (2048, 16))
N_DIL = 3
ATT_HEADS = 4
HEAD_DIM = 128
ATT_BLOCK = 128
ROPE_THETA = 10000.0
N_GROUPS = 4
EXPERTS_PER_GROUP = 8
N_EXPERTS = N_GROUPS * EXPERTS_PER_GROUP
N_MOD = 6
EPS = 1e-6

LANES = 128
VMEM_LIMIT = 56 * 1024 * 1024
NEG_BIG = -1e30

MOD_TN = 2048
MOD_SPLIT = 4
NORM_ROWS = 512
MM_TM = 1024
MM_TN = 512
GLA_ROWS = 512
PREP_ROWS = 512
ATT_SPAN = 2048
MERGE_ROWS = 1024
EXPERT_TM = 256
COMBINE_ROWS = 512
INVERT_BLOCK = 4096


def _params(*sem):
    return pltpu.CompilerParams(dimension_semantics=sem, vmem_limit_bytes=VMEM_LIMIT)


def _dot(a, b):
    return jnp.dot(a, b, preferred_element_type=F32)


def _dot_nt(a, b):
    return lax.dot_general(a, b, (((1,), (1,)), ((), ())), preferred_element_type=F32)


def _dot_tn(a, b):
    return lax.dot_general(a, b, (((0,), (0,)), ((), ())), preferred_element_type=F32)


def _split_bf16(x):
    hi = x.astype(BF16)
    lo = (x - hi.astype(F32)).astype(BF16)
    return hi, lo


def _sigmoid(x):
    return 1.0 / (1.0 + jnp.exp(-x))


def _pack_rows(xb):
    bits = lax.bitcast_convert_type(xb, jnp.uint32)
    nw = xb.shape[1] // (2 * LANES)
    out = []
    for j in range(nw):
        lo = bits[:, j * LANES:(j + 1) * LANES] >> 16
        hi = bits[:, (nw + j) * LANES:(nw + j + 1) * LANES] & jnp.uint32(0xFFFF0000)
        out.append(lo | hi)
    return out


def _unpack_rows(words):
    lo = [lax.bitcast_convert_type(w << 16, F32) for w in words]
    hi = [lax.bitcast_convert_type(w & jnp.uint32(0xFFFF0000), F32) for w in words]
    return jnp.concatenate(lo + hi, axis=1)


def _grid_call(kernel, l, grid, in_specs, out_specs, out_shape, scratch=(), sem=None, name=None):
    sem = sem or ("arbitrary",) * len(grid)
    return pl.pallas_call(
        kernel,
        grid_spec=pltpu.PrefetchScalarGridSpec(
            num_scalar_prefetch=1, grid=grid, in_specs=in_specs, out_specs=out_specs,
            scratch_shapes=list(scratch)),
        out_shape=out_shape,
        compiler_params=_params(*sem),
        name=name,
    ), jnp.reshape(l, (1,)).astype(jnp.int32)


def _mod_kernel(c_ref, *refs):
    w_refs, (b_ref, o_ref, act_ref) = refs[:MOD_SPLIT], refs[MOD_SPLIT:]
    n_batch = c_ref.shape[0]
    tn = o_ref.shape[1]

    @pl.when((pl.program_id(0) == 0) & (pl.program_id(1) == 0))
    def _():
        c = c_ref[...]
        act_ref[...] = c * _sigmoid(c)

    rows = [[] for _ in range(n_batch)]
    for w_ref in w_refs:
        for jb in range(w_ref.shape[1] // LANES):
            w = w_ref[:, jb * LANES:(jb + 1) * LANES]
            for b in range(n_batch):
                rows[b].append(jnp.sum(w * act_ref[b], axis=0, keepdims=True))
    out = jnp.concatenate([jnp.concatenate(r, axis=1) for r in rows], axis=0) + b_ref[...]
    o_ref[0:n_batch, :] = out
    o_ref[n_batch:, :] = jnp.zeros((o_ref.shape[0] - n_batch, tn), F32)


def _modulation(c, w_ada, b_ada):
    L, D, N = w_ada.shape
    B = c.shape[0]
    rows = 8
    assert B <= rows
    c_rep = jnp.broadcast_to(c[:, :, None], (B, D, LANES))
    tn = MOD_TN
    ts = tn // MOD_SPLIT
    w_specs = [pl.BlockSpec((None, D, ts), lambda l, j, k=k: (l, 0, j * MOD_SPLIT + k))
               for k in range(MOD_SPLIT)]
    return pl.pallas_call(
        _mod_kernel,
        grid=(L, N // tn),
        in_specs=[pl.BlockSpec((B, D, LANES), lambda l, j: (0, 0, 0))] + w_specs +
                 [pl.BlockSpec((None, 1, tn), lambda l, j: (l, 0, j))],
        out_specs=pl.BlockSpec((None, rows, tn), lambda l, j: (l, 0, j)),
        out_shape=jax.ShapeDtypeStruct((L, rows, N), F32),
        scratch_shapes=[pltpu.VMEM((B, D, LANES), F32)],
        compiler_params=_params("arbitrary", "arbitrary"),
        name="adaln_mod",
    )(c_rep, *([w_ada] * MOD_SPLIT), b_ada.reshape(L, 1, N))


def _rope_kernel(pos_ref, freq_ref, cos_ref, sin_ref):
    ang = pos_ref[...].astype(F32) * freq_ref[...]
    lane = lax.broadcasted_iota(jnp.int32, ang.shape, 1)
    cos_ref[...] = jnp.cos(ang)
    sin_ref[...] = jnp.where(lane < HEAD_DIM // 2, -jnp.sin(ang), jnp.sin(ang))


def _rope_tables(positions):
    T = positions.size
    inv_freq = ROPE_THETA ** (-jnp.arange(0, HEAD_DIM, 2, dtype=F32) / HEAD_DIM)
    freq2 = jnp.concatenate([inv_freq, inv_freq]).reshape(1, HEAD_DIM)
    rows = 2048
    return pl.pallas_call(
        _rope_kernel,
        grid=(T // rows,),
        in_specs=[pl.BlockSpec((rows, 1), lambda i: (i, 0)),
                  pl.BlockSpec((1, HEAD_DIM), lambda i: (0, 0))],
        out_specs=[pl.BlockSpec((rows, HEAD_DIM), lambda i: (i, 0))] * 2,
        out_shape=[jax.ShapeDtypeStruct((T, HEAD_DIM), F32)] * 2,
        compiler_params=_params("parallel"),
        name="rope_tables",
    )(positions.reshape(T, 1), freq2)


def _ada_norm(x, g, scale, shift):
    y = x * lax.rsqrt(jnp.mean(x * x, axis=-1, keepdims=True) + EPS)
    return y * g * (1.0 + scale) + shift


def _norm_kernel(l_ref, x_ref, g_ref, sh_ref, sc_ref, o_ref, xcopy_ref):
    x = x_ref[...]
    o_ref[...] = _ada_norm(x, g_ref[...], sc_ref[...], sh_ref[...]).astype(o_ref.dtype)
    xcopy_ref[...] = x


def _mod_spec(which, rows, seq):
    return lambda i, l: (i * rows // seq, 0, which)


def _norm(x, l, g_all, mod3, which_shift, seq):
    T, D = x.shape
    rows = NORM_ROWS
    call, lidx = _grid_call(
        _norm_kernel, l, (T // rows,),
        [pl.BlockSpec((rows, D), lambda i, l: (i, 0)),
         pl.BlockSpec((None, 1, D), lambda i, l: (l[0], 0, 0)),
         pl.BlockSpec((None, 1, D), _mod_spec(which_shift, rows, seq)),
         pl.BlockSpec((None, 1, D), _mod_spec(which_shift + 1, rows, seq))],
        [pl.BlockSpec((rows, D), lambda i, l: (i, 0))] * 2,
        [jax.ShapeDtypeStruct((T, D), BF16), jax.ShapeDtypeStruct((T, D), x.dtype)],
        sem=("parallel",), name="ada_norm")
    return call(lidx, x, g_all, mod3, mod3)


def _mm_kernel(l_ref, a_ref, w_ref, *rest, epilogue, n_extra, shift):
    n_w = 1 if shift else 0
    extra = rest[n_w:n_w + n_extra]
    o_ref = rest[n_w + n_extra]
    wbf_ref = rest[n_w + n_extra + 1]
    tn = o_ref.shape[1]

    @pl.when(pl.program_id(1) == 0)
    def _():
        if shift:
            wide = jnp.concatenate([w_ref[...], rest[0][...]], axis=1)
            wbf_ref[...] = wide[:, shift:shift + tn].astype(BF16)
        else:
            wbf_ref[...] = w_ref[...].astype(BF16)

    acc = _dot(a_ref[...], wbf_ref[...])
    if epilogue is not None:
        acc = epilogue(acc, *[e[...] for e in extra])
    o_ref[...] = acc.astype(o_ref.dtype)


def _matmul(a, w, l, *, col0, n, out_dtype, tn=MM_TN, tm=MM_TM, shift=0, epilogue=None,
            extras=(), extra_specs=(), name="matmul"):
    M, K = a.shape
    assert col0 % tn == 0 and n % tn == 0 and M % tm == 0 and 0 <= shift < LANES
    cb0 = col0 // tn
    kern = functools.partial(_mm_kernel, epilogue=epilogue, n_extra=len(extras), shift=shift)
    w_specs = [pl.BlockSpec((None, K, tn), lambda j, i, l: (l[0], 0, cb0 + j))]
    if shift:
        per = tn // LANES
        w_specs.append(pl.BlockSpec((None, K, LANES), lambda j, i, l: (l[0], 0, (cb0 + j + 1) * per)))
    call, lidx = _grid_call(
        kern, l, (n // tn, M // tm),
        [pl.BlockSpec((tm, K), lambda j, i, l: (i, 0))] + w_specs + list(extra_specs),
        pl.BlockSpec((tm, tn), lambda j, i, l: (i, j)),
        jax.ShapeDtypeStruct((M, n), out_dtype),
        scratch=[pltpu.VMEM((K, tn), BF16)], name=name)
    return call(lidx, a, *([w] * len(w_specs)), *extras)


def _gla_kernel(l_ref, q_ref, k_ref, v_ref, r_ref, fg_ref, wfg_ref, bfg_ref, g_ref, o_ref,
                state_ref, *, head_k, head_v):
    rows = q_ref.shape[0]
    C = GLA_CHUNK

    @pl.when(pl.program_id(1) == 0)
    def _():
        state_ref[...] = jnp.zeros_like(state_ref)

    fh, fl = _split_bf16(fg_ref[...])
    wh, wl = _split_bf16(wfg_ref[...])
    z = _dot(fh, wh) + _dot(fl, wh) + _dot(fh, wl) + bfg_ref[...]
    log_a = (jnp.minimum(z, 0.0) - jnp.log(1.0 + jnp.exp(-jnp.abs(z)))) * (1.0 / GLA_TAU)

    ri = lax.broadcasted_iota(jnp.int32, (C, C), 0)
    ci = lax.broadcasted_iota(jnp.int32, (C, C), 1)
    causal = ri >= ci
    tri = causal.astype(BF16)
    scale = head_k ** -0.5

    for c in range(rows // C):
        rs = slice(c * C, (c + 1) * C)
        lh, ll = _split_bf16(log_a[rs])
        b = _dot(tri, lh) + _dot(tri, ll)
        b_last = b[C - 1:C]
        q = q_ref[rs, :].astype(F32) * scale
        k = k_ref[rs, :].astype(F32)
        q_dec = (q * jnp.exp(b)).astype(BF16)
        k_inv = (k * jnp.exp(-b)).astype(BF16)
        k_upd = (k * jnp.exp(b_last - b)).astype(BF16)
        decay = jnp.exp(b_last)
        for h in range(GLA_HEADS):
            ks = slice(h * head_k, (h + 1) * head_k)
            vs = slice(h * head_v, (h + 1) * head_v)
            v = v_ref[rs, vs]
            att = jnp.where(causal, _dot_nt(q_dec[:, ks], k_inv[:, ks]), 0.0).astype(BF16)
            st = state_ref[h]
            o = _dot(att, v) + _dot_nt(q_dec[:, ks], st.astype(BF16))
            state_ref[h] = st * decay[:, ks] + _dot_tn(v, k_upd[:, ks])
            o = o * lax.rsqrt(jnp.mean(o * o, axis=-1, keepdims=True) + EPS) * g_ref[:, vs]
            r = r_ref[rs, vs].astype(F32)
            o_ref[rs, vs] = (o * (r * _sigmoid(r))).astype(o_ref.dtype)


def _gla(proj, fg, l, wfg_pad, b_fg, g_gla_out, batch, seq):
    T = proj.shape[0]
    dk_all = wfg_pad.shape[-1]
    dv_all = g_gla_out.shape[-1]
    rows = GLA_ROWS
    nblk = seq // rows
    kern = functools.partial(_gla_kernel, head_k=dk_all // GLA_HEADS, head_v=dv_all // GLA_HEADS)
    row = lambda b, n, l: b * nblk + n
    call, lidx = _grid_call(
        kern, l, (batch, nblk),
        [pl.BlockSpec((rows, dk_all), lambda b, n, l: (row(b, n, l), 0)),
         pl.BlockSpec((rows, dk_all), lambda b, n, l: (row(b, n, l), 1)),
         pl.BlockSpec((rows, dv_all), lambda b, n, l: (row(b, n, l), 1)),
         pl.BlockSpec((rows, dv_all), lambda b, n, l: (row(b, n, l), 2)),
         pl.BlockSpec((rows, LANES), lambda b, n, l: (row(b, n, l), 0)),
         pl.BlockSpec((None, LANES, dk_all), lambda b, n, l: (l[0], 0, 0)),
         pl.BlockSpec((None, 1, dk_all), lambda b, n, l: (l[0], 0, 0)),
         pl.BlockSpec((None, 1, dv_all), lambda b, n, l: (l[0], 0, 0))],
        pl.BlockSpec((rows, dv_all), lambda b, n, l: (row(b, n, l), 0)),
        jax.ShapeDtypeStruct((T, dv_all), BF16),
        scratch=[pltpu.VMEM((GLA_HEADS, dv_all // GLA_HEADS, dk_all // GLA_HEADS), F32)],
        sem=("parallel", "arbitrary"), name="gla")
    return call(lidx, proj, proj, proj, proj, fg, wfg_pad, b_fg, g_gla_out)


def _qkprep_kernel(l_ref, q_ref, k_ref, v_ref, cos_ref, sin_ref, gq_ref, gk_ref, *refs):
    out_refs, scr_ref = refs[:-1], refs[-1]
    cos = cos_ref[...]
    sin = sin_ref[...]
    gw = ATT_HEADS * HEAD_DIM
    rows = q_ref.shape[0]

    def tables(g_ref, scale):
        g = jnp.broadcast_to(g_ref[...], cos.shape)
        return g * cos * scale, pltpu.roll(g, HEAD_DIM // 2, 1) * sin * scale

    def prep(ref, tab):
        x = ref.astype(F32)
        n = x * lax.rsqrt(jnp.mean(x * x, axis=-1, keepdims=True) + EPS)
        return n * tab[0] + pltpu.roll(n, HEAD_DIM // 2, 1) * tab[1]

    def emit(dst_ref, dil):
        n = rows // dil
        for p in range(dil):
            for h in range(ATT_HEADS):
                cs = slice(p * gw + h * HEAD_DIM, p * gw + (h + 1) * HEAD_DIM)
                dst_ref[:, cs] = scr_ref[h, pl.ds(p, n, stride=dil), :].astype(BF16)

    tab_q = tables(gq_ref, HEAD_DIM ** -0.5)
    tab_k = tables(gk_ref, 1.0)
    for grp, (_, dil) in enumerate(DIL_GROUPS):
        for ref, tab, out in ((q_ref, tab_q, out_refs[grp]), (k_ref, tab_k, out_refs[N_DIL + grp])):
            for h in range(ATT_HEADS):
                src = slice(grp * gw + h * HEAD_DIM, grp * gw + (h + 1) * HEAD_DIM)
                y = prep(ref[:, src], tab)
                if dil == 1:
                    out[:, h * HEAD_DIM:(h + 1) * HEAD_DIM] = y.astype(BF16)
                else:
                    scr_ref[h] = y
            if dil > 1:
                emit(out, dil)
        if dil == 1:
            out_refs[2 * N_DIL + grp][...] = v_ref[:, grp * gw:(grp + 1) * gw]
        else:
            for h in range(ATT_HEADS):
                src = slice(grp * gw + h * HEAD_DIM, grp * gw + (h + 1) * HEAD_DIM)
                scr_ref[h] = v_ref[:, src].astype(F32)
            emit(out_refs[2 * N_DIL + grp], dil)


def _qkprep(proj2, l, cos2, sin2, g_q, g_k):
    T = proj2.shape[0]
    rows = PREP_ROWS
    width = N_DIL * ATT_HEADS * HEAD_DIM
    gw = ATT_HEADS * HEAD_DIM
    dils = [dil for _, dil in DIL_GROUPS] * 3
    call, lidx = _grid_call(
        _qkprep_kernel, l, (T // rows,),
        [pl.BlockSpec((rows, width), lambda i, l: (i, 0)),
         pl.BlockSpec((rows, width), lambda i, l: (i, 1)),
         pl.BlockSpec((rows, width), lambda i, l: (i, 2)),
         pl.BlockSpec((rows, HEAD_DIM), lambda i, l: (i, 0)),
         pl.BlockSpec((rows, HEAD_DIM), lambda i, l: (i, 0)),
         pl.BlockSpec((None, 1, HEAD_DIM), lambda i, l: (l[0], 0, 0)),
         pl.BlockSpec((None, 1, HEAD_DIM), lambda i, l: (l[0], 0, 0))],
        [pl.BlockSpec((rows // d, d * gw), lambda i, l: (i, 0)) for d in dils],
        [jax.ShapeDtypeStruct((T // d, d * gw), BF16) for d in dils],
        scratch=[pltpu.VMEM((ATT_HEADS, rows, HEAD_DIM), F32)],
        sem=("parallel",), name="qk_prep")
    return call(lidx, proj2, proj2, proj2, cos2, sin2, g_q, g_k)


def _attn_kernel(q_ref, kc_ref, kp_ref, vc_ref, vp_ref, o_ref, lse_ref, *, dil):
    first = pl.program_id(1) == 0
    QB = q_ref.shape[0]
    A = ATT_BLOCK
    qi = lax.broadcasted_iota(jnp.int32, (A, 2 * A), 0)
    kj = lax.broadcasted_iota(jnp.int32, (A, 2 * A), 1) - A
    dist = qi - kj
    valid = (dist >= 0) & (dist <= A)
    bias = jnp.where(valid, 0.0, NEG_BIG).astype(F32)
    bias_first = jnp.where(first, jnp.where(valid & (kj >= 0), 0.0, NEG_BIG), bias).astype(F32)

    for p in range(dil):
        for h in range(ATT_HEADS):
            cs = slice((p * ATT_HEADS + h) * HEAD_DIM, (p * ATT_HEADS + h + 1) * HEAD_DIM)
            for j in range(QB // A):
                rs = slice(j * A, (j + 1) * A)
                if j == 0:
                    kcat = jnp.concatenate([kp_ref[:, cs], kc_ref[0:A, cs]], axis=0)
                    vcat = jnp.concatenate([vp_ref[:, cs], vc_ref[0:A, cs]], axis=0)
                    bb = bias_first
                else:
                    kcat = kc_ref[(j - 1) * A:(j + 1) * A, cs]
                    vcat = vc_ref[(j - 1) * A:(j + 1) * A, cs]
                    bb = bias
                s = _dot_nt(q_ref[rs, cs], kcat) + bb
                m = jnp.max(s, axis=-1, keepdims=True)
                e = jnp.exp(s - m)
                den = jnp.sum(e, axis=-1, keepdims=True)
                o = _dot(e.astype(BF16), vcat) / den
                o_ref[rs, cs] = o.astype(o_ref.dtype)
                lse_ref[rs, cs] = jnp.broadcast_to(m + jnp.log(den), (A, HEAD_DIM))


def _attn_group(q, k, v, dil, batch, seq):
    rows_all, W = q.shape
    A = ATT_BLOCK
    QB = ATT_SPAN // dil
    nsp = seq // ATT_SPAN
    per = QB // A
    cur = pl.BlockSpec((QB, W), lambda b, n: (b * nsp + n, 0))
    prev = pl.BlockSpec((A, W), lambda b, n: (jnp.maximum((b * nsp + n) * per - 1, b * nsp * per), 0))
    return pl.pallas_call(
        functools.partial(_attn_kernel, dil=dil),
        grid=(batch, nsp),
        in_specs=[cur, cur, prev, cur, prev],
        out_specs=[cur, cur],
        out_shape=[jax.ShapeDtypeStruct((rows_all, W), BF16),
                   jax.ShapeDtypeStruct((rows_all, W), F32)],
        compiler_params=_params("parallel", "arbitrary"),
        name=f"dilated_attn_r{dil}",
    )(q, k, k, v, v)


def _attn_mix_kernel(o0, o1, o2, l0, l1, l2, out_ref, so_ref, sl_ref):
    rows, gw = out_ref.shape
    o_refs, l_refs = (o0, o1, o2), (l0, l1, l2)
    for h in range(ATT_HEADS):
        hs = slice(h * HEAD_DIM, (h + 1) * HEAD_DIM)
        outs, lses = [], []
        for g, (_, dil) in enumerate(DIL_GROUPS):
            if dil == 1:
                outs.append(o_refs[g][:, hs].astype(F32))
                lses.append(l_refs[g][:, hs])
                continue
            n = rows // dil
            for p in range(dil):
                cs = slice(p * gw + h * HEAD_DIM, p * gw + (h + 1) * HEAD_DIM)
                so_ref[g, pl.ds(p, n, stride=dil), :] = o_refs[g][:, cs].astype(F32)
                sl_ref[g, pl.ds(p, n, stride=dil), :] = l_refs[g][:, cs]
            outs.append(so_ref[g])
            lses.append(sl_ref[g])
        m = jnp.maximum(jnp.maximum(lses[0], lses[1]), lses[2])
        ws = [jnp.exp(x - m) for x in lses]
        den = ws[0] + ws[1] + ws[2]
        out_ref[:, hs] = ((ws[0] * outs[0] + ws[1] * outs[1] + ws[2] * outs[2]) / den
                          ).astype(out_ref.dtype)


def _attn_mix(outs, lses):
    gw = ATT_HEADS * HEAD_DIM
    T = outs[0].shape[0]
    rows = PREP_ROWS
    dils = [dil for _, dil in DIL_GROUPS]
    specs = [pl.BlockSpec((rows // d, d * gw), lambda i: (i, 0)) for d in dils]
    return pl.pallas_call(
        _attn_mix_kernel,
        grid=(T // rows,),
        in_specs=specs + specs,
        out_specs=pl.BlockSpec((rows, gw), lambda i: (i, 0)),
        out_shape=jax.ShapeDtypeStruct((T, gw), BF16),
        scratch_shapes=[pltpu.VMEM((N_DIL, rows, HEAD_DIM), F32),
                        pltpu.VMEM((N_DIL, rows, HEAD_DIM), F32)],
        compiler_params=_params("parallel"),
        name="attn_mix",
    )(*outs, *lses)


def _merge_kernel(l_ref, ogla_ref, oatt_ref, gg_ref, ga_ref, wg_ref, wa_ref,
                  out_ref, wgbf_ref, wabf_ref):
    @pl.when(pl.program_id(1) == 0)
    def _():
        wgbf_ref[...] = wg_ref[...].astype(BF16)
        wabf_ref[...] = wa_ref[...].astype(BF16)

    a = _dot(ogla_ref[...], wgbf_ref[...])
    b = _dot(oatt_ref[...], wabf_ref[...])
    gg = _sigmoid(gg_ref[...].astype(F32))
    ga = _sigmoid(ga_ref[...].astype(F32))
    out_ref[...] = (gg * a + ga * b).astype(out_ref.dtype)


def _merge(o_gla, o_att, gates, l, w_gla_proj, w_attn_proj):
    T, dv = o_gla.shape
    gw = o_att.shape[1]
    D = w_gla_proj.shape[-1]
    tm, tn = MERGE_ROWS, 2 * MM_TN
    gcb = 0
    dcb = D // tn
    call, lidx = _grid_call(
        _merge_kernel, l, (D // tn, T // tm),
        [pl.BlockSpec((tm, dv), lambda j, i, l: (i, 0)),
         pl.BlockSpec((tm, gw), lambda j, i, l: (i, 0)),
         pl.BlockSpec((tm, tn), lambda j, i, l: (i, gcb + j)),
         pl.BlockSpec((tm, tn), lambda j, i, l: (i, gcb + dcb + j)),
         pl.BlockSpec((None, dv, tn), lambda j, i, l: (l[0], 0, j)),
         pl.BlockSpec((None, gw, tn), lambda j, i, l: (l[0], 0, j))],
        pl.BlockSpec((tm, tn), lambda j, i, l: (i, j)),
        jax.ShapeDtypeStruct((T, D), BF16),
        scratch=[pltpu.VMEM((dv, tn), BF16), pltpu.VMEM((gw, tn), BF16)], name="gated_merge")
    return call(lidx, o_gla, o_att, gates, gates, w_gla_proj, w_attn_proj)


def _router_kernel(l_ref, x_ref, g_ref, sh_ref, sc_ref, wr_ref, br_ref, h_ref, route_ref):
    h = _ada_norm(x_ref[...], g_ref[...], sc_ref[...], sh_ref[...])
    rows = h.shape[0]
    hb = h.astype(BF16)
    words = _pack_rows(hb.astype(F32))
    nw = len(words)
    for j in range(nw):
        h_ref[pl.ds(j, rows, stride=nw), :] = words[j]
    lg = _dot(hb, wr_ref[...].astype(BF16)) + br_ref[...]
    lane = lax.broadcasted_iota(jnp.int32, lg.shape, 1).astype(F32)
    big = float(4 * LANES)

    def first_argmax(vals):
        mx = jnp.max(vals, axis=-1, keepdims=True)
        idx = jnp.min(jnp.where(vals == mx, lane, big), axis=-1, keepdims=True)
        return mx, idx

    gl = jnp.where(lane < N_GROUPS, lg, NEG_BIG)
    gmax, gidx = first_argmax(gl)
    g_weight = 1.0 / jnp.sum(jnp.exp(gl - gmax), axis=-1, keepdims=True)
    lo = N_GROUPS + gidx * EXPERTS_PER_GROUP
    el = jnp.where((lane >= lo) & (lane < lo + EXPERTS_PER_GROUP), lg, NEG_BIG)
    m1, i1 = first_argmax(el)
    m2, i2 = first_argmax(jnp.where(lane == i1, NEG_BIG, el))
    e2 = jnp.exp(m2 - m1)
    w1 = g_weight / (1.0 + e2)
    w2 = g_weight * e2 / (1.0 + e2)
    route = jnp.where(lane == 0, i1 - N_GROUPS,
                      jnp.where(lane == 1, i2 - N_GROUPS,
                                jnp.where(lane == 2, w1, jnp.where(lane == 3, w2, 0.0))))
    route_ref[...] = route


def _router(x, l, g_all, mod3, wr_pad, br_pad, seq):
    T, D = x.shape
    rows = NORM_ROWS
    call, lidx = _grid_call(
        _router_kernel, l, (T // rows,),
        [pl.BlockSpec((rows, D), lambda i, l: (i, 0)),
         pl.BlockSpec((None, 1, D), lambda i, l: (l[0], 0, 0)),
         pl.BlockSpec((None, 1, D), _mod_spec(3, rows, seq)),
         pl.BlockSpec((None, 1, D), _mod_spec(4, rows, seq)),
         pl.BlockSpec((None, D, LANES), lambda i, l: (l[0], 0, 0)),
         pl.BlockSpec((None, 1, LANES), lambda i, l: (l[0], 0, 0))],
        [pl.BlockSpec((rows * (D // (2 * LANES)), LANES), lambda i, l: (i, 0)),
         pl.BlockSpec((rows, LANES), lambda i, l: (i, 0))],
        [jax.ShapeDtypeStruct((T * (D // (2 * LANES)), LANES), jnp.uint32),
         jax.ShapeDtypeStruct((T, LANES), F32)],
        sem=("parallel",), name="norm_router")
    return call(lidx, x, g_all, mod3, mod3, wr_pad, br_pad)


def _expert_kernel(meta_ref, src_one_ref, src_cur_ref, src_nxt_ref, dst_ref, h_ref,
                   wg_ref, wu_ref, wd_ref, ys_ref, xbuf_ref, obuf_ref, wgf_ref, wuf_ref, wdf_ref,
                   wgbf_ref, wubf_ref, wdbf_ref, gsem_ref, ssem_ref, wsem_ref):
    i = pl.program_id(0)
    n_tiles = pl.num_programs(0)
    n_used = meta_ref[1]
    slot = i % 2
    nj, tm = xbuf_ref.shape[1], xbuf_ref.shape[2]
    layer = meta_ref[0]
    expert = meta_ref[2 + i]
    next_expert = meta_ref[2 + n_tiles + i]
    wset = meta_ref[2 + 2 * n_tiles + i]

    def weight_copies(e, s):
        return [pltpu.make_async_copy(w.at[layer, e], buf.at[s], wsem_ref.at[s])
                for w, buf in ((wg_ref, wgf_ref), (wu_ref, wuf_ref), (wd_ref, wdf_ref))]

    def gather_copy(idx_ref, r, s):
        return pltpu.make_async_copy(h_ref.at[idx_ref[0, 0, r]], xbuf_ref.at[s, :, r, :],
                                     gsem_ref.at[s])

    def scatter_copy(idx_ref, r, s):
        return pltpu.make_async_copy(obuf_ref.at[s, :, r, :], ys_ref.at[idx_ref[0, 0, r]],
                                     ssem_ref.at[s])

    def gather_start(idx_ref, s):
        def body(r, c):
            gather_copy(idx_ref, r, s).start()
            return c
        lax.fori_loop(0, tm, body, 0, unroll=8)

    def gather_wait(s):
        pltpu.make_async_copy(obuf_ref.at[0], xbuf_ref.at[s], gsem_ref.at[s]).wait()

    def scatter_wait(s):
        pltpu.make_async_copy(xbuf_ref.at[0], obuf_ref.at[s], ssem_ref.at[s]).wait()

    @pl.when(i == 0)
    def _():
        for cp in weight_copies(expert, wset):
            cp.start()
        gather_start(src_cur_ref, 0)

        @pl.when(n_used > 1)
        def _():
            gather_start(src_one_ref, 1)
        obuf_ref[1] = jnp.zeros(obuf_ref.shape[1:], jnp.uint32)
        n_rows = ys_ref.shape[0]

        def fill(r, c):
            pltpu.make_async_copy(obuf_ref.at[1, :, r % tm, :], ys_ref.at[n_rows - 2 * tm + r],
                                  ssem_ref.at[1]).start()
            return c
        lax.fori_loop(0, 2 * tm, fill, 0, unroll=8)
        scatter_wait(1)
        scatter_wait(1)

    @pl.when(i < n_used)
    def _():
        xslot = i % 3
        gather_wait(xslot)

        @pl.when(i >= 2)
        def _():
            scatter_wait(slot)

        @pl.when((i == 0) | (expert != meta_ref[2 + jnp.maximum(i - 1, 0)]))
        def _():
            for cp in weight_copies(expert, wset):
                cp.wait()
            wgbf_ref[...] = wgf_ref[wset].astype(BF16)
            wubf_ref[...] = wuf_ref[wset].astype(BF16)
            wdbf_ref[...] = wdf_ref[wset].astype(BF16)

            @pl.when(next_expert >= 0)
            def _():
                for cp in weight_copies(next_expert, 1 - wset):
                    cp.start()

        x = _unpack_rows([xbuf_ref[xslot, j] for j in range(nj)]).astype(BF16)
        a = _dot(x, wgbf_ref[...])
        u = _dot(x, wubf_ref[...])
        hid = (a * _sigmoid(a)) * u
        y = _dot(hid.astype(BF16), wdbf_ref[...])
        for j, word in enumerate(_pack_rows(y.astype(BF16).astype(F32))):
            obuf_ref[slot, j] = word
        for r in range(tm):
            scatter_copy(dst_ref, r, slot).start()

        @pl.when(i + 2 < n_used)
        def _():
            nslot = (i + 2) % 3
            for r in range(tm):
                gather_copy(src_nxt_ref, r, nslot).start()

        @pl.when(i == n_used - 1)
        def _():
            @pl.when(i >= 1)
            def _():
                scatter_wait(1 - slot)
            scatter_wait(slot)


def _experts(h3, src, dst, meta, w_gate, w_up, w_down, n_out):
    T, nj, _ = h3.shape
    D = 2 * nj * LANES
    F = w_gate.shape[-1]
    n_tiles, _, tm = src.shape
    idx_spec = lambda f: pl.BlockSpec((1, 1, tm), lambda i, m: (f(i), 0, 0), memory_space=pltpu.SMEM)
    hbm = pl.BlockSpec(memory_space=pl.ANY)
    return pl.pallas_call(
        _expert_kernel,
        grid_spec=pltpu.PrefetchScalarGridSpec(
            num_scalar_prefetch=1, grid=(n_tiles,),
            in_specs=[idx_spec(lambda i: min(1, n_tiles - 1)), idx_spec(lambda i: i),
                      idx_spec(lambda i: jnp.minimum(i + 2, n_tiles - 1)), idx_spec(lambda i: i),
                      hbm, hbm, hbm, hbm],
            out_specs=hbm,
            scratch_shapes=[pltpu.VMEM((3, nj, tm, LANES), jnp.uint32),
                            pltpu.VMEM((2, nj, tm, LANES), jnp.uint32),
                            pltpu.VMEM((2, D, F), F32), pltpu.VMEM((2, D, F), F32),
                            pltpu.VMEM((2, F, D), F32),
                            pltpu.VMEM((D, F), BF16), pltpu.VMEM((D, F), BF16),
                            pltpu.VMEM((F, D), BF16),
                            pltpu.SemaphoreType.DMA((3,)), pltpu.SemaphoreType.DMA((2,)),
                            pltpu.SemaphoreType.DMA((2,))]),
        out_shape=jax.ShapeDtypeStruct((n_out, nj, LANES), jnp.uint32),
        compiler_params=_params("arbitrary"),
        name="grouped_experts",
    )(meta, src, src, src, dst, h3, w_gate, w_up, w_down)


def _combine_kernel(x_ref, gt_ref, route_ref, y0_ref, y1_ref, g_ref, sh_ref, sc_ref, o_ref, h_ref):
    rows, d = x_ref.shape
    nw = d // (2 * LANES)
    w1 = jnp.broadcast_to(route_ref[:, 2:3], (rows, LANES))
    w2 = jnp.broadcast_to(route_ref[:, 3:4], (rows, LANES))
    for j in range(nw):
        u0 = y0_ref[pl.ds(j, rows, stride=nw), :]
        u1 = y1_ref[pl.ds(j, rows, stride=nw), :]
        for blk, half in ((j, lambda u: u << 16), (nw + j, lambda u: u & jnp.uint32(0xFFFF0000))):
            js = slice(blk * LANES, (blk + 1) * LANES)
            y = (w1 * lax.bitcast_convert_type(half(u0), F32)
                 + w2 * lax.bitcast_convert_type(half(u1), F32))
            o_ref[:, js] = x_ref[:, js] + gt_ref[:, js] * y
    h_ref[...] = _ada_norm(o_ref[...], g_ref[...], sc_ref[...], sh_ref[...]).astype(h_ref.dtype)


def _combine(x, ys, route, mod3, g_next, mod3_next, seq):
    T, D = x.shape
    rows = COMBINE_ROWS
    nblk = T // rows
    nj = D // (2 * LANES)
    batch_of = lambda i: i * rows // seq
    return pl.pallas_call(
        _combine_kernel,
        grid=(nblk,),
        in_specs=[pl.BlockSpec((rows, D), lambda i: (i, 0)),
                  pl.BlockSpec((None, 1, D), lambda i: (batch_of(i), 0, 5)),
                  pl.BlockSpec((rows, LANES), lambda i: (i, 0)),
                  pl.BlockSpec((rows * nj, LANES), lambda i: (i, 0)),
                  pl.BlockSpec((rows * nj, LANES), lambda i: (nblk + i, 0)),
                  pl.BlockSpec((1, D), lambda i: (0, 0)),
                  pl.BlockSpec((None, 1, D), lambda i: (batch_of(i), 0, 0)),
                  pl.BlockSpec((None, 1, D), lambda i: (batch_of(i), 0, 1))],
        out_specs=[pl.BlockSpec((rows, D), lambda i: (i, 0)),
                   pl.BlockSpec((rows, D), lambda i: (i, 0))],
        out_shape=[jax.ShapeDtypeStruct((T, D), F32), jax.ShapeDtypeStruct((T, D), BF16)],
        compiler_params=_params("parallel"),
        name="moe_combine",
    )(x, mod3, route, ys, ys, g_next, mod3_next, mod3_next)


def _invert_kernel(pos_ref, asg_ref, fill_ref, sem_ref):
    i = pl.program_id(0)
    blk = pos_ref.shape[0]

    @pl.when(i == 0)
    def _():
        fill_ref[...] = jnp.full(fill_ref.shape, -1, jnp.int32)
        cp = pltpu.make_async_copy(fill_ref, asg_ref, sem_ref)
        cp.start()
        cp.wait()

    base = i * blk

    def body(a, c):
        asg_ref[pos_ref[a]] = base + a
        return c
    lax.fori_loop(0, blk, body, 0, unroll=8)


def _invert_slots(pos, n_slots):
    n, = pos.shape
    blk = INVERT_BLOCK
    return pl.pallas_call(
        _invert_kernel,
        grid=(n // blk,),
        in_specs=[pl.BlockSpec((blk,), lambda i: (i,), memory_space=pltpu.SMEM)],
        out_specs=pl.BlockSpec((n_slots,), lambda i: (0,), memory_space=pltpu.SMEM),
        out_shape=jax.ShapeDtypeStruct((n_slots,), jnp.int32),
        scratch_shapes=[pltpu.VMEM((n_slots,), jnp.int32), pltpu.SemaphoreType.DMA(())],
        compiler_params=_params("arbitrary"),
        name="invert_slots",
    )(pos)


def _dispatch_plan(route, l):
    T = route.shape[0]
    tm = EXPERT_TM
    n_tiles = (2 * T) // tm + N_EXPERTS
    P = n_tiles * tm
    e = route[:, 0:2].astype(jnp.int32).T.reshape(-1)
    onehot = (e[:, None] == jnp.arange(N_EXPERTS, dtype=jnp.int32)[None, :]).astype(jnp.int32)
    csum = jnp.cumsum(onehot, axis=0)
    counts = csum[-1]
    tiles_per = (counts + tm - 1) // tm
    tile_end = jnp.cumsum(tiles_per)
    off = (tile_end - tiles_per) * tm
    pos = jnp.sum((csum - onehot + off[None, :]) * onehot, axis=1)
    asg = _invert_slots(pos.astype(jnp.int32), P)
    slot = jnp.arange(P, dtype=jnp.int32)
    spare = 2 * T + ((slot // tm) % 2) * tm + slot % tm
    src = jnp.where(asg >= 0, asg % T, 0).reshape(n_tiles, 1, tm)
    dst = jnp.where(asg >= 0, asg, spare).reshape(n_tiles, 1, tm)
    tile_ids = jnp.arange(n_tiles, dtype=jnp.int32)
    tile_expert = jnp.minimum(jnp.sum(tile_ids[:, None] >= tile_end[None, :], axis=1),
                              N_EXPERTS - 1).astype(jnp.int32)
    ids = jnp.arange(N_EXPERTS, dtype=jnp.int32)
    used = counts > 0
    later = lax.cummin(jnp.where(used, ids, N_EXPERTS)[::-1])[::-1]
    nxt = jnp.concatenate([later[1:], jnp.full((1,), N_EXPERTS, jnp.int32)])
    nxt = jnp.where(nxt < N_EXPERTS, nxt, -1).astype(jnp.int32)
    wset = ((jnp.cumsum(used.astype(jnp.int32)) - 1) % 2).astype(jnp.int32)
    of_tile = tile_expert[:, None] == ids[None, :]
    nxt_tile = jnp.sum(jnp.where(of_tile, nxt[None, :], 0), axis=1)
    wset_tile = jnp.sum(jnp.where(of_tile, wset[None, :], 0), axis=1)
    meta = jnp.concatenate([jnp.stack([l, tile_end[-1].astype(jnp.int32)]), tile_expert,
                            nxt_tile, wset_tile])
    return src, dst, meta


def kernel(x, c, positions, w_ada, b_ada, g_norm_mix, g_norm_ffn, w_in, w_fg, b_fg, g_gla_out,
           g_q, g_k, w_gla_proj, w_attn_proj, w_out, w_route_group, b_route_group,
           w_route_expert, b_route_expert, w_exp_gate, w_exp_up, w_exp_down):
    B, S, D = x.shape
    L = w_ada.shape[0]
    T = B * S
    dk_all = w_fg.shape[-1]
    dv_all = g_gla_out.shape[-1]
    att_w = N_DIL * ATT_HEADS * HEAD_DIM
    gla_cols = 2 * dk_all + 2 * dv_all

    mod = _modulation(c, w_ada, b_ada)
    cos2, sin2 = _rope_tables(positions)

    wfg_pad = jnp.pad(w_fg, ((0, 0), (0, LANES - GLA_RANK), (0, 0)))
    n_route = N_GROUPS + N_EXPERTS
    wr_pad = jnp.pad(jnp.concatenate([w_route_group, w_route_expert], axis=-1),
                     ((0, 0), (0, 0), (0, LANES - n_route)))
    br_pad = jnp.pad(jnp.concatenate([b_route_group, b_route_expert], axis=-1),
                     ((0, 0), (0, LANES - n_route))).reshape(L, 1, LANES)
    g_mix = g_norm_mix.reshape(L, 1, D)
    g_ffn = g_norm_ffn.reshape(L, 1, D)
    b_fg3 = b_fg.reshape(L, 1, dk_all)
    g_gla3 = g_gla_out.reshape(L, 1, dv_all)
    g_q3 = g_q.reshape(L, 1, HEAD_DIM)
    g_k3 = g_k.reshape(L, 1, HEAD_DIM)

    def residual(acc, xres, gt):
        return xres + gt * acc

    def mod_of(l):
        return lax.dynamic_index_in_dim(mod, l, 0, keepdims=False).reshape(8, 1, N_MOD * D)

    def layer(l, carry):
        xt, h = carry
        l = jnp.asarray(l, jnp.int32)
        mod3 = mod_of(l)
        proj1 = _matmul(h, w_in, l, col0=0, n=gla_cols, out_dtype=BF16, tn=2 * MM_TN,
                        tm=2 * MM_TM, name="proj_gla")
        fg = _matmul(h, w_in, l, col0=gla_cols, n=LANES, tn=LANES, out_dtype=F32, name="proj_fg")
        proj2 = _matmul(h, w_in, l, col0=gla_cols, shift=GLA_RANK, n=3 * att_w,
                        out_dtype=BF16, tm=2 * MM_TM, name="proj_att")
        gates = _matmul(h, w_in, l, col0=gla_cols + 3 * att_w, shift=GLA_RANK, n=2 * D,
                        out_dtype=BF16, tm=2 * MM_TM, name="proj_gates")
        o_gla = _gla(proj1, fg, l, wfg_pad, b_fg3, g_gla3, B, S)
        prep = _qkprep(proj2, l, cos2, sin2, g_q3, g_k3)
        outs, lses = [], []
        for grp, (_, dil) in enumerate(DIL_GROUPS):
            o, lse = _attn_group(prep[grp], prep[N_DIL + grp], prep[2 * N_DIL + grp], dil, B, S)
            outs.append(o)
            lses.append(lse)
        o_att = _attn_mix(outs, lses)
        merged = _merge(o_gla, o_att, gates, l, w_gla_proj, w_attn_proj)
        tn_out = 2 * MM_TN
        gt_spec = pl.BlockSpec((None, 1, tn_out),
                               lambda j, i, l: (i * MM_TM // S, 0, 2 * (D // tn_out) + j))
        xt = _matmul(merged, w_out, l, col0=0, n=D, out_dtype=F32, tn=tn_out, epilogue=residual,
                     extras=(xt, mod3),
                     extra_specs=(pl.BlockSpec((MM_TM, tn_out), lambda j, i, l: (i, j)), gt_spec),
                     name="out_proj")
        h2, route = _router(xt, l, g_ffn, mod3, wr_pad, br_pad, S)
        src, dst, meta = _dispatch_plan(route, l)
        nj = D // (2 * LANES)
        n_out = 2 * T + 2 * EXPERT_TM
        ys = _experts(h2.reshape(T, nj, LANES), src, dst, meta, w_exp_gate, w_exp_up, w_exp_down,
                      n_out)
        l_next = jnp.minimum(l + 1, L - 1)
        g_next = lax.dynamic_index_in_dim(g_mix, l_next, 0, keepdims=False)
        return tuple(_combine(xt, ys.reshape(n_out * nj, LANES), route, mod3, g_next,
                              mod_of(l_next), S))

    x0 = x.reshape(T, D)
    zero = jnp.int32(0)
    h0, x_init = _norm(x0, zero, g_mix, mod_of(zero), 0, S)
    xt, _ = lax.fori_loop(0, L, layer, (x_init, h0))
    return xt.reshape(B, S, D)
```

```python
import functools

import jax
import jax.numpy as jnp
from jax import lax
from jax.experimental import pallas as pl
from jax.experimental.pallas import tpu as pltpu

F32 = jnp.float32
BF16 = jnp.bfloat16

GLA_HEADS = 4
GLA_RANK = 16
GLA_TAU = 16.0
GLA_CHUNK = 64
DIL_GROUPS = ((128, 1), (512, 4), (2048, 16))
N_DIL = 3
ATT_HEADS = 4
HEAD_DIM = 128
ATT_BLOCK = 128
ROPE_THETA = 10000.0
N_GROUPS = 4
EXPERTS_PER_GROUP = 8
N_EXPERTS = N_GROUPS * EXPERTS_PER_GROUP
N_MOD = 6
EPS = 1e-6

LANES = 128
VMEM_LIMIT = 56 * 1024 * 1024
NEG_BIG = -1e30

MOD_TN = 2048
MOD_SPLIT = 4
MOD_CHUNK = 64
NORM_ROWS = 512
MM_TM = 1024
MM_TN = 512
GLA_ROWS = 512
PREP_ROWS = 512
ATT_SPAN = 2048
MERGE_ROWS = 1024
EXPERT_TM = 256
COMBINE_ROWS = 512
INVERT_BLOCK = 4096


def _params(*sem):
    return pltpu.CompilerParams(dimension_semantics=sem, vmem_limit_bytes=VMEM_LIMIT)


def _dot(a, b):
    return jnp.dot(a, b, preferred_element_type=F32)


def _dot_nt(a, b):
    return lax.dot_general(a, b, (((1,), (1,)), ((), ())), preferred_element_type=F32)


def _dot_tn(a, b):
    return lax.dot_general(a, b, (((0,), (0,)), ((), ())), preferred_element_type=F32)


def _split_bf16(x):
    hi = x.astype(BF16)
    lo = (x - hi.astype(F32)).astype(BF16)
    return hi, lo


def _sigmoid(x):
    return 1.0 / (1.0 + jnp.exp(-x))


def _pack_rows(xb):
    bits = lax.bitcast_convert_type(xb, jnp.uint32)
    nw = xb.shape[1] // (2 * LANES)
    out = []
    for j in range(nw):
        lo = bits[:, j * LANES:(j + 1) * LANES] >> 16
        hi = bits[:, (nw + j) * LANES:(nw + j + 1) * LANES] & jnp.uint32(0xFFFF0000)
        out.append(lo | hi)
    return out


def _unpack_rows(words):
    lo = [lax.bitcast_convert_type(w << 16, F32) for w in words]
    hi = [lax.bitcast_convert_type(w & jnp.uint32(0xFFFF0000), F32) for w in words]
    return jnp.concatenate(lo + hi, axis=1)


def _grid_call(kernel, l, grid, in_specs, out_specs, out_shape, scratch=(), sem=None, name=None):
    sem = sem or ("arbitrary",) * len(grid)
    return pl.pallas_call(
        kernel,
        grid_spec=pltpu.PrefetchScalarGridSpec(
            num_scalar_prefetch=1, grid=grid, in_specs=in_specs, out_specs=out_specs,
            scratch_shapes=list(scratch)),
        out_shape=out_shape,
        compiler_params=_params(*sem),
        name=name,
    ), jnp.reshape(l, (1,)).astype(jnp.int32)


def _mod_kernel(c_ref, *refs):
    w_refs, (b_ref, o_ref, act_ref, acc_ref) = refs[:MOD_SPLIT], refs[MOD_SPLIT:]
    n_batch, d = c_ref.shape[0], c_ref.shape[1]
    tn = o_ref.shape[1]
    ch = MOD_CHUNK

    @pl.when((pl.program_id(0) == 0) & (pl.program_id(1) == 0))
    def _():
        c = c_ref[...]
        act_ref[...] = c * _sigmoid(c)

    acc_ref[...] = jnp.zeros(acc_ref.shape, F32)

    def chunk(ci, carry):
        r0 = pl.multiple_of(ci * ch, ch)
        acts = [act_ref[b, pl.ds(r0, ch), :] for b in range(n_batch)]
        col = 0
        for w_ref in w_refs:
            for jb in range(w_ref.shape[1] // LANES):
                w = w_ref[pl.ds(r0, ch), jb * LANES:(jb + 1) * LANES]
                for b in range(n_batch):
                    part = (w * acts[b]).reshape(ch // 8, 8, LANES).sum(axis=0)
                    acc_ref[b, :, col:col + LANES] += part
                col += LANES
        return carry

    lax.fori_loop(0, d // ch, chunk, 0)
    out = jnp.concatenate([jnp.sum(acc_ref[b], axis=0, keepdims=True) for b in range(n_batch)],
                          axis=0) + b_ref[...]
    o_ref[0:n_batch, :] = out
    o_ref[n_batch:, :] = jnp.zeros((o_ref.shape[0] - n_batch, tn), F32)


def _modulation(c, w_ada, b_ada):
    L, D, N = w_ada.shape
    B = c.shape[0]
    rows = 8
    assert B <= rows
    c_rep = jnp.broadcast_to(c[:, :, None], (B, D, LANES))
    tn = MOD_TN
    ts = tn // MOD_SPLIT
    w_specs = [pl.BlockSpec((None, D, ts), lambda l, j, k=k: (l, 0, j * MOD_SPLIT + k))
               for k in range(MOD_SPLIT)]
    return pl.pallas_call(
        _mod_kernel,
        grid=(L, N // tn),
        in_specs=[pl.BlockSpec((B, D, LANES), lambda l, j: (0, 0, 0))] + w_specs +
                 [pl.BlockSpec((None, 1, tn), lambda l, j: (l, 0, j))],
        out_specs=pl.BlockSpec((None, rows, tn), lambda l, j: (l, 0, j)),
        out_shape=jax.ShapeDtypeStruct((L, rows, N), F32),
        scratch_shapes=[pltpu.VMEM((B, D, LANES), F32), pltpu.VMEM((B, 8, tn), F32)],
        compiler_params=_params("arbitrary", "arbitrary"),
        name="adaln_mod",
    )(c_rep, *([w_ada] * MOD_SPLIT), b_ada.reshape(L, 1, N))


def _rope_kernel(pos_ref, freq_ref, cos_ref, sin_ref):
    ang = pos_ref[...].astype(F32) * freq_ref[...]
    lane = lax.broadcasted_iota(jnp.int32, ang.shape, 1)
    cos_ref[...] = jnp.cos(ang)
    sin_ref[...] = jnp.where(lane < HEAD_DIM // 2, -jnp.sin(ang), jnp.sin(ang))


def _rope_tables(positions):
    T = positions.size
    inv_freq = ROPE_THETA ** (-jnp.arange(0, HEAD_DIM, 2, dtype=F32) / HEAD_DIM)
    freq2 = jnp.concatenate([inv_freq, inv_freq]).reshape(1, HEAD_DIM)
    rows = 2048
    return pl.pallas_call(
        _rope_kernel,
        grid=(T // rows,),
        in_specs=[pl.BlockSpec((rows, 1), lambda i: (i, 0)),
                  pl.BlockSpec((1, HEAD_DIM), lambda i: (0, 0))],
        out_specs=[pl.BlockSpec((rows, HEAD_DIM), lambda i: (i, 0))] * 2,
        out_shape=[jax.ShapeDtypeStruct((T, HEAD_DIM), F32)] * 2,
        compiler_params=_params("parallel"),
        name="rope_tables",
    )(positions.reshape(T, 1), freq2)


def _ada_norm(x, g, scale, shift):
    y = x * lax.rsqrt(jnp.mean(x * x, axis=-1, keepdims=True) + EPS)
    return y * g * (1.0 + scale) + shift


def _norm_kernel(l_ref, x_ref, g_ref, sh_ref, sc_ref, o_ref, xcopy_ref):
    x = x_ref[...]
    o_ref[...] = _ada_norm(x, g_ref[...], sc_ref[...], sh_ref[...]).astype(o_ref.dtype)
    xcopy_ref[...] = x


def _mod_spec(which, rows, seq):
    return lambda i, l: (i * rows // seq, 0, which)


def _norm(x, l, g_all, mod3, which_shift, seq):
    T, D = x.shape
    rows = NORM_ROWS
    call, lidx = _grid_call(
        _norm_kernel, l, (T // rows,),
        [pl.BlockSpec((rows, D), lambda i, l: (i, 0)),
         pl.BlockSpec((None, 1, D), lambda i, l: (l[0], 0, 0)),
         pl.BlockSpec((None, 1, D), _mod_spec(which_shift, rows, seq)),
         pl.BlockSpec((None, 1, D), _mod_spec(which_shift + 1, rows, seq))],
        [pl.BlockSpec((rows, D), lambda i, l: (i, 0))] * 2,
        [jax.ShapeDtypeStruct((T, D), BF16), jax.ShapeDtypeStruct((T, D), x.dtype)],
        sem=("parallel",), name="ada_norm")
    return call(lidx, x, g_all, mod3, mod3)


def _mm_kernel(l_ref, a_ref, w_ref, *rest, epilogue, n_extra, shift):
    n_w = 1 if shift else 0
    extra = rest[n_w:n_w + n_extra]
    o_ref = rest[n_w + n_extra]
    wbf_ref = rest[n_w + n_extra + 1]
    tn = o_ref.shape[1]

    @pl.when(pl.program_id(1) == 0)
    def _():
        if shift:
            wide = jnp.concatenate([w_ref[...], rest[0][...]], axis=1)
            wbf_ref[...] = wide[:, shift:shift + tn].astype(BF16)
        else:
            wbf_ref[...] = w_ref[...].astype(BF16)

    acc = _dot(a_ref[...], wbf_ref[...])
    if epilogue is not None:
        acc = epilogue(acc, *[e[...] for e in extra])
    o_ref[...] = acc.astype(o_ref.dtype)


def _matmul(a, w, l, *, col0, n, out_dtype, tn=MM_TN, tm=MM_TM, shift=0, epilogue=None,
            extras=(), extra_specs=(), name="matmul"):
    M, K = a.shape
    assert col0 % tn == 0 and n % tn == 0 and M % tm == 0 and 0 <= shift < LANES
    cb0 = col0 // tn
    kern = functools.partial(_mm_kernel, epilogue=epilogue, n_extra=len(extras), shift=shift)
    w_specs = [pl.BlockSpec((None, K, tn), lambda j, i, l: (l[0], 0, cb0 + j))]
    if shift:
        per = tn // LANES
        w_specs.append(pl.BlockSpec((None, K, LANES), lambda j, i, l: (l[0], 0, (cb0 + j + 1) * per)))
    call, lidx = _grid_call(
        kern, l, (n // tn, M // tm),
        [pl.BlockSpec((tm, K), lambda j, i, l: (i, 0))] + w_specs + list(extra_specs),
        pl.BlockSpec((tm, tn), lambda j, i, l: (i, j)),
        jax.ShapeDtypeStruct((M, n), out_dtype),
        scratch=[pltpu.VMEM((K, tn), BF16)], name=name)
    return call(lidx, a, *([w] * len(w_specs)), *extras)


def _gla_kernel(l_ref, q_ref, k_ref, v_ref, r_ref, fg_ref, wfg_ref, bfg_ref, g_ref, o_ref,
                state_ref, *, head_k, head_v):
    rows = q_ref.shape[0]
    C = GLA_CHUNK

    @pl.when(pl.program_id(1) == 0)
    def _():
        state_ref[...] = jnp.zeros_like(state_ref)

    fh, fl = _split_bf16(fg_ref[...])
    wh, wl = _split_bf16(wfg_ref[...])
    z = _dot(fh, wh) + _dot(fl, wh) + _dot(fh, wl) + bfg_ref[...]
    log_a = (jnp.minimum(z, 0.0) - jnp.log(1.0 + jnp.exp(-jnp.abs(z)))) * (1.0 / GLA_TAU)

    ri = lax.broadcasted_iota(jnp.int32, (C, C), 0)
    ci = lax.broadcasted_iota(jnp.int32, (C, C), 1)
    causal = ri >= ci
    tri = causal.astype(BF16)
    scale = head_k ** -0.5

    for c in range(rows // C):
        rs = slice(c * C, (c + 1) * C)
        lh, ll = _split_bf16(log_a[rs])
        b = _dot(tri, lh) + _dot(tri, ll)
        b_last = b[C - 1:C]
        q = q_ref[rs, :].astype(F32) * scale
        k = k_ref[rs, :].astype(F32)
        q_dec = (q * jnp.exp(b)).astype(BF16)
        k_inv = (k * jnp.exp(-b)).astype(BF16)
        k_upd = (k * jnp.exp(b_last - b)).astype(BF16)
        decay = jnp.exp(b_last)
        for h in range(GLA_HEADS):
            ks = slice(h * head_k, (h + 1) * head_k)
            vs = slice(h * head_v, (h + 1) * head_v)
            v = v_ref[rs, vs]
            att = jnp.where(causal, _dot_nt(q_dec[:, ks], k_inv[:, ks]), 0.0).astype(BF16)
            st = state_ref[h]
            o = _dot(att, v) + _dot_nt(q_dec[:, ks], st.astype(BF16))
            state_ref[h] = st * decay[:, ks] + _dot_tn(v, k_upd[:, ks])
            o = o * lax.rsqrt(jnp.mean(o * o, axis=-1, keepdims=True) + EPS) * g_ref[:, vs]
            r = r_ref[rs, vs].astype(F32)
            o_ref[rs, vs] = (o * (r * _sigmoid(r))).astype(o_ref.dtype)


def _gla(proj, fg, l, wfg_pad, b_fg, g_gla_out, batch, seq):
    T = proj.shape[0]
    dk_all = wfg_pad.shape[-1]
    dv_all = g_gla_out.shape[-1]
    rows = GLA_ROWS
    nblk = seq // rows
    kern = functools.partial(_gla_kernel, head_k=dk_all // GLA_HEADS, head_v=dv_all // GLA_HEADS)
    row = lambda b, n, l: b * nblk + n
    call, lidx = _grid_call(
        kern, l, (batch, nblk),
        [pl.BlockSpec((rows, dk_all), lambda b, n, l: (row(b, n, l), 0)),
         pl.BlockSpec((rows, dk_all), lambda b, n, l: (row(b, n, l), 1)),
         pl.BlockSpec((rows, dv_all), lambda b, n, l: (row(b, n, l), 1)),
         pl.BlockSpec((rows, dv_all), lambda b, n, l: (row(b, n, l), 2)),
         pl.BlockSpec((rows, LANES), lambda b, n, l: (row(b, n, l), 0)),
         pl.BlockSpec((None, LANES, dk_all), lambda b, n, l: (l[0], 0, 0)),
         pl.BlockSpec((None, 1, dk_all), lambda b, n, l: (l[0], 0, 0)),
         pl.BlockSpec((None, 1, dv_all), lambda b, n, l: (l[0], 0, 0))],
        pl.BlockSpec((rows, dv_all), lambda b, n, l: (row(b, n, l), 0)),
        jax.ShapeDtypeStruct((T, dv_all), BF16),
        scratch=[pltpu.VMEM((GLA_HEADS, dv_all // GLA_HEADS, dk_all // GLA_HEADS), F32)],
        sem=("parallel", "arbitrary"), name="gla")
    return call(lidx, proj, proj, proj, proj, fg, wfg_pad, b_fg, g_gla_out)


def _qkprep_kernel(l_ref, q_ref, k_ref, v_ref, cos_ref, sin_ref, gq_ref, gk_ref, *refs):
    out_refs, scr_ref = refs[:-1], refs[-1]
    cos = cos_ref[...]
    sin = sin_ref[...]
    gw = ATT_HEADS * HEAD_DIM
    rows = q_ref.shape[0]

    def tables(g_ref, scale):
        g = jnp.broadcast_to(g_ref[...], cos.shape)
        return g * cos * scale, pltpu.roll(g, HEAD_DIM // 2, 1) * sin * scale

    def prep(ref, tab):
        x = ref.astype(F32)
        n = x * lax.rsqrt(jnp.mean(x * x, axis=-1, keepdims=True) + EPS)
        return n * tab[0] + pltpu.roll(n, HEAD_DIM // 2, 1) * tab[1]

    def emit(dst_ref, dil):
        n = rows // dil
        for p in range(dil):
            for h in range(ATT_HEADS):
                cs = slice(p * gw + h * HEAD_DIM, p * gw + (h + 1) * HEAD_DIM)
                dst_ref[:, cs] = scr_ref[h, pl.ds(p, n, stride=dil), :].astype(BF16)

    tab_q = tables(gq_ref, HEAD_DIM ** -0.5)
    tab_k = tables(gk_ref, 1.0)
    for grp, (_, dil) in enumerate(DIL_GROUPS):
        for ref, tab, out in ((q_ref, tab_q, out_refs[grp]), (k_ref, tab_k, out_refs[N_DIL + grp])):
            for h in range(ATT_HEADS):
                src = slice(grp * gw + h * HEAD_DIM, grp * gw + (h + 1) * HEAD_DIM)
                y = prep(ref[:, src], tab)
                if dil == 1:
                    out[:, h * HEAD_DIM:(h + 1) * HEAD_DIM] = y.astype(BF16)
                else:
                    scr_ref[h] = y
            if dil > 1:
                emit(out, dil)
        if dil == 1:
            out_refs[2 * N_DIL + grp][...] = v_ref[:, grp * gw:(grp + 1) * gw]
        else:
            for h in range(ATT_HEADS):
                src = slice(grp * gw + h * HEAD_DIM, grp * gw + (h + 1) * HEAD_DIM)
                scr_ref[h] = v_ref[:, src].astype(F32)
            emit(out_refs[2 * N_DIL + grp], dil)


def _qkprep(proj2, l, cos2, sin2, g_q, g_k):
    T = proj2.shape[0]
    rows = PREP_ROWS
    width = N_DIL * ATT_HEADS * HEAD_DIM
    gw = ATT_HEADS * HEAD_DIM
    dils = [dil for _, dil in DIL_GROUPS] * 3
    call, lidx = _grid_call(
        _qkprep_kernel, l, (T // rows,),
        [pl.BlockSpec((rows, width), lambda i, l: (i, 0)),
         pl.BlockSpec((rows, width), lambda i, l: (i, 1)),
         pl.BlockSpec((rows, width), lambda i, l: (i, 2)),
         pl.BlockSpec((rows, HEAD_DIM), lambda i, l: (i, 0)),
         pl.BlockSpec((rows, HEAD_DIM), lambda i, l: (i, 0)),
         pl.BlockSpec((None, 1, HEAD_DIM), lambda i, l: (l[0], 0, 0)),
         pl.BlockSpec((None, 1, HEAD_DIM), lambda i, l: (l[0], 0, 0))],
        [pl.BlockSpec((rows // d, d * gw), lambda i, l: (i, 0)) for d in dils],
        [jax.ShapeDtypeStruct((T // d, d * gw), BF16) for d in dils],
        scratch=[pltpu.VMEM((ATT_HEADS, rows, HEAD_DIM), F32)],
        sem=("parallel",), name="qk_prep")
    return call(lidx, proj2, proj2, proj2, cos2, sin2, g_q, g_k)


def _attn_kernel(q_ref, kc_ref, kp_ref, vc_ref, vp_ref, o_ref, lse_ref, *, dil):
    first = pl.program_id(1) == 0
    QB = q_ref.shape[0]
    A = ATT_BLOCK
    qi = lax.broadcasted_iota(jnp.int32, (A, 2 * A), 0)
    kj = lax.broadcasted_iota(jnp.int32, (A, 2 * A), 1) - A
    dist = qi - kj
    valid = (dist >= 0) & (dist <= A)
    bias = jnp.where(valid, 0.0, NEG_BIG).astype(F32)
    bias_first = jnp.where(first, jnp.where(valid & (kj >= 0), 0.0, NEG_BIG), bias).astype(F32)

    for p in range(dil):
        for h in range(ATT_HEADS):
            cs = slice((p * ATT_HEADS + h) * HEAD_DIM, (p * ATT_HEADS + h + 1) * HEAD_DIM)
            for j in range(QB // A):
                rs = slice(j * A, (j + 1) * A)
                if j == 0:
                    kcat = jnp.concatenate([kp_ref[:, cs], kc_ref[0:A, cs]], axis=0)
                    vcat = jnp.concatenate([vp_ref[:, cs], vc_ref[0:A, cs]], axis=0)
                    bb = bias_first
                else:
                    kcat = kc_ref[(j - 1) * A:(j + 1) * A, cs]
                    vcat = vc_ref[(j - 1) * A:(j + 1) * A, cs]
                    bb = bias
                s = _dot_nt(q_ref[rs, cs], kcat) + bb
                m = jnp.max(s, axis=-1, keepdims=True)
                e = jnp.exp(s - m)
                den = jnp.sum(e, axis=-1, keepdims=True)
                o = _dot(e.astype(BF16), vcat) / den
                o_ref[rs, cs] = o.astype(o_ref.dtype)
                lse_ref[rs, cs] = jnp.broadcast_to(m + jnp.log(den), (A, HEAD_DIM))


def _attn_group(q, k, v, dil, batch, seq):
    rows_all, W = q.shape
    A = ATT_BLOCK
    QB = ATT_SPAN // dil
    nsp = seq // ATT_SPAN
    per = QB // A
    cur = pl.BlockSpec((QB, W), lambda b, n: (b * nsp + n, 0))
    prev = pl.BlockSpec((A, W), lambda b, n: (jnp.maximum((b * nsp + n) * per - 1, b * nsp * per), 0))
    return pl.pallas_call(
        functools.partial(_attn_kernel, dil=dil),
        grid=(batch, nsp),
        in_specs=[cur, cur, prev, cur, prev],
        out_specs=[cur, cur],
        out_shape=[jax.ShapeDtypeStruct((rows_all, W), BF16),
                   jax.ShapeDtypeStruct((rows_all, W), F32)],
        compiler_params=_params("parallel", "arbitrary"),
        name=f"dilated_attn_r{dil}",
    )(q, k, k, v, v)


def _attn_mix_kernel(o0, o1, o2, l0, l1, l2, out_ref, so_ref, sl_ref):
    rows, gw = out_ref.shape
    o_refs, l_refs = (o0, o1, o2), (l0, l1, l2)
    for h in range(ATT_HEADS):
        hs = slice(h * HEAD_DIM, (h + 1) * HEAD_DIM)
        outs, lses = [], []
        for g, (_, dil) in enumerate(DIL_GROUPS):
            if dil == 1:
                outs.append(o_refs[g][:, hs].astype(F32))
                lses.append(l_refs[g][:, hs])
                continue
            n = rows // dil
            for p in range(dil):
                cs = slice(p * gw + h * HEAD_DIM, p * gw + (h + 1) * HEAD_DIM)
                so_ref[g, pl.ds(p, n, stride=dil), :] = o_refs[g][:, cs].astype(F32)
                sl_ref[g, pl.ds(p, n, stride=dil), :] = l_refs[g][:, cs]
            outs.append(so_ref[g])
            lses.append(sl_ref[g])
        m = jnp.maximum(jnp.maximum(lses[0], lses[1]), lses[2])
        ws = [jnp.exp(x - m) for x in lses]
        den = ws[0] + ws[1] + ws[2]
        out_ref[:, hs] = ((ws[0] * outs[0] + ws[1] * outs[1] + ws[2] * outs[2]) / den
                          ).astype(out_ref.dtype)


def _attn_mix(outs, lses):
    gw = ATT_HEADS * HEAD_DIM
    T = outs[0].shape[0]
    rows = PREP_ROWS
    dils = [dil for _, dil in DIL_GROUPS]
    specs = [pl.BlockSpec((rows // d, d * gw), lambda i: (i, 0)) for d in dils]
    return pl.pallas_call(
        _attn_mix_kernel,
        grid=(T // rows,),
        in_specs=specs + specs,
        out_specs=pl.BlockSpec((rows, gw), lambda i: (i, 0)),
        out_shape=jax.ShapeDtypeStruct((T, gw), BF16),
        scratch_shapes=[pltpu.VMEM((N_DIL, rows, HEAD_DIM), F32),
                        pltpu.VMEM((N_DIL, rows, HEAD_DIM), F32)],
        compiler_params=_params("parallel"),
        name="attn_mix",
    )(*outs, *lses)


def _merge_kernel(l_ref, ogla_ref, oatt_ref, gg_ref, ga_ref, wg_ref, wa_ref,
                  out_ref, wgbf_ref, wabf_ref):
    @pl.when(pl.program_id(1) == 0)
    def _():
        wgbf_ref[...] = wg_ref[...].astype(BF16)
        wabf_ref[...] = wa_ref[...].astype(BF16)

    a = _dot(ogla_ref[...], wgbf_ref[...])
    b = _dot(oatt_ref[...], wabf_ref[...])
    gg = _sigmoid(gg_ref[...].astype(F32))
    ga = _sigmoid(ga_ref[...].astype(F32))
    out_ref[...] = (gg * a + ga * b).astype(out_ref.dtype)


def _merge(o_gla, o_att, gates, l, w_gla_proj, w_attn_proj):
    T, dv = o_gla.shape
    gw = o_att.shape[1]
    D = w_gla_proj.shape[-1]
    tm, tn = MERGE_ROWS, 2 * MM_TN
    gcb = 0
    dcb = D // tn
    call, lidx = _grid_call(
        _merge_kernel, l, (D // tn, T // tm),
        [pl.BlockSpec((tm, dv), lambda j, i, l: (i, 0)),
         pl.BlockSpec((tm, gw), lambda j, i, l: (i, 0)),
         pl.BlockSpec((tm, tn), lambda j, i, l: (i, gcb + j)),
         pl.BlockSpec((tm, tn), lambda j, i, l: (i, gcb + dcb + j)),
         pl.BlockSpec((None, dv, tn), lambda j, i, l: (l[0], 0, j)),
         pl.BlockSpec((None, gw, tn), lambda j, i, l: (l[0], 0, j))],
        pl.BlockSpec((tm, tn), lambda j, i, l: (i, j)),
        jax.ShapeDtypeStruct((T, D), BF16),
        scratch=[pltpu.VMEM((dv, tn), BF16), pltpu.VMEM((gw, tn), BF16)], name="gated_merge")
    return call(lidx, o_gla, o_att, gates, gates, w_gla_proj, w_attn_proj)


def _router_kernel(l_ref, x_ref, g_ref, sh_ref, sc_ref, wr_ref, br_ref, h_ref, route_ref):
    h = _ada_norm(x_ref[...], g_ref[...], sc_ref[...], sh_ref[...])
    rows = h.shape[0]
    hb = h.astype(BF16)
    words = _pack_rows(hb.astype(F32))
    nw = len(words)
    for j in range(nw):
        h_ref[pl.ds(j, rows, stride=nw), :] = words[j]
    lg = _dot(hb, wr_ref[...].astype(BF16)) + br_ref[...]
    lane = lax.broadcasted_iota(jnp.int32, lg.shape, 1).astype(F32)
    big = float(4 * LANES)

    def first_argmax(vals):
        mx = jnp.max(vals, axis=-1, keepdims=True)
        idx = jnp.min(jnp.where(vals == mx, lane, big), axis=-1, keepdims=True)
        return mx, idx

    gl = jnp.where(lane < N_GROUPS, lg, NEG_BIG)
    gmax, gidx = first_argmax(gl)
    g_weight = 1.0 / jnp.sum(jnp.exp(gl - gmax), axis=-1, keepdims=True)
    lo = N_GROUPS + gidx * EXPERTS_PER_GROUP
    el = jnp.where((lane >= lo) & (lane < lo + EXPERTS_PER_GROUP), lg, NEG_BIG)
    m1, i1 = first_argmax(el)
    m2, i2 = first_argmax(jnp.where(lane == i1, NEG_BIG, el))
    e2 = jnp.exp(m2 - m1)
    w1 = g_weight / (1.0 + e2)
    w2 = g_weight * e2 / (1.0 + e2)
    route = jnp.where(lane == 0, i1 - N_GROUPS,
                      jnp.where(lane == 1, i2 - N_GROUPS,
                                jnp.where(lane == 2, w1, jnp.where(lane == 3, w2, 0.0))))
    route_ref[...] = route


def _router(x, l, g_all, mod3, wr_pad, br_pad, seq):
    T, D = x.shape
    rows = NORM_ROWS
    call, lidx = _grid_call(
        _router_kernel, l, (T // rows,),
        [pl.BlockSpec((rows, D), lambda i, l: (i, 0)),
         pl.BlockSpec((None, 1, D), lambda i, l: (l[0], 0, 0)),
         pl.BlockSpec((None, 1, D), _mod_spec(3, rows, seq)),
         pl.BlockSpec((None, 1, D), _mod_spec(4, rows, seq)),
         pl.BlockSpec((None, D, LANES), lambda i, l: (l[0], 0, 0)),
         pl.BlockSpec((None, 1, LANES), lambda i, l: (l[0], 0, 0))],
        [pl.BlockSpec((rows * (D // (2 * LANES)), LANES), lambda i, l: (i, 0)),
         pl.BlockSpec((rows, LANES), lambda i, l: (i, 0))],
        [jax.ShapeDtypeStruct((T * (D // (2 * LANES)), LANES), jnp.uint32),
         jax.ShapeDtypeStruct((T, LANES), F32)],
        sem=("parallel",), name="norm_router")
    return call(lidx, x, g_all, mod3, mod3, wr_pad, br_pad)


def _expert_kernel(meta_ref, src_one_ref, src_cur_ref, src_nxt_ref, dst_ref, h_ref,
                   wg_ref, wu_ref, wd_ref, ys_ref, xbuf_ref, obuf_ref, wgf_ref, wuf_ref, wdf_ref,
                   wgbf_ref, wubf_ref, wdbf_ref, gsem_ref, ssem_ref, wsem_ref):
    i = pl.program_id(0)
    n_tiles = pl.num_programs(0)
    n_used = meta_ref[1]
    slot = i % 2
    nj, tm = xbuf_ref.shape[1], xbuf_ref.shape[2]
    layer = meta_ref[0]
    expert = meta_ref[2 + i]
    next_expert = meta_ref[2 + n_tiles + i]
    wset = meta_ref[2 + 2 * n_tiles + i]

    def weight_copies(e, s):
        return [pltpu.make_async_copy(w.at[layer, e], buf.at[s], wsem_ref.at[s])
                for w, buf in ((wg_ref, wgf_ref), (wu_ref, wuf_ref), (wd_ref, wdf_ref))]

    def gather_copy(idx_ref, r, s):
        return pltpu.make_async_copy(h_ref.at[idx_ref[0, 0, r]], xbuf_ref.at[s, :, r, :],
                                     gsem_ref.at[s])

    def scatter_copy(idx_ref, r, s):
        return pltpu.make_async_copy(obuf_ref.at[s, :, r, :], ys_ref.at[idx_ref[0, 0, r]],
                                     ssem_ref.at[s])

    def gather_start(idx_ref, s):
        def body(r, c):
            gather_copy(idx_ref, r, s).start()
            return c
        lax.fori_loop(0, tm, body, 0, unroll=8)

    def gather_wait(s):
        pltpu.make_async_copy(obuf_ref.at[0], xbuf_ref.at[s], gsem_ref.at[s]).wait()

    def scatter_wait(s):
        pltpu.make_async_copy(xbuf_ref.at[0], obuf_ref.at[s], ssem_ref.at[s]).wait()

    @pl.when(i == 0)
    def _():
        for cp in weight_copies(expert, wset):
            cp.start()
        gather_start(src_cur_ref, 0)

        @pl.when(n_used > 1)
        def _():
            gather_start(src_one_ref, 1)
        obuf_ref[1] = jnp.zeros(obuf_ref.shape[1:], jnp.uint32)
        n_rows = ys_ref.shape[0]

        def fill(r, c):
            pltpu.make_async_copy(obuf_ref.at[1, :, r % tm, :], ys_ref.at[n_rows - 2 * tm + r],
                                  ssem_ref.at[1]).start()
            return c
        lax.fori_loop(0, 2 * tm, fill, 0, unroll=8)
        scatter_wait(1)
        scatter_wait(1)

    @pl.when(i < n_used)
    def _():
        xslot = i % 3
        gather_wait(xslot)

        @pl.when(i >= 2)
        def _():
            scatter_wait(slot)

        @pl.when((i == 0) | (expert != meta_ref[2 + jnp.maximum(i - 1, 0)]))
        def _():
            for cp in weight_copies(expert, wset):
                cp.wait()
            wgbf_ref[...] = wgf_ref[wset].astype(BF16)
            wubf_ref[...] = wuf_ref[wset].astype(BF16)
            wdbf_ref[...] = wdf_ref[wset].astype(BF16)

            @pl.when(next_expert >= 0)
            def _():
                for cp in weight_copies(next_expert, 1 - wset):
                    cp.start()

        x = _unpack_rows([xbuf_ref[xslot, j] for j in range(nj)]).astype(BF16)
        a = _dot(x, wgbf_ref[...])
        u = _dot(x, wubf_ref[...])
        hid = (a * _sigmoid(a)) * u
        y = _dot(hid.astype(BF16), wdbf_ref[...])
        for j, word in enumerate(_pack_rows(y.astype(BF16).astype(F32))):
            obuf_ref[slot, j] = word
        for r in range(tm):
            scatter_copy(dst_ref, r, slot).start()

        @pl.when(i + 2 < n_used)
        def _():
            nslot = (i + 2) % 3
            for r in range(tm):
                gather_copy(src_nxt_ref, r, nslot).start()

        @pl.when(i == n_used - 1)
        def _():
            @pl.when(i >= 1)
            def _():
                scatter_wait(1 - slot)
            scatter_wait(slot)


def _experts(h3, src, dst, meta, w_gate, w_up, w_down, n_out):
    T, nj, _ = h3.shape
    D = 2 * nj * LANES
    F = w_gate.shape[-1]
    n_tiles, _, tm = src.shape
    idx_spec = lambda f: pl.BlockSpec((1, 1, tm), lambda i, m: (f(i), 0, 0), memory_space=pltpu.SMEM)
    hbm = pl.BlockSpec(memory_space=pl.ANY)
    return pl.pallas_call(
        _expert_kernel,
        grid_spec=pltpu.PrefetchScalarGridSpec(
            num_scalar_prefetch=1, grid=(n_tiles,),
            in_specs=[idx_spec(lambda i: min(1, n_tiles - 1)), idx_spec(lambda i: i),
                      idx_spec(lambda i: jnp.minimum(i + 2, n_tiles - 1)), idx_spec(lambda i: i),
                      hbm, hbm, hbm, hbm],
            out_specs=hbm,
            scratch_shapes=[pltpu.VMEM((3, nj, tm, LANES), jnp.uint32),
                            pltpu.VMEM((2, nj, tm, LANES), jnp.uint32),
                            pltpu.VMEM((2, D, F), F32), pltpu.VMEM((2, D, F), F32),
                            pltpu.VMEM((2, F, D), F32),
                            pltpu.VMEM((D, F), BF16), pltpu.VMEM((D, F), BF16),
                            pltpu.VMEM((F, D), BF16),
                            pltpu.SemaphoreType.DMA((3,)), pltpu.SemaphoreType.DMA((2,)),
                            pltpu.SemaphoreType.DMA((2,))]),
        out_shape=jax.ShapeDtypeStruct((n_out, nj, LANES), jnp.uint32),
        compiler_params=_params("arbitrary"),
        name="grouped_experts",
    )(meta, src, src, src, dst, h3, w_gate, w_up, w_down)


def _combine_kernel(x_ref, gt_ref, route_ref, y0_ref, y1_ref, g_ref, sh_ref, sc_ref, o_ref, h_ref):
    rows, d = x_ref.shape
    nw = d // (2 * LANES)
    w1 = jnp.broadcast_to(route_ref[:, 2:3], (rows, LANES))
    w2 = jnp.broadcast_to(route_ref[:, 3:4], (rows, LANES))
    for j in range(nw):
        u0 = y0_ref[pl.ds(j, rows, stride=nw), :]
        u1 = y1_ref[pl.ds(j, rows, stride=nw), :]
        for blk, half in ((j, lambda u: u << 16), (nw + j, lambda u: u & jnp.uint32(0xFFFF0000))):
            js = slice(blk * LANES, (blk + 1) * LANES)
            y = (w1 * lax.bitcast_convert_type(half(u0), F32)
                 + w2 * lax.bitcast_convert_type(half(u1), F32))
            o_ref[:, js] = x_ref[:, js] + gt_ref[:, js] * y
    h_ref[...] = _ada_norm(o_ref[...], g_ref[...], sc_ref[...], sh_ref[...]).astype(h_ref.dtype)


def _combine(x, ys, route, mod3, g_next, mod3_next, seq):
    T, D = x.shape
    rows = COMBINE_ROWS
    nblk = T // rows
    nj = D // (2 * LANES)
    batch_of = lambda i: i * rows // seq
    return pl.pallas_call(
        _combine_kernel,
        grid=(nblk,),
        in_specs=[pl.BlockSpec((rows, D), lambda i: (i, 0)),
                  pl.BlockSpec((None, 1, D), lambda i: (batch_of(i), 0, 5)),
                  pl.BlockSpec((rows, LANES), lambda i: (i, 0)),
                  pl.BlockSpec((rows * nj, LANES), lambda i: (i, 0)),
                  pl.BlockSpec((rows * nj, LANES), lambda i: (nblk + i, 0)),
                  pl.BlockSpec((1, D), lambda i: (0, 0)),
                  pl.BlockSpec((None, 1, D), lambda i: (batch_of(i), 0, 0)),
                  pl.BlockSpec((None, 1, D), lambda i: (batch_of(i), 0, 1))],
        out_specs=[pl.BlockSpec((rows, D), lambda i: (i, 0)),
                   pl.BlockSpec((rows, D), lambda i: (i, 0))],
        out_shape=[jax.ShapeDtypeStruct((T, D), F32), jax.ShapeDtypeStruct((T, D), BF16)],
        compiler_params=_params("parallel"),
        name="moe_combine",
    )(x, mod3, route, ys, ys, g_next, mod3_next, mod3_next)


def _invert_kernel(pos_ref, asg_ref, fill_ref, sem_ref):
    i = pl.program_id(0)
    blk = pos_ref.shape[0]

    @pl.when(i == 0)
    def _():
        fill_ref[...] = jnp.full(fill_ref.shape, -1, jnp.int32)
        cp = pltpu.make_async_copy(fill_ref, asg_ref, sem_ref)
        cp.start()
        cp.wait()

    base = i * blk

    def body(a, c):
        asg_ref[pos_ref[a]] = base + a
        return c
    lax.fori_loop(0, blk, body, 0, unroll=8)


def _invert_slots(pos, n_slots):
    n, = pos.shape
    blk = INVERT_BLOCK
    return pl.pallas_call(
        _invert_kernel,
        grid=(n // blk,),
        in_specs=[pl.BlockSpec((blk,), lambda i: (i,), memory_space=pltpu.SMEM)],
        out_specs=pl.BlockSpec((n_slots,), lambda i: (0,), memory_space=pltpu.SMEM),
        out_shape=jax.ShapeDtypeStruct((n_slots,), jnp.int32),
        scratch_shapes=[pltpu.VMEM((n_slots,), jnp.int32), pltpu.SemaphoreType.DMA(())],
        compiler_params=_params("arbitrary"),
        name="invert_slots",
    )(pos)


def _dispatch_plan(route, l):
    T = route.shape[0]
    tm = EXPERT_TM
    n_tiles = (2 * T) // tm + N_EXPERTS
    P = n_tiles * tm
    e = route[:, 0:2].astype(jnp.int32).T.reshape(-1)
    onehot = (e[:, None] == jnp.arange(N_EXPERTS, dtype=jnp.int32)[None, :]).astype(jnp.int32)
    csum = jnp.cumsum(onehot, axis=0)
    counts = csum[-1]
    tiles_per = (counts + tm - 1) // tm
    tile_end = jnp.cumsum(tiles_per)
    off = (tile_end - tiles_per) * tm
    pos = jnp.sum((csum - onehot + off[None, :]) * onehot, axis=1)
    asg = _invert_slots(pos.astype(jnp.int32), P)
    slot = jnp.arange(P, dtype=jnp.int32)
    spare = 2 * T + ((slot // tm) % 2) * tm + slot % tm
    src = jnp.where(asg >= 0, asg % T, 0).reshape(n_tiles, 1, tm)
    dst = jnp.where(asg >= 0, asg, spare).reshape(n_tiles, 1, tm)
    tile_ids = jnp.arange(n_tiles, dtype=jnp.int32)
    tile_expert = jnp.minimum(jnp.sum(tile_ids[:, None] >= tile_end[None, :], axis=1),
                              N_EXPERTS - 1).astype(jnp.int32)
    ids = jnp.arange(N_EXPERTS, dtype=jnp.int32)
    used = counts > 0
    later = lax.cummin(jnp.where(used, ids, N_EXPERTS)[::-1])[::-1]
    nxt = jnp.concatenate([later[1:], jnp.full((1,), N_EXPERTS, jnp.int32)])
    nxt = jnp.where(nxt < N_EXPERTS, nxt, -1).astype(jnp.int32)
    wset = ((jnp.cumsum(used.astype(jnp.int32)) - 1) % 2).astype(jnp.int32)
    of_tile = tile_expert[:, None] == ids[None, :]
    nxt_tile = jnp.sum(jnp.where(of_tile, nxt[None, :], 0), axis=1)
    wset_tile = jnp.sum(jnp.where(of_tile, wset[None, :], 0), axis=1)
    meta = jnp.concatenate([jnp.stack([l, tile_end[-1].astype(jnp.int32)]), tile_expert,
                            nxt_tile, wset_tile])
    return src, dst, meta


def kernel(x, c, positions, w_ada, b_ada, g_norm_mix, g_norm_ffn, w_in, w_fg, b_fg, g_gla_out,
           g_q, g_k, w_gla_proj, w_attn_proj, w_out, w_route_group, b_route_group,
           w_route_expert, b_route_expert, w_exp_gate, w_exp_up, w_exp_down):
    B, S, D = x.shape
    L = w_ada.shape[0]
    T = B * S
    dk_all = w_fg.shape[-1]
    dv_all = g_gla_out.shape[-1]
    att_w = N_DIL * ATT_HEADS * HEAD_DIM
    gla_cols = 2 * dk_all + 2 * dv_all

    mod = _modulation(c, w_ada, b_ada)
    cos2, sin2 = _rope_tables(positions)

    wfg_pad = jnp.pad(w_fg, ((0, 0), (0, LANES - GLA_RANK), (0, 0)))
    n_route = N_GROUPS + N_EXPERTS
    wr_pad = jnp.pad(jnp.concatenate([w_route_group, w_route_expert], axis=-1),
                     ((0, 0), (0, 0), (0, LANES - n_route)))
    br_pad = jnp.pad(jnp.concatenate([b_route_group, b_route_expert], axis=-1),
                     ((0, 0), (0, LANES - n_route))).reshape(L, 1, LANES)
    g_mix = g_norm_mix.reshape(L, 1, D)
    g_ffn = g_norm_ffn.reshape(L, 1, D)
    b_fg3 = b_fg.reshape(L, 1, dk_all)
    g_gla3 = g_gla_out.reshape(L, 1, dv_all)
    g_q3 = g_q.reshape(L, 1, HEAD_DIM)
    g_k3 = g_k.reshape(L, 1, HEAD_DIM)

    def residual(acc, xres, gt):
        return xres + gt * acc

    def mod_of(l):
        return lax.dynamic_index_in_dim(mod, l, 0, keepdims=False).reshape(8, 1, N_MOD * D)

    def layer(l, carry):
        xt, h = carry
        l = jnp.asarray(l, jnp.int32)
        mod3 = mod_of(l)
        proj1 = _matmul(h, w_in, l, col0=0, n=gla_cols, out_dtype=BF16, tn=2 * MM_TN,
                        tm=2 * MM_TM, name="proj_gla")
        fg = _matmul(h, w_in, l, col0=gla_cols, n=LANES, tn=LANES, out_dtype=F32, name="proj_fg")
        proj2 = _matmul(h, w_in, l, col0=gla_cols, shift=GLA_RANK, n=3 * att_w,
                        out_dtype=BF16, tm=2 * MM_TM, name="proj_att")
        gates = _matmul(h, w_in, l, col0=gla_cols + 3 * att_w, shift=GLA_RANK, n=2 * D,
                        out_dtype=BF16, tm=2 * MM_TM, name="proj_gates")
        o_gla = _gla(proj1, fg, l, wfg_pad, b_fg3, g_gla3, B, S)
        prep = _qkprep(proj2, l, cos2, sin2, g_q3, g_k3)
        outs, lses = [], []
        for grp, (_, dil) in enumerate(DIL_GROUPS):
            o, lse = _attn_group(prep[grp], prep[N_DIL + grp], prep[2 * N_DIL + grp], dil, B, S)
            outs.append(o)
            lses.append(lse)
        o_att = _attn_mix(outs, lses)
        merged = _merge(o_gla, o_att, gates, l, w_gla_proj, w_attn_proj)
        tn_out = 2 * MM_TN
        gt_spec = pl.BlockSpec((None, 1, tn_out),
                               lambda j, i, l: (i * MM_TM // S, 0, 2 * (D // tn_out) + j))
        xt = _matmul(merged, w_out, l, col0=0, n=D, out_dtype=F32, tn=tn_out, epilogue=residual,
                     extras=(xt, mod3),
                     extra_specs=(pl.BlockSpec((MM_TM, tn_out), lambda j, i, l: (i, j)), gt_spec),
                     name="out_proj")
        h2, route = _router(xt, l, g_ffn, mod3, wr_pad, br_pad, S)
        src, dst, meta = _dispatch_plan(route, l)
        nj = D // (2 * LANES)
        n_out = 2 * T + 2 * EXPERT_TM
        ys = _experts(h2.reshape(T, nj, LANES), src, dst, meta, w_exp_gate, w_exp_up, w_exp_down,
                      n_out)
        l_next = jnp.minimum(l + 1, L - 1)
        g_next = lax.dynamic_index_in_dim(g_mix, l_next, 0, keepdims=False)
        return tuple(_combine(xt, ys.reshape(n_out * nj, LANES), route, mod3, g_next,
                              mod_of(l_next), S))

    x0 = x.reshape(T, D)
    zero = jnp.int32(0)
    h0, x_init = _norm(x0, zero, g_mix, mod_of(zero), 0, S)
    xt, _ = lax.fori_loop(0, L, layer, (x_init, h0))
    return xt.reshape(B, S, D)
```

```python
import functools

import jax
import jax.numpy as jnp
from jax import lax
from jax.experimental import pallas as pl
from jax.experimental.pallas import tpu as pltpu

F32 = jnp.float32
BF16 = jnp.bfloat16

GLA_HEADS = 4
GLA_RANK = 16
GLA_TAU = 16.0
GLA_CHUNK = 64
DIL_GROUPS = ((128, 1), (512, 4), (2048, 16))
N_DIL = 3
ATT_HEADS = 4
HEAD_DIM = 128
ATT_BLOCK = 128
ROPE_THETA = 10000.0
N_GROUPS = 4
EXPERTS_PER_GROUP = 8
N_EXPERTS = N_GROUPS * EXPERTS_PER_GROUP
N_MOD = 6
EPS = 1e-6

LANES = 128
VMEM_LIMIT = 56 * 1024 * 1024
NEG_BIG = -1e30

MOD_TN = 2048
MOD_SPLIT = 4
MOD_CHUNK = 64
NORM_ROWS = 512
MM_TM = 1024
MM_TN = 512
GLA_ROWS = 512
PREP_ROWS = 512
ATT_SPAN = 2048
MERGE_ROWS = 1024
EXPERT_TM = 256
COMBINE_ROWS = 512
INVERT_BLOCK = 4096


def _params(*sem):
    return pltpu.CompilerParams(dimension_semantics=sem, vmem_limit_bytes=VMEM_LIMIT)


def _dot(a, b):
    return jnp.dot(a, b, preferred_element_type=F32)


def _dot_nt(a, b):
    return lax.dot_general(a, b, (((1,), (1,)), ((), ())), preferred_element_type=F32)


def _dot_tn(a, b):
    return lax.dot_general(a, b, (((0,), (0,)), ((), ())), preferred_element_type=F32)


def _split_bf16(x):
    hi = x.astype(BF16)
    lo = (x - hi.astype(F32)).astype(BF16)
    return hi, lo


def _sigmoid(x):
    return 1.0 / (1.0 + jnp.exp(-x))


def _pack_rows(xb):
    bits = lax.bitcast_convert_type(xb, jnp.uint32)
    nw = xb.shape[1] // (2 * LANES)
    out = []
    for j in range(nw):
        lo = bits[:, j * LANES:(j + 1) * LANES] >> 16
        hi = bits[:, (nw + j) * LANES:(nw + j + 1) * LANES] & jnp.uint32(0xFFFF0000)
        out.append(lo | hi)
    return out


def _unpack_rows(words):
    lo = [lax.bitcast_convert_type(w << 16, F32) for w in words]
    hi = [lax.bitcast_convert_type(w & jnp.uint32(0xFFFF0000), F32) for w in words]
    return jnp.concatenate(lo + hi, axis=1)


def _grid_call(kernel, l, grid, in_specs, out_specs, out_shape, scratch=(), sem=None, name=None):
    sem = sem or ("arbitrary",) * len(grid)
    return pl.pallas_call(
        kernel,
        grid_spec=pltpu.PrefetchScalarGridSpec(
            num_scalar_prefetch=1, grid=grid, in_specs=in_specs, out_specs=out_specs,
            scratch_shapes=list(scratch)),
        out_shape=out_shape,
        compiler_params=_params(*sem),
        name=name,
    ), jnp.reshape(l, (1,)).astype(jnp.int32)


def _mod_kernel(c_ref, *refs):
    w_refs, (b_ref, o_ref, act_ref, acc_ref) = refs[:MOD_SPLIT], refs[MOD_SPLIT:]
    n_batch, d = c_ref.shape[0], c_ref.shape[1]
    tn = o_ref.shape[1]
    ch = MOD_CHUNK

    @pl.when((pl.program_id(0) == 0) & (pl.program_id(1) == 0))
    def _():
        c = c_ref[...]
        act_ref[...] = c * _sigmoid(c)

    acc_ref[...] = jnp.zeros(acc_ref.shape, F32)

    def chunk(ci, carry):
        r0 = pl.multiple_of(ci * ch, ch)
        acts = [act_ref[b, pl.ds(r0, ch), :] for b in range(n_batch)]
        col = 0
        for w_ref in w_refs:
            for jb in range(w_ref.shape[1] // LANES):
                w = w_ref[pl.ds(r0, ch), jb * LANES:(jb + 1) * LANES]
                for b in range(n_batch):
                    part = (w * acts[b]).reshape(ch // 8, 8, LANES).sum(axis=0)
                    acc_ref[b, :, col:col + LANES] += part
                col += LANES
        return carry

    lax.fori_loop(0, d // ch, chunk, 0)
    out = jnp.concatenate([jnp.sum(acc_ref[b], axis=0, keepdims=True) for b in range(n_batch)],
                          axis=0) + b_ref[...]
    o_ref[0:n_batch, :] = out
    o_ref[n_batch:, :] = jnp.zeros((o_ref.shape[0] - n_batch, tn), F32)


def _modulation(c, w_ada, b_ada):
    L, D, N = w_ada.shape
    B = c.shape[0]
    rows = 8
    assert B <= rows
    c_rep = jnp.broadcast_to(c[:, :, None], (B, D, LANES))
    tn = MOD_TN
    ts = tn // MOD_SPLIT
    w_specs = [pl.BlockSpec((None, D, ts), lambda l, j, k=k: (l, 0, j * MOD_SPLIT + k))
               for k in range(MOD_SPLIT)]
    return pl.pallas_call(
        _mod_kernel,
        grid=(L, N // tn),
        in_specs=[pl.BlockSpec((B, D, LANES), lambda l, j: (0, 0, 0))] + w_specs +
                 [pl.BlockSpec((None, 1, tn), lambda l, j: (l, 0, j))],
        out_specs=pl.BlockSpec((None, rows, tn), lambda l, j: (l, 0, j)),
        out_shape=jax.ShapeDtypeStruct((L, rows, N), F32),
        scratch_shapes=[pltpu.VMEM((B, D, LANES), F32), pltpu.VMEM((B, 8, tn), F32)],
        compiler_params=_params("arbitrary", "arbitrary"),
        name="adaln_mod",
    )(c_rep, *([w_ada] * MOD_SPLIT), b_ada.reshape(L, 1, N))


def _rope_kernel(pos_ref, freq_ref, cos_ref, sin_ref):
    ang = pos_ref[...].astype(F32) * freq_ref[...]
    lane = lax.broadcasted_iota(jnp.int32, ang.shape, 1)
    cos_ref[...] = jnp.cos(ang)
    sin_ref[...] = jnp.where(lane < HEAD_DIM // 2, -jnp.sin(ang), jnp.sin(ang))


def _rope_tables(positions):
    T = positions.size
    inv_freq = ROPE_THETA ** (-jnp.arange(0, HEAD_DIM, 2, dtype=F32) / HEAD_DIM)
    freq2 = jnp.concatenate([inv_freq, inv_freq]).reshape(1, HEAD_DIM)
    rows = 2048
    return pl.pallas_call(
        _rope_kernel,
        grid=(T // rows,),
        in_specs=[pl.BlockSpec((rows, 1), lambda i: (i, 0)),
                  pl.BlockSpec((1, HEAD_DIM), lambda i: (0, 0))],
        out_specs=[pl.BlockSpec((rows, HEAD_DIM), lambda i: (i, 0))] * 2,
        out_shape=[jax.ShapeDtypeStruct((T, HEAD_DIM), F32)] * 2,
        compiler_params=_params("parallel"),
        name="rope_tables",
    )(positions.reshape(T, 1), freq2)


def _ada_norm(x, g, scale, shift):
    y = x * lax.rsqrt(jnp.mean(x * x, axis=-1, keepdims=True) + EPS)
    return y * g * (1.0 + scale) + shift


def _norm_kernel(l_ref, x_ref, g_ref, sh_ref, sc_ref, o_ref, xcopy_ref):
    x = x_ref[...]
    o_ref[...] = _ada_norm(x, g_ref[...], sc_ref[...], sh_ref[...]).astype(o_ref.dtype)
    xcopy_ref[...] = x


def _mod_spec(which, rows, seq):
    return lambda i, l: (i * rows // seq, 0, which)


def _norm(x, l, g_all, mod3, which_shift, seq):
    T, D = x.shape
    rows = NORM_ROWS
    call, lidx = _grid_call(
        _norm_kernel, l, (T // rows,),
        [pl.BlockSpec((rows, D), lambda i, l: (i, 0)),
         pl.BlockSpec((None, 1, D), lambda i, l: (l[0], 0, 0)),
         pl.BlockSpec((None, 1, D), _mod_spec(which_shift, rows, seq)),
         pl.BlockSpec((None, 1, D), _mod_spec(which_shift + 1, rows, seq))],
        [pl.BlockSpec((rows, D), lambda i, l: (i, 0))] * 2,
        [jax.ShapeDtypeStruct((T, D), BF16), jax.ShapeDtypeStruct((T, D), x.dtype)],
        sem=("parallel",), name="ada_norm")
    return call(lidx, x, g_all, mod3, mod3)


def _mm_kernel(l_ref, a_ref, w_ref, *rest, epilogue, n_extra, shift):
    n_w = 1 if shift else 0
    extra = rest[n_w:n_w + n_extra]
    o_ref = rest[n_w + n_extra]
    wbf_ref = rest[n_w + n_extra + 1]
    tn = o_ref.shape[1]

    @pl.when(pl.program_id(1) == 0)
    def _():
        if shift:
            wide = jnp.concatenate([w_ref[...], rest[0][...]], axis=1)
            wbf_ref[...] = wide[:, shift:shift + tn].astype(BF16)
        else:
            wbf_ref[...] = w_ref[...].astype(BF16)

    acc = _dot(a_ref[...], wbf_ref[...])
    if epilogue is not None:
        acc = epilogue(acc, *[e[...] for e in extra])
    o_ref[...] = acc.astype(o_ref.dtype)


def _matmul(a, w, l, *, col0, n, out_dtype, tn=MM_TN, tm=MM_TM, shift=0, epilogue=None,
            extras=(), extra_specs=(), name="matmul"):
    M, K = a.shape
    assert col0 % tn == 0 and n % tn == 0 and M % tm == 0 and 0 <= shift < LANES
    cb0 = col0 // tn
    kern = functools.partial(_mm_kernel, epilogue=epilogue, n_extra=len(extras), shift=shift)
    w_specs = [pl.BlockSpec((None, K, tn), lambda j, i, l: (l[0], 0, cb0 + j))]
    if shift:
        per = tn // LANES
        w_specs.append(pl.BlockSpec((None, K, LANES), lambda j, i, l: (l[0], 0, (cb0 + j + 1) * per)))
    call, lidx = _grid_call(
        kern, l, (n // tn, M // tm),
        [pl.BlockSpec((tm, K), lambda j, i, l: (i, 0))] + w_specs + list(extra_specs),
        pl.BlockSpec((tm, tn), lambda j, i, l: (i, j)),
        jax.ShapeDtypeStruct((M, n), out_dtype),
        scratch=[pltpu.VMEM((K, tn), BF16)], name=name)
    return call(lidx, a, *([w] * len(w_specs)), *extras)


def _gla_kernel(l_ref, q_ref, k_ref, v_ref, r_ref, fg_ref, wfg_ref, bfg_ref, g_ref, o_ref,
                state_ref, *, head_k, head_v):
    rows = q_ref.shape[0]
    C = GLA_CHUNK

    @pl.when(pl.program_id(1) == 0)
    def _():
        state_ref[...] = jnp.zeros_like(state_ref)

    fh, fl = _split_bf16(fg_ref[...])
    wh, wl = _split_bf16(wfg_ref[...])
    z = _dot(fh, wh) + _dot(fl, wh) + _dot(fh, wl) + bfg_ref[...]
    log_a = (jnp.minimum(z, 0.0) - jnp.log(1.0 + jnp.exp(-jnp.abs(z)))) * (1.0 / GLA_TAU)

    ri = lax.broadcasted_iota(jnp.int32, (C, C), 0)
    ci = lax.broadcasted_iota(jnp.int32, (C, C), 1)
    causal = ri >= ci
    tri = causal.astype(BF16)
    scale = head_k ** -0.5

    for c in range(rows // C):
        rs = slice(c * C, (c + 1) * C)
        lh, ll = _split_bf16(log_a[rs])
        b = _dot(tri, lh) + _dot(tri, ll)
        b_last = b[C - 1:C]
        q = q_ref[rs, :].astype(F32) * scale
        k = k_ref[rs, :].astype(F32)
        q_dec = (q * jnp.exp(b)).astype(BF16)
        k_inv = (k * jnp.exp(-b)).astype(BF16)
        k_upd = (k * jnp.exp(b_last - b)).astype(BF16)
        decay = jnp.exp(b_last)
        for h in range(GLA_HEADS):
            ks = slice(h * head_k, (h + 1) * head_k)
            vs = slice(h * head_v, (h + 1) * head_v)
            v = v_ref[rs, vs]
            att = jnp.where(causal, _dot_nt(q_dec[:, ks], k_inv[:, ks]), 0.0).astype(BF16)
            st = state_ref[h]
            o = _dot(att, v) + _dot_nt(q_dec[:, ks], st.astype(BF16))
            state_ref[h] = st * decay[:, ks] + _dot_tn(v, k_upd[:, ks])
            o = o * lax.rsqrt(jnp.mean(o * o, axis=-1, keepdims=True) + EPS) * g_ref[:, vs]
            r = r_ref[rs, vs].astype(F32)
            o_ref[rs, vs] = (o * (r * _sigmoid(r))).astype(o_ref.dtype)


def _gla(proj, fg, l, wfg_pad, b_fg, g_gla_out, batch, seq):
    T = proj.shape[0]
    dk_all = wfg_pad.shape[-1]
    dv_all = g_gla_out.shape[-1]
    rows = GLA_ROWS
    nblk = seq // rows
    kern = functools.partial(_gla_kernel, head_k=dk_all // GLA_HEADS, head_v=dv_all // GLA_HEADS)
    row = lambda b, n, l: b * nblk + n
    call, lidx = _grid_call(
        kern, l, (batch, nblk),
        [pl.BlockSpec((rows, dk_all), lambda b, n, l: (row(b, n, l), 0)),
         pl.BlockSpec((rows, dk_all), lambda b, n, l: (row(b, n, l), 1)),
         pl.BlockSpec((rows, dv_all), lambda b, n, l: (row(b, n, l), 1)),
         pl.BlockSpec((rows, dv_all), lambda b, n, l: (row(b, n, l), 2)),
         pl.BlockSpec((rows, LANES), lambda b, n, l: (row(b, n, l), 0)),
         pl.BlockSpec((None, LANES, dk_all), lambda b, n, l: (l[0], 0, 0)),
         pl.BlockSpec((None, 1, dk_all), lambda b, n, l: (l[0], 0, 0)),
         pl.BlockSpec((None, 1, dv_all), lambda b, n, l: (l[0], 0, 0))],
        pl.BlockSpec((rows, dv_all), lambda b, n, l: (row(b, n, l), 0)),
        jax.ShapeDtypeStruct((T, dv_all), BF16),
        scratch=[pltpu.VMEM((GLA_HEADS, dv_all // GLA_HEADS, dk_all // GLA_HEADS), F32)],
        sem=("parallel", "arbitrary"), name="gla")
    return call(lidx, proj, proj, proj, proj, fg, wfg_pad, b_fg, g_gla_out)


def _qkprep_kernel(l_ref, q_ref, k_ref, v_ref, cos_ref, sin_ref, gq_ref, gk_ref, *refs):
    out_refs, scr_ref = refs[:-1], refs[-1]
    cos = cos_ref[...]
    sin = sin_ref[...]
    gw = ATT_HEADS * HEAD_DIM
    rows = q_ref.shape[0]

    ones_bf = jnp.ones((HEAD_DIM, HEAD_DIM), BF16)

    def tables(g_ref, scale):
        g = jnp.broadcast_to(g_ref[...], cos.shape)
        return g * cos * scale, pltpu.roll(g, HEAD_DIM // 2, 1) * sin * scale

    def prep(ref, tab):
        x = ref.astype(F32)
        ms = _dot((x * x).astype(BF16), ones_bf) * (1.0 / HEAD_DIM)
        n = x * lax.rsqrt(ms + EPS)
        return n * tab[0] + pltpu.roll(n, HEAD_DIM // 2, 1) * tab[1]

    def emit(dst_ref, dil):
        n = rows // dil
        for p in range(dil):
            for h in range(ATT_HEADS):
                cs = slice(p * gw + h * HEAD_DIM, p * gw + (h + 1) * HEAD_DIM)
                dst_ref[:, cs] = scr_ref[h, pl.ds(p, n, stride=dil), :].astype(BF16)

    tab_q = tables(gq_ref, HEAD_DIM ** -0.5)
    tab_k = tables(gk_ref, 1.0)
    for grp, (_, dil) in enumerate(DIL_GROUPS):
        for ref, tab, out in ((q_ref, tab_q, out_refs[grp]), (k_ref, tab_k, out_refs[N_DIL + grp])):
            for h in range(ATT_HEADS):
                src = slice(grp * gw + h * HEAD_DIM, grp * gw + (h + 1) * HEAD_DIM)
                y = prep(ref[:, src], tab)
                if dil == 1:
                    out[:, h * HEAD_DIM:(h + 1) * HEAD_DIM] = y.astype(BF16)
                else:
                    scr_ref[h] = y
            if dil > 1:
                emit(out, dil)
        if dil == 1:
            out_refs[2 * N_DIL + grp][...] = v_ref[:, grp * gw:(grp + 1) * gw]
        else:
            for h in range(ATT_HEADS):
                src = slice(grp * gw + h * HEAD_DIM, grp * gw + (h + 1) * HEAD_DIM)
                scr_ref[h] = v_ref[:, src].astype(F32)
            emit(out_refs[2 * N_DIL + grp], dil)


def _qkprep(proj2, l, cos2, sin2, g_q, g_k):
    T = proj2.shape[0]
    rows = PREP_ROWS
    width = N_DIL * ATT_HEADS * HEAD_DIM
    gw = ATT_HEADS * HEAD_DIM
    dils = [dil for _, dil in DIL_GROUPS] * 3
    call, lidx = _grid_call(
        _qkprep_kernel, l, (T // rows,),
        [pl.BlockSpec((rows, width), lambda i, l: (i, 0)),
         pl.BlockSpec((rows, width), lambda i, l: (i, 1)),
         pl.BlockSpec((rows, width), lambda i, l: (i, 2)),
         pl.BlockSpec((rows, HEAD_DIM), lambda i, l: (i, 0)),
         pl.BlockSpec((rows, HEAD_DIM), lambda i, l: (i, 0)),
         pl.BlockSpec((None, 1, HEAD_DIM), lambda i, l: (l[0], 0, 0)),
         pl.BlockSpec((None, 1, HEAD_DIM), lambda i, l: (l[0], 0, 0))],
        [pl.BlockSpec((rows // d, d * gw), lambda i, l: (i, 0)) for d in dils],
        [jax.ShapeDtypeStruct((T // d, d * gw), BF16) for d in dils],
        scratch=[pltpu.VMEM((ATT_HEADS, rows, HEAD_DIM), F32)],
        sem=("parallel",), name="qk_prep")
    return call(lidx, proj2, proj2, proj2, cos2, sin2, g_q, g_k)


def _attn_kernel(q_ref, kc_ref, kp_ref, vc_ref, vp_ref, o_ref, lse_ref, *, dil):
    first = pl.program_id(1) == 0
    QB = q_ref.shape[0]
    A = ATT_BLOCK
    qi = lax.broadcasted_iota(jnp.int32, (A, 2 * A), 0)
    kj = lax.broadcasted_iota(jnp.int32, (A, 2 * A), 1) - A
    dist = qi - kj
    valid = (dist >= 0) & (dist <= A)
    bias = jnp.where(valid, 0.0, NEG_BIG).astype(F32)
    bias_first = jnp.where(first, jnp.where(valid & (kj >= 0), 0.0, NEG_BIG), bias).astype(F32)

    for p in range(dil):
        for h in range(ATT_HEADS):
            cs = slice((p * ATT_HEADS + h) * HEAD_DIM, (p * ATT_HEADS + h + 1) * HEAD_DIM)
            for j in range(QB // A):
                rs = slice(j * A, (j + 1) * A)
                if j == 0:
                    kcat = jnp.concatenate([kp_ref[:, cs], kc_ref[0:A, cs]], axis=0)
                    vcat = jnp.concatenate([vp_ref[:, cs], vc_ref[0:A, cs]], axis=0)
                    bb = bias_first
                else:
                    kcat = kc_ref[(j - 1) * A:(j + 1) * A, cs]
                    vcat = vc_ref[(j - 1) * A:(j + 1) * A, cs]
                    bb = bias
                s = _dot_nt(q_ref[rs, cs], kcat) + bb
                m = jnp.max(s, axis=-1, keepdims=True)
                e = jnp.exp(s - m)
                den = jnp.sum(e, axis=-1, keepdims=True)
                o = _dot(e.astype(BF16), vcat) / den
                o_ref[rs, cs] = o.astype(o_ref.dtype)
                lse_ref[rs, cs] = jnp.broadcast_to(m + jnp.log(den), (A, HEAD_DIM))


def _attn_group(q, k, v, dil, batch, seq):
    rows_all, W = q.shape
    A = ATT_BLOCK
    QB = ATT_SPAN // dil
    nsp = seq // ATT_SPAN
    per = QB // A
    cur = pl.BlockSpec((QB, W), lambda b, n: (b * nsp + n, 0))
    prev = pl.BlockSpec((A, W), lambda b, n: (jnp.maximum((b * nsp + n) * per - 1, b * nsp * per), 0))
    return pl.pallas_call(
        functools.partial(_attn_kernel, dil=dil),
        grid=(batch, nsp),
        in_specs=[cur, cur, prev, cur, prev],
        out_specs=[cur, cur],
        out_shape=[jax.ShapeDtypeStruct((rows_all, W), BF16),
                   jax.ShapeDtypeStruct((rows_all, W), F32)],
        compiler_params=_params("parallel", "arbitrary"),
        name=f"dilated_attn_r{dil}",
    )(q, k, k, v, v)


def _attn_mix_kernel(o0, o1, o2, l0, l1, l2, out_ref, so_ref, sl_ref):
    rows, gw = out_ref.shape
    o_refs, l_refs = (o0, o1, o2), (l0, l1, l2)
    for h in range(ATT_HEADS):
        hs = slice(h * HEAD_DIM, (h + 1) * HEAD_DIM)
        outs, lses = [], []
        for g, (_, dil) in enumerate(DIL_GROUPS):
            if dil == 1:
                outs.append(o_refs[g][:, hs].astype(F32))
                lses.append(l_refs[g][:, hs])
                continue
            n = rows // dil
            for p in range(dil):
                cs = slice(p * gw + h * HEAD_DIM, p * gw + (h + 1) * HEAD_DIM)
                so_ref[g, pl.ds(p, n, stride=dil), :] = o_refs[g][:, cs].astype(F32)
                sl_ref[g, pl.ds(p, n, stride=dil), :] = l_refs[g][:, cs]
            outs.append(so_ref[g])
            lses.append(sl_ref[g])
        m = jnp.maximum(jnp.maximum(lses[0], lses[1]), lses[2])
        ws = [jnp.exp(x - m) for x in lses]
        den = ws[0] + ws[1] + ws[2]
        out_ref[:, hs] = ((ws[0] * outs[0] + ws[1] * outs[1] + ws[2] * outs[2]) / den
                          ).astype(out_ref.dtype)


def _attn_mix(outs, lses):
    gw = ATT_HEADS * HEAD_DIM
    T = outs[0].shape[0]
    rows = PREP_ROWS
    dils = [dil for _, dil in DIL_GROUPS]
    specs = [pl.BlockSpec((rows // d, d * gw), lambda i: (i, 0)) for d in dils]
    return pl.pallas_call(
        _attn_mix_kernel,
        grid=(T // rows,),
        in_specs=specs + specs,
        out_specs=pl.BlockSpec((rows, gw), lambda i: (i, 0)),
        out_shape=jax.ShapeDtypeStruct((T, gw), BF16),
        scratch_shapes=[pltpu.VMEM((N_DIL, rows, HEAD_DIM), F32),
                        pltpu.VMEM((N_DIL, rows, HEAD_DIM), F32)],
        compiler_params=_params("parallel"),
        name="attn_mix",
    )(*outs, *lses)


def _merge_kernel(l_ref, ogla_ref, oatt_ref, gg_ref, ga_ref, wg_ref, wa_ref,
                  out_ref, wgbf_ref, wabf_ref):
    @pl.when(pl.program_id(1) == 0)
    def _():
        wgbf_ref[...] = wg_ref[...].astype(BF16)
        wabf_ref[...] = wa_ref[...].astype(BF16)

    a = _dot(ogla_ref[...], wgbf_ref[...])
    b = _dot(oatt_ref[...], wabf_ref[...])
    gg = _sigmoid(gg_ref[...].astype(F32))
    ga = _sigmoid(ga_ref[...].astype(F32))
    out_ref[...] = (gg * a + ga * b).astype(out_ref.dtype)


def _merge(o_gla, o_att, gates, l, w_gla_proj, w_attn_proj):
    T, dv = o_gla.shape
    gw = o_att.shape[1]
    D = w_gla_proj.shape[-1]
    tm, tn = MERGE_ROWS, 2 * MM_TN
    gcb = 0
    dcb = D // tn
    call, lidx = _grid_call(
        _merge_kernel, l, (D // tn, T // tm),
        [pl.BlockSpec((tm, dv), lambda j, i, l: (i, 0)),
         pl.BlockSpec((tm, gw), lambda j, i, l: (i, 0)),
         pl.BlockSpec((tm, tn), lambda j, i, l: (i, gcb + j)),
         pl.BlockSpec((tm, tn), lambda j, i, l: (i, gcb + dcb + j)),
         pl.BlockSpec((None, dv, tn), lambda j, i, l: (l[0], 0, j)),
         pl.BlockSpec((None, gw, tn), lambda j, i, l: (l[0], 0, j))],
        pl.BlockSpec((tm, tn), lambda j, i, l: (i, j)),
        jax.ShapeDtypeStruct((T, D), BF16),
        scratch=[pltpu.VMEM((dv, tn), BF16), pltpu.VMEM((gw, tn), BF16)], name="gated_merge")
    return call(lidx, o_gla, o_att, gates, gates, w_gla_proj, w_attn_proj)


def _router_kernel(l_ref, x_ref, g_ref, sh_ref, sc_ref, wr_ref, br_ref, h_ref, route_ref):
    h = _ada_norm(x_ref[...], g_ref[...], sc_ref[...], sh_ref[...])
    rows = h.shape[0]
    hb = h.astype(BF16)
    words = _pack_rows(hb.astype(F32))
    nw = len(words)
    for j in range(nw):
        h_ref[pl.ds(j, rows, stride=nw), :] = words[j]
    lg = _dot(hb, wr_ref[...].astype(BF16)) + br_ref[...]
    lane = lax.broadcasted_iota(jnp.int32, lg.shape, 1).astype(F32)
    big = float(4 * LANES)

    def first_argmax(vals):
        mx = jnp.max(vals, axis=-1, keepdims=True)
        idx = jnp.min(jnp.where(vals == mx, lane, big), axis=-1, keepdims=True)
        return mx, idx

    gl = jnp.where(lane < N_GROUPS, lg, NEG_BIG)
    gmax, gidx = first_argmax(gl)
    g_weight = 1.0 / jnp.sum(jnp.exp(gl - gmax), axis=-1, keepdims=True)
    lo = N_GROUPS + gidx * EXPERTS_PER_GROUP
    el = jnp.where((lane >= lo) & (lane < lo + EXPERTS_PER_GROUP), lg, NEG_BIG)
    m1, i1 = first_argmax(el)
    m2, i2 = first_argmax(jnp.where(lane == i1, NEG_BIG, el))
    e2 = jnp.exp(m2 - m1)
    w1 = g_weight / (1.0 + e2)
    w2 = g_weight * e2 / (1.0 + e2)
    route = jnp.where(lane == 0, i1 - N_GROUPS,
                      jnp.where(lane == 1, i2 - N_GROUPS,
                                jnp.where(lane == 2, w1, jnp.where(lane == 3, w2, 0.0))))
    route_ref[...] = route


def _router(x, l, g_all, mod3, wr_pad, br_pad, seq):
    T, D = x.shape
    rows = NORM_ROWS
    call, lidx = _grid_call(
        _router_kernel, l, (T // rows,),
        [pl.BlockSpec((rows, D), lambda i, l: (i, 0)),
         pl.BlockSpec((None, 1, D), lambda i, l: (l[0], 0, 0)),
         pl.BlockSpec((None, 1, D), _mod_spec(3, rows, seq)),
         pl.BlockSpec((None, 1, D), _mod_spec(4, rows, seq)),
         pl.BlockSpec((None, D, LANES), lambda i, l: (l[0], 0, 0)),
         pl.BlockSpec((None, 1, LANES), lambda i, l: (l[0], 0, 0))],
        [pl.BlockSpec((rows * (D // (2 * LANES)), LANES), lambda i, l: (i, 0)),
         pl.BlockSpec((rows, LANES), lambda i, l: (i, 0))],
        [jax.ShapeDtypeStruct((T * (D // (2 * LANES)), LANES), jnp.uint32),
         jax.ShapeDtypeStruct((T, LANES), F32)],
        sem=("parallel",), name="norm_router")
    return call(lidx, x, g_all, mod3, mod3, wr_pad, br_pad)


def _expert_kernel(meta_ref, src_one_ref, src_cur_ref, src_nxt_ref, dst_ref, h_ref,
                   wg_ref, wu_ref, wd_ref, ys_ref, xbuf_ref, obuf_ref, wgf_ref, wuf_ref, wdf_ref,
                   wgbf_ref, wubf_ref, wdbf_ref, gsem_ref, ssem_ref, wsem_ref):
    i = pl.program_id(0)
    n_tiles = pl.num_programs(0)
    n_used = meta_ref[1]
    slot = i % 2
    nj, tm = xbuf_ref.shape[1], xbuf_ref.shape[2]
    layer = meta_ref[0]
    expert = meta_ref[2 + i]
    next_expert = meta_ref[2 + n_tiles + i]
    wset = meta_ref[2 + 2 * n_tiles + i]

    def weight_copies(e, s):
        return [pltpu.make_async_copy(w.at[layer, e], buf.at[s], wsem_ref.at[s])
                for w, buf in ((wg_ref, wgf_ref), (wu_ref, wuf_ref), (wd_ref, wdf_ref))]

    def gather_copy(idx_ref, r, s):
        return pltpu.make_async_copy(h_ref.at[idx_ref[0, 0, r]], xbuf_ref.at[s, :, r, :],
                                     gsem_ref.at[s])

    def scatter_copy(idx_ref, r, s):
        return pltpu.make_async_copy(obuf_ref.at[s, :, r, :], ys_ref.at[idx_ref[0, 0, r]],
                                     ssem_ref.at[s])

    def gather_start(idx_ref, s):
        def body(r, c):
            gather_copy(idx_ref, r, s).start()
            return c
        lax.fori_loop(0, tm, body, 0, unroll=8)

    def gather_wait(s):
        pltpu.make_async_copy(obuf_ref.at[0], xbuf_ref.at[s], gsem_ref.at[s]).wait()

    def scatter_wait(s):
        pltpu.make_async_copy(xbuf_ref.at[0], obuf_ref.at[s], ssem_ref.at[s]).wait()

    @pl.when(i == 0)
    def _():
        for cp in weight_copies(expert, wset):
            cp.start()
        gather_start(src_cur_ref, 0)

        @pl.when(n_used > 1)
        def _():
            gather_start(src_one_ref, 1)
        obuf_ref[1] = jnp.zeros(obuf_ref.shape[1:], jnp.uint32)
        n_rows = ys_ref.shape[0]

        def fill(r, c):
            pltpu.make_async_copy(obuf_ref.at[1, :, r % tm, :], ys_ref.at[n_rows - 2 * tm + r],
                                  ssem_ref.at[1]).start()
            return c
        lax.fori_loop(0, 2 * tm, fill, 0, unroll=8)
        scatter_wait(1)
        scatter_wait(1)

    @pl.when(i < n_used)
    def _():
        xslot = i % 3
        gather_wait(xslot)

        @pl.when(i >= 2)
        def _():
            scatter_wait(slot)

        @pl.when((i == 0) | (expert != meta_ref[2 + jnp.maximum(i - 1, 0)]))
        def _():
            for cp in weight_copies(expert, wset):
                cp.wait()
            wgbf_ref[...] = wgf_ref[wset].astype(BF16)
            wubf_ref[...] = wuf_ref[wset].astype(BF16)
            wdbf_ref[...] = wdf_ref[wset].astype(BF16)

            @pl.when(next_expert >= 0)
            def _():
                for cp in weight_copies(next_expert, 1 - wset):
                    cp.start()

        x = _unpack_rows([xbuf_ref[xslot, j] for j in range(nj)]).astype(BF16)
        a = _dot(x, wgbf_ref[...])
        u = _dot(x, wubf_ref[...])
        hid = (a * _sigmoid(a)) * u
        y = _dot(hid.astype(BF16), wdbf_ref[...])
        for j, word in enumerate(_pack_rows(y.astype(BF16).astype(F32))):
            obuf_ref[slot, j] = word
        for r in range(tm):
            scatter_copy(dst_ref, r, slot).start()

        @pl.when(i + 2 < n_used)
        def _():
            nslot = (i + 2) % 3
            for r in range(tm):
                gather_copy(src_nxt_ref, r, nslot).start()

        @pl.when(i == n_used - 1)
        def _():
            @pl.when(i >= 1)
            def _():
                scatter_wait(1 - slot)
            scatter_wait(slot)


def _experts(h3, src, dst, meta, w_gate, w_up, w_down, n_out):
    T, nj, _ = h3.shape
    D = 2 * nj * LANES
    F = w_gate.shape[-1]
    n_tiles, _, tm = src.shape
    idx_spec = lambda f: pl.BlockSpec((1, 1, tm), lambda i, m: (f(i), 0, 0), memory_space=pltpu.SMEM)
    hbm = pl.BlockSpec(memory_space=pl.ANY)
    return pl.pallas_call(
        _expert_kernel,
        grid_spec=pltpu.PrefetchScalarGridSpec(
            num_scalar_prefetch=1, grid=(n_tiles,),
            in_specs=[idx_spec(lambda i: min(1, n_tiles - 1)), idx_spec(lambda i: i),
                      idx_spec(lambda i: jnp.minimum(i + 2, n_tiles - 1)), idx_spec(lambda i: i),
                      hbm, hbm, hbm, hbm],
            out_specs=hbm,
            scratch_shapes=[pltpu.VMEM((3, nj, tm, LANES), jnp.uint32),
                            pltpu.VMEM((2, nj, tm, LANES), jnp.uint32),
                            pltpu.VMEM((2, D, F), F32), pltpu.VMEM((2, D, F), F32),
                            pltpu.VMEM((2, F, D), F32),
                            pltpu.VMEM((D, F), BF16), pltpu.VMEM((D, F), BF16),
                            pltpu.VMEM((F, D), BF16),
                            pltpu.SemaphoreType.DMA((3,)), pltpu.SemaphoreType.DMA((2,)),
                            pltpu.SemaphoreType.DMA((2,))]),
        out_shape=jax.ShapeDtypeStruct((n_out, nj, LANES), jnp.uint32),
        compiler_params=_params("arbitrary"),
        name="grouped_experts",
    )(meta, src, src, src, dst, h3, w_gate, w_up, w_down)


def _combine_kernel(x_ref, gt_ref, route_ref, y0_ref, y1_ref, g_ref, sh_ref, sc_ref, o_ref, h_ref):
    rows, d = x_ref.shape
    nw = d // (2 * LANES)
    w1 = jnp.broadcast_to(route_ref[:, 2:3], (rows, LANES))
    w2 = jnp.broadcast_to(route_ref[:, 3:4], (rows, LANES))
    for j in range(nw):
        u0 = y0_ref[pl.ds(j, rows, stride=nw), :]
        u1 = y1_ref[pl.ds(j, rows, stride=nw), :]
        for blk, half in ((j, lambda u: u << 16), (nw + j, lambda u: u & jnp.uint32(0xFFFF0000))):
            js = slice(blk * LANES, (blk + 1) * LANES)
            y = (w1 * lax.bitcast_convert_type(half(u0), F32)
                 + w2 * lax.bitcast_convert_type(half(u1), F32))
            o_ref[:, js] = x_ref[:, js] + gt_ref[:, js] * y
    h_ref[...] = _ada_norm(o_ref[...], g_ref[...], sc_ref[...], sh_ref[...]).astype(h_ref.dtype)


def _combine(x, ys, route, mod3, g_next, mod3_next, seq):
    T, D = x.shape
    rows = COMBINE_ROWS
    nblk = T // rows
    nj = D // (2 * LANES)
    batch_of = lambda i: i * rows // seq
    return pl.pallas_call(
        _combine_kernel,
        grid=(nblk,),
        in_specs=[pl.BlockSpec((rows, D), lambda i: (i, 0)),
                  pl.BlockSpec((None, 1, D), lambda i: (batch_of(i), 0, 5)),
                  pl.BlockSpec((rows, LANES), lambda i: (i, 0)),
                  pl.BlockSpec((rows * nj, LANES), lambda i: (i, 0)),
                  pl.BlockSpec((rows * nj, LANES), lambda i: (nblk + i, 0)),
                  pl.BlockSpec((1, D), lambda i: (0, 0)),
                  pl.BlockSpec((None, 1, D), lambda i: (batch_of(i), 0, 0)),
                  pl.BlockSpec((None, 1, D), lambda i: (batch_of(i), 0, 1))],
        out_specs=[pl.BlockSpec((rows, D), lambda i: (i, 0)),
                   pl.BlockSpec((rows, D), lambda i: (i, 0))],
        out_shape=[jax.ShapeDtypeStruct((T, D), F32), jax.ShapeDtypeStruct((T, D), BF16)],
        compiler_params=_params("parallel"),
        name="moe_combine",
    )(x, mod3, route, ys, ys, g_next, mod3_next, mod3_next)


def _invert_kernel(pos_ref, asg_ref, fill_ref, sem_ref):
    i = pl.program_id(0)
    blk = pos_ref.shape[0]

    @pl.when(i == 0)
    def _():
        fill_ref[...] = jnp.full(fill_ref.shape, -1, jnp.int32)
        cp = pltpu.make_async_copy(fill_ref, asg_ref, sem_ref)
        cp.start()
        cp.wait()

    base = i * blk

    def body(a, c):
        asg_ref[pos_ref[a]] = base + a
        return c
    lax.fori_loop(0, blk, body, 0, unroll=8)


def _invert_slots(pos, n_slots):
    n, = pos.shape
    blk = INVERT_BLOCK
    return pl.pallas_call(
        _invert_kernel,
        grid=(n // blk,),
        in_specs=[pl.BlockSpec((blk,), lambda i: (i,), memory_space=pltpu.SMEM)],
        out_specs=pl.BlockSpec((n_slots,), lambda i: (0,), memory_space=pltpu.SMEM),
        out_shape=jax.ShapeDtypeStruct((n_slots,), jnp.int32),
        scratch_shapes=[pltpu.VMEM((n_slots,), jnp.int32), pltpu.SemaphoreType.DMA(())],
        compiler_params=_params("arbitrary"),
        name="invert_slots",
    )(pos)


def _dispatch_plan(route, l):
    T = route.shape[0]
    tm = EXPERT_TM
    n_tiles = (2 * T) // tm + N_EXPERTS
    P = n_tiles * tm
    e = route[:, 0:2].astype(jnp.int32).T.reshape(-1)
    onehot = (e[:, None] == jnp.arange(N_EXPERTS, dtype=jnp.int32)[None, :]).astype(jnp.int32)
    csum = jnp.cumsum(onehot, axis=0)
    counts = csum[-1]
    tiles_per = (counts + tm - 1) // tm
    tile_end = jnp.cumsum(tiles_per)
    off = (tile_end - tiles_per) * tm
    pos = jnp.sum((csum - onehot + off[None, :]) * onehot, axis=1)
    asg = _invert_slots(pos.astype(jnp.int32), P)
    slot = jnp.arange(P, dtype=jnp.int32)
    spare = 2 * T + ((slot // tm) % 2) * tm + slot % tm
    src = jnp.where(asg >= 0, asg % T, 0).reshape(n_tiles, 1, tm)
    dst = jnp.where(asg >= 0, asg, spare).reshape(n_tiles, 1, tm)
    tile_ids = jnp.arange(n_tiles, dtype=jnp.int32)
    tile_expert = jnp.minimum(jnp.sum(tile_ids[:, None] >= tile_end[None, :], axis=1),
                              N_EXPERTS - 1).astype(jnp.int32)
    ids = jnp.arange(N_EXPERTS, dtype=jnp.int32)
    used = counts > 0
    later = lax.cummin(jnp.where(used, ids, N_EXPERTS)[::-1])[::-1]
    nxt = jnp.concatenate([later[1:], jnp.full((1,), N_EXPERTS, jnp.int32)])
    nxt = jnp.where(nxt < N_EXPERTS, nxt, -1).astype(jnp.int32)
    wset = ((jnp.cumsum(used.astype(jnp.int32)) - 1) % 2).astype(jnp.int32)
    of_tile = tile_expert[:, None] == ids[None, :]
    nxt_tile = jnp.sum(jnp.where(of_tile, nxt[None, :], 0), axis=1)
    wset_tile = jnp.sum(jnp.where(of_tile, wset[None, :], 0), axis=1)
    meta = jnp.concatenate([jnp.stack([l, tile_end[-1].astype(jnp.int32)]), tile_expert,
                            nxt_tile, wset_tile])
    return src, dst, meta


def kernel(x, c, positions, w_ada, b_ada, g_norm_mix, g_norm_ffn, w_in, w_fg, b_fg, g_gla_out,
           g_q, g_k, w_gla_proj, w_attn_proj, w_out, w_route_group, b_route_group,
           w_route_expert, b_route_expert, w_exp_gate, w_exp_up, w_exp_down):
    B, S, D = x.shape
    L = w_ada.shape[0]
    T = B * S
    dk_all = w_fg.shape[-1]
    dv_all = g_gla_out.shape[-1]
    att_w = N_DIL * ATT_HEADS * HEAD_DIM
    gla_cols = 2 * dk_all + 2 * dv_all

    mod = _modulation(c, w_ada, b_ada)
    cos2, sin2 = _rope_tables(positions)

    wfg_pad = jnp.pad(w_fg, ((0, 0), (0, LANES - GLA_RANK), (0, 0)))
    n_route = N_GROUPS + N_EXPERTS
    wr_pad = jnp.pad(jnp.concatenate([w_route_group, w_route_expert], axis=-1),
                     ((0, 0), (0, 0), (0, LANES - n_route)))
    br_pad = jnp.pad(jnp.concatenate([b_route_group, b_route_expert], axis=-1),
                     ((0, 0), (0, LANES - n_route))).reshape(L, 1, LANES)
    g_mix = g_norm_mix.reshape(L, 1, D)
    g_ffn = g_norm_ffn.reshape(L, 1, D)
    b_fg3 = b_fg.reshape(L, 1, dk_all)
    g_gla3 = g_gla_out.reshape(L, 1, dv_all)
    g_q3 = g_q.reshape(L, 1, HEAD_DIM)
    g_k3 = g_k.reshape(L, 1, HEAD_DIM)

    def residual(acc, xres, gt):
        return xres + gt * acc

    def mod_of(l):
        return lax.dynamic_index_in_dim(mod, l, 0, keepdims=False).reshape(8, 1, N_MOD * D)

    def layer(l, carry):
        xt, h = carry
        l = jnp.asarray(l, jnp.int32)
        mod3 = mod_of(l)
        proj1 = _matmul(h, w_in, l, col0=0, n=gla_cols, out_dtype=BF16, tn=2 * MM_TN,
                        tm=2 * MM_TM, name="proj_gla")
        fg = _matmul(h, w_in, l, col0=gla_cols, n=LANES, tn=LANES, out_dtype=F32, name="proj_fg")
        proj2 = _matmul(h, w_in, l, col0=gla_cols, shift=GLA_RANK, n=3 * att_w,
                        out_dtype=BF16, tm=2 * MM_TM, name="proj_att")
        gates = _matmul(h, w_in, l, col0=gla_cols + 3 * att_w, shift=GLA_RANK, n=2 * D,
                        out_dtype=BF16, tm=2 * MM_TM, name="proj_gates")
        o_gla = _gla(proj1, fg, l, wfg_pad, b_fg3, g_gla3, B, S)
        prep = _qkprep(proj2, l, cos2, sin2, g_q3, g_k3)
        outs, lses = [], []
        for grp, (_, dil) in enumerate(DIL_GROUPS):
            o, lse = _attn_group(prep[grp], prep[N_DIL + grp], prep[2 * N_DIL + grp], dil, B, S)
            outs.append(o)
            lses.append(lse)
        o_att = _attn_mix(outs, lses)
        merged = _merge(o_gla, o_att, gates, l, w_gla_proj, w_attn_proj)
        tn_out = 2 * MM_TN
        gt_spec = pl.BlockSpec((None, 1, tn_out),
                               lambda j, i, l: (i * MM_TM // S, 0, 2 * (D // tn_out) + j))
        xt = _matmul(merged, w_out, l, col0=0, n=D, out_dtype=F32, tn=tn_out, epilogue=residual,
                     extras=(xt, mod3),
                     extra_specs=(pl.BlockSpec((MM_TM, tn_out), lambda j, i, l: (i, j)), gt_spec),
                     name="out_proj")
        h2, route = _router(xt, l, g_ffn, mod3, wr_pad, br_pad, S)
        src, dst, meta = _dispatch_plan(route, l)
        nj = D // (2 * LANES)
        n_out = 2 * T + 2 * EXPERT_TM
        ys = _experts(h2.reshape(T, nj, LANES), src, dst, meta, w_exp_gate, w_exp_up, w_exp_down,
                      n_out)
        l_next = jnp.minimum(l + 1, L - 1)
        g_next = lax.dynamic_index_in_dim(g_mix, l_next, 0, keepdims=False)
        return tuple(_combine(xt, ys.reshape(n_out * nj, LANES), route, mod3, g_next,
                              mod_of(l_next), S))

    x0 = x.reshape(T, D)
    zero = jnp.int32(0)
    h0, x_init = _norm(x0, zero, g_mix, mod_of(zero), 0, S)
    xt, _ = lax.fori_loop(0, L, layer, (x_init, h0))
    return xt.reshape(B, S, D)
```
